```python
import math
import jax, jax.numpy as jnp
from jax import lax
import numpy as np

D_MODEL = 2048
BATCH = 8
SEQ = 2048
DEPTH = 1
DEC_BATCH = 16
DEC_SEQ = 32
PAST_LEN = 1024

CHUNK = 64
Q_BLOCK = 128
N_MEM = 256
EPS = 1e-6
NEG_INF = -1e30

GLA_HEADS = 4
GLA_DK = 128
GLA_DV = 256
GLA_GATE_RANK = 16
GLA_GATE_NORM = 16.0
DIFF_HEADS = 4
DIFF_DH = 64
DIFF_DV = 128
MEM_HEADS = 4
MEM_DH = 128
REL_BUCKETS = 32
REL_MAX_DIST = 128
D_FF = 5504

GLA_QK = GLA_HEADS * GLA_DK
GLA_V = GLA_HEADS * GLA_DV
DIFF_QK = DIFF_HEADS * 2 * DIFF_DH
DIFF_V = DIFF_HEADS * DIFF_DV
MEM_W = MEM_HEADS * MEM_DH
MIX_WIDTH = GLA_V + DIFF_V + MEM_W
SPLITS = (GLA_QK, GLA_QK, GLA_V, GLA_V, GLA_GATE_RANK, DIFF_QK, DIFF_QK, DIFF_V, MEM_W)
IN_WIDTH = GLA_QK * 2 + GLA_V * 2 + GLA_GATE_RANK + DIFF_QK * 2 + DIFF_V + MEM_W

kernel_name = 'hybrid_gla_diffattn_memory_macaron_step'


def rmsnorm(x, g):
    xf = x.astype(jnp.float32)
    y = xf * lax.rsqrt(jnp.mean(xf * xf, axis=-1, keepdims=True) + EPS)
    return (y * g.astype(jnp.float32)).astype(x.dtype)


def swiglu(x, w_in, w_out):
    gate, up = jnp.split(x @ w_in, 2, axis=-1)
    return (jax.nn.silu(gate) * up) @ w_out


def rel_bucket(rel):
    nb = REL_BUCKETS // 2
    max_exact = nb // 2
    ret = jnp.where(rel > 0, nb, 0)
    n = jnp.abs(rel)
    nf = jnp.maximum(n, 1).astype(jnp.float32)
    large = max_exact + (jnp.log(nf / max_exact) / math.log(REL_MAX_DIST / max_exact)
                         * (nb - max_exact)).astype(jnp.int32)
    large = jnp.minimum(large, nb - 1)
    return ret + jnp.where(n < max_exact, n, large)


def rel_bias(q_pos, k_pos, table):
    b = rel_bucket(k_pos[None, :] - q_pos[:, None])
    return jnp.transpose(table.astype(jnp.float32)[b], (2, 0, 1))


def project_groups(h, lp):
    B, T, _ = h.shape
    idx = np.cumsum(np.array(SPLITS))[:-1].tolist()
    gq, gk, gv, gr, glr, dq, dk, dv, mq = jnp.split(h @ lp['w_in'], idx, axis=-1)
    f32 = jnp.float32
    gq = gq.reshape(B, T, GLA_HEADS, GLA_DK).astype(f32) * (GLA_DK ** -0.5)
    gk = gk.reshape(B, T, GLA_HEADS, GLA_DK).astype(f32)
    gv = gv.reshape(B, T, GLA_HEADS, GLA_DV).astype(f32)
    gg = jax.nn.log_sigmoid((glr @ lp['w_gla_g2'] + lp['b_gla_g']).astype(f32)) / GLA_GATE_NORM
    gg = gg.reshape(B, T, GLA_HEADS, GLA_DK)
    dq = rmsnorm(dq.reshape(B, T, DIFF_HEADS, 2, DIFF_DH), lp['diff_q_norm'])
    dk = rmsnorm(dk.reshape(B, T, DIFF_HEADS, 2, DIFF_DH), lp['diff_k_norm'])
    dv = dv.reshape(B, T, DIFF_HEADS, DIFF_DV)
    mq = rmsnorm(mq.reshape(B, T, MEM_HEADS, MEM_DH), lp['mem_q_norm'])
    return gq, gk, gv, gr, gg, dq, dk, dv, mq


def gla_chunk(s0, q, k, v, g):
    b = jnp.cumsum(g, axis=2)
    qe = q * jnp.exp(b)
    ke = k * jnp.exp(-b)
    C = q.shape[2]
    causal = jnp.tril(jnp.ones((C, C), dtype=bool))
    a = jnp.where(causal, jnp.einsum('bhid,bhjd->bhij', qe, ke), 0.0)
    o = jnp.einsum('bhcd,bhde->bhce', qe, s0) + jnp.einsum('bhij,bhje->bhie', a, v)
    b_last = b[:, :, -1:, :]
    s1 = jnp.exp(b_last[:, :, 0, :, None]) * s0 + jnp.einsum('bhjd,bhje->bhde', k * jnp.exp(b_last - b), v)
    return s1, o


def diff_lambda_value(lam_params, layer):
    l = lam_params.astype(jnp.float32)
    lam_init = 0.8 - 0.6 * math.exp(-0.3 * layer)
    lam = jnp.exp(jnp.sum(l[0] * l[1])) - jnp.exp(jnp.sum(l[2] * l[3])) + lam_init
    return lam, lam_init


def diff_attend(q, k, v, q_pos, k_pos, lam, table):
    s = jnp.einsum('bqhcd,bkhcd->bhcqk', q.astype(jnp.float32), k.astype(jnp.float32)) * (DIFF_DH ** -0.5)
    s = s + rel_bias(q_pos, k_pos, table)[None, :, None]
    visible = (k_pos[None, :] // CHUNK) <= (q_pos[:, None] // CHUNK)
    p = jax.nn.softmax(jnp.where(visible, s, NEG_INF), axis=-1)
    a = p[:, :, 0] - lam * p[:, :, 1]
    return jnp.einsum('bhqk,bkhd->bqhd', a, v.astype(jnp.float32))


def mem_kv(mem, lp):
    B, M, _ = mem.shape
    k, v = jnp.split(rmsnorm(mem, lp['mem_norm']) @ lp['w_mem_kv'], 2, axis=-1)
    k = rmsnorm(k.reshape(B, M, MEM_HEADS, MEM_DH), lp['mem_k_norm'])
    return k, v.reshape(B, M, MEM_HEADS, MEM_DH)


def mem_attend(q, k, v):
    s = jnp.einsum('bqhd,bmhd->bhqm', q.astype(jnp.float32), k.astype(jnp.float32)) * (MEM_DH ** -0.5)
    p = jax.nn.softmax(s, axis=-1)
    return jnp.einsum('bhqm,bmhd->bqhd', p, v.astype(jnp.float32))


def merge(gla_o, gla_r, diff_o, mem_o, lam_init, lp, dtype):
    B, T = gla_o.shape[:2]
    g = rmsnorm(gla_o, lp['gla_out_norm']) * jax.nn.silu(gla_r.reshape(B, T, GLA_HEADS, GLA_DV).astype(jnp.float32))
    d = rmsnorm(diff_o, lp['diff_out_norm']) * (1.0 - lam_init)
    cat = jnp.concatenate([g.reshape(B, T, GLA_V), d.reshape(B, T, DIFF_V), mem_o.reshape(B, T, MEM_W)], axis=-1)
    return cat.astype(dtype) @ lp['w_o']


def mix_prompt(h, mem, lp, table, layer):
    B, S, _ = h.shape
    gq, gk, gv, gr, gg, dq, dk, dv, mq = project_groups(h, lp)
    n = S // CHUNK

    def to_chunks(t):
        return t.reshape(B, n, CHUNK, t.shape[2], t.shape[3]).transpose(1, 0, 3, 2, 4)

    s0 = jnp.zeros((B, GLA_HEADS, GLA_DK, GLA_DV), jnp.float32)
    gla_state, oc = lax.scan(lambda st, xs: gla_chunk(st, xs[0], xs[1], xs[2], xs[3]), s0,
                             (to_chunks(gq), to_chunks(gk), to_chunks(gv), to_chunks(gg)))
    gla_o = oc.transpose(1, 0, 3, 2, 4).reshape(B, S, GLA_HEADS, GLA_DV)
    lam, lam_init = diff_lambda_value(lp['diff_lambda'], layer)
    nb = S // Q_BLOCK
    k_pos = jnp.arange(S)
    q_blocks = dq.reshape(B, nb, Q_BLOCK, DIFF_HEADS, 2, DIFF_DH).swapaxes(0, 1)
    pos_blocks = k_pos.reshape(nb, Q_BLOCK)
    ob = lax.map(lambda xs: diff_attend(xs[0], dk, dv, xs[1], k_pos, lam, table), (q_blocks, pos_blocks))
    diff_o = ob.swapaxes(0, 1).reshape(B, S, DIFF_HEADS, DIFF_DV)
    mk, mv = mem_kv(mem, lp)
    mem_o = mem_attend(mq, mk, mv)
    y = merge(gla_o, gr, diff_o, mem_o, lam_init, lp, h.dtype)
    return y, (dk.reshape(B, S, DIFF_HEADS, 2 * DIFF_DH), dv, gla_state, mk, mv)


def mix_sample(h, past_k, past_v, gla_s0, mem_k, mem_v, lp, table, layer):
    B, T, _ = h.shape
    P = past_k.shape[1]
    gq, gk, gv, gr, gg, dq, dk, dv, mq = project_groups(h, lp)
    gla_state, o = gla_chunk(gla_s0.astype(jnp.float32), gq.transpose(0, 2, 1, 3), gk.transpose(0, 2, 1, 3),
                             gv.transpose(0, 2, 1, 3), gg.transpose(0, 2, 1, 3))
    gla_o = o.transpose(0, 2, 1, 3)
    lam, lam_init = diff_lambda_value(lp['diff_lambda'], layer)
    k_all = jnp.concatenate([past_k.reshape(B, P, DIFF_HEADS, 2, DIFF_DH).astype(dk.dtype), dk], axis=1)
    v_all = jnp.concatenate([past_v.astype(dv.dtype), dv], axis=1)
    q_pos = P + jnp.arange(T)
    k_pos = jnp.arange(P + T)
    diff_o = diff_attend(dq, k_all, v_all, q_pos, k_pos, lam, table)
    mem_o = mem_attend(mq, mem_k, mem_v)
    y = merge(gla_o, gr, diff_o, mem_o, lam_init, lp, h.dtype)
    return y, (dk.reshape(B, T, DIFF_HEADS, 2 * DIFF_DH), dv, gla_state)


def conformer_layer(x, mix, lp):
    x = x + 0.5 * swiglu(rmsnorm(x, lp['norm_ffn1']), lp['w_ffn1_in'], lp['w_ffn1_out'])
    y, state = mix(rmsnorm(x, lp['norm_mix']))
    x = x + y
    x = x + 0.5 * swiglu(rmsnorm(x, lp['norm_ffn2']), lp['w_ffn2_in'], lp['w_ffn2_out'])
    return rmsnorm(x, lp['norm_final']), state


def setup_inputs(seed: int = 0) -> dict:
    key = jax.random.key(seed)
    ks = list(jax.random.split(key, 32))
    f32 = jnp.float32

    def nrm(shape, scale):
        return jax.random.normal(ks.pop(), shape, f32) * scale

    def gain(shape):
        return 1.0 + nrm(shape, 0.02)

    D, L = D_MODEL, DEPTH
    return {
        'x_prompt': nrm((BATCH, SEQ, D), 1.0),
        'x_sample': nrm((DEC_BATCH, DEC_SEQ, D), 1.0),
        'mem_prompt': nrm((BATCH, N_MEM, D), 1.0),
        'cache_diff_k': nrm((L, DEC_BATCH, PAST_LEN, DIFF_HEADS, 2 * DIFF_DH), 1.0),
        'cache_diff_v': nrm((L, DEC_BATCH, PAST_LEN, DIFF_HEADS, DIFF_DV), 1.0),
        'state_gla': nrm((L, DEC_BATCH, GLA_HEADS, GLA_DK, GLA_DV), 0.5),
        'cache_mem_k': nrm((L, DEC_BATCH, N_MEM, MEM_HEADS, MEM_DH), 1.0),
        'cache_mem_v': nrm((L, DEC_BATCH, N_MEM, MEM_HEADS, MEM_DH), 1.0),
        'rel_bias_table': nrm((REL_BUCKETS, DIFF_HEADS), 0.5),
        'norm_ffn1': gain((L, D)),
        'w_ffn1_in': nrm((L, D, 2 * D_FF), D ** -0.5),
        'w_ffn1_out': nrm((L, D_FF, D), D_FF ** -0.5),
        'norm_mix': gain((L, D)),
        'w_in': nrm((L, D, IN_WIDTH), D ** -0.5),
        'w_gla_g2': nrm((L, GLA_GATE_RANK, GLA_QK), GLA_GATE_RANK ** -0.5),
        'b_gla_g': nrm((L, GLA_QK), 0.1),
        'gla_out_norm': gain((L, GLA_DV)),
        'diff_q_norm': gain((L, DIFF_DH)),
        'diff_k_norm': gain((L, DIFF_DH)),
        'diff_lambda': nrm((L, 4, DIFF_DH), 0.1),
        'diff_out_norm': gain((L, DIFF_DV)),
        'mem_norm': gain((L, D)),
        'w_mem_kv': nrm((L, D, 2 * MEM_W), D ** -0.5),
        'mem_q_norm': gain((L, MEM_DH)),
        'mem_k_norm': gain((L, MEM_DH)),
        'w_o': nrm((L, MIX_WIDTH, D), MIX_WIDTH ** -0.5),
        'norm_ffn2': gain((L, D)),
        'w_ffn2_in': nrm((L, D, 2 * D_FF), D ** -0.5),
        'w_ffn2_out': nrm((L, D_FF, D), D_FF ** -0.5),
        'norm_final': gain((L, D)),
    }


def reference(x_prompt, x_sample, mem_prompt, cache_diff_k, cache_diff_v, state_gla, cache_mem_k, cache_mem_v,
              rel_bias_table, norm_ffn1, w_ffn1_in, w_ffn1_out, norm_mix, w_in, w_gla_g2, b_gla_g, gla_out_norm,
              diff_q_norm, diff_k_norm, diff_lambda, diff_out_norm, mem_norm, w_mem_kv, mem_q_norm, mem_k_norm,
              w_o, norm_ffn2, w_ffn2_in, w_ffn2_out, norm_final):
    yp, ys = x_prompt, x_sample
    pk, pv, pg, pmk, pmv, sk, sv, sg = [], [], [], [], [], [], [], []
    for l in range(DEPTH):
        lp = {
            'norm_ffn1': norm_ffn1[l], 'w_ffn1_in': w_ffn1_in[l], 'w_ffn1_out': w_ffn1_out[l],
            'norm_mix': norm_mix[l], 'w_in': w_in[l], 'w_gla_g2': w_gla_g2[l], 'b_gla_g': b_gla_g[l],
            'gla_out_norm': gla_out_norm[l], 'diff_q_norm': diff_q_norm[l], 'diff_k_norm': diff_k_norm[l],
            'diff_lambda': diff_lambda[l], 'diff_out_norm': diff_out_norm[l], 'mem_norm': mem_norm[l],
            'w_mem_kv': w_mem_kv[l], 'mem_q_norm': mem_q_norm[l], 'mem_k_norm': mem_k_norm[l], 'w_o': w_o[l],
            'norm_ffn2': norm_ffn2[l], 'w_ffn2_in': w_ffn2_in[l], 'w_ffn2_out': w_ffn2_out[l],
            'norm_final': norm_final[l],
        }
        yp, (dk_p, dv_p, g_p, mk_p, mv_p) = conformer_layer(
            yp, lambda h: mix_prompt(h, mem_prompt, lp, rel_bias_table, l), lp)
        ys, (dk_s, dv_s, g_s) = conformer_layer(
            ys, lambda h: mix_sample(h, cache_diff_k[l], cache_diff_v[l], state_gla[l], cache_mem_k[l],
                                     cache_mem_v[l], lp, rel_bias_table, l), lp)
        pk.append(dk_p); pv.append(dv_p); pg.append(g_p); pmk.append(mk_p); pmv.append(mv_p)
        sk.append(dk_s); sv.append(dv_s); sg.append(g_s)
    return (yp, ys, jnp.stack(pk), jnp.stack(pv), jnp.stack(pg), jnp.stack(pmk), jnp.stack(pmv),
            jnp.stack(sk), jnp.stack(sv), jnp.stack(sg))
```

```python
import functools
import math

import jax
import jax.numpy as jnp
from jax import lax
from jax.experimental import pallas as pl
from jax.experimental.pallas import tpu as pltpu

F32 = jnp.float32
BF16 = jnp.bfloat16

D_MODEL = 2048
CHUNK = 64
EPS = 1e-6
NEG_INF = -1e30
GLA_HEADS, GLA_DK, GLA_DV, GLA_GATE_RANK, GLA_GATE_NORM = 4, 128, 256, 16, 16.0
DIFF_HEADS, DIFF_DH, DIFF_DV = 4, 64, 128
MEM_HEADS, MEM_DH = 4, 128
REL_BUCKETS, REL_MAX_DIST = 32, 128
D_FF = 5504
GLA_QK = GLA_HEADS * GLA_DK
GLA_V = GLA_HEADS * GLA_DV
DIFF_QK = DIFF_HEADS * 2 * DIFF_DH
DIFF_V = DIFF_HEADS * DIFF_DV
MEM_W = MEM_HEADS * MEM_DH
GLR_OFF = 2 * GLA_QK + 2 * GLA_V

LANES = 128
D_FF_PAD = 5632
FF_TILE = 512
TOK_TILE = 512
COL_BLOCK = 512
N_COL_BLOCKS = 10
Q_BLOCK = 256
MASKED_BUCKET = REL_BUCKETS
VMEM_LIMIT = 50 * 1024 * 1024


def _dot(a, b):
    return jnp.dot(a, b, preferred_element_type=F32)


def _dot_nt(a, b):
    return lax.dot_general(a, b, (((1,), (1,)), ((), ())), preferred_element_type=F32)


def _dot_tn(a, b):
    return lax.dot_general(a, b, (((0,), (0,)), ((), ())), preferred_element_type=F32)


def _rms(x, gain):
    return x * lax.rsqrt(jnp.mean(x * x, axis=-1, keepdims=True) + EPS) * gain


def _params(sem):
    return pltpu.CompilerParams(dimension_semantics=sem, vmem_limit_bytes=VMEM_LIMIT)


def _ffn_body(*refs, n_ff, final_norm):
    if final_norm:
        x_ref, g_ref, wg_ref, wu_ref, wo_ref, fg_ref, o_ref, xn_ref = refs
    else:
        x_ref, g_ref, wg_ref, wu_ref, wo_ref, o_ref, xn_ref = refs
    j = pl.program_id(1)

    @pl.when(j == 0)
    def _():
        x = x_ref[...]
        xn_ref[...] = _rms(x, g_ref[...]).astype(BF16)
        o_ref[...] = x

    xn = xn_ref[...]
    gate = _dot(xn, wg_ref[...])
    up = _dot(xn, wu_ref[...])
    act = (jax.nn.silu(gate) * up).astype(BF16)
    o_ref[...] += 0.5 * _dot(act, wo_ref[...])

    if final_norm:
        @pl.when(j == n_ff - 1)
        def _():
            o_ref[...] = _rms(o_ref[...], fg_ref[...])


def _ffn(x, gain, wg, wu, wo, final_gain=None):
    n = x.shape[0]
    tm = min(TOK_TILE, n)
    n_ff = D_FF_PAD // FF_TILE
    row = pl.BlockSpec((1, D_MODEL), lambda i, j: (0, 0))
    in_specs = [
        pl.BlockSpec((tm, D_MODEL), lambda i, j: (i, 0)),
        row,
        pl.BlockSpec((D_MODEL, FF_TILE), lambda i, j: (0, j)),
        pl.BlockSpec((D_MODEL, FF_TILE), lambda i, j: (0, j)),
        pl.BlockSpec((FF_TILE, D_MODEL), lambda i, j: (j, 0)),
    ]
    args = [x, gain, wg, wu, wo]
    if final_gain is not None:
        in_specs.append(row)
        args.append(final_gain)
    return pl.pallas_call(
        functools.partial(_ffn_body, n_ff=n_ff, final_norm=final_gain is not None),
        grid=(n // tm, n_ff),
        in_specs=in_specs,
        out_specs=pl.BlockSpec((tm, D_MODEL), lambda i, j: (i, 0)),
        out_shape=jax.ShapeDtypeStruct((n, D_MODEL), F32),
        scratch_shapes=[pltpu.VMEM((tm, D_MODEL), BF16)],
        compiler_params=_params(("parallel", "arbitrary")),
        name="ffn",
    )(*args)


def _store_group_rms(acc, gain_ref, out_ref, width, scale):
    lane = lax.broadcasted_iota(jnp.int32, (1, LANES), 1)
    low = lane < width
    for c in range(COL_BLOCK // LANES):
        cols = slice(c * LANES, (c + 1) * LANES)
        xc = acc[:, cols]
        sq = xc * xc
        if width == LANES:
            r = lax.rsqrt(jnp.mean(sq, axis=-1, keepdims=True) + EPS)
        else:
            s_lo = jnp.sum(jnp.where(low, sq, 0.0), axis=-1, keepdims=True)
            s_hi = jnp.sum(jnp.where(low, 0.0, sq), axis=-1, keepdims=True)
            r = jnp.where(low, lax.rsqrt(s_lo / width + EPS), lax.rsqrt(s_hi / width + EPS))
        y = xc * r * gain_ref[:, cols]
        if scale != 1.0:
            y = y * scale
        out_ref[:, cols] = y.astype(out_ref.dtype)


def _proj_body(x_ref, g_ref, w_ref, wglr_ref, wg2_ref, bg_ref, qn_ref, kn_ref, mqn_ref,
               gq_ref, gk_ref, gv_ref, gr_ref, gg_ref, dq_ref, dk_ref, dv_ref, mq_ref, xn_ref):
    j = pl.program_id(1)

    @pl.when(j == 0)
    def _():
        xn = _rms(x_ref[...], g_ref[...]).astype(BF16)
        xn_ref[...] = xn
        glr = _dot(xn, wglr_ref[...])
        z = _dot(glr.astype(BF16), wg2_ref[...]) + bg_ref[...]
        gg_ref[...] = jax.nn.log_sigmoid(z) / GLA_GATE_NORM

    acc = _dot(xn_ref[...], w_ref[...])

    @pl.when(j == 0)
    def _():
        gq_ref[...] = acc * (GLA_DK ** -0.5)

    @pl.when(j == 1)
    def _():
        gk_ref[...] = acc

    for half in range(2):
        cols = slice(half * COL_BLOCK, (half + 1) * COL_BLOCK)

        @pl.when(j == 2 + half)
        def _():
            gv_ref[:, cols] = acc.astype(BF16)

        @pl.when(j == 4 + half)
        def _():
            gr_ref[:, cols] = acc

    @pl.when(j == 6)
    def _():
        _store_group_rms(acc, qn_ref, dq_ref, DIFF_DH, DIFF_DH ** -0.5)

    @pl.when(j == 7)
    def _():
        _store_group_rms(acc, kn_ref, dk_ref, DIFF_DH, 1.0)

    @pl.when(j == 8)
    def _():
        dv_ref[...] = acc

    @pl.when(j == 9)
    def _():
        _store_group_rms(acc, mqn_ref, mq_ref, MEM_DH, 1.0)


def _proj(x, gain, w, wglr, wg2, bg, qn, kn, mqn):
    n = x.shape[0]
    tm = min(TOK_TILE, n)
    const = lambda shape: pl.BlockSpec(shape, lambda i, j: (0, 0))
    out = lambda width: pl.BlockSpec((tm, width), lambda i, j: (i, 0))
    shp = lambda width, dt: jax.ShapeDtypeStruct((n, width), dt)
    return pl.pallas_call(
        _proj_body,
        grid=(n // tm, N_COL_BLOCKS),
        in_specs=[
            pl.BlockSpec((tm, D_MODEL), lambda i, j: (i, 0)),
            const((1, D_MODEL)),
            pl.BlockSpec((D_MODEL, COL_BLOCK), lambda i, j: (0, j)),
            const((D_MODEL, LANES)),
            const((LANES, GLA_QK)),
            const((1, GLA_QK)),
            const((1, DIFF_QK)),
            const((1, DIFF_QK)),
            const((1, MEM_W)),
        ],
        out_specs=[out(GLA_QK), out(GLA_QK), out(GLA_V), out(GLA_V), out(GLA_QK),
                   out(DIFF_QK), out(DIFF_QK), out(DIFF_V), out(MEM_W)],
        out_shape=[shp(GLA_QK, F32), shp(GLA_QK, F32), shp(GLA_V, BF16), shp(GLA_V, F32), shp(GLA_QK, F32),
                   shp(DIFF_QK, BF16), shp(DIFF_QK, F32), shp(DIFF_V, F32), shp(MEM_W, BF16)],
        scratch_shapes=[pltpu.VMEM((tm, D_MODEL), BF16)],
        compiler_params=_params(("parallel", "arbitrary")),
        name="proj",
    )(x, gain, w, wglr, wg2, bg, qn, kn, mqn)


def _memkv_body(x_ref, g_ref, w_ref, kn_ref, k_ref, v_ref, xn_ref):
    j = pl.program_id(1)

    @pl.when(j == 0)
    def _():
        xn_ref[...] = _rms(x_ref[...], g_ref[...]).astype(BF16)

    acc = _dot(xn_ref[...], w_ref[...])

    @pl.when(j == 0)
    def _():
        _store_group_rms(acc, kn_ref, k_ref, MEM_DH, 1.0)

    @pl.when(j == 1)
    def _():
        v_ref[...] = acc


def _memkv(mem, gain, w, kn):
    n = mem.shape[0]
    tm = min(TOK_TILE, n)
    const = lambda shape: pl.BlockSpec(shape, lambda i, j: (0, 0))
    out = pl.BlockSpec((tm, MEM_W), lambda i, j: (i, 0))
    return pl.pallas_call(
        _memkv_body,
        grid=(n // tm, 2),
        in_specs=[
            pl.BlockSpec((tm, D_MODEL), lambda i, j: (i, 0)),
            const((1, D_MODEL)),
            pl.BlockSpec((D_MODEL, MEM_W), lambda i, j: (0, j)),
            const((1, MEM_W)),
        ],
        out_specs=[out, out],
        out_shape=[jax.ShapeDtypeStruct((n, MEM_W), F32)] * 2,
        scratch_shapes=[pltpu.VMEM((tm, D_MODEL), BF16)],
        compiler_params=_params(("parallel", "arbitrary")),
        name="memkv",
    )(mem, gain, w, kn)


def _split3(x):
    hi = x.astype(BF16)
    r1 = x - hi.astype(F32)
    mid = r1.astype(BF16)
    lo = (r1 - mid.astype(F32)).astype(BF16)
    return hi, mid, lo


def _gla_body(*refs, chunk, n_chunks, n_steps, has_state):
    if has_state:
        gq_ref, gk_ref, gg_ref, gv_ref, gr_ref, on_ref, s0_ref, go_ref, st_ref, state = refs
    else:
        gq_ref, gk_ref, gg_ref, gv_ref, gr_ref, on_ref, go_ref, st_ref, state = refs
    t = pl.program_id(1)

    @pl.when(t == 0)
    def _():
        for h in range(GLA_HEADS):
            if has_state:
                state[h] = s0_ref[0, h].T
            else:
                state[h] = jnp.zeros((GLA_DV, GLA_DK), F32)

    row = lax.broadcasted_iota(jnp.int32, (chunk, chunk), 0)
    col = lax.broadcasted_iota(jnp.int32, (chunk, chunk), 1)
    causal = row >= col
    tril = causal.astype(BF16)

    for c in range(n_chunks):
        rows = slice(c * chunk, (c + 1) * chunk)
        g_hi, g_mid, g_lo = _split3(gg_ref[rows, :])
        b = _dot(tril, g_hi) + _dot(tril, g_mid) + _dot(tril, g_lo)
        b_last = b[chunk - 1:chunk, :]
        q = gq_ref[rows, :]
        k = gk_ref[rows, :]
        qe = (q * jnp.exp(b)).astype(BF16)
        ke = (k * jnp.exp(-b)).astype(BF16)
        kd = (k * jnp.exp(b_last - b)).astype(BF16)
        decay = jnp.exp(b_last)
        for h in range(GLA_HEADS):
            kc = slice(h * GLA_DK, (h + 1) * GLA_DK)
            vc = slice(h * GLA_DV, (h + 1) * GLA_DV)
            v = gv_ref[rows, vc]
            a = jnp.where(causal, _dot_nt(qe[:, kc], ke[:, kc]), 0.0).astype(BF16)
            s_t = state[h]
            o = _dot_nt(qe[:, kc], s_t.astype(BF16)) + _dot(a, v)
            state[h] = s_t * decay[:, kc] + _dot_tn(v, kd[:, kc])
            go_ref[rows, vc] = (_rms(o, on_ref[...]) * jax.nn.silu(gr_ref[rows, vc])).astype(BF16)

    @pl.when(t == n_steps - 1)
    def _():
        for h in range(GLA_HEADS):
            st_ref[0, h] = state[h].T


def _gla(gq, gk, gg, gv, gr, onorm, s0, batch, seq, chunk):
    tt = min(TOK_TILE, seq)
    n_steps = seq // tt
    tok = lambda width: pl.BlockSpec((tt, width), lambda b, t: (b * n_steps + t, 0))
    st_spec = pl.BlockSpec((1, GLA_HEADS, GLA_DK, GLA_DV), lambda b, t: (b, 0, 0, 0))
    in_specs = [tok(GLA_QK), tok(GLA_QK), tok(GLA_QK), tok(GLA_V), tok(GLA_V),
                pl.BlockSpec((1, GLA_DV), lambda b, t: (0, 0))]
    args = [gq, gk, gg, gv, gr, onorm]
    if s0 is not None:
        in_specs.append(st_spec)
        args.append(s0)
    return pl.pallas_call(
        functools.partial(_gla_body, chunk=chunk, n_chunks=tt // chunk, n_steps=n_steps, has_state=s0 is not None),
        grid=(batch, n_steps),
        in_specs=in_specs,
        out_specs=[tok(GLA_V), st_spec],
        out_shape=[jax.ShapeDtypeStruct((batch * seq, GLA_V), BF16),
                   jax.ShapeDtypeStruct((batch, GLA_HEADS, GLA_DK, GLA_DV), F32)],
        scratch_shapes=[pltpu.VMEM((GLA_HEADS, GLA_DV, GLA_DK), F32)],
        compiler_params=_params(("parallel", "arbitrary")),
        name="gla",
    )(*args)


def _t5_bucket(rel):
    nb = REL_BUCKETS // 2
    max_exact = nb // 2
    ret = jnp.where(rel > 0, nb, 0)
    n = jnp.abs(rel)
    nf = jnp.maximum(n, 1).astype(F32)
    large = max_exact + (jnp.log(nf / max_exact) / math.log(REL_MAX_DIST / max_exact)
                         * (nb - max_exact)).astype(jnp.int32)
    large = jnp.minimum(large, nb - 1)
    return ret + jnp.where(n < max_exact, n, large)


def _bucket_tile(q_pos, k_pos):
    visible = (k_pos[None, :] // CHUNK) <= (q_pos[:, None] // CHUNK)
    return jnp.where(visible, _t5_bucket(k_pos[None, :] - q_pos[:, None]), MASKED_BUCKET).astype(jnp.int32)


def _bias_from_buckets(idx, tab_ref, head):
    def step(bk, acc):
        return jnp.where(idx == bk, tab_ref[bk * DIFF_HEADS + head], acc)
    return lax.fori_loop(0, REL_BUCKETS, step, jnp.full(idx.shape, NEG_INF, F32))


def _lambda(lam_ref, lam_init):
    l = lam_ref[...]
    return (jnp.exp(jnp.sum(l[0:1] * l[1:2], axis=-1, keepdims=True))
            - jnp.exp(jnp.sum(l[2:3] * l[3:4], axis=-1, keepdims=True)) + lam_init)


def _comp_masks(q):
    lane = lax.broadcasted_iota(jnp.int32, q.shape, 1)
    zero = jnp.zeros_like(q)
    return jnp.where(lane < DIFF_DH, q, zero), jnp.where(lane < DIFF_DH, zero, q)


def _softmax_pv(q_c, parts):
    scores = [_dot_nt(q_c, k) + bias for k, _, bias in parts]
    m = scores[0].max(axis=-1, keepdims=True)
    for s in scores[1:]:
        m = jnp.maximum(m, s.max(axis=-1, keepdims=True))
    l = 0.0
    o = 0.0
    for s, (_, v, _) in zip(scores, parts):
        e = jnp.exp(s - m)
        l = l + e.sum(axis=-1, keepdims=True)
        o = o + _dot(e.astype(BF16), v)
    return o * (1.0 / l)


def _diff_finish(o0, o1, lam, on_ref, lam_init, out_dtype):
    o = o0 - lam * o1
    return (_rms(o, on_ref[...]) * (1.0 - lam_init)).astype(out_dtype)


def _diff_prompt_body(tab_ref, q_ref, k_ref, v_ref, idx_ref, lam_ref, on_ref, o_ref, kb, vb, bias, *, seq, lam_init):
    b = pl.program_id(0)
    h = pl.program_id(1)

    @pl.when(b == 0)
    def _():
        for t in range(2):
            bias[h, t] = _bias_from_buckets(idx_ref[t], tab_ref, h)

    kb[...] = k_ref[...].astype(BF16)
    vb[...] = v_ref[...].astype(BF16)
    lam = _lambda(lam_ref, lam_init)
    far_bias = tab_ref[(REL_BUCKETS // 2 - 1) * DIFF_HEADS + h]

    for i in range(seq // Q_BLOCK):
        near0 = max(i - 1, 0) * Q_BLOCK
        near = slice(near0, near0 + 2 * Q_BLOCK)
        near_bias = bias[h, min(i, 1)]
        outs = []
        for q_c in _comp_masks(q_ref[i * Q_BLOCK:(i + 1) * Q_BLOCK, :]):
            parts = [(kb[near, :], vb[near, :], near_bias)]
            if near0 > 0:
                parts.append((kb[0:near0, :], vb[0:near0, :], far_bias))
            outs.append(_softmax_pv(q_c, parts))
        o_ref[i * Q_BLOCK:(i + 1) * Q_BLOCK, :] = _diff_finish(outs[0], outs[1], lam, on_ref, lam_init, o_ref.dtype)


def _diff_prompt(table, dq, dk, dv, lam_p, onorm, batch, seq, lam_init):
    q_pos = jnp.arange(Q_BLOCK)
    idx = jnp.stack([_bucket_tile(q_pos, jnp.arange(2 * Q_BLOCK)),
                     _bucket_tile(q_pos + Q_BLOCK, jnp.arange(2 * Q_BLOCK))])
    head = lambda: pl.BlockSpec((seq, LANES), lambda b, h: (b, h))
    return pl.pallas_call(
        functools.partial(_diff_prompt_body, seq=seq, lam_init=lam_init),
        grid=(batch, DIFF_HEADS),
        in_specs=[
            pl.BlockSpec(memory_space=pltpu.SMEM),
            head(), head(), head(),
            pl.BlockSpec((2, Q_BLOCK, 2 * Q_BLOCK), lambda b, h: (0, 0, 0)),
            pl.BlockSpec((4, DIFF_DH), lambda b, h: (0, 0)),
            pl.BlockSpec((1, DIFF_DV), lambda b, h: (0, 0)),
        ],
        out_specs=head(),
        out_shape=jax.ShapeDtypeStruct((batch * seq, DIFF_V), BF16),
        scratch_shapes=[pltpu.VMEM((seq, LANES), BF16), pltpu.VMEM((seq, LANES), BF16),
                        pltpu.VMEM((DIFF_HEADS, 2, Q_BLOCK, 2 * Q_BLOCK), F32)],
        compiler_params=_params(("arbitrary", "arbitrary")),
        name="diff_prompt",
    )(table, dq, dk, dv, idx, lam_p, onorm)


def _diff_sample_body(tab_ref, q_ref, k_ref, v_ref, ck_ref, cv_ref, idxc_ref, idxn_ref, lam_ref, on_ref, o_ref,
                      bias_c, bias_n, *, lam_init):
    @pl.when(pl.program_id(0) == 0)
    def _():
        for h in range(DIFF_HEADS):
            bias_c[h] = _bias_from_buckets(idxc_ref[...], tab_ref, h)
            bias_n[h] = _bias_from_buckets(idxn_ref[...], tab_ref, h)

    lam = _lambda(lam_ref, lam_init)
    for h in range(DIFF_HEADS):
        cols = slice(h * LANES, (h + 1) * LANES)
        parts = [(ck_ref[:, cols].astype(BF16), cv_ref[:, cols].astype(BF16), bias_c[h]),
                 (k_ref[:, cols].astype(BF16), v_ref[:, cols].astype(BF16), bias_n[h])]
        outs = [_softmax_pv(q_c, parts) for q_c in _comp_masks(q_ref[:, cols])]
        o_ref[:, cols] = _diff_finish(outs[0], outs[1], lam, on_ref, lam_init, o_ref.dtype)


def _diff_sample(table, dq, dk, dv, cache_k, cache_v, lam_p, onorm, batch, seq, past, lam_init):
    q_pos = past + jnp.arange(seq)
    idx_c = _bucket_tile(q_pos, jnp.arange(past))
    idx_n = _bucket_tile(q_pos, past + jnp.arange(seq))
    new = lambda: pl.BlockSpec((seq, DIFF_V), lambda b: (b, 0))
    old = lambda: pl.BlockSpec((past, DIFF_V), lambda b: (b, 0))
    const = lambda shape: pl.BlockSpec(shape, lambda b: (0, 0))
    return pl.pallas_call(
        functools.partial(_diff_sample_body, lam_init=lam_init),
        grid=(batch,),
        in_specs=[pl.BlockSpec(memory_space=pltpu.SMEM), new(), new(), new(), old(), old(),
                  const((seq, past)), const((seq, seq)), const((4, DIFF_DH)), const((1, DIFF_DV))],
        out_specs=new(),
        out_shape=jax.ShapeDtypeStruct((batch * seq, DIFF_V), BF16),
        scratch_shapes=[pltpu.VMEM((DIFF_HEADS, seq, past), F32), pltpu.VMEM((DIFF_HEADS, seq, seq), F32)],
        compiler_params=_params(("arbitrary",)),
        name="diff_sample",
    )(table, dq, dk, dv, cache_k, cache_v, idx_c, idx_n, lam_p, onorm)


def _memattn_body(q_ref, k_ref, v_ref, o_ref):
    for h in range(MEM_HEADS):
        cols = slice(h * MEM_DH, (h + 1) * MEM_DH)
        s = _dot_nt(q_ref[:, cols], k_ref[:, cols].astype(BF16)) * (MEM_DH ** -0.5)
        e = jnp.exp(s - s.max(axis=-1, keepdims=True))
        p = e * (1.0 / e.sum(axis=-1, keepdims=True))
        o_ref[:, cols] = _dot(p.astype(BF16), v_ref[:, cols].astype(BF16)).astype(o_ref.dtype)


def _memattn(mq, mk, mv, batch, seq, n_mem):
    tq = min(TOK_TILE, seq)
    nq = seq // tq
    q_spec = pl.BlockSpec((tq, MEM_W), lambda b, i: (b * nq + i, 0))
    kv_spec = pl.BlockSpec((n_mem, MEM_W), lambda b, i: (b, 0))
    return pl.pallas_call(
        _memattn_body,
        grid=(batch, nq),
        in_specs=[q_spec, kv_spec, kv_spec],
        out_specs=q_spec,
        out_shape=jax.ShapeDtypeStruct((batch * seq, MEM_W), BF16),
        compiler_params=_params(("parallel", "arbitrary")),
        name="memattn",
    )(mq, mk, mv)


def _outproj_body(x_ref, g_ref, d_ref, m_ref, wg_ref, wd_ref, wm_ref, o_ref):
    o_ref[...] = (x_ref[...] + _dot(g_ref[...], wg_ref[...]) + _dot(d_ref[...], wd_ref[...])
                  + _dot(m_ref[...], wm_ref[...]))


def _outproj(x, g, d, m, wg, wd, wm):
    n = x.shape[0]
    tm = min(TOK_TILE, n)
    tok = lambda width: pl.BlockSpec((tm, width), lambda i: (i, 0))
    const = lambda rows: pl.BlockSpec((rows, D_MODEL), lambda i: (0, 0))
    return pl.pallas_call(
        _outproj_body,
        grid=(n // tm,),
        in_specs=[tok(D_MODEL), tok(GLA_V), tok(DIFF_V), tok(MEM_W), const(GLA_V), const(DIFF_V), const(MEM_W)],
        out_specs=tok(D_MODEL),
        out_shape=jax.ShapeDtypeStruct((n, D_MODEL), F32),
        compiler_params=_params(("parallel",)),
        name="outproj",
    )(x, g, d, m, wg, wd, wm)


def _row(v):
    return v.reshape(1, -1).astype(F32)


def _prep_ffn(w_in, w_out):
    pad = D_FF_PAD - D_FF
    wg = jnp.pad(w_in[:, :D_FF].astype(BF16), ((0, 0), (0, pad)))
    wu = jnp.pad(w_in[:, D_FF:].astype(BF16), ((0, 0), (0, pad)))
    wo = jnp.pad(w_out.astype(BF16), ((0, pad), (0, 0)))
    return wg, wu, wo


def kernel(x_prompt, x_sample, mem_prompt, cache_diff_k, cache_diff_v, state_gla, cache_mem_k, cache_mem_v, rel_bias_table, norm_ffn1, w_ffn1_in, w_ffn1_out, norm_mix, w_in, w_gla_g2, b_gla_g, gla_out_norm, diff_q_norm, diff_k_norm, diff_lambda, diff_out_norm, mem_norm, w_mem_kv, mem_q_norm, mem_k_norm, w_o, norm_ffn2, w_ffn2_in, w_ffn2_out, norm_final):
    depth = norm_ffn1.shape[0]
    assert depth == 1, "single-layer step"
    layer = 0
    batch, seq, _ = x_prompt.shape
    dec_batch, dec_seq, _ = x_sample.shape
    past = cache_diff_k.shape[2]
    n_mem = mem_prompt.shape[1]
    lam_init = 0.8 - 0.6 * math.exp(-0.3 * layer)

    ffn1 = _prep_ffn(w_ffn1_in[layer], w_ffn1_out[layer])
    ffn2 = _prep_ffn(w_ffn2_in[layer], w_ffn2_out[layer])
    w = w_in[layer]
    w_main = jnp.concatenate([w[:, :GLR_OFF], w[:, GLR_OFF + GLA_GATE_RANK:]], axis=1).astype(BF16)
    w_glr = jnp.pad(w[:, GLR_OFF:GLR_OFF + GLA_GATE_RANK].astype(BF16), ((0, 0), (0, LANES - GLA_GATE_RANK)))
    w_g2 = jnp.pad(w_gla_g2[layer].astype(BF16), ((0, LANES - GLA_GATE_RANK), (0, 0)))
    b_g = _row(b_gla_g[layer])
    qn = _row(jnp.tile(diff_q_norm[layer], DIFF_QK // DIFF_DH))
    kn = _row(jnp.tile(diff_k_norm[layer], DIFF_QK // DIFF_DH))
    mqn = _row(jnp.tile(mem_q_norm[layer], MEM_HEADS))
    mkn = _row(jnp.tile(mem_k_norm[layer], MEM_HEADS))
    wo = w_o[layer].astype(BF16)
    wo_g, wo_d, wo_m = wo[:GLA_V], wo[GLA_V:GLA_V + DIFF_V], wo[GLA_V + DIFF_V:]
    table = rel_bias_table.astype(F32).reshape(-1)
    lam_p = diff_lambda[layer].astype(F32)
    gla_on = _row(gla_out_norm[layer])
    diff_on = _row(diff_out_norm[layer])

    mk, mv = _memkv(mem_prompt.reshape(batch * n_mem, D_MODEL), _row(mem_norm[layer]),
                    w_mem_kv[layer].astype(BF16), mkn)

    def layer_fn(x, b, t, chunk, s0, diff_fn, mem_k, mem_v):
        x = _ffn(x, _row(norm_ffn1[layer]), *ffn1)
        gq, gk, gv, gr, gg, dq, dk, dv, mq = _proj(x, _row(norm_mix[layer]), w_main, w_glr, w_g2, b_g, qn, kn, mqn)
        g_out, g_state = _gla(gq, gk, gg, gv, gr, gla_on, s0, b, t, chunk)
        d_out = diff_fn(dq, dk, dv)
        m_out = _memattn(mq, mem_k, mem_v, b, t, n_mem)
        x = _outproj(x, g_out, d_out, m_out, wo_g, wo_d, wo_m)
        x = _ffn(x, _row(norm_ffn2[layer]), *ffn2, final_gain=_row(norm_final[layer]))
        return x, dk, dv, g_state

    yp, dk_p, dv_p, g_p = layer_fn(
        x_prompt.reshape(batch * seq, D_MODEL), batch, seq, CHUNK, None,
        lambda dq, dk, dv: _diff_prompt(table, dq, dk, dv, lam_p, diff_on, batch, seq, lam_init), mk, mv)
    ys, dk_s, dv_s, g_s = layer_fn(
        x_sample.reshape(dec_batch * dec_seq, D_MODEL), dec_batch, dec_seq, dec_seq, state_gla[layer],
        lambda dq, dk, dv: _diff_sample(table, dq, dk, dv,
                                        cache_diff_k[layer].reshape(dec_batch * past, DIFF_QK),
                                        cache_diff_v[layer].reshape(dec_batch * past, DIFF_V),
                                        lam_p, diff_on, dec_batch, dec_seq, past, lam_init),
        cache_mem_k[layer].reshape(dec_batch * n_mem, MEM_W), cache_mem_v[layer].reshape(dec_batch * n_mem, MEM_W))

    head4 = lambda a, b, t: a.reshape(1, b, t, DIFF_HEADS, DIFF_DV)
    return (yp.reshape(batch, seq, D_MODEL), ys.reshape(dec_batch, dec_seq, D_MODEL),
            head4(dk_p, batch, seq), head4(dv_p, batch, seq), g_p[None],
            head4(mk, batch, n_mem), head4(mv, batch, n_mem),
            head4(dk_s, dec_batch, dec_seq), head4(dv_s, dec_batch, dec_seq), g_s[None])
```

```python
import functools
import math

import jax
import jax.numpy as jnp
from jax import lax
from jax.experimental import pallas as pl
from jax.experimental.pallas import tpu as pltpu

F32 = jnp.float32
BF16 = jnp.bfloat16

D_MODEL = 2048
CHUNK = 64
EPS = 1e-6
NEG_INF = -1e30
GLA_HEADS, GLA_DK, GLA_DV, GLA_GATE_RANK, GLA_GATE_NORM = 4, 128, 256, 16, 16.0
DIFF_HEADS, DIFF_DH, DIFF_DV = 4, 64, 128
MEM_HEADS, MEM_DH = 4, 128
REL_BUCKETS, REL_MAX_DIST = 32, 128
D_FF = 5504
GLA_QK = GLA_HEADS * GLA_DK
GLA_V = GLA_HEADS * GLA_DV
DIFF_QK = DIFF_HEADS * 2 * DIFF_DH
DIFF_V = DIFF_HEADS * DIFF_DV
MEM_W = MEM_HEADS * MEM_DH
GLR_OFF = 2 * GLA_QK + 2 * GLA_V

LANES = 128
HEADS = 4
D_FF_PAD = 5632
FF_TILE = 512
FFN_TOK_TILE = 1024
TOK_TILE = 512
PROJ_TILE = 256
Q_BLOCK = 256
MASKED_BUCKET = REL_BUCKETS
MIB = 1024 * 1024

_W_GQ, _W_GK, _W_GV, _W_GR = 0, GLA_QK, 2 * GLA_QK, 2 * GLA_QK + GLA_V
_W_DQ = _W_GR + GLA_V
_W_DK = _W_DQ + DIFF_QK
_W_DV = _W_DK + DIFF_QK
_W_MQ = _W_DV + DIFF_V
_W_GLR = _W_MQ + MEM_W
PROJ_COLS = _W_GLR + LANES


def _dot(a, b):
    return jnp.dot(a, b, preferred_element_type=F32)


def _dot_nt(a, b):
    return lax.dot_general(a, b, (((1,), (1,)), ((), ())), preferred_element_type=F32)


def _dot_tn(a, b):
    return lax.dot_general(a, b, (((0,), (0,)), ((), ())), preferred_element_type=F32)


def _rms(x, gain):
    return x * lax.rsqrt(jnp.mean(x * x, axis=-1, keepdims=True) + EPS) * gain


def _params(sem, vmem_mib=48):
    return pltpu.CompilerParams(dimension_semantics=sem, vmem_limit_bytes=vmem_mib * MIB)


def _ffn_body(*refs, n_ff, final_norm):
    if final_norm:
        x_ref, g_ref, wg_ref, wu_ref, wo_ref, fg_ref, o_ref, xn_ref = refs
    else:
        x_ref, g_ref, wg_ref, wu_ref, wo_ref, o_ref, xn_ref = refs
    j = pl.program_id(1)

    @pl.when(j == 0)
    def _():
        x = x_ref[...]
        xn_ref[...] = _rms(x, g_ref[...]).astype(BF16)
        o_ref[...] = x

    xn = xn_ref[...]
    gate = _dot(xn, wg_ref[...])
    up = _dot(xn, wu_ref[...])
    act = (jax.nn.silu(gate) * up).astype(BF16)
    o_ref[...] += 0.5 * _dot(act, wo_ref[...])

    if final_norm:
        @pl.when(j == n_ff - 1)
        def _():
            o_ref[...] = _rms(o_ref[...], fg_ref[...])


def _ffn(x, gain, wg, wu, wo, final_gain=None):
    n = x.shape[0]
    tm = min(FFN_TOK_TILE, n)
    n_ff = D_FF_PAD // FF_TILE
    row = pl.BlockSpec((1, D_MODEL), lambda i, j: (0, 0))
    in_specs = [
        pl.BlockSpec((tm, D_MODEL), lambda i, j: (i, 0)),
        row,
        pl.BlockSpec((D_MODEL, FF_TILE), lambda i, j: (0, j)),
        pl.BlockSpec((D_MODEL, FF_TILE), lambda i, j: (0, j)),
        pl.BlockSpec((FF_TILE, D_MODEL), lambda i, j: (j, 0)),
    ]
    args = [x, gain, wg, wu, wo]
    if final_gain is not None:
        in_specs.append(row)
        args.append(final_gain)
    vmem = (4 * tm * D_MODEL * 4 + tm * D_MODEL * 2 + 6 * D_MODEL * FF_TILE * 2 + 3 * tm * FF_TILE * 4) // MIB + 4
    return pl.pallas_call(
        functools.partial(_ffn_body, n_ff=n_ff, final_norm=final_gain is not None),
        grid=(n // tm, n_ff),
        in_specs=in_specs,
        out_specs=pl.BlockSpec((tm, D_MODEL), lambda i, j: (i, 0)),
        out_shape=jax.ShapeDtypeStruct((n, D_MODEL), F32),
        scratch_shapes=[pltpu.VMEM((tm, D_MODEL), BF16)],
        compiler_params=_params(("parallel", "arbitrary"), vmem),
        name="ffn",
    )(*args)


def _group_rms(acc, gain_ref, width, scale, put):
    lane = lax.broadcasted_iota(jnp.int32, (1, LANES), 1)
    low = lane < width
    for c in range(acc.shape[1] // LANES):
        cols = slice(c * LANES, (c + 1) * LANES)
        xc = acc[:, cols]
        sq = xc * xc
        if width == LANES:
            r = lax.rsqrt(jnp.mean(sq, axis=-1, keepdims=True) + EPS)
        else:
            s_lo = jnp.sum(jnp.where(low, sq, 0.0), axis=-1, keepdims=True)
            s_hi = jnp.sum(jnp.where(low, 0.0, sq), axis=-1, keepdims=True)
            r = jnp.where(low, lax.rsqrt(s_lo / width + EPS), lax.rsqrt(s_hi / width + EPS))
        y = xc * r * gain_ref[:, cols]
        if scale != 1.0:
            y = y * scale
        put(c, y)


def _put_cols(ref):
    def put(c, y):
        ref[:, c * LANES:(c + 1) * LANES] = y.astype(ref.dtype)
    return put


def _put_heads(ref):
    def put(c, y):
        ref[:, c, :] = y.astype(ref.dtype)
    return put


def _proj_body(x_ref, g_ref, w_ref, wg2_ref, bg_ref, qn_ref, kn_ref, mqn_ref,
               gq_ref, gk_ref, gv_ref, gr_ref, gg_ref, dq_ref, dk_ref, dv_ref, mq_ref, xn_ref):
    xn_ref[...] = _rms(x_ref[...], g_ref[...]).astype(BF16)

    def cols(start, width):
        return _dot(xn_ref[...], w_ref[:, start:start + width])

    gq_ref[...] = cols(_W_GQ, GLA_QK) * (GLA_DK ** -0.5)
    gk_ref[...] = cols(_W_GK, GLA_QK)
    gv_ref[...] = cols(_W_GV, GLA_V).astype(BF16)
    gr_ref[...] = cols(_W_GR, GLA_V)
    glr = cols(_W_GLR, LANES)
    z = _dot(glr.astype(BF16), wg2_ref[...]) + bg_ref[...]
    gg_ref[...] = jax.nn.log_sigmoid(z) / GLA_GATE_NORM
    _group_rms(cols(_W_DQ, DIFF_QK), qn_ref, DIFF_DH, DIFF_DH ** -0.5, _put_cols(dq_ref))
    _group_rms(cols(_W_DK, DIFF_QK), kn_ref, DIFF_DH, 1.0, _put_heads(dk_ref))
    dv = cols(_W_DV, DIFF_V)
    for h in range(HEADS):
        dv_ref[:, h, :] = dv[:, h * LANES:(h + 1) * LANES]
    _group_rms(cols(_W_MQ, MEM_W), mqn_ref, MEM_DH, 1.0, _put_cols(mq_ref))


def _proj(x, gain, w, wg2, bg, qn, kn, mqn):
    n = x.shape[0]
    tm = min(PROJ_TILE, n)
    const = lambda shape: pl.BlockSpec(shape, lambda i: (0, 0))
    out = lambda width: pl.BlockSpec((tm, width), lambda i: (i, 0))
    heads = pl.BlockSpec((tm, HEADS, LANES), lambda i: (i, 0, 0))
    shp = lambda width, dt: jax.ShapeDtypeStruct((n, width), dt)
    shp_heads = jax.ShapeDtypeStruct((n, HEADS, LANES), F32)
    return pl.pallas_call(
        _proj_body,
        grid=(n // tm,),
        in_specs=[
            pl.BlockSpec((tm, D_MODEL), lambda i: (i, 0)),
            const((1, D_MODEL)),
            pl.BlockSpec((D_MODEL, PROJ_COLS), lambda i: (0, 0), pipeline_mode=pl.Buffered(1)),
            const((LANES, GLA_QK)),
            const((1, GLA_QK)),
            const((1, DIFF_QK)),
            const((1, DIFF_QK)),
            const((1, MEM_W)),
        ],
        out_specs=[out(GLA_QK), out(GLA_QK), out(GLA_V), out(GLA_V), out(GLA_QK),
                   out(DIFF_QK), heads, heads, out(MEM_W)],
        out_shape=[shp(GLA_QK, F32), shp(GLA_QK, F32), shp(GLA_V, BF16), shp(GLA_V, F32), shp(GLA_QK, F32),
                   shp(DIFF_QK, BF16), shp_heads, shp_heads, shp(MEM_W, BF16)],
        scratch_shapes=[pltpu.VMEM((tm, D_MODEL), BF16)],
        compiler_params=_params(("parallel",)),
        name="proj",
    )(x, gain, w, wg2, bg, qn, kn, mqn)


def _memkv_body(x_ref, g_ref, w_ref, kn_ref, k_ref, v_ref):
    xn = _rms(x_ref[...], g_ref[...]).astype(BF16)
    _group_rms(_dot(xn, w_ref[:, :MEM_W]), kn_ref, MEM_DH, 1.0, _put_heads(k_ref))
    v = _dot(xn, w_ref[:, MEM_W:])
    for h in range(HEADS):
        v_ref[:, h, :] = v[:, h * LANES:(h + 1) * LANES]


def _memkv(mem, gain, w, kn):
    n = mem.shape[0]
    tm = min(TOK_TILE, n)
    const = lambda shape: pl.BlockSpec(shape, lambda i: (0, 0))
    out = pl.BlockSpec((tm, HEADS, LANES), lambda i: (i, 0, 0))
    return pl.pallas_call(
        _memkv_body,
        grid=(n // tm,),
        in_specs=[pl.BlockSpec((tm, D_MODEL), lambda i: (i, 0)), const((1, D_MODEL)),
                  const((D_MODEL, 2 * MEM_W)), const((1, MEM_W))],
        out_specs=[out, out],
        out_shape=[jax.ShapeDtypeStruct((n, HEADS, LANES), F32)] * 2,
        compiler_params=_params(("parallel",)),
        name="memkv",
    )(mem, gain, w, kn)


def _split3(x):
    hi = x.astype(BF16)
    r1 = x - hi.astype(F32)
    mid = r1.astype(BF16)
    lo = (r1 - mid.astype(F32)).astype(BF16)
    return hi, mid, lo


def _gla_body(*refs, chunk, n_chunks, n_steps, has_state):
    if has_state:
        gq_ref, gk_ref, gg_ref, gv_ref, gr_ref, on_ref, s0_ref, go_ref, st_ref, state = refs
    else:
        gq_ref, gk_ref, gg_ref, gv_ref, gr_ref, on_ref, go_ref, st_ref, state = refs
    t = pl.program_id(1)

    @pl.when(t == 0)
    def _():
        for h in range(GLA_HEADS):
            if has_state:
                state[h] = s0_ref[0, h].T
            else:
                state[h] = jnp.zeros((GLA_DV, GLA_DK), F32)

    row = lax.broadcasted_iota(jnp.int32, (chunk, chunk), 0)
    col = lax.broadcasted_iota(jnp.int32, (chunk, chunk), 1)
    causal = row >= col
    tril = causal.astype(BF16)

    for c in range(n_chunks):
        rows = slice(c * chunk, (c + 1) * chunk)
        g_hi, g_mid, g_lo = _split3(gg_ref[rows, :])
        b = _dot(tril, g_hi) + _dot(tril, g_mid) + _dot(tril, g_lo)
        b_last = b[chunk - 1:chunk, :]
        q = gq_ref[rows, :]
        k = gk_ref[rows, :]
        qe = (q * jnp.exp(b)).astype(BF16)
        ke = (k * jnp.exp(-b)).astype(BF16)
        kd = (k * jnp.exp(b_last - b)).astype(BF16)
        decay = jnp.exp(b_last)
        for h in range(GLA_HEADS):
            kc = slice(h * GLA_DK, (h + 1) * GLA_DK)
            vc = slice(h * GLA_DV, (h + 1) * GLA_DV)
            v = gv_ref[rows, vc]
            a = jnp.where(causal, _dot_nt(qe[:, kc], ke[:, kc]), 0.0).astype(BF16)
            s_t = state[h]
            o = _dot_nt(qe[:, kc], s_t.astype(BF16)) + _dot(a, v)
            state[h] = s_t * decay[:, kc] + _dot_tn(v, kd[:, kc])
            go_ref[rows, vc] = (_rms(o, on_ref[...]) * jax.nn.silu(gr_ref[rows, vc])).astype(BF16)

    @pl.when(t == n_steps - 1)
    def _():
        for h in range(GLA_HEADS):
            st_ref[0, h] = state[h].T


def _gla(gq, gk, gg, gv, gr, onorm, s0, batch, seq, chunk):
    tt = min(TOK_TILE, seq)
    n_steps = seq // tt
    tok = lambda width: pl.BlockSpec((tt, width), lambda b, t: (b * n_steps + t, 0))
    st_spec = pl.BlockSpec((1, GLA_HEADS, GLA_DK, GLA_DV), lambda b, t: (b, 0, 0, 0))
    in_specs = [tok(GLA_QK), tok(GLA_QK), tok(GLA_QK), tok(GLA_V), tok(GLA_V),
                pl.BlockSpec((1, GLA_DV), lambda b, t: (0, 0))]
    args = [gq, gk, gg, gv, gr, onorm]
    if s0 is not None:
        in_specs.append(st_spec)
        args.append(s0)
    return pl.pallas_call(
        functools.partial(_gla_body, chunk=chunk, n_chunks=tt // chunk, n_steps=n_steps, has_state=s0 is not None),
        grid=(batch, n_steps),
        in_specs=in_specs,
        out_specs=[tok(GLA_V), st_spec],
        out_shape=[jax.ShapeDtypeStruct((batch * seq, GLA_V), BF16),
                   jax.ShapeDtypeStruct((batch, GLA_HEADS, GLA_DK, GLA_DV), F32)],
        scratch_shapes=[pltpu.VMEM((GLA_HEADS, GLA_DV, GLA_DK), F32)],
        compiler_params=_params(("parallel", "arbitrary")),
        name="gla",
    )(*args)


def _t5_bucket(rel):
    nb = REL_BUCKETS // 2
    max_exact = nb // 2
    ret = jnp.where(rel > 0, nb, 0)
    n = jnp.abs(rel)
    nf = jnp.maximum(n, 1).astype(F32)
    large = max_exact + (jnp.log(nf / max_exact) / math.log(REL_MAX_DIST / max_exact)
                         * (nb - max_exact)).astype(jnp.int32)
    large = jnp.minimum(large, nb - 1)
    return ret + jnp.where(n < max_exact, n, large)


def _bucket_tile(q_pos, k_pos):
    visible = (k_pos[None, :] // CHUNK) <= (q_pos[:, None] // CHUNK)
    return jnp.where(visible, _t5_bucket(k_pos[None, :] - q_pos[:, None]), MASKED_BUCKET).astype(jnp.int32)


def _bias_from_buckets(idx, tab_ref, head):
    def step(bk, acc):
        return jnp.where(idx == bk, tab_ref[bk * DIFF_HEADS + head], acc)
    return lax.fori_loop(0, REL_BUCKETS, step, jnp.full(idx.shape, NEG_INF, F32))


def _lambda(lam_ref, lam_init):
    l = lam_ref[...]
    return (jnp.exp(jnp.sum(l[0:1] * l[1:2], axis=-1, keepdims=True))
            - jnp.exp(jnp.sum(l[2:3] * l[3:4], axis=-1, keepdims=True)) + lam_init)


def _comp_masks(q):
    lane = lax.broadcasted_iota(jnp.int32, q.shape, 1)
    zero = jnp.zeros_like(q)
    return jnp.where(lane < DIFF_DH, q, zero), jnp.where(lane < DIFF_DH, zero, q)


def _softmax_pv(q_c, parts):
    scores = [_dot_nt(q_c, k) + bias for k, _, bias in parts]
    m = scores[0].max(axis=-1, keepdims=True)
    for s in scores[1:]:
        m = jnp.maximum(m, s.max(axis=-1, keepdims=True))
    l = 0.0
    o = 0.0
    for s, (_, v, _) in zip(scores, parts):
        e = jnp.exp(s - m)
        l = l + e.sum(axis=-1, keepdims=True)
        o = o + _dot(e.astype(BF16), v)
    return o * (1.0 / l)


def _diff_finish(o0, o1, lam, on_ref, lam_init, out_dtype):
    o = o0 - lam * o1
    return (_rms(o, on_ref[...]) * (1.0 - lam_init)).astype(out_dtype)


def _diff_prompt_body(tab_ref, q_ref, k_ref, v_ref, idx_ref, lam_ref, on_ref, o_ref, kb, vb, bias, *, seq, lam_init):
    b = pl.program_id(0)
    h = pl.program_id(1)

    @pl.when(b == 0)
    def _():
        for t in range(2):
            bias[h, t] = _bias_from_buckets(idx_ref[t], tab_ref, h)

    kb[...] = k_ref[:, h, :].astype(BF16)
    vb[...] = v_ref[:, h, :].astype(BF16)
    lam = _lambda(lam_ref, lam_init)
    far_bias = tab_ref[(REL_BUCKETS // 2 - 1) * DIFF_HEADS + h]

    for i in range(seq // Q_BLOCK):
        near0 = max(i - 1, 0) * Q_BLOCK
        near = slice(near0, near0 + 2 * Q_BLOCK)
        near_bias = bias[h, min(i, 1)]
        outs = []
        for q_c in _comp_masks(q_ref[i * Q_BLOCK:(i + 1) * Q_BLOCK, :]):
            parts = [(kb[near, :], vb[near, :], near_bias)]
            if near0 > 0:
                parts.append((kb[0:near0, :], vb[0:near0, :], far_bias))
            outs.append(_softmax_pv(q_c, parts))
        o_ref[i * Q_BLOCK:(i + 1) * Q_BLOCK, :] = _diff_finish(outs[0], outs[1], lam, on_ref, lam_init, o_ref.dtype)


def _diff_prompt(table, dq, dk, dv, lam_p, onorm, batch, seq, lam_init):
    q_pos = jnp.arange(Q_BLOCK)
    idx = jnp.stack([_bucket_tile(q_pos, jnp.arange(2 * Q_BLOCK)),
                     _bucket_tile(q_pos + Q_BLOCK, jnp.arange(2 * Q_BLOCK))])
    head = lambda: pl.BlockSpec((seq, LANES), lambda b, h: (b, h))
    all_heads = lambda: pl.BlockSpec((seq, HEADS, LANES), lambda b, h: (b, 0, 0))
    return pl.pallas_call(
        functools.partial(_diff_prompt_body, seq=seq, lam_init=lam_init),
        grid=(batch, DIFF_HEADS),
        in_specs=[
            pl.BlockSpec(memory_space=pltpu.SMEM),
            head(), all_heads(), all_heads(),
            pl.BlockSpec((2, Q_BLOCK, 2 * Q_BLOCK), lambda b, h: (0, 0, 0)),
            pl.BlockSpec((4, DIFF_DH), lambda b, h: (0, 0)),
            pl.BlockSpec((1, DIFF_DV), lambda b, h: (0, 0)),
        ],
        out_specs=head(),
        out_shape=jax.ShapeDtypeStruct((batch * seq, DIFF_V), BF16),
        scratch_shapes=[pltpu.VMEM((seq, LANES), BF16), pltpu.VMEM((seq, LANES), BF16),
                        pltpu.VMEM((DIFF_HEADS, 2, Q_BLOCK, 2 * Q_BLOCK), F32)],
        compiler_params=_params(("arbitrary", "arbitrary")),
        name="diff_prompt",
    )(table, dq, dk, dv, idx, lam_p, onorm)


def _diff_sample_body(tab_ref, q_ref, k_ref, v_ref, ck_ref, cv_ref, idxc_ref, idxn_ref, lam_ref, on_ref, o_ref,
                      bias_c, bias_n, *, lam_init):
    @pl.when(pl.program_id(0) == 0)
    def _():
        for h in range(DIFF_HEADS):
            bias_c[h] = _bias_from_buckets(idxc_ref[...], tab_ref, h)
            bias_n[h] = _bias_from_buckets(idxn_ref[...], tab_ref, h)

    lam = _lambda(lam_ref, lam_init)
    for h in range(DIFF_HEADS):
        cols = slice(h * LANES, (h + 1) * LANES)
        parts = [(ck_ref[:, h, :].astype(BF16), cv_ref[:, h, :].astype(BF16), bias_c[h]),
                 (k_ref[:, h, :].astype(BF16), v_ref[:, h, :].astype(BF16), bias_n[h])]
        outs = [_softmax_pv(q_c, parts) for q_c in _comp_masks(q_ref[:, cols])]
        o_ref[:, cols] = _diff_finish(outs[0], outs[1], lam, on_ref, lam_init, o_ref.dtype)


def _diff_sample(table, dq, dk, dv, cache_k, cache_v, lam_p, onorm, batch, seq, past, lam_init):
    q_pos = past + jnp.arange(seq)
    idx_c = _bucket_tile(q_pos, jnp.arange(past))
    idx_n = _bucket_tile(q_pos, past + jnp.arange(seq))
    new_q = pl.BlockSpec((seq, DIFF_QK), lambda b: (b, 0))
    new = lambda: pl.BlockSpec((seq, HEADS, LANES), lambda b: (b, 0, 0))
    old = lambda: pl.BlockSpec((past, HEADS, LANES), lambda b: (b, 0, 0))
    const = lambda shape: pl.BlockSpec(shape, lambda b: (0, 0))
    return pl.pallas_call(
        functools.partial(_diff_sample_body, lam_init=lam_init),
        grid=(batch,),
        in_specs=[pl.BlockSpec(memory_space=pltpu.SMEM), new_q, new(), new(), old(), old(),
                  const((seq, past)), const((seq, seq)), const((4, DIFF_DH)), const((1, DIFF_DV))],
        out_specs=new_q,
        out_shape=jax.ShapeDtypeStruct((batch * seq, DIFF_V), BF16),
        scratch_shapes=[pltpu.VMEM((DIFF_HEADS, seq, past), F32), pltpu.VMEM((DIFF_HEADS, seq, seq), F32)],
        compiler_params=_params(("arbitrary",)),
        name="diff_sample",
    )(table, dq, dk, dv, cache_k, cache_v, idx_c, idx_n, lam_p, onorm)


def _memattn_body(q_ref, k_ref, v_ref, o_ref):
    for h in range(MEM_HEADS):
        cols = slice(h * MEM_DH, (h + 1) * MEM_DH)
        s = _dot_nt(q_ref[:, cols], k_ref[:, h, :].astype(BF16)) * (MEM_DH ** -0.5)
        e = jnp.exp(s - s.max(axis=-1, keepdims=True))
        p = e * (1.0 / e.sum(axis=-1, keepdims=True))
        o_ref[:, cols] = _dot(p.astype(BF16), v_ref[:, h, :].astype(BF16)).astype(o_ref.dtype)


def _memattn(mq, mk, mv, batch, seq, n_mem):
    tq = min(TOK_TILE, seq)
    nq = seq // tq
    q_spec = pl.BlockSpec((tq, MEM_W), lambda b, i: (b * nq + i, 0))
    kv_spec = pl.BlockSpec((n_mem, HEADS, LANES), lambda b, i: (b, 0, 0))
    return pl.pallas_call(
        _memattn_body,
        grid=(batch, nq),
        in_specs=[q_spec, kv_spec, kv_spec],
        out_specs=q_spec,
        out_shape=jax.ShapeDtypeStruct((batch * seq, MEM_W), BF16),
        compiler_params=_params(("parallel", "arbitrary")),
        name="memattn",
    )(mq, mk, mv)


def _outproj_body(x_ref, g_ref, d_ref, m_ref, wg_ref, wd_ref, wm_ref, o_ref):
    o_ref[...] = (x_ref[...] + _dot(g_ref[...], wg_ref[...]) + _dot(d_ref[...], wd_ref[...])
                  + _dot(m_ref[...], wm_ref[...]))


def _outproj(x, g, d, m, wo):
    n = x.shape[0]
    tm = min(TOK_TILE, n)
    tok = lambda width: pl.BlockSpec((tm, width), lambda i: (i, 0))
    w_rows = lambda rows, blk: pl.BlockSpec((rows, D_MODEL), lambda i: (blk, 0))
    return pl.pallas_call(
        _outproj_body,
        grid=(n // tm,),
        in_specs=[tok(D_MODEL), tok(GLA_V), tok(DIFF_V), tok(MEM_W),
                  w_rows(GLA_V, 0), w_rows(DIFF_V, GLA_V // DIFF_V), w_rows(MEM_W, (GLA_V + DIFF_V) // MEM_W)],
        out_specs=tok(D_MODEL),
        out_shape=jax.ShapeDtypeStruct((n, D_MODEL), F32),
        compiler_params=_params(("parallel",)),
        name="outproj",
    )(x, g, d, m, wo, wo, wo)


def _row(v):
    return v.reshape(1, -1).astype(F32)


def _prep_ffn(w_in, w_out):
    pad = D_FF_PAD - D_FF
    wg = jnp.pad(w_in[:, :D_FF].astype(BF16), ((0, 0), (0, pad)))
    wu = jnp.pad(w_in[:, D_FF:].astype(BF16), ((0, 0), (0, pad)))
    wo = jnp.pad(w_out.astype(BF16), ((0, pad), (0, 0)))
    return wg, wu, wo


def kernel(x_prompt, x_sample, mem_prompt, cache_diff_k, cache_diff_v, state_gla, cache_mem_k, cache_mem_v, rel_bias_table, norm_ffn1, w_ffn1_in, w_ffn1_out, norm_mix, w_in, w_gla_g2, b_gla_g, gla_out_norm, diff_q_norm, diff_k_norm, diff_lambda, diff_out_norm, mem_norm, w_mem_kv, mem_q_norm, mem_k_norm, w_o, norm_ffn2, w_ffn2_in, w_ffn2_out, norm_final):
    depth = norm_ffn1.shape[0]
    assert depth == 1, "single-layer step"
    layer = 0
    batch, seq, _ = x_prompt.shape
    dec_batch, dec_seq, _ = x_sample.shape
    past = cache_diff_k.shape[2]
    n_mem = mem_prompt.shape[1]
    lam_init = 0.8 - 0.6 * math.exp(-0.3 * layer)

    ffn1 = _prep_ffn(w_ffn1_in[layer], w_ffn1_out[layer])
    ffn2 = _prep_ffn(w_ffn2_in[layer], w_ffn2_out[layer])
    w = w_in[layer].astype(BF16)
    glr_end = GLR_OFF + GLA_GATE_RANK
    w_proj = jnp.concatenate([w[:, :GLR_OFF], w[:, glr_end:], w[:, GLR_OFF:glr_end],
                              jnp.zeros((D_MODEL, LANES - GLA_GATE_RANK), BF16)], axis=1)
    w_g2 = jnp.pad(w_gla_g2[layer].astype(BF16), ((0, LANES - GLA_GATE_RANK), (0, 0)))
    b_g = _row(b_gla_g[layer])
    qn = _row(jnp.tile(diff_q_norm[layer], DIFF_QK // DIFF_DH))
    kn = _row(jnp.tile(diff_k_norm[layer], DIFF_QK // DIFF_DH))
    mqn = _row(jnp.tile(mem_q_norm[layer], MEM_HEADS))
    mkn = _row(jnp.tile(mem_k_norm[layer], MEM_HEADS))
    wo = w_o[layer].astype(BF16)
    table = rel_bias_table.astype(F32).reshape(-1)
    lam_p = diff_lambda[layer].astype(F32)
    gla_on = _row(gla_out_norm[layer])
    diff_on = _row(diff_out_norm[layer])
    per_head = lambda a: a.reshape(-1, HEADS, LANES)

    mk, mv = _memkv(mem_prompt.reshape(batch * n_mem, D_MODEL), _row(mem_norm[layer]),
                    w_mem_kv[layer].astype(BF16), mkn)

    def layer_fn(x, b, t, chunk, s0, diff_fn, mem_k, mem_v):
        x = _ffn(x, _row(norm_ffn1[layer]), *ffn1)
        gq, gk, gv, gr, gg, dq, dk, dv, mq = _proj(x, _row(norm_mix[layer]), w_proj, w_g2, b_g, qn, kn, mqn)
        g_out, g_state = _gla(gq, gk, gg, gv, gr, gla_on, s0, b, t, chunk)
        d_out = diff_fn(dq, dk, dv)
        m_out = _memattn(mq, mem_k, mem_v, b, t, n_mem)
        x = _outproj(x, g_out, d_out, m_out, wo)
        x = _ffn(x, _row(norm_ffn2[layer]), *ffn2, final_gain=_row(norm_final[layer]))
        return x, dk, dv, g_state

    yp, dk_p, dv_p, g_p = layer_fn(
        x_prompt.reshape(batch * seq, D_MODEL), batch, seq, CHUNK, None,
        lambda dq, dk, dv: _diff_prompt(table, dq, dk, dv, lam_p, diff_on, batch, seq, lam_init), mk, mv)
    ys, dk_s, dv_s, g_s = layer_fn(
        x_sample.reshape(dec_batch * dec_seq, D_MODEL), dec_batch, dec_seq, dec_seq, state_gla[layer],
        lambda dq, dk, dv: _diff_sample(table, dq, dk, dv, per_head(cache_diff_k[layer]), per_head(cache_diff_v[layer]),
                                        lam_p, diff_on, dec_batch, dec_seq, past, lam_init),
        per_head(cache_mem_k[layer]), per_head(cache_mem_v[layer]))

    head4 = lambda a, b, t: a.reshape(1, b, t, HEADS, LANES)
    return (yp.reshape(batch, seq, D_MODEL), ys.reshape(dec_batch, dec_seq, D_MODEL),
            head4(dk_p, batch, seq), head4(dv_p, batch, seq), g_p[None],
            head4(mk, batch, n_mem), head4(mv, batch, n_mem),
            head4(dk_s, dec_batch, dec_seq), head4(dv_s, dec_batch, dec_seq), g_s[None])
```

```python
import functools
import math

import jax
import jax.numpy as jnp
from jax import lax
from jax.experimental import pallas as pl
from jax.experimental.pallas import tpu as pltpu

F32 = jnp.float32
BF16 = jnp.bfloat16

D_MODEL = 2048
CHUNK = 64
EPS = 1e-6
NEG_INF = -1e30
GLA_HEADS, GLA_DK, GLA_DV, GLA_GATE_RANK, GLA_GATE_NORM = 4, 128, 256, 16, 16.0
DIFF_HEADS, DIFF_DH, DIFF_DV = 4, 64, 128
MEM_HEADS, MEM_DH = 4, 128
REL_BUCKETS, REL_MAX_DIST = 32, 128
D_FF = 5504
GLA_QK = GLA_HEADS * GLA_DK
GLA_V = GLA_HEADS * GLA_DV
DIFF_QK = DIFF_HEADS * 2 * DIFF_DH
DIFF_V = DIFF_HEADS * DIFF_DV
MEM_W = MEM_HEADS * MEM_DH
GLR_OFF = 2 * GLA_QK + 2 * GLA_V

LANES = 128
HEADS = 4
FF_TILE = 512
N_FF_TILES = -(-D_FF // FF_TILE)
FF_LAST_START = D_FF - FF_TILE
FF_OVERLAP = N_FF_TILES * FF_TILE - D_FF
FFN_TOK_TILE = 1024
TOK_TILE = 512
PROJ_TILE = 256
Q_BLOCK = 256
MASKED_BUCKET = REL_BUCKETS
MIB = 1024 * 1024

_A_GQ, _A_GK, _A_GV, _A_GR, _A_GLR = 0, GLA_QK, 2 * GLA_QK, 2 * GLA_QK + GLA_V, GLR_OFF
PROJ_A_COLS = GLR_OFF + LANES
PROJ_B_START = GLR_OFF + GLA_GATE_RANK
_B_DQ, _B_DK, _B_DV, _B_MQ = 0, DIFF_QK, 2 * DIFF_QK, 2 * DIFF_QK + DIFF_V
PROJ_B_COLS = _B_MQ + MEM_W


def _dot(a, b):
    return jnp.dot(a, b, preferred_element_type=F32)


def _dot_nt(a, b):
    return lax.dot_general(a, b, (((1,), (1,)), ((), ())), preferred_element_type=F32)


def _dot_tn(a, b):
    return lax.dot_general(a, b, (((0,), (0,)), ((), ())), preferred_element_type=F32)


def _rms(x, gain):
    return x * lax.rsqrt(jnp.mean(x * x, axis=-1, keepdims=True) + EPS) * gain


def _params(sem, vmem_mib=48):
    return pltpu.CompilerParams(dimension_semantics=sem, vmem_limit_bytes=vmem_mib * MIB)


def _ffn_body(*refs, n_ff, final_norm):
    if final_norm:
        x_ref, g_ref, wg_ref, wu_ref, wo_ref, fg_ref, o_ref, xn_ref = refs
    else:
        x_ref, g_ref, wg_ref, wu_ref, wo_ref, o_ref, xn_ref = refs
    j = pl.program_id(1)

    @pl.when(j == 0)
    def _():
        x = x_ref[...]
        xn_ref[...] = _rms(x, g_ref[...]).astype(BF16)
        o_ref[...] = x

    xn = xn_ref[...]
    gate = _dot(xn, wg_ref[...])
    up = _dot(xn, wu_ref[...])
    col = lax.broadcasted_iota(jnp.int32, (1, FF_TILE), 1)
    seen = (j == n_ff - 1) & (col < FF_OVERLAP)
    act = jnp.where(seen, 0.0, jax.nn.silu(gate) * up).astype(BF16)
    o_ref[...] += 0.5 * _dot(act, wo_ref[...])

    if final_norm:
        @pl.when(j == n_ff - 1)
        def _():
            o_ref[...] = _rms(o_ref[...], fg_ref[...])


def _ffn(x, gain, w_in, w_out, final_gain=None):
    n = x.shape[0]
    tm = min(FFN_TOK_TILE, n)
    n_ff = N_FF_TILES
    row = pl.BlockSpec((1, D_MODEL), lambda i, j: (0, 0))
    start = lambda j: jnp.minimum(j * (FF_TILE // LANES), FF_LAST_START // LANES)
    w_in_tile = (pl.Element(D_MODEL), pl.Element(FF_TILE))
    in_specs = [
        pl.BlockSpec((tm, D_MODEL), lambda i, j: (i, 0)),
        row,
        pl.BlockSpec(w_in_tile, lambda i, j: (0, start(j) * LANES)),
        pl.BlockSpec(w_in_tile, lambda i, j: (0, (D_FF // LANES + start(j)) * LANES)),
        pl.BlockSpec((pl.Element(FF_TILE), pl.Element(D_MODEL)), lambda i, j: (start(j) * LANES, 0)),
    ]
    args = [x, gain, w_in, w_in, w_out]
    if final_gain is not None:
        in_specs.append(row)
        args.append(final_gain)
    vmem = (4 * tm * D_MODEL * 4 + tm * D_MODEL * 2 + 6 * D_MODEL * FF_TILE * 2 + 3 * tm * FF_TILE * 4) // MIB + 4
    return pl.pallas_call(
        functools.partial(_ffn_body, n_ff=n_ff, final_norm=final_gain is not None),
        grid=(n // tm, n_ff),
        in_specs=in_specs,
        out_specs=pl.BlockSpec((tm, D_MODEL), lambda i, j: (i, 0)),
        out_shape=jax.ShapeDtypeStruct((n, D_MODEL), F32),
        scratch_shapes=[pltpu.VMEM((tm, D_MODEL), BF16)],
        compiler_params=_params(("parallel", "arbitrary"), vmem),
        name="ffn",
    )(*args)


def _group_rms(acc, gain_ref, width, scale, put):
    lane = lax.broadcasted_iota(jnp.int32, (1, LANES), 1)
    low = lane < width
    for c in range(acc.shape[1] // LANES):
        cols = slice(c * LANES, (c + 1) * LANES)
        xc = acc[:, cols]
        sq = xc * xc
        if width == LANES:
            r = lax.rsqrt(jnp.mean(sq, axis=-1, keepdims=True) + EPS)
        else:
            s_lo = jnp.sum(jnp.where(low, sq, 0.0), axis=-1, keepdims=True)
            s_hi = jnp.sum(jnp.where(low, 0.0, sq), axis=-1, keepdims=True)
            r = jnp.where(low, lax.rsqrt(s_lo / width + EPS), lax.rsqrt(s_hi / width + EPS))
        y = xc * r * gain_ref[:, cols]
        if scale != 1.0:
            y = y * scale
        put(c, y)


def _put_cols(ref):
    def put(c, y):
        ref[:, c * LANES:(c + 1) * LANES] = y.astype(ref.dtype)
    return put


def _put_heads_and_cols(heads_ref, cols_ref):
    def put(c, y):
        heads_ref[:, c, :] = y
        cols_ref[:, c * LANES:(c + 1) * LANES] = y.astype(cols_ref.dtype)
    return put


def _proj_body(x_ref, g_ref, wa_ref, wb_ref, wg2_ref, bg_ref, qn_ref, kn_ref, mqn_ref,
               gq_ref, gk_ref, gv_ref, gr_ref, gg_ref, dq_ref, dk_ref, dv_ref, mq_ref, dkb_ref, dvb_ref, xn_ref):
    xn_ref[...] = _rms(x_ref[...], g_ref[...]).astype(BF16)

    def cols(w_ref, start, width):
        return _dot(xn_ref[...], w_ref[:, start:start + width])

    gq_ref[...] = cols(wa_ref, _A_GQ, GLA_QK) * (GLA_DK ** -0.5)
    gk_ref[...] = cols(wa_ref, _A_GK, GLA_QK)
    gv_ref[...] = cols(wa_ref, _A_GV, GLA_V).astype(BF16)
    gr_ref[...] = cols(wa_ref, _A_GR, GLA_V)
    glr = cols(wa_ref, _A_GLR, LANES)
    z = _dot(glr.astype(BF16), wg2_ref[...]) + bg_ref[...]
    gg_ref[...] = jax.nn.log_sigmoid(z) / GLA_GATE_NORM
    _group_rms(cols(wb_ref, _B_DQ, DIFF_QK), qn_ref, DIFF_DH, DIFF_DH ** -0.5, _put_cols(dq_ref))
    _group_rms(cols(wb_ref, _B_DK, DIFF_QK), kn_ref, DIFF_DH, 1.0, _put_heads_and_cols(dk_ref, dkb_ref))
    dv = cols(wb_ref, _B_DV, DIFF_V)
    put_v = _put_heads_and_cols(dv_ref, dvb_ref)
    for h in range(HEADS):
        put_v(h, dv[:, h * LANES:(h + 1) * LANES])
    _group_rms(cols(wb_ref, _B_MQ, MEM_W), mqn_ref, MEM_DH, 1.0, _put_cols(mq_ref))


def _proj(x, gain, wa, wb, wg2, bg, qn, kn, mqn):
    n = x.shape[0]
    tm = min(PROJ_TILE, n)
    const = lambda shape: pl.BlockSpec(shape, lambda i: (0, 0))
    out = lambda width: pl.BlockSpec((tm, width), lambda i: (i, 0))
    heads = pl.BlockSpec((tm, HEADS, LANES), lambda i: (i, 0, 0))
    shp = lambda width, dt: jax.ShapeDtypeStruct((n, width), dt)
    shp_heads = jax.ShapeDtypeStruct((n, HEADS, LANES), F32)
    return pl.pallas_call(
        _proj_body,
        grid=(n // tm,),
        in_specs=[
            pl.BlockSpec((tm, D_MODEL), lambda i: (i, 0)),
            const((1, D_MODEL)),
            pl.BlockSpec((D_MODEL, PROJ_A_COLS), lambda i: (0, 0), pipeline_mode=pl.Buffered(1)),
            pl.BlockSpec((D_MODEL, PROJ_B_COLS), lambda i: (0, 0), pipeline_mode=pl.Buffered(1)),
            const((LANES, GLA_QK)),
            const((1, GLA_QK)),
            const((1, DIFF_QK)),
            const((1, DIFF_QK)),
            const((1, MEM_W)),
        ],
        out_specs=[out(GLA_QK), out(GLA_QK), out(GLA_V), out(GLA_V), out(GLA_QK),
                   out(DIFF_QK), heads, heads, out(MEM_W), out(DIFF_QK), out(DIFF_V)],
        out_shape=[shp(GLA_QK, F32), shp(GLA_QK, F32), shp(GLA_V, BF16), shp(GLA_V, F32), shp(GLA_QK, F32),
                   shp(DIFF_QK, BF16), shp_heads, shp_heads, shp(MEM_W, BF16), shp(DIFF_QK, BF16), shp(DIFF_V, BF16)],
        scratch_shapes=[pltpu.VMEM((tm, D_MODEL), BF16)],
        compiler_params=_params(("parallel",)),
        name="proj",
    )(x, gain, wa, wb, wg2, bg, qn, kn, mqn)


def _memkv_body(x_ref, g_ref, w_ref, kn_ref, k_ref, v_ref, kb_ref, vb_ref):
    xn = _rms(x_ref[...], g_ref[...]).astype(BF16)
    _group_rms(_dot(xn, w_ref[:, :MEM_W]), kn_ref, MEM_DH, 1.0, _put_heads_and_cols(k_ref, kb_ref))
    v = _dot(xn, w_ref[:, MEM_W:])
    put_v = _put_heads_and_cols(v_ref, vb_ref)
    for h in range(HEADS):
        put_v(h, v[:, h * LANES:(h + 1) * LANES])


def _memkv(mem, gain, w, kn):
    n = mem.shape[0]
    tm = min(TOK_TILE, n)
    const = lambda shape: pl.BlockSpec(shape, lambda i: (0, 0))
    heads = pl.BlockSpec((tm, HEADS, LANES), lambda i: (i, 0, 0))
    dense = pl.BlockSpec((tm, MEM_W), lambda i: (i, 0))
    return pl.pallas_call(
        _memkv_body,
        grid=(n // tm,),
        in_specs=[pl.BlockSpec((tm, D_MODEL), lambda i: (i, 0)), const((1, D_MODEL)),
                  const((D_MODEL, 2 * MEM_W)), const((1, MEM_W))],
        out_specs=[heads, heads, dense, dense],
        out_shape=[jax.ShapeDtypeStruct((n, HEADS, LANES), F32)] * 2 + [jax.ShapeDtypeStruct((n, MEM_W), BF16)] * 2,
        compiler_params=_params(("parallel",)),
        name="memkv",
    )(mem, gain, w, kn)


def _split3(x):
    hi = x.astype(BF16)
    r1 = x - hi.astype(F32)
    mid = r1.astype(BF16)
    lo = (r1 - mid.astype(F32)).astype(BF16)
    return hi, mid, lo


def _gla_body(*refs, chunk, n_chunks, n_steps, has_state):
    if has_state:
        gq_ref, gk_ref, gg_ref, gv_ref, gr_ref, on_ref, s0_ref, go_ref, st_ref, state = refs
    else:
        gq_ref, gk_ref, gg_ref, gv_ref, gr_ref, on_ref, go_ref, st_ref, state = refs
    t = pl.program_id(1)

    @pl.when(t == 0)
    def _():
        for h in range(GLA_HEADS):
            if has_state:
                state[h] = s0_ref[0, h].T
            else:
                state[h] = jnp.zeros((GLA_DV, GLA_DK), F32)

    row = lax.broadcasted_iota(jnp.int32, (chunk, chunk), 0)
    col = lax.broadcasted_iota(jnp.int32, (chunk, chunk), 1)
    causal = row >= col
    tril = causal.astype(BF16)

    for c in range(n_chunks):
        rows = slice(c * chunk, (c + 1) * chunk)
        g_hi, g_mid, g_lo = _split3(gg_ref[rows, :])
        b = _dot(tril, g_hi) + _dot(tril, g_mid) + _dot(tril, g_lo)
        b_last = b[chunk - 1:chunk, :]
        q = gq_ref[rows, :]
        k = gk_ref[rows, :]
        qe = (q * jnp.exp(b)).astype(BF16)
        ke = (k * jnp.exp(-b)).astype(BF16)
        kd = (k * jnp.exp(b_last - b)).astype(BF16)
        decay = jnp.exp(b_last)
        for h in range(GLA_HEADS):
            kc = slice(h * GLA_DK, (h + 1) * GLA_DK)
            vc = slice(h * GLA_DV, (h + 1) * GLA_DV)
            v = gv_ref[rows, vc]
            a = jnp.where(causal, _dot_nt(qe[:, kc], ke[:, kc]), 0.0).astype(BF16)
            s_t = state[h]
            o = _dot_nt(qe[:, kc], s_t.astype(BF16)) + _dot(a, v)
            state[h] = s_t * decay[:, kc] + _dot_tn(v, kd[:, kc])
            go_ref[rows, vc] = (_rms(o, on_ref[...]) * jax.nn.silu(gr_ref[rows, vc])).astype(BF16)

    @pl.when(t == n_steps - 1)
    def _():
        for h in range(GLA_HEADS):
            st_ref[0, h] = state[h].T


def _gla(gq, gk, gg, gv, gr, onorm, s0, batch, seq, chunk):
    tt = min(TOK_TILE, seq)
    n_steps = seq // tt
    tok = lambda width: pl.BlockSpec((tt, width), lambda b, t: (b * n_steps + t, 0))
    st_spec = pl.BlockSpec((1, GLA_HEADS, GLA_DK, GLA_DV), lambda b, t: (b, 0, 0, 0))
    in_specs = [tok(GLA_QK), tok(GLA_QK), tok(GLA_QK), tok(GLA_V), tok(GLA_V),
                pl.BlockSpec((1, GLA_DV), lambda b, t: (0, 0))]
    args = [gq, gk, gg, gv, gr, onorm]
    if s0 is not None:
        in_specs.append(st_spec)
        args.append(s0)
    return pl.pallas_call(
        functools.partial(_gla_body, chunk=chunk, n_chunks=tt // chunk, n_steps=n_steps, has_state=s0 is not None),
        grid=(batch, n_steps),
        in_specs=in_specs,
        out_specs=[tok(GLA_V), st_spec],
        out_shape=[jax.ShapeDtypeStruct((batch * seq, GLA_V), BF16),
                   jax.ShapeDtypeStruct((batch, GLA_HEADS, GLA_DK, GLA_DV), F32)],
        scratch_shapes=[pltpu.VMEM((GLA_HEADS, GLA_DV, GLA_DK), F32)],
        compiler_params=_params(("parallel", "arbitrary")),
        name="gla",
    )(*args)


def _t5_bucket(rel):
    nb = REL_BUCKETS // 2
    max_exact = nb // 2
    ret = jnp.where(rel > 0, nb, 0)
    n = jnp.abs(rel)
    nf = jnp.maximum(n, 1).astype(F32)
    large = max_exact + (jnp.log(nf / max_exact) / math.log(REL_MAX_DIST / max_exact)
                         * (nb - max_exact)).astype(jnp.int32)
    large = jnp.minimum(large, nb - 1)
    return ret + jnp.where(n < max_exact, n, large)


def _bucket_tile(q_pos, k_pos):
    visible = (k_pos[None, :] // CHUNK) <= (q_pos[:, None] // CHUNK)
    return jnp.where(visible, _t5_bucket(k_pos[None, :] - q_pos[:, None]), MASKED_BUCKET).astype(jnp.int32)


def _bias_from_buckets(idx, tab_ref, head):
    def step(bk, acc):
        return jnp.where(idx == bk, tab_ref[bk * DIFF_HEADS + head], acc)
    return lax.fori_loop(0, REL_BUCKETS, step, jnp.full(idx.shape, NEG_INF, F32))


def _lambda(lam_ref, lam_init):
    l = lam_ref[...]
    return (jnp.exp(jnp.sum(l[0:1] * l[1:2], axis=-1, keepdims=True))
            - jnp.exp(jnp.sum(l[2:3] * l[3:4], axis=-1, keepdims=True)) + lam_init)


def _comp_masks(q):
    lane = lax.broadcasted_iota(jnp.int32, q.shape, 1)
    zero = jnp.zeros_like(q)
    return jnp.where(lane < DIFF_DH, q, zero), jnp.where(lane < DIFF_DH, zero, q)


def _softmax_pv(q_c, parts):
    scores = [_dot_nt(q_c, k) + bias for k, _, bias in parts]
    m = scores[0].max(axis=-1, keepdims=True)
    for s in scores[1:]:
        m = jnp.maximum(m, s.max(axis=-1, keepdims=True))
    l = 0.0
    o = 0.0
    for s, (_, v, _) in zip(scores, parts):
        e = jnp.exp(s - m)
        l = l + e.sum(axis=-1, keepdims=True)
        o = o + _dot(e.astype(BF16), v)
    return o * (1.0 / l)


def _diff_finish(o0, o1, lam, on_ref, lam_init, out_dtype):
    o = o0 - lam * o1
    return (_rms(o, on_ref[...]) * (1.0 - lam_init)).astype(out_dtype)


def _diff_prompt_body(tab_ref, q_ref, kb, vb, idx_ref, lam_ref, on_ref, o_ref, bias, *, seq, lam_init):
    b = pl.program_id(0)
    h = pl.program_id(1)

    @pl.when(b == 0)
    def _():
        for t in range(2):
            bias[h, t] = _bias_from_buckets(idx_ref[t], tab_ref, h)

    lam = _lambda(lam_ref, lam_init)
    far_bias = tab_ref[(REL_BUCKETS // 2 - 1) * DIFF_HEADS + h]

    for i in range(seq // Q_BLOCK):
        near0 = max(i - 1, 0) * Q_BLOCK
        near = slice(near0, near0 + 2 * Q_BLOCK)
        near_bias = bias[h, min(i, 1)]
        outs = []
        for q_c in _comp_masks(q_ref[i * Q_BLOCK:(i + 1) * Q_BLOCK, :]):
            parts = [(kb[near, :], vb[near, :], near_bias)]
            if near0 > 0:
                parts.append((kb[0:near0, :], vb[0:near0, :], far_bias))
            outs.append(_softmax_pv(q_c, parts))
        o_ref[i * Q_BLOCK:(i + 1) * Q_BLOCK, :] = _diff_finish(outs[0], outs[1], lam, on_ref, lam_init, o_ref.dtype)


def _diff_prompt(table, dq, dk, dv, lam_p, onorm, batch, seq, lam_init):
    q_pos = jnp.arange(Q_BLOCK)
    idx = jnp.stack([_bucket_tile(q_pos, jnp.arange(2 * Q_BLOCK)),
                     _bucket_tile(q_pos + Q_BLOCK, jnp.arange(2 * Q_BLOCK))])
    head = lambda: pl.BlockSpec((seq, LANES), lambda b, h: (b, h))
    return pl.pallas_call(
        functools.partial(_diff_prompt_body, seq=seq, lam_init=lam_init),
        grid=(batch, DIFF_HEADS),
        in_specs=[
            pl.BlockSpec(memory_space=pltpu.SMEM),
            head(), head(), head(),
            pl.BlockSpec((2, Q_BLOCK, 2 * Q_BLOCK), lambda b, h: (0, 0, 0)),
            pl.BlockSpec((4, DIFF_DH), lambda b, h: (0, 0)),
            pl.BlockSpec((1, DIFF_DV), lambda b, h: (0, 0)),
        ],
        out_specs=head(),
        out_shape=jax.ShapeDtypeStruct((batch * seq, DIFF_V), BF16),
        scratch_shapes=[pltpu.VMEM((DIFF_HEADS, 2, Q_BLOCK, 2 * Q_BLOCK), F32)],
        compiler_params=_params(("arbitrary", "arbitrary")),
        name="diff_prompt",
    )(table, dq, dk, dv, idx, lam_p, onorm)


def _diff_sample_body(tab_ref, q_ref, k_ref, v_ref, ck_ref, cv_ref, idxc_ref, idxn_ref, lam_ref, on_ref, o_ref,
                      bias_c, bias_n, *, lam_init):
    @pl.when(pl.program_id(0) == 0)
    def _():
        for h in range(DIFF_HEADS):
            bias_c[h] = _bias_from_buckets(idxc_ref[...], tab_ref, h)
            bias_n[h] = _bias_from_buckets(idxn_ref[...], tab_ref, h)

    lam = _lambda(lam_ref, lam_init)
    for h in range(DIFF_HEADS):
        cols = slice(h * LANES, (h + 1) * LANES)
        parts = [(ck_ref[:, h, :].astype(BF16), cv_ref[:, h, :].astype(BF16), bias_c[h]),
                 (k_ref[:, cols], v_ref[:, cols], bias_n[h])]
        outs = [_softmax_pv(q_c, parts) for q_c in _comp_masks(q_ref[:, cols])]
        o_ref[:, cols] = _diff_finish(outs[0], outs[1], lam, on_ref, lam_init, o_ref.dtype)


def _diff_sample(table, dq, dk, dv, cache_k, cache_v, lam_p, onorm, batch, seq, past, lam_init):
    q_pos = past + jnp.arange(seq)
    idx_c = _bucket_tile(q_pos, jnp.arange(past))
    idx_n = _bucket_tile(q_pos, past + jnp.arange(seq))
    new_q = pl.BlockSpec((seq, DIFF_QK), lambda b: (b, 0))
    old = lambda: pl.BlockSpec((past, HEADS, LANES), lambda b: (b, 0, 0))
    const = lambda shape: pl.BlockSpec(shape, lambda b: (0, 0))
    return pl.pallas_call(
        functools.partial(_diff_sample_body, lam_init=lam_init),
        grid=(batch,),
        in_specs=[pl.BlockSpec(memory_space=pltpu.SMEM), new_q, new_q, new_q, old(), old(),
                  const((seq, past)), const((seq, seq)), const((4, DIFF_DH)), const((1, DIFF_DV))],
        out_specs=new_q,
        out_shape=jax.ShapeDtypeStruct((batch * seq, DIFF_V), BF16),
        scratch_shapes=[pltpu.VMEM((DIFF_HEADS, seq, past), F32), pltpu.VMEM((DIFF_HEADS, seq, seq), F32)],
        compiler_params=_params(("arbitrary",)),
        name="diff_sample",
    )(table, dq, dk, dv, cache_k, cache_v, idx_c, idx_n, lam_p, onorm)


def _memattn_body(q_ref, k_ref, v_ref, o_ref, *, per_head):
    for h in range(MEM_HEADS):
        cols = slice(h * MEM_DH, (h + 1) * MEM_DH)
        if per_head:
            k, v = k_ref[:, h, :].astype(BF16), v_ref[:, h, :].astype(BF16)
        else:
            k, v = k_ref[:, cols], v_ref[:, cols]
        s = _dot_nt(q_ref[:, cols], k) * (MEM_DH ** -0.5)
        e = jnp.exp(s - s.max(axis=-1, keepdims=True))
        p = e * (1.0 / e.sum(axis=-1, keepdims=True))
        o_ref[:, cols] = _dot(p.astype(BF16), v).astype(o_ref.dtype)


def _memattn(mq, mk, mv, batch, seq, n_mem):
    tq = min(TOK_TILE, seq)
    nq = seq // tq
    q_spec = pl.BlockSpec((tq, MEM_W), lambda b, i: (b * nq + i, 0))
    per_head = mk.ndim == 3
    if per_head:
        kv_spec = pl.BlockSpec((n_mem, HEADS, LANES), lambda b, i: (b, 0, 0))
    else:
        kv_spec = pl.BlockSpec((n_mem, MEM_W), lambda b, i: (b, 0))
    return pl.pallas_call(
        functools.partial(_memattn_body, per_head=per_head),
        grid=(batch, nq),
        in_specs=[q_spec, kv_spec, kv_spec],
        out_specs=q_spec,
        out_shape=jax.ShapeDtypeStruct((batch * seq, MEM_W), BF16),
        compiler_params=_params(("parallel", "arbitrary")),
        name="memattn",
    )(mq, mk, mv)


def _outproj_body(x_ref, g_ref, d_ref, m_ref, wg_ref, wd_ref, wm_ref, o_ref):
    o_ref[...] = (x_ref[...] + _dot(g_ref[...], wg_ref[...]) + _dot(d_ref[...], wd_ref[...])
                  + _dot(m_ref[...], wm_ref[...]))


def _outproj(x, g, d, m, wo):
    n = x.shape[0]
    tm = min(TOK_TILE, n)
    tok = lambda width: pl.BlockSpec((tm, width), lambda i: (i, 0))
    w_rows = lambda rows, blk: pl.BlockSpec((rows, D_MODEL), lambda i: (blk, 0))
    return pl.pallas_call(
        _outproj_body,
        grid=(n // tm,),
        in_specs=[tok(D_MODEL), tok(GLA_V), tok(DIFF_V), tok(MEM_W),
                  w_rows(GLA_V, 0), w_rows(DIFF_V, GLA_V // DIFF_V), w_rows(MEM_W, (GLA_V + DIFF_V) // MEM_W)],
        out_specs=tok(D_MODEL),
        out_shape=jax.ShapeDtypeStruct((n, D_MODEL), F32),
        compiler_params=_params(("parallel",)),
        name="outproj",
    )(x, g, d, m, wo, wo, wo)


def _row(v):
    return v.reshape(1, -1).astype(F32)


def kernel(x_prompt, x_sample, mem_prompt, cache_diff_k, cache_diff_v, state_gla, cache_mem_k, cache_mem_v, rel_bias_table, norm_ffn1, w_ffn1_in, w_ffn1_out, norm_mix, w_in, w_gla_g2, b_gla_g, gla_out_norm, diff_q_norm, diff_k_norm, diff_lambda, diff_out_norm, mem_norm, w_mem_kv, mem_q_norm, mem_k_norm, w_o, norm_ffn2, w_ffn2_in, w_ffn2_out, norm_final):
    depth = norm_ffn1.shape[0]
    assert depth == 1, "single-layer step"
    layer = 0
    batch, seq, _ = x_prompt.shape
    dec_batch, dec_seq, _ = x_sample.shape
    past = cache_diff_k.shape[2]
    n_mem = mem_prompt.shape[1]
    lam_init = 0.8 - 0.6 * math.exp(-0.3 * layer)

    ffn1 = (w_ffn1_in[layer].astype(BF16), w_ffn1_out[layer].astype(BF16))
    ffn2 = (w_ffn2_in[layer].astype(BF16), w_ffn2_out[layer].astype(BF16))
    w = w_in[layer].astype(BF16)
    w_proj = (w, w[:, PROJ_B_START:])
    w_g2 = jnp.pad(w_gla_g2[layer].astype(BF16), ((0, LANES - GLA_GATE_RANK), (0, 0)))
    b_g = _row(b_gla_g[layer])
    qn = _row(jnp.tile(diff_q_norm[layer], DIFF_QK // DIFF_DH))
    kn = _row(jnp.tile(diff_k_norm[layer], DIFF_QK // DIFF_DH))
    mqn = _row(jnp.tile(mem_q_norm[layer], MEM_HEADS))
    mkn = _row(jnp.tile(mem_k_norm[layer], MEM_HEADS))
    wo = w_o[layer].astype(BF16)
    table = rel_bias_table.astype(F32).reshape(-1)
    lam_p = diff_lambda[layer].astype(F32)
    gla_on = _row(gla_out_norm[layer])
    diff_on = _row(diff_out_norm[layer])
    per_head = lambda a: a.reshape(-1, HEADS, LANES)

    mk, mv, mk_b, mv_b = _memkv(mem_prompt.reshape(batch * n_mem, D_MODEL), _row(mem_norm[layer]),
                                w_mem_kv[layer].astype(BF16), mkn)

    def layer_fn(x, b, t, chunk, s0, diff_fn, mem_k, mem_v):
        x = _ffn(x, _row(norm_ffn1[layer]), *ffn1)
        gq, gk, gv, gr, gg, dq, dk, dv, mq, dk_b, dv_b = _proj(x, _row(norm_mix[layer]), *w_proj, w_g2, b_g,
                                                               qn, kn, mqn)
        g_out, g_state = _gla(gq, gk, gg, gv, gr, gla_on, s0, b, t, chunk)
        d_out = diff_fn(dq, dk_b, dv_b)
        m_out = _memattn(mq, mem_k, mem_v, b, t, n_mem)
        x = _outproj(x, g_out, d_out, m_out, wo)
        x = _ffn(x, _row(norm_ffn2[layer]), *ffn2, final_gain=_row(norm_final[layer]))
        return x, dk, dv, g_state

    yp, dk_p, dv_p, g_p = layer_fn(
        x_prompt.reshape(batch * seq, D_MODEL), batch, seq, CHUNK, None,
        lambda dq, dk, dv: _diff_prompt(table, dq, dk, dv, lam_p, diff_on, batch, seq, lam_init), mk_b, mv_b)
    ys, dk_s, dv_s, g_s = layer_fn(
        x_sample.reshape(dec_batch * dec_seq, D_MODEL), dec_batch, dec_seq, dec_seq, state_gla[layer],
        lambda dq, dk, dv: _diff_sample(table, dq, dk, dv, per_head(cache_diff_k[layer]), per_head(cache_diff_v[layer]),
                                        lam_p, diff_on, dec_batch, dec_seq, past, lam_init),
        per_head(cache_mem_k[layer]), per_head(cache_mem_v[layer]))

    head4 = lambda a, b, t: a.reshape(1, b, t, HEADS, LANES)
    return (yp.reshape(batch, seq, D_MODEL), ys.reshape(dec_batch, dec_seq, D_MODEL),
            head4(dk_p, batch, seq), head4(dv_p, batch, seq), g_p[None],
            head4(mk, batch, n_mem), head4(mv, batch, n_mem),
            head4(dk_s, dec_batch, dec_seq), head4(dv_s, dec_batch, dec_seq), g_s[None])
```

```python
import functools
import math

import jax
import jax.numpy as jnp
from jax import lax
from jax.experimental import pallas as pl
from jax.experimental.pallas import tpu as pltpu

F32 = jnp.float32
BF16 = jnp.bfloat16

D_MODEL = 2048
CHUNK = 64
EPS = 1e-6
NEG_INF = -1e30
GLA_HEADS, GLA_DK, GLA_DV, GLA_GATE_RANK, GLA_GATE_NORM = 4, 128, 256, 16, 16.0
DIFF_HEADS, DIFF_DH, DIFF_DV = 4, 64, 128
MEM_HEADS, MEM_DH = 4, 128
REL_BUCKETS, REL_MAX_DIST = 32, 128
D_FF = 5504
GLA_QK = GLA_HEADS * GLA_DK
GLA_V = GLA_HEADS * GLA_DV
DIFF_QK = DIFF_HEADS * 2 * DIFF_DH
DIFF_V = DIFF_HEADS * DIFF_DV
MEM_W = MEM_HEADS * MEM_DH
GLR_OFF = 2 * GLA_QK + 2 * GLA_V

LANES = 128
HEADS = 4
FF_TILE = 512
N_FF_TILES = -(-D_FF // FF_TILE)
FF_LAST_START = D_FF - FF_TILE
FF_OVERLAP = N_FF_TILES * FF_TILE - D_FF
FFN_TOK_TILE = 1024
TOK_TILE = 512
PROJ_TILE = 256
Q_BLOCK = 256
MASKED_BUCKET = REL_BUCKETS
MIB = 1024 * 1024

_A_GQ, _A_GK, _A_GV, _A_GR, _A_GLR = 0, GLA_QK, 2 * GLA_QK, 2 * GLA_QK + GLA_V, GLR_OFF
PROJ_A_COLS = GLR_OFF + LANES
PROJ_B_START = GLR_OFF + GLA_GATE_RANK
_B_DQ, _B_DK, _B_DV, _B_MQ = 0, DIFF_QK, 2 * DIFF_QK, 2 * DIFF_QK + DIFF_V
PROJ_B_COLS = _B_MQ + MEM_W


def _dot(a, b):
    return jnp.dot(a, b, preferred_element_type=F32)


def _dot_nt(a, b):
    return lax.dot_general(a, b, (((1,), (1,)), ((), ())), preferred_element_type=F32)


def _dot_tn(a, b):
    return lax.dot_general(a, b, (((0,), (0,)), ((), ())), preferred_element_type=F32)


def _rms(x, gain):
    return x * lax.rsqrt(jnp.mean(x * x, axis=-1, keepdims=True) + EPS) * gain


def _params(sem, vmem_mib=48):
    return pltpu.CompilerParams(dimension_semantics=sem, vmem_limit_bytes=vmem_mib * MIB)


def _ffn_body(*refs, n_ff, final_norm):
    if final_norm:
        x_ref, g_ref, wg_ref, wu_ref, wo_ref, fg_ref, o_ref, xn_ref = refs
    else:
        x_ref, g_ref, wg_ref, wu_ref, wo_ref, o_ref, xn_ref = refs
    j = pl.program_id(1)

    @pl.when(j == 0)
    def _():
        x = x_ref[...]
        xn_ref[...] = _rms(x, g_ref[...]).astype(BF16)
        o_ref[...] = x

    xn = xn_ref[...]
    gate = _dot(xn, wg_ref[...])
    up = _dot(xn, wu_ref[...])
    col = lax.broadcasted_iota(jnp.int32, (1, FF_TILE), 1)
    seen = (j == n_ff - 1) & (col < FF_OVERLAP)
    act = jnp.where(seen, 0.0, jax.nn.silu(gate) * up).astype(BF16)
    o_ref[...] += 0.5 * _dot(act, wo_ref[...])

    if final_norm:
        @pl.when(j == n_ff - 1)
        def _():
            o_ref[...] = _rms(o_ref[...], fg_ref[...])


def _ffn(x, gain, w_in, w_out, final_gain=None):
    n = x.shape[0]
    tm = min(FFN_TOK_TILE, n)
    n_ff = N_FF_TILES
    row = pl.BlockSpec((1, D_MODEL), lambda i, j: (0, 0))
    start = lambda j: jnp.minimum(j * (FF_TILE // LANES), FF_LAST_START // LANES)
    w_in_tile = (pl.Element(D_MODEL), pl.Element(FF_TILE))
    in_specs = [
        pl.BlockSpec((tm, D_MODEL), lambda i, j: (i, 0)),
        row,
        pl.BlockSpec(w_in_tile, lambda i, j: (0, start(j) * LANES)),
        pl.BlockSpec(w_in_tile, lambda i, j: (0, (D_FF // LANES + start(j)) * LANES)),
        pl.BlockSpec((pl.Element(FF_TILE), pl.Element(D_MODEL)), lambda i, j: (start(j) * LANES, 0)),
    ]
    args = [x, gain, w_in, w_in, w_out]
    if final_gain is not None:
        in_specs.append(row)
        args.append(final_gain)
    vmem = (4 * tm * D_MODEL * 4 + tm * D_MODEL * 2 + 6 * D_MODEL * FF_TILE * 2 + 3 * tm * FF_TILE * 4) // MIB + 4
    return pl.pallas_call(
        functools.partial(_ffn_body, n_ff=n_ff, final_norm=final_gain is not None),
        grid=(n // tm, n_ff),
        in_specs=in_specs,
        out_specs=pl.BlockSpec((tm, D_MODEL), lambda i, j: (i, 0)),
        out_shape=jax.ShapeDtypeStruct((n, D_MODEL), F32),
        scratch_shapes=[pltpu.VMEM((tm, D_MODEL), BF16)],
        compiler_params=_params(("parallel", "arbitrary"), vmem),
        name="ffn",
    )(*args)


def _group_rms(acc, gain_ref, width, scale, put):
    lane = lax.broadcasted_iota(jnp.int32, (1, LANES), 1)
    low = lane < width
    for c in range(acc.shape[1] // LANES):
        cols = slice(c * LANES, (c + 1) * LANES)
        xc = acc[:, cols]
        sq = xc * xc
        if width == LANES:
            r = lax.rsqrt(jnp.mean(sq, axis=-1, keepdims=True) + EPS)
        else:
            s_lo = jnp.sum(jnp.where(low, sq, 0.0), axis=-1, keepdims=True)
            s_hi = jnp.sum(jnp.where(low, 0.0, sq), axis=-1, keepdims=True)
            r = jnp.where(low, lax.rsqrt(s_lo / width + EPS), lax.rsqrt(s_hi / width + EPS))
        y = xc * r * gain_ref[:, cols]
        if scale != 1.0:
            y = y * scale
        put(c, y)


def _put_cols(ref):
    def put(c, y):
        ref[:, c * LANES:(c + 1) * LANES] = y.astype(ref.dtype)
    return put


def _put_heads_and_cols(heads_ref, cols_ref):
    def put(c, y):
        heads_ref[:, c, :] = y
        cols_ref[:, c * LANES:(c + 1) * LANES] = y.astype(cols_ref.dtype)
    return put


def _proj_body(x_ref, g_ref, wa_ref, wb_ref, wg2_ref, bg_ref, qn_ref, kn_ref, mqn_ref,
               gq_ref, gk_ref, gv_ref, gr_ref, gg_ref, dq_ref, dk_ref, dv_ref, mq_ref, dkb_ref, dvb_ref, xn_ref):
    xn_ref[...] = _rms(x_ref[...], g_ref[...]).astype(BF16)

    def cols(w_ref, start, width):
        return _dot(xn_ref[...], w_ref[:, start:start + width])

    gq_ref[...] = cols(wa_ref, _A_GQ, GLA_QK) * (GLA_DK ** -0.5)
    gk_ref[...] = cols(wa_ref, _A_GK, GLA_QK)
    gv_ref[...] = cols(wa_ref, _A_GV, GLA_V).astype(BF16)
    gr_ref[...] = cols(wa_ref, _A_GR, GLA_V)
    glr = cols(wa_ref, _A_GLR, LANES)
    z = _dot(glr.astype(BF16), wg2_ref[...]) + bg_ref[...]
    gg_ref[...] = jax.nn.log_sigmoid(z) / GLA_GATE_NORM
    _group_rms(cols(wb_ref, _B_DQ, DIFF_QK), qn_ref, DIFF_DH, DIFF_DH ** -0.5, _put_cols(dq_ref))
    _group_rms(cols(wb_ref, _B_DK, DIFF_QK), kn_ref, DIFF_DH, 1.0, _put_heads_and_cols(dk_ref, dkb_ref))
    dv = cols(wb_ref, _B_DV, DIFF_V)
    put_v = _put_heads_and_cols(dv_ref, dvb_ref)
    for h in range(HEADS):
        put_v(h, dv[:, h * LANES:(h + 1) * LANES])
    _group_rms(cols(wb_ref, _B_MQ, MEM_W), mqn_ref, MEM_DH, 1.0, _put_cols(mq_ref))


def _proj(x, gain, wa, wb, wg2, bg, qn, kn, mqn):
    n = x.shape[0]
    tm = min(PROJ_TILE, n)
    const = lambda shape: pl.BlockSpec(shape, lambda i: (0, 0))
    out = lambda width: pl.BlockSpec((tm, width), lambda i: (i, 0))
    heads = pl.BlockSpec((tm, HEADS, LANES), lambda i: (i, 0, 0))
    shp = lambda width, dt: jax.ShapeDtypeStruct((n, width), dt)
    shp_heads = jax.ShapeDtypeStruct((n, HEADS, LANES), F32)
    return pl.pallas_call(
        _proj_body,
        grid=(n // tm,),
        in_specs=[
            pl.BlockSpec((tm, D_MODEL), lambda i: (i, 0)),
            const((1, D_MODEL)),
            pl.BlockSpec((D_MODEL, PROJ_A_COLS), lambda i: (0, 0), pipeline_mode=pl.Buffered(1)),
            pl.BlockSpec((D_MODEL, PROJ_B_COLS), lambda i: (0, 0), pipeline_mode=pl.Buffered(1)),
            const((LANES, GLA_QK)),
            const((1, GLA_QK)),
            const((1, DIFF_QK)),
            const((1, DIFF_QK)),
            const((1, MEM_W)),
        ],
        out_specs=[out(GLA_QK), out(GLA_QK), out(GLA_V), out(GLA_V), out(GLA_QK),
                   out(DIFF_QK), heads, heads, out(MEM_W), out(DIFF_QK), out(DIFF_V)],
        out_shape=[shp(GLA_QK, F32), shp(GLA_QK, F32), shp(GLA_V, BF16), shp(GLA_V, F32), shp(GLA_QK, F32),
                   shp(DIFF_QK, BF16), shp_heads, shp_heads, shp(MEM_W, BF16), shp(DIFF_QK, BF16), shp(DIFF_V, BF16)],
        scratch_shapes=[pltpu.VMEM((tm, D_MODEL), BF16)],
        compiler_params=_params(("parallel",)),
        name="proj",
    )(x, gain, wa, wb, wg2, bg, qn, kn, mqn)


def _memkv_body(x_ref, g_ref, w_ref, kn_ref, k_ref, v_ref, kb_ref, vb_ref):
    xn = _rms(x_ref[...], g_ref[...]).astype(BF16)
    _group_rms(_dot(xn, w_ref[:, :MEM_W]), kn_ref, MEM_DH, 1.0, _put_heads_and_cols(k_ref, kb_ref))
    v = _dot(xn, w_ref[:, MEM_W:])
    put_v = _put_heads_and_cols(v_ref, vb_ref)
    for h in range(HEADS):
        put_v(h, v[:, h * LANES:(h + 1) * LANES])


def _memkv(mem, gain, w, kn):
    n = mem.shape[0]
    tm = min(TOK_TILE, n)
    const = lambda shape: pl.BlockSpec(shape, lambda i: (0, 0))
    heads = pl.BlockSpec((tm, HEADS, LANES), lambda i: (i, 0, 0))
    dense = pl.BlockSpec((tm, MEM_W), lambda i: (i, 0))
    return pl.pallas_call(
        _memkv_body,
        grid=(n // tm,),
        in_specs=[pl.BlockSpec((tm, D_MODEL), lambda i: (i, 0)), const((1, D_MODEL)),
                  const((D_MODEL, 2 * MEM_W)), const((1, MEM_W))],
        out_specs=[heads, heads, dense, dense],
        out_shape=[jax.ShapeDtypeStruct((n, HEADS, LANES), F32)] * 2 + [jax.ShapeDtypeStruct((n, MEM_W), BF16)] * 2,
        compiler_params=_params(("parallel",)),
        name="memkv",
    )(mem, gain, w, kn)


def _split3(x):
    hi = x.astype(BF16)
    r1 = x - hi.astype(F32)
    mid = r1.astype(BF16)
    lo = (r1 - mid.astype(F32)).astype(BF16)
    return hi, mid, lo


def _gla_body(*refs, chunk, n_chunks, n_steps, has_state):
    if has_state:
        gq_ref, gk_ref, gg_ref, gv_ref, gr_ref, on_ref, s0_ref, go_ref, st_ref, state = refs
    else:
        gq_ref, gk_ref, gg_ref, gv_ref, gr_ref, on_ref, go_ref, st_ref, state = refs
    t = pl.program_id(1)

    @pl.when(t == 0)
    def _():
        for h in range(GLA_HEADS):
            if has_state:
                state[h] = s0_ref[0, h].T
            else:
                state[h] = jnp.zeros((GLA_DV, GLA_DK), F32)

    row = lax.broadcasted_iota(jnp.int32, (chunk, chunk), 0)
    col = lax.broadcasted_iota(jnp.int32, (chunk, chunk), 1)
    causal = row >= col
    tril = causal.astype(BF16)

    chunks = [slice(c * chunk, (c + 1) * chunk) for c in range(n_chunks)]
    kcs = [slice(h * GLA_DK, (h + 1) * GLA_DK) for h in range(GLA_HEADS)]
    vcs = [slice(h * GLA_DV, (h + 1) * GLA_DV) for h in range(GLA_HEADS)]

    bs = []
    for rows in chunks:
        g_hi, g_mid, g_lo = _split3(gg_ref[rows, :])
        bs.append(_dot(tril, g_hi) + _dot(tril, g_mid) + _dot(tril, g_lo))

    qes, kes, kds, decays = [], [], [], []
    for rows, b in zip(chunks, bs):
        b_last = b[chunk - 1:chunk, :]
        q = gq_ref[rows, :]
        k = gk_ref[rows, :]
        qes.append((q * jnp.exp(b)).astype(BF16))
        kes.append((k * jnp.exp(-b)).astype(BF16))
        kds.append((k * jnp.exp(b_last - b)).astype(BF16))
        decays.append(jnp.exp(b_last))

    a_s = [[jnp.where(causal, _dot_nt(qe[:, kc], ke[:, kc]), 0.0).astype(BF16) for kc in kcs]
           for qe, ke in zip(qes, kes)]
    incs = [[_dot_tn(gv_ref[rows, vc], kd[:, kc]) for kc, vc in zip(kcs, vcs)] for rows, kd in zip(chunks, kds)]

    s_in = []
    s_cur = [state[h] for h in range(GLA_HEADS)]
    for c in range(n_chunks):
        s_in.append([s.astype(BF16) for s in s_cur])
        s_cur = [s * decays[c][:, kc] + inc for s, kc, inc in zip(s_cur, kcs, incs[c])]
    for h in range(GLA_HEADS):
        state[h] = s_cur[h]

    for c, rows in enumerate(chunks):
        for h, (kc, vc) in enumerate(zip(kcs, vcs)):
            o = _dot_nt(qes[c][:, kc], s_in[c][h]) + _dot(a_s[c][h], gv_ref[rows, vc])
            go_ref[rows, vc] = (_rms(o, on_ref[...]) * jax.nn.silu(gr_ref[rows, vc])).astype(BF16)

    @pl.when(t == n_steps - 1)
    def _():
        for h in range(GLA_HEADS):
            st_ref[0, h] = state[h].T


def _gla(gq, gk, gg, gv, gr, onorm, s0, batch, seq, chunk):
    tt = min(TOK_TILE, seq)
    n_steps = seq // tt
    tok = lambda width: pl.BlockSpec((tt, width), lambda b, t: (b * n_steps + t, 0))
    st_spec = pl.BlockSpec((1, GLA_HEADS, GLA_DK, GLA_DV), lambda b, t: (b, 0, 0, 0))
    in_specs = [tok(GLA_QK), tok(GLA_QK), tok(GLA_QK), tok(GLA_V), tok(GLA_V),
                pl.BlockSpec((1, GLA_DV), lambda b, t: (0, 0))]
    args = [gq, gk, gg, gv, gr, onorm]
    if s0 is not None:
        in_specs.append(st_spec)
        args.append(s0)
    return pl.pallas_call(
        functools.partial(_gla_body, chunk=chunk, n_chunks=tt // chunk, n_steps=n_steps, has_state=s0 is not None),
        grid=(batch, n_steps),
        in_specs=in_specs,
        out_specs=[tok(GLA_V), st_spec],
        out_shape=[jax.ShapeDtypeStruct((batch * seq, GLA_V), BF16),
                   jax.ShapeDtypeStruct((batch, GLA_HEADS, GLA_DK, GLA_DV), F32)],
        scratch_shapes=[pltpu.VMEM((GLA_HEADS, GLA_DV, GLA_DK), F32)],
        compiler_params=_params(("parallel", "arbitrary")),
        name="gla",
    )(*args)


def _t5_bucket(rel):
    nb = REL_BUCKETS // 2
    max_exact = nb // 2
    ret = jnp.where(rel > 0, nb, 0)
    n = jnp.abs(rel)
    nf = jnp.maximum(n, 1).astype(F32)
    large = max_exact + (jnp.log(nf / max_exact) / math.log(REL_MAX_DIST / max_exact)
                         * (nb - max_exact)).astype(jnp.int32)
    large = jnp.minimum(large, nb - 1)
    return ret + jnp.where(n < max_exact, n, large)


def _bucket_tile(q_pos, k_pos):
    visible = (k_pos[None, :] // CHUNK) <= (q_pos[:, None] // CHUNK)
    return jnp.where(visible, _t5_bucket(k_pos[None, :] - q_pos[:, None]), MASKED_BUCKET).astype(jnp.int32)


def _bias_from_buckets(idx, tab_ref, head):
    def step(bk, acc):
        return jnp.where(idx == bk, tab_ref[bk * DIFF_HEADS + head], acc)
    return lax.fori_loop(0, REL_BUCKETS, step, jnp.full(idx.shape, NEG_INF, F32))


def _lambda(lam_ref, lam_init):
    l = lam_ref[...]
    return (jnp.exp(jnp.sum(l[0:1] * l[1:2], axis=-1, keepdims=True))
            - jnp.exp(jnp.sum(l[2:3] * l[3:4], axis=-1, keepdims=True)) + lam_init)


def _comp_masks(q):
    lane = lax.broadcasted_iota(jnp.int32, q.shape, 1)
    zero = jnp.zeros_like(q)
    return jnp.where(lane < DIFF_DH, q, zero), jnp.where(lane < DIFF_DH, zero, q)


def _scores(q_c, parts):
    return [_dot_nt(q_c, k) + bias if jnp.ndim(bias) == 2 else _dot_nt(q_c, k) for k, _, bias in parts]


def _softmax_pv(scores, parts):
    shifts = [0.0 if jnp.ndim(bias) == 2 else bias for _, _, bias in parts]
    m = None
    for s, shift in zip(scores, shifts):
        part_max = s.max(axis=-1, keepdims=True) + shift
        m = part_max if m is None else jnp.maximum(m, part_max)
    l = 0.0
    o = 0.0
    for s, shift, (_, v, _) in zip(scores, shifts, parts):
        e = jnp.exp(s - (m - shift))
        l = l + e.sum(axis=-1, keepdims=True)
        o = o + _dot(e.astype(BF16), v)
    return o * (1.0 / l)


def _diff_finish(o0, o1, lam, on_ref, lam_init, out_dtype):
    o = o0 - lam * o1
    return (_rms(o, on_ref[...]) * (1.0 - lam_init)).astype(out_dtype)


def _diff_prompt_body(tab_ref, q_ref, kb, vb, idx_ref, lam_ref, on_ref, o_ref, bias, *, seq, lam_init):
    b = pl.program_id(0)
    h = pl.program_id(1)

    @pl.when(b == 0)
    def _():
        for t in range(2):
            bias[h, t] = _bias_from_buckets(idx_ref[t], tab_ref, h)

    lam = _lambda(lam_ref, lam_init)
    far_bias = tab_ref[(REL_BUCKETS // 2 - 1) * DIFF_HEADS + h]

    def key_parts(i):
        near0 = max(i - 1, 0) * Q_BLOCK
        near = slice(near0, near0 + 2 * Q_BLOCK)
        parts = [(kb[near, :], vb[near, :], bias[h, min(i, 1)])]
        if near0 > 0:
            parts.append((kb[0:near0, :], vb[0:near0, :], far_bias))
        return parts

    items = [(i, c) for i in range(seq // Q_BLOCK) for c in range(2)]
    comps = {}

    def scores_of(item):
        i, c = item
        if i not in comps:
            comps[i] = _comp_masks(q_ref[i * Q_BLOCK:(i + 1) * Q_BLOCK, :])
        return _scores(comps[i][c], key_parts(i))

    outs = {}
    ahead = scores_of(items[0])
    for n, (i, c) in enumerate(items):
        cur = ahead
        if n + 1 < len(items):
            ahead = scores_of(items[n + 1])
        outs[c] = _softmax_pv(cur, key_parts(i))
        if c == 1:
            o_ref[i * Q_BLOCK:(i + 1) * Q_BLOCK, :] = _diff_finish(outs[0], outs[1], lam, on_ref, lam_init,
                                                                    o_ref.dtype)


def _diff_prompt(table, dq, dk, dv, lam_p, onorm, batch, seq, lam_init):
    q_pos = jnp.arange(Q_BLOCK)
    idx = jnp.stack([_bucket_tile(q_pos, jnp.arange(2 * Q_BLOCK)),
                     _bucket_tile(q_pos + Q_BLOCK, jnp.arange(2 * Q_BLOCK))])
    head = lambda: pl.BlockSpec((seq, LANES), lambda b, h: (b, h))
    return pl.pallas_call(
        functools.partial(_diff_prompt_body, seq=seq, lam_init=lam_init),
        grid=(batch, DIFF_HEADS),
        in_specs=[
            pl.BlockSpec(memory_space=pltpu.SMEM),
            head(), head(), head(),
            pl.BlockSpec((2, Q_BLOCK, 2 * Q_BLOCK), lambda b, h: (0, 0, 0)),
            pl.BlockSpec((4, DIFF_DH), lambda b, h: (0, 0)),
            pl.BlockSpec((1, DIFF_DV), lambda b, h: (0, 0)),
        ],
        out_specs=head(),
        out_shape=jax.ShapeDtypeStruct((batch * seq, DIFF_V), BF16),
        scratch_shapes=[pltpu.VMEM((DIFF_HEADS, 2, Q_BLOCK, 2 * Q_BLOCK), F32)],
        compiler_params=_params(("arbitrary", "arbitrary")),
        name="diff_prompt",
    )(table, dq, dk, dv, idx, lam_p, onorm)


def _diff_sample_body(tab_ref, q_ref, k_ref, v_ref, ck_ref, cv_ref, idxc_ref, idxn_ref, lam_ref, on_ref, o_ref,
                      bias_c, bias_n, *, lam_init):
    @pl.when(pl.program_id(0) == 0)
    def _():
        for h in range(DIFF_HEADS):
            bias_c[h] = _bias_from_buckets(idxc_ref[...], tab_ref, h)
            bias_n[h] = _bias_from_buckets(idxn_ref[...], tab_ref, h)

    lam = _lambda(lam_ref, lam_init)
    for h in range(DIFF_HEADS):
        cols = slice(h * LANES, (h + 1) * LANES)
        parts = [(ck_ref[:, h, :].astype(BF16), cv_ref[:, h, :].astype(BF16), bias_c[h]),
                 (k_ref[:, cols], v_ref[:, cols], bias_n[h])]
        outs = [_softmax_pv(_scores(q_c, parts), parts) for q_c in _comp_masks(q_ref[:, cols])]
        o_ref[:, cols] = _diff_finish(outs[0], outs[1], lam, on_ref, lam_init, o_ref.dtype)


def _diff_sample(table, dq, dk, dv, cache_k, cache_v, lam_p, onorm, batch, seq, past, lam_init):
    q_pos = past + jnp.arange(seq)
    idx_c = _bucket_tile(q_pos, jnp.arange(past))
    idx_n = _bucket_tile(q_pos, past + jnp.arange(seq))
    new_q = pl.BlockSpec((seq, DIFF_QK), lambda b: (b, 0))
    old = lambda: pl.BlockSpec((past, HEADS, LANES), lambda b: (b, 0, 0))
    const = lambda shape: pl.BlockSpec(shape, lambda b: (0, 0))
    return pl.pallas_call(
        functools.partial(_diff_sample_body, lam_init=lam_init),
        grid=(batch,),
        in_specs=[pl.BlockSpec(memory_space=pltpu.SMEM), new_q, new_q, new_q, old(), old(),
                  const((seq, past)), const((seq, seq)), const((4, DIFF_DH)), const((1, DIFF_DV))],
        out_specs=new_q,
        out_shape=jax.ShapeDtypeStruct((batch * seq, DIFF_V), BF16),
        scratch_shapes=[pltpu.VMEM((DIFF_HEADS, seq, past), F32), pltpu.VMEM((DIFF_HEADS, seq, seq), F32)],
        compiler_params=_params(("arbitrary",)),
        name="diff_sample",
    )(table, dq, dk, dv, cache_k, cache_v, idx_c, idx_n, lam_p, onorm)


def _memattn_body(q_ref, k_ref, v_ref, o_ref, *, per_head):
    for h in range(MEM_HEADS):
        cols = slice(h * MEM_DH, (h + 1) * MEM_DH)
        if per_head:
            k, v = k_ref[:, h, :].astype(BF16), v_ref[:, h, :].astype(BF16)
        else:
            k, v = k_ref[:, cols], v_ref[:, cols]
        s = _dot_nt(q_ref[:, cols], k) * (MEM_DH ** -0.5)
        e = jnp.exp(s - s.max(axis=-1, keepdims=True))
        p = e * (1.0 / e.sum(axis=-1, keepdims=True))
        o_ref[:, cols] = _dot(p.astype(BF16), v).astype(o_ref.dtype)


def _memattn(mq, mk, mv, batch, seq, n_mem):
    tq = min(TOK_TILE, seq)
    nq = seq // tq
    q_spec = pl.BlockSpec((tq, MEM_W), lambda b, i: (b * nq + i, 0))
    per_head = mk.ndim == 3
    if per_head:
        kv_spec = pl.BlockSpec((n_mem, HEADS, LANES), lambda b, i: (b, 0, 0))
    else:
        kv_spec = pl.BlockSpec((n_mem, MEM_W), lambda b, i: (b, 0))
    return pl.pallas_call(
        functools.partial(_memattn_body, per_head=per_head),
        grid=(batch, nq),
        in_specs=[q_spec, kv_spec, kv_spec],
        out_specs=q_spec,
        out_shape=jax.ShapeDtypeStruct((batch * seq, MEM_W), BF16),
        compiler_params=_params(("parallel", "arbitrary")),
        name="memattn",
    )(mq, mk, mv)


def _outproj_body(x_ref, g_ref, d_ref, m_ref, wg_ref, wd_ref, wm_ref, o_ref):
    o_ref[...] = (x_ref[...] + _dot(g_ref[...], wg_ref[...]) + _dot(d_ref[...], wd_ref[...])
                  + _dot(m_ref[...], wm_ref[...]))


def _outproj(x, g, d, m, wo):
    n = x.shape[0]
    tm = min(TOK_TILE, n)
    tok = lambda width: pl.BlockSpec((tm, width), lambda i: (i, 0))
    w_rows = lambda rows, blk: pl.BlockSpec((rows, D_MODEL), lambda i: (blk, 0))
    return pl.pallas_call(
        _outproj_body,
        grid=(n // tm,),
        in_specs=[tok(D_MODEL), tok(GLA_V), tok(DIFF_V), tok(MEM_W),
                  w_rows(GLA_V, 0), w_rows(DIFF_V, GLA_V // DIFF_V), w_rows(MEM_W, (GLA_V + DIFF_V) // MEM_W)],
        out_specs=tok(D_MODEL),
        out_shape=jax.ShapeDtypeStruct((n, D_MODEL), F32),
        compiler_params=_params(("parallel",)),
        name="outproj",
    )(x, g, d, m, wo, wo, wo)


def _row(v):
    return v.reshape(1, -1).astype(F32)


def kernel(x_prompt, x_sample, mem_prompt, cache_diff_k, cache_diff_v, state_gla, cache_mem_k, cache_mem_v, rel_bias_table, norm_ffn1, w_ffn1_in, w_ffn1_out, norm_mix, w_in, w_gla_g2, b_gla_g, gla_out_norm, diff_q_norm, diff_k_norm, diff_lambda, diff_out_norm, mem_norm, w_mem_kv, mem_q_norm, mem_k_norm, w_o, norm_ffn2, w_ffn2_in, w_ffn2_out, norm_final):
    depth = norm_ffn1.shape[0]
    assert depth == 1, "single-layer step"
    layer = 0
    batch, seq, _ = x_prompt.shape
    dec_batch, dec_seq, _ = x_sample.shape
    past = cache_diff_k.shape[2]
    n_mem = mem_prompt.shape[1]
    lam_init = 0.8 - 0.6 * math.exp(-0.3 * layer)

    ffn1 = (w_ffn1_in[layer].astype(BF16), w_ffn1_out[layer].astype(BF16))
    ffn2 = (w_ffn2_in[layer].astype(BF16), w_ffn2_out[layer].astype(BF16))
    w = w_in[layer].astype(BF16)
    w_proj = (w, w[:, PROJ_B_START:])
    w_g2 = jnp.pad(w_gla_g2[layer].astype(BF16), ((0, LANES - GLA_GATE_RANK), (0, 0)))
    b_g = _row(b_gla_g[layer])
    qn = _row(jnp.tile(diff_q_norm[layer], DIFF_QK // DIFF_DH))
    kn = _row(jnp.tile(diff_k_norm[layer], DIFF_QK // DIFF_DH))
    mqn = _row(jnp.tile(mem_q_norm[layer], MEM_HEADS))
    mkn = _row(jnp.tile(mem_k_norm[layer], MEM_HEADS))
    wo = w_o[layer].astype(BF16)
    table = rel_bias_table.astype(F32).reshape(-1)
    lam_p = diff_lambda[layer].astype(F32)
    gla_on = _row(gla_out_norm[layer])
    diff_on = _row(diff_out_norm[layer])
    per_head = lambda a: a.reshape(-1, HEADS, LANES)

    mk, mv, mk_b, mv_b = _memkv(mem_prompt.reshape(batch * n_mem, D_MODEL), _row(mem_norm[layer]),
                                w_mem_kv[layer].astype(BF16), mkn)

    def layer_fn(x, b, t, chunk, s0, diff_fn, mem_k, mem_v):
        x = _ffn(x, _row(norm_ffn1[layer]), *ffn1)
        gq, gk, gv, gr, gg, dq, dk, dv, mq, dk_b, dv_b = _proj(x, _row(norm_mix[layer]), *w_proj, w_g2, b_g,
                                                               qn, kn, mqn)
        g_out, g_state = _gla(gq, gk, gg, gv, gr, gla_on, s0, b, t, chunk)
        d_out = diff_fn(dq, dk_b, dv_b)
        m_out = _memattn(mq, mem_k, mem_v, b, t, n_mem)
        x = _outproj(x, g_out, d_out, m_out, wo)
        x = _ffn(x, _row(norm_ffn2[layer]), *ffn2, final_gain=_row(norm_final[layer]))
        return x, dk, dv, g_state

    yp, dk_p, dv_p, g_p = layer_fn(
        x_prompt.reshape(batch * seq, D_MODEL), batch, seq, CHUNK, None,
        lambda dq, dk, dv: _diff_prompt(table, dq, dk, dv, lam_p, diff_on, batch, seq, lam_init), mk_b, mv_b)
    ys, dk_s, dv_s, g_s = layer_fn(
        x_sample.reshape(dec_batch * dec_seq, D_MODEL), dec_batch, dec_seq, dec_seq, state_gla[layer],
        lambda dq, dk, dv: _diff_sample(table, dq, dk, dv, per_head(cache_diff_k[layer]), per_head(cache_diff_v[layer]),
                                        lam_p, diff_on, dec_batch, dec_seq, past, lam_init),
        per_head(cache_mem_k[layer]), per_head(cache_mem_v[layer]))

    head4 = lambda a, b, t: a.reshape(1, b, t, HEADS, LANES)
    return (yp.reshape(batch, seq, D_MODEL), ys.reshape(dec_batch, dec_seq, D_MODEL),
            head4(dk_p, batch, seq), head4(dv_p, batch, seq), g_p[None],
            head4(mk, batch, n_mem), head4(mv, batch, n_mem),
            head4(dk_s, dec_batch, dec_seq), head4(dv_s, dec_batch, dec_seq), g_s[None])
```

```python
import functools
import math

import jax
import jax.numpy as jnp
from jax import lax
from jax.experimental import pallas as pl
from jax.experimental.pallas import tpu as pltpu

F32 = jnp.float32
BF16 = jnp.bfloat16

D_MODEL = 2048
CHUNK = 64
EPS = 1e-6
NEG_INF = -1e30
GLA_HEADS, GLA_DK, GLA_DV, GLA_GATE_RANK, GLA_GATE_NORM = 4, 128, 256, 16, 16.0
DIFF_HEADS, DIFF_DH, DIFF_DV = 4, 64, 128
MEM_HEADS, MEM_DH = 4, 128
REL_BUCKETS, REL_MAX_DIST = 32, 128
D_FF = 5504
GLA_QK = GLA_HEADS * GLA_DK
GLA_V = GLA_HEADS * GLA_DV
DIFF_QK = DIFF_HEADS * 2 * DIFF_DH
DIFF_V = DIFF_HEADS * DIFF_DV
MEM_W = MEM_HEADS * MEM_DH
GLR_OFF = 2 * GLA_QK + 2 * GLA_V

LANES = 128
HEADS = 4
FF_TILE = 512
N_FF_TILES = -(-D_FF // FF_TILE)
FF_LAST_START = D_FF - FF_TILE
FF_OVERLAP = N_FF_TILES * FF_TILE - D_FF
FFN_TOK_TILE = 1024
TOK_TILE = 512
PROJ_TILE = 256
PROJ_DOT_COLS = 256
Q_BLOCK = 256
MASKED_BUCKET = REL_BUCKETS
MIB = 1024 * 1024
VMEM_CAP_MIB = 60

_W_GQ, _W_GK, _W_GV, _W_GR, _W_GLR = 0, GLA_QK, 2 * GLA_QK, 2 * GLA_QK + GLA_V, GLR_OFF
_W_DQ = GLR_OFF + GLA_GATE_RANK
_W_DK = _W_DQ + DIFF_QK
_W_DV = _W_DK + DIFF_QK
_W_MQ = _W_DV + DIFF_V
IN_WIDTH = _W_MQ + MEM_W


def _dot(a, b):
    return jnp.dot(a, b, preferred_element_type=F32)


def _dot_nt(a, b):
    return lax.dot_general(a, b, (((1,), (1,)), ((), ())), preferred_element_type=F32)


def _dot_tn(a, b):
    return lax.dot_general(a, b, (((0,), (0,)), ((), ())), preferred_element_type=F32)


def _rms(x, gain):
    return x * lax.rsqrt(jnp.mean(x * x, axis=-1, keepdims=True) + EPS) * gain


def _params(sem, vmem_mib=48):
    return pltpu.CompilerParams(dimension_semantics=sem, vmem_limit_bytes=min(vmem_mib, VMEM_CAP_MIB) * MIB)


def _ffn_body(*refs, n_ff, final_norm, n_cast):
    x_ref, g_ref, wg_ref, wu_ref, wo_ref = refs[:5]
    refs = refs[5:]
    if final_norm:
        fg_ref, refs = refs[0], refs[1:]
    cast_src, refs = refs[:n_cast], refs[n_cast:]
    o_ref, refs = refs[0], refs[1:]
    cast_dst, (xn_ref,) = refs[:n_cast], refs[n_cast:]
    j = pl.program_id(1)

    for src, dst in zip(cast_src, cast_dst):
        dst[...] = src[...].astype(BF16)

    @pl.when(j == 0)
    def _():
        x = x_ref[...]
        xn_ref[...] = _rms(x, g_ref[...]).astype(BF16)
        o_ref[...] = x

    xn = xn_ref[...]
    gate = _dot(xn, wg_ref[...])
    up = _dot(xn, wu_ref[...])
    col = lax.broadcasted_iota(jnp.int32, (1, FF_TILE), 1)
    seen = (j == n_ff - 1) & (col < FF_OVERLAP)
    act = jnp.where(seen, 0.0, jax.nn.silu(gate) * up).astype(BF16)
    o_ref[...] += 0.5 * _dot(act, wo_ref[...])

    if final_norm:
        @pl.when(j == n_ff - 1)
        def _():
            o_ref[...] = _rms(o_ref[...], fg_ref[...])


def _ffn(x, gain, w_in, w_out, final_gain=None, cast_later=()):
    n = x.shape[0]
    tm = min(FFN_TOK_TILE, n)
    n_ff = N_FF_TILES
    row = pl.BlockSpec((1, D_MODEL), lambda i, j: (0, 0))
    start = lambda j: jnp.minimum(j * (FF_TILE // LANES), FF_LAST_START // LANES)
    w_in_tile = (pl.Element(D_MODEL), pl.Element(FF_TILE))
    in_specs = [
        pl.BlockSpec((tm, D_MODEL), lambda i, j: (i, 0)),
        row,
        pl.BlockSpec(w_in_tile, lambda i, j: (0, start(j) * LANES)),
        pl.BlockSpec(w_in_tile, lambda i, j: (0, (D_FF // LANES + start(j)) * LANES)),
        pl.BlockSpec((pl.Element(FF_TILE), pl.Element(D_MODEL)), lambda i, j: (start(j) * LANES, 0)),
    ]
    args = [x, gain, w_in, w_in, w_out]
    if final_gain is not None:
        in_specs.append(row)
        args.append(final_gain)
    vmem_bytes = 4 * tm * D_MODEL * 4 + tm * D_MODEL * 2 + 6 * D_MODEL * FF_TILE * 2 + 3 * tm * FF_TILE * 4
    out_specs = [pl.BlockSpec((tm, D_MODEL), lambda i, j: (i, 0))]
    out_shape = [jax.ShapeDtypeStruct((n, D_MODEL), F32)]
    n_steps = (n // tm) * n_ff
    for mat, blk, first in cast_later:
        rows, cols = mat.shape
        assert rows % blk[0] == 0 and cols % blk[1] == 0 and (blk[0] == rows or blk[1] == cols)
        n_blk = (rows // blk[0]) * (cols // blk[1])
        assert first + n_blk <= n_steps
        by_rows = blk[1] == cols

        def index(i, j, first=first, n_blk=n_blk, by_rows=by_rows):
            k = jnp.clip(i * n_ff + j - first, 0, n_blk - 1)
            return (k, 0) if by_rows else (0, k)

        in_specs.append(pl.BlockSpec(blk, index))
        args.append(mat)
        out_specs.append(pl.BlockSpec(blk, index))
        out_shape.append(jax.ShapeDtypeStruct(mat.shape, BF16))
        vmem_bytes += 2 * blk[0] * blk[1] * (4 + 2)
    outs = pl.pallas_call(
        functools.partial(_ffn_body, n_ff=n_ff, final_norm=final_gain is not None, n_cast=len(cast_later)),
        grid=(n // tm, n_ff),
        in_specs=in_specs,
        out_specs=out_specs,
        out_shape=out_shape,
        scratch_shapes=[pltpu.VMEM((tm, D_MODEL), BF16)],
        compiler_params=_params(("arbitrary", "arbitrary"), vmem_bytes // MIB + 4),
        name="ffn",
    )(*args)
    return outs[0] if not cast_later else outs


def _group_rms(acc, gain_ref, width, scale, put, c0=0):
    lane = lax.broadcasted_iota(jnp.int32, (1, LANES), 1)
    low = lane < width
    for i in range(acc.shape[1] // LANES):
        c = c0 + i
        cols = slice(c * LANES, (c + 1) * LANES)
        xc = acc[:, i * LANES:(i + 1) * LANES]
        sq = xc * xc
        if width == LANES:
            r = lax.rsqrt(jnp.mean(sq, axis=-1, keepdims=True) + EPS)
        else:
            s_lo = jnp.sum(jnp.where(low, sq, 0.0), axis=-1, keepdims=True)
            s_hi = jnp.sum(jnp.where(low, 0.0, sq), axis=-1, keepdims=True)
            r = jnp.where(low, lax.rsqrt(s_lo / width + EPS), lax.rsqrt(s_hi / width + EPS))
        y = xc * r * gain_ref[:, cols]
        if scale != 1.0:
            y = y * scale
        put(c, y)


def _put_cols(ref):
    def put(c, y):
        ref[:, c * LANES:(c + 1) * LANES] = y.astype(ref.dtype)
    return put


def _put_heads_and_cols(heads_ref, cols_ref):
    def put(c, y):
        heads_ref[:, c, :] = y
        cols_ref[:, c * LANES:(c + 1) * LANES] = y.astype(cols_ref.dtype)
    return put


def _proj_body(x_ref, g_ref, wt_ref, wg2_ref, bg_ref, qn_ref, kn_ref, mqn_ref,
               gq_ref, gk_ref, gv_ref, gr_ref, gg_ref, dq_ref, dk_ref, dv_ref, mq_ref, dkb_ref, dvb_ref, xn_ref):
    xn_ref[...] = _rms(x_ref[...], g_ref[...]).astype(BF16)

    def cols(start, width):
        return _dot_nt(xn_ref[...], wt_ref[start:start + width, :])

    def piece(ref, a, c0):
        return ref.at[:, c0 * LANES:c0 * LANES + a.shape[1]]

    def put_gq(a, c0):
        piece(gq_ref, a, c0)[...] = a * (GLA_DK ** -0.5)

    def put_gk(a, c0):
        piece(gk_ref, a, c0)[...] = a

    def put_gv(a, c0):
        piece(gv_ref, a, c0)[...] = a.astype(BF16)

    def put_gr(a, c0):
        piece(gr_ref, a, c0)[...] = a

    def put_gate(glr, c0):
        z = _dot(glr.astype(BF16), wg2_ref[...]) + bg_ref[...]
        gg_ref[...] = jax.nn.log_sigmoid(z) / GLA_GATE_NORM

    def put_dq(a, c0):
        _group_rms(a, qn_ref, DIFF_DH, DIFF_DH ** -0.5, _put_cols(dq_ref), c0)

    def put_dk(a, c0):
        _group_rms(a, kn_ref, DIFF_DH, 1.0, _put_heads_and_cols(dk_ref, dkb_ref), c0)

    def put_dv(a, c0):
        put = _put_heads_and_cols(dv_ref, dvb_ref)
        for i in range(a.shape[1] // LANES):
            put(c0 + i, a[:, i * LANES:(i + 1) * LANES])

    def put_mq(a, c0):
        _group_rms(a, mqn_ref, MEM_DH, 1.0, _put_cols(mq_ref), c0)

    groups = [(_W_GQ, GLA_QK, put_gq), (_W_GK, GLA_QK, put_gk), (_W_GV, GLA_V, put_gv), (_W_GR, GLA_V, put_gr),
              (_W_GLR, LANES, put_gate), (_W_DQ, DIFF_QK, put_dq), (_W_DK, DIFF_QK, put_dk), (_W_DV, DIFF_V, put_dv),
              (_W_MQ, MEM_W, put_mq)]
    pieces = [(start + off, min(PROJ_DOT_COLS, width - off), finish, off // LANES)
              for start, width, finish in groups for off in range(0, width, PROJ_DOT_COLS)]
    for start, width, finish, c0 in pieces:
        finish(cols(start, width), c0)


def _proj(x, gain, wt, wg2, bg, qn, kn, mqn):
    n = x.shape[0]
    tm = min(PROJ_TILE, n)
    const = lambda shape: pl.BlockSpec(shape, lambda i: (0, 0))
    out = lambda width: pl.BlockSpec((tm, width), lambda i: (i, 0))
    heads = pl.BlockSpec((tm, HEADS, LANES), lambda i: (i, 0, 0))
    shp = lambda width, dt: jax.ShapeDtypeStruct((n, width), dt)
    shp_heads = jax.ShapeDtypeStruct((n, HEADS, LANES), F32)
    return pl.pallas_call(
        _proj_body,
        grid=(n // tm,),
        in_specs=[
            pl.BlockSpec((tm, D_MODEL), lambda i: (i, 0)),
            const((1, D_MODEL)),
            pl.BlockSpec((IN_WIDTH, D_MODEL), lambda i: (0, 0), pipeline_mode=pl.Buffered(1)),
            const((LANES, GLA_QK)),
            const((1, GLA_QK)),
            const((1, DIFF_QK)),
            const((1, DIFF_QK)),
            const((1, MEM_W)),
        ],
        out_specs=[out(GLA_QK), out(GLA_QK), out(GLA_V), out(GLA_V), out(GLA_QK),
                   out(DIFF_QK), heads, heads, out(MEM_W), out(DIFF_QK), out(DIFF_V)],
        out_shape=[shp(GLA_QK, F32), shp(GLA_QK, F32), shp(GLA_V, BF16), shp(GLA_V, F32), shp(GLA_QK, F32),
                   shp(DIFF_QK, BF16), shp_heads, shp_heads, shp(MEM_W, BF16), shp(DIFF_QK, BF16), shp(DIFF_V, BF16)],
        scratch_shapes=[pltpu.VMEM((tm, D_MODEL), BF16)],
        compiler_params=_params(("parallel",)),
        name="proj",
    )(x, gain, wt, wg2, bg, qn, kn, mqn)


def _memkv_body(x_ref, g_ref, w_ref, kn_ref, k_ref, v_ref, kb_ref, vb_ref):
    xn = _rms(x_ref[...], g_ref[...]).astype(BF16)
    _group_rms(_dot(xn, w_ref[:, :MEM_W]), kn_ref, MEM_DH, 1.0, _put_heads_and_cols(k_ref, kb_ref))
    v = _dot(xn, w_ref[:, MEM_W:])
    put_v = _put_heads_and_cols(v_ref, vb_ref)
    for h in range(HEADS):
        put_v(h, v[:, h * LANES:(h + 1) * LANES])


def _memkv(mem, gain, w, kn):
    n = mem.shape[0]
    tm = min(TOK_TILE, n)
    const = lambda shape: pl.BlockSpec(shape, lambda i: (0, 0))
    heads = pl.BlockSpec((tm, HEADS, LANES), lambda i: (i, 0, 0))
    dense = pl.BlockSpec((tm, MEM_W), lambda i: (i, 0))
    return pl.pallas_call(
        _memkv_body,
        grid=(n // tm,),
        in_specs=[pl.BlockSpec((tm, D_MODEL), lambda i: (i, 0)), const((1, D_MODEL)),
                  const((D_MODEL, 2 * MEM_W)), const((1, MEM_W))],
        out_specs=[heads, heads, dense, dense],
        out_shape=[jax.ShapeDtypeStruct((n, HEADS, LANES), F32)] * 2 + [jax.ShapeDtypeStruct((n, MEM_W), BF16)] * 2,
        compiler_params=_params(("parallel",)),
        name="memkv",
    )(mem, gain, w, kn)


def _split3(x):
    hi = x.astype(BF16)
    r1 = x - hi.astype(F32)
    mid = r1.astype(BF16)
    lo = (r1 - mid.astype(F32)).astype(BF16)
    return hi, mid, lo


def _gla_body(*refs, chunk, n_chunks, n_steps, has_state):
    if has_state:
        gq_ref, gk_ref, gg_ref, gv_ref, gr_ref, on_ref, s0_ref, go_ref, st_ref, state = refs
    else:
        gq_ref, gk_ref, gg_ref, gv_ref, gr_ref, on_ref, go_ref, st_ref, state = refs
    t = pl.program_id(1)

    @pl.when(t == 0)
    def _():
        for h in range(GLA_HEADS):
            if has_state:
                state[h] = s0_ref[0, h].T
            else:
                state[h] = jnp.zeros((GLA_DV, GLA_DK), F32)

    row = lax.broadcasted_iota(jnp.int32, (chunk, chunk), 0)
    col = lax.broadcasted_iota(jnp.int32, (chunk, chunk), 1)
    causal = row >= col
    tril = causal.astype(BF16)

    chunks = [slice(c * chunk, (c + 1) * chunk) for c in range(n_chunks)]
    kcs = [slice(h * GLA_DK, (h + 1) * GLA_DK) for h in range(GLA_HEADS)]
    vcs = [slice(h * GLA_DV, (h + 1) * GLA_DV) for h in range(GLA_HEADS)]

    bs = []
    for rows in chunks:
        g_hi, g_mid, g_lo = _split3(gg_ref[rows, :])
        bs.append(_dot(tril, g_hi) + _dot(tril, g_mid) + _dot(tril, g_lo))

    qes, kes, kds, decays = [], [], [], []
    for rows, b in zip(chunks, bs):
        b_last = b[chunk - 1:chunk, :]
        q = gq_ref[rows, :]
        k = gk_ref[rows, :]
        qes.append((q * jnp.exp(b)).astype(BF16))
        kes.append((k * jnp.exp(-b)).astype(BF16))
        kds.append((k * jnp.exp(b_last - b)).astype(BF16))
        decays.append(jnp.exp(b_last))

    a_s = [[jnp.where(causal, _dot_nt(qe[:, kc], ke[:, kc]), 0.0).astype(BF16) for kc in kcs]
           for qe, ke in zip(qes, kes)]
    incs = [[_dot_tn(gv_ref[rows, vc], kd[:, kc]) for kc, vc in zip(kcs, vcs)] for rows, kd in zip(chunks, kds)]

    s_in = []
    s_cur = [state[h] for h in range(GLA_HEADS)]
    for c in range(n_chunks):
        s_in.append([s.astype(BF16) for s in s_cur])
        s_cur = [s * decays[c][:, kc] + inc for s, kc, inc in zip(s_cur, kcs, incs[c])]
    for h in range(GLA_HEADS):
        state[h] = s_cur[h]

    for c, rows in enumerate(chunks):
        for h, (kc, vc) in enumerate(zip(kcs, vcs)):
            o = _dot_nt(qes[c][:, kc], s_in[c][h]) + _dot(a_s[c][h], gv_ref[rows, vc])
            go_ref[rows, vc] = (_rms(o, on_ref[...]) * jax.nn.silu(gr_ref[rows, vc])).astype(BF16)

    @pl.when(t == n_steps - 1)
    def _():
        for h in range(GLA_HEADS):
            st_ref[0, h] = state[h].T


def _gla(gq, gk, gg, gv, gr, onorm, s0, batch, seq, chunk):
    tt = min(TOK_TILE, seq)
    n_steps = seq // tt
    tok = lambda width: pl.BlockSpec((tt, width), lambda b, t: (b * n_steps + t, 0))
    st_spec = pl.BlockSpec((1, GLA_HEADS, GLA_DK, GLA_DV), lambda b, t: (b, 0, 0, 0))
    in_specs = [tok(GLA_QK), tok(GLA_QK), tok(GLA_QK), tok(GLA_V), tok(GLA_V),
                pl.BlockSpec((1, GLA_DV), lambda b, t: (0, 0))]
    args = [gq, gk, gg, gv, gr, onorm]
    if s0 is not None:
        in_specs.append(st_spec)
        args.append(s0)
    return pl.pallas_call(
        functools.partial(_gla_body, chunk=chunk, n_chunks=tt // chunk, n_steps=n_steps, has_state=s0 is not None),
        grid=(batch, n_steps),
        in_specs=in_specs,
        out_specs=[tok(GLA_V), st_spec],
        out_shape=[jax.ShapeDtypeStruct((batch * seq, GLA_V), BF16),
                   jax.ShapeDtypeStruct((batch, GLA_HEADS, GLA_DK, GLA_DV), F32)],
        scratch_shapes=[pltpu.VMEM((GLA_HEADS, GLA_DV, GLA_DK), F32)],
        compiler_params=_params(("parallel", "arbitrary")),
        name="gla",
    )(*args)


def _t5_bucket(rel):
    nb = REL_BUCKETS // 2
    max_exact = nb // 2
    ret = jnp.where(rel > 0, nb, 0)
    n = jnp.abs(rel)
    nf = jnp.maximum(n, 1).astype(F32)
    large = max_exact + (jnp.log(nf / max_exact) / math.log(REL_MAX_DIST / max_exact)
                         * (nb - max_exact)).astype(jnp.int32)
    large = jnp.minimum(large, nb - 1)
    return ret + jnp.where(n < max_exact, n, large)


def _bucket_tile(q_pos, k_pos):
    visible = (k_pos[None, :] // CHUNK) <= (q_pos[:, None] // CHUNK)
    return jnp.where(visible, _t5_bucket(k_pos[None, :] - q_pos[:, None]), MASKED_BUCKET).astype(jnp.int32)


def _bias_from_buckets(idx, tab_ref, head):
    def step(bk, acc):
        return jnp.where(idx == bk, tab_ref[bk * DIFF_HEADS + head], acc)
    return lax.fori_loop(0, REL_BUCKETS, step, jnp.full(idx.shape, NEG_INF, F32))


def _lambda(lam_ref, lam_init):
    l = lam_ref[...]
    return (jnp.exp(jnp.sum(l[0:1] * l[1:2], axis=-1, keepdims=True))
            - jnp.exp(jnp.sum(l[2:3] * l[3:4], axis=-1, keepdims=True)) + lam_init)


def _comp_masks(q):
    lane = lax.broadcasted_iota(jnp.int32, q.shape, 1)
    zero = jnp.zeros_like(q)
    return jnp.where(lane < DIFF_DH, q, zero), jnp.where(lane < DIFF_DH, zero, q)


def _scores(q_c, parts):
    return [_dot_nt(q_c, k) + bias if jnp.ndim(bias) == 2 else _dot_nt(q_c, k) for k, _, bias in parts]


def _softmax_pv(scores, parts):
    shifts = [0.0 if jnp.ndim(bias) == 2 else bias for _, _, bias in parts]
    m = None
    for s, shift in zip(scores, shifts):
        part_max = s.max(axis=-1, keepdims=True) + shift
        m = part_max if m is None else jnp.maximum(m, part_max)
    l = 0.0
    o = 0.0
    for s, shift, (_, v, _) in zip(scores, shifts, parts):
        e = jnp.exp(s - (m - shift))
        l = l + e.sum(axis=-1, keepdims=True)
        o = o + _dot(e.astype(BF16), v)
    return o * (1.0 / l)


def _diff_finish(o0, o1, lam, on_ref, lam_init, out_dtype):
    o = o0 - lam * o1
    return (_rms(o, on_ref[...]) * (1.0 - lam_init)).astype(out_dtype)


def _diff_prompt_body(tab_ref, q_ref, kb, vb, idx_ref, lam_ref, on_ref, o_ref, bias, *, seq, lam_init):
    b = pl.program_id(0)
    h = pl.program_id(1)

    @pl.when(b == 0)
    def _():
        for t in range(2):
            bias[h, t] = _bias_from_buckets(idx_ref[t], tab_ref, h)

    lam = _lambda(lam_ref, lam_init)
    far_bias = tab_ref[(REL_BUCKETS // 2 - 1) * DIFF_HEADS + h]

    def key_parts(i):
        near0 = max(i - 1, 0) * Q_BLOCK
        near = slice(near0, near0 + 2 * Q_BLOCK)
        parts = [(kb[near, :], vb[near, :], bias[h, min(i, 1)])]
        if near0 > 0:
            parts.append((kb[0:near0, :], vb[0:near0, :], far_bias))
        return parts

    items = [(i, c) for i in range(seq // Q_BLOCK) for c in range(2)]
    comps = {}

    def scores_of(item):
        i, c = item
        if i not in comps:
            comps[i] = _comp_masks(q_ref[i * Q_BLOCK:(i + 1) * Q_BLOCK, :])
        return _scores(comps[i][c], key_parts(i))

    outs = {}
    ahead = scores_of(items[0])
    for n, (i, c) in enumerate(items):
        cur = ahead
        if n + 1 < len(items):
            ahead = scores_of(items[n + 1])
        outs[c] = _softmax_pv(cur, key_parts(i))
        if c == 1:
            o_ref[i * Q_BLOCK:(i + 1) * Q_BLOCK, :] = _diff_finish(outs[0], outs[1], lam, on_ref, lam_init,
                                                                    o_ref.dtype)


def _diff_prompt(table, dq, dk, dv, lam_p, onorm, batch, seq, lam_init):
    q_pos = jnp.arange(Q_BLOCK)
    idx = jnp.stack([_bucket_tile(q_pos, jnp.arange(2 * Q_BLOCK)),
                     _bucket_tile(q_pos + Q_BLOCK, jnp.arange(2 * Q_BLOCK))])
    head = lambda: pl.BlockSpec((seq, LANES), lambda b, h: (b, h))
    return pl.pallas_call(
        functools.partial(_diff_prompt_body, seq=seq, lam_init=lam_init),
        grid=(batch, DIFF_HEADS),
        in_specs=[
            pl.BlockSpec(memory_space=pltpu.SMEM),
            head(), head(), head(),
            pl.BlockSpec((2, Q_BLOCK, 2 * Q_BLOCK), lambda b, h: (0, 0, 0)),
            pl.BlockSpec((4, DIFF_DH), lambda b, h: (0, 0)),
            pl.BlockSpec((1, DIFF_DV), lambda b, h: (0, 0)),
        ],
        out_specs=head(),
        out_shape=jax.ShapeDtypeStruct((batch * seq, DIFF_V), BF16),
        scratch_shapes=[pltpu.VMEM((DIFF_HEADS, 2, Q_BLOCK, 2 * Q_BLOCK), F32)],
        compiler_params=_params(("arbitrary", "arbitrary")),
        name="diff_prompt",
    )(table, dq, dk, dv, idx, lam_p, onorm)


def _diff_sample_body(tab_ref, q_ref, k_ref, v_ref, ck_ref, cv_ref, idxc_ref, idxn_ref, lam_ref, on_ref, o_ref,
                      bias_c, bias_n, *, seq, lam_init):
    groups = [(h, c) for h in range(DIFF_HEADS) for c in range(2)]

    @pl.when(pl.program_id(0) == 0)
    def _():
        for g, (h, _) in enumerate(groups):
            bias_c[g * seq:(g + 1) * seq, :] = _bias_from_buckets(idxc_ref[h], tab_ref, h)
        for h in range(DIFF_HEADS):
            bias_n[h] = _bias_from_buckets(idxn_ref[...], tab_ref, h)

    lam = _lambda(lam_ref, lam_init)
    head_cols = [slice(h * LANES, (h + 1) * LANES) for h in range(DIFF_HEADS)]
    q_groups = [q_c for cols in head_cols for q_c in _comp_masks(q_ref[:, cols])]
    s_old = _dot_nt(jnp.concatenate(q_groups, axis=0), ck_ref[...].astype(BF16)) + bias_c[...]
    s_new = jnp.concatenate([_dot_nt(q_g, k_ref[:, head_cols[h]]) + bias_n[h]
                             for q_g, (h, _) in zip(q_groups, groups)], axis=0)
    m = jnp.maximum(s_old.max(axis=-1, keepdims=True), s_new.max(axis=-1, keepdims=True))
    e_old = jnp.exp(s_old - m)
    e_new = jnp.exp(s_new - m)
    l = e_old.sum(axis=-1, keepdims=True) + e_new.sum(axis=-1, keepdims=True)
    o_old = _dot(e_old.astype(BF16), cv_ref[...].astype(BF16))
    e_new = e_new.astype(BF16)
    o_new = jnp.concatenate([_dot(e_new[g * seq:(g + 1) * seq, :], v_ref[:, head_cols[h]])
                             for g, (h, _) in enumerate(groups)], axis=0)
    o = (o_old + o_new) * (1.0 / l)
    for h, cols in enumerate(head_cols):
        o0 = o[(2 * h) * seq:(2 * h + 1) * seq, :]
        o1 = o[(2 * h + 1) * seq:(2 * h + 2) * seq, :]
        o_ref[:, cols] = _diff_finish(o0, o1, lam, on_ref, lam_init, o_ref.dtype)


def _diff_sample(table, dq, dk, dv, cache_k, cache_v, lam_p, onorm, batch, seq, past, lam_init):
    q_pos = past + jnp.arange(seq)
    idx_n = _bucket_tile(q_pos, past + jnp.arange(seq))
    idx_rows = jnp.repeat(_bucket_tile(q_pos, jnp.arange(past)), HEADS, axis=1)
    row_head = jnp.arange(past * HEADS) % HEADS
    idx_c = jnp.where(row_head[None, None, :] == jnp.arange(HEADS)[:, None, None], idx_rows[None], MASKED_BUCKET)
    new_q = pl.BlockSpec((seq, DIFF_QK), lambda b: (b, 0))
    old = lambda: pl.BlockSpec((past * HEADS, LANES), lambda b: (b, 0))
    const = lambda shape: pl.BlockSpec(shape, lambda b: (0,) * len(shape))
    n_groups = 2 * DIFF_HEADS
    return pl.pallas_call(
        functools.partial(_diff_sample_body, seq=seq, lam_init=lam_init),
        grid=(batch,),
        in_specs=[pl.BlockSpec(memory_space=pltpu.SMEM), new_q, new_q, new_q, old(), old(),
                  const((HEADS, seq, past * HEADS)), const((seq, seq)), const((4, DIFF_DH)), const((1, DIFF_DV))],
        out_specs=new_q,
        out_shape=jax.ShapeDtypeStruct((batch * seq, DIFF_V), BF16),
        scratch_shapes=[pltpu.VMEM((n_groups * seq, past * HEADS), F32), pltpu.VMEM((DIFF_HEADS, seq, seq), F32)],
        compiler_params=_params(("arbitrary",)),
        name="diff_sample",
    )(table, dq, dk, dv, cache_k, cache_v, idx_c, idx_n, lam_p, onorm)


def _memattn_body(q_ref, k_ref, v_ref, o_ref, *, per_head):
    for h in range(MEM_HEADS):
        cols = slice(h * MEM_DH, (h + 1) * MEM_DH)
        if per_head:
            k, v = k_ref[:, h, :].astype(BF16), v_ref[:, h, :].astype(BF16)
        else:
            k, v = k_ref[:, cols], v_ref[:, cols]
        s = _dot_nt(q_ref[:, cols], k) * (MEM_DH ** -0.5)
        e = jnp.exp(s - s.max(axis=-1, keepdims=True))
        p = e * (1.0 / e.sum(axis=-1, keepdims=True))
        o_ref[:, cols] = _dot(p.astype(BF16), v).astype(o_ref.dtype)


def _memattn(mq, mk, mv, batch, seq, n_mem):
    tq = min(TOK_TILE, seq)
    nq = seq // tq
    q_spec = pl.BlockSpec((tq, MEM_W), lambda b, i: (b * nq + i, 0))
    per_head = mk.ndim == 3
    if per_head:
        kv_spec = pl.BlockSpec((n_mem, HEADS, LANES), lambda b, i: (b, 0, 0))
    else:
        kv_spec = pl.BlockSpec((n_mem, MEM_W), lambda b, i: (b, 0))
    return pl.pallas_call(
        functools.partial(_memattn_body, per_head=per_head),
        grid=(batch, nq),
        in_specs=[q_spec, kv_spec, kv_spec],
        out_specs=q_spec,
        out_shape=jax.ShapeDtypeStruct((batch * seq, MEM_W), BF16),
        compiler_params=_params(("parallel", "arbitrary")),
        name="memattn",
    )(mq, mk, mv)


def _outproj_body(x_ref, g_ref, d_ref, m_ref, wg_ref, wd_ref, wm_ref, o_ref):
    o_ref[...] = (x_ref[...] + _dot(g_ref[...], wg_ref[...]) + _dot(d_ref[...], wd_ref[...])
                  + _dot(m_ref[...], wm_ref[...]))


def _outproj(x, g, d, m, wo):
    n = x.shape[0]
    tm = min(TOK_TILE, n)
    tok = lambda width: pl.BlockSpec((tm, width), lambda i: (i, 0))
    w_rows = lambda rows, blk: pl.BlockSpec((rows, D_MODEL), lambda i: (blk, 0))
    return pl.pallas_call(
        _outproj_body,
        grid=(n // tm,),
        in_specs=[tok(D_MODEL), tok(GLA_V), tok(DIFF_V), tok(MEM_W),
                  w_rows(GLA_V, 0), w_rows(DIFF_V, GLA_V // DIFF_V), w_rows(MEM_W, (GLA_V + DIFF_V) // MEM_W)],
        out_specs=tok(D_MODEL),
        out_shape=jax.ShapeDtypeStruct((n, D_MODEL), F32),
        compiler_params=_params(("parallel",)),
        name="outproj",
    )(x, g, d, m, wo, wo, wo)


def _row(v):
    return v.reshape(1, -1).astype(F32)


def kernel(x_prompt, x_sample, mem_prompt, cache_diff_k, cache_diff_v, state_gla, cache_mem_k, cache_mem_v, rel_bias_table, norm_ffn1, w_ffn1_in, w_ffn1_out, norm_mix, w_in, w_gla_g2, b_gla_g, gla_out_norm, diff_q_norm, diff_k_norm, diff_lambda, diff_out_norm, mem_norm, w_mem_kv, mem_q_norm, mem_k_norm, w_o, norm_ffn2, w_ffn2_in, w_ffn2_out, norm_final):
    depth = norm_ffn1.shape[0]
    assert depth == 1, "single-layer step"
    layer = 0
    batch, seq, _ = x_prompt.shape
    dec_batch, dec_seq, _ = x_sample.shape
    past = cache_diff_k.shape[2]
    n_mem = mem_prompt.shape[1]
    lam_init = 0.8 - 0.6 * math.exp(-0.3 * layer)

    ffn1 = (w_ffn1_in[layer].astype(BF16), w_ffn1_out[layer].astype(BF16))
    w_g2 = jnp.pad(w_gla_g2[layer].astype(BF16), ((0, LANES - GLA_GATE_RANK), (0, 0)))
    b_g = _row(b_gla_g[layer])
    qn = _row(jnp.tile(diff_q_norm[layer], DIFF_QK // DIFF_DH))
    kn = _row(jnp.tile(diff_k_norm[layer], DIFF_QK // DIFF_DH))
    mqn = _row(jnp.tile(mem_q_norm[layer], MEM_HEADS))
    mkn = _row(jnp.tile(mem_k_norm[layer], MEM_HEADS))
    wo = w_o[layer].astype(BF16)
    table = rel_bias_table.astype(F32).reshape(-1)
    lam_p = diff_lambda[layer].astype(F32)
    gla_on = _row(gla_out_norm[layer])
    diff_on = _row(diff_out_norm[layer])
    per_head = lambda a: a.reshape(-1, HEADS, LANES)

    mk, mv, mk_b, mv_b = _memkv(mem_prompt.reshape(batch * n_mem, D_MODEL), _row(mem_norm[layer]),
                                w_mem_kv[layer].astype(BF16), mkn)

    n1 = _row(norm_ffn1[layer])
    xs1 = _ffn(x_sample.reshape(dec_batch * dec_seq, D_MODEL), n1, *ffn1)
    n_in_blocks = w_ffn2_in.shape[2] // LANES
    xp1, *ffn2 = _ffn(x_prompt.reshape(batch * seq, D_MODEL), n1, *ffn1,
                      cast_later=[(w_ffn2_in[layer], (D_MODEL, LANES), 0),
                                  (w_ffn2_out[layer], (D_FF // n_in_blocks, D_MODEL), n_in_blocks)])
    w_t = jnp.swapaxes(w_in[layer], 0, 1).astype(BF16)

    def layer_fn(x, b, t, chunk, s0, diff_fn, mem_k, mem_v):
        gq, gk, gv, gr, gg, dq, dk, dv, mq, dk_b, dv_b = _proj(x, _row(norm_mix[layer]), w_t, w_g2, b_g,
                                                               qn, kn, mqn)
        g_out, g_state = _gla(gq, gk, gg, gv, gr, gla_on, s0, b, t, chunk)
        d_out = diff_fn(dq, dk_b, dv_b)
        m_out = _memattn(mq, mem_k, mem_v, b, t, n_mem)
        x = _outproj(x, g_out, d_out, m_out, wo)
        x = _ffn(x, _row(norm_ffn2[layer]), *ffn2, final_gain=_row(norm_final[layer]))
        return x, dk, dv, g_state

    yp, dk_p, dv_p, g_p = layer_fn(
        xp1, batch, seq, CHUNK, None,
        lambda dq, dk, dv: _diff_prompt(table, dq, dk, dv, lam_p, diff_on, batch, seq, lam_init), mk_b, mv_b)
    ys, dk_s, dv_s, g_s = layer_fn(
        xs1, dec_batch, dec_seq, dec_seq, state_gla[layer],
        lambda dq, dk, dv: _diff_sample(table, dq, dk, dv, cache_diff_k[layer].reshape(-1, LANES),
                                        cache_diff_v[layer].reshape(-1, LANES),
                                        lam_p, diff_on, dec_batch, dec_seq, past, lam_init),
        per_head(cache_mem_k[layer]), per_head(cache_mem_v[layer]))

    head4 = lambda a, b, t: a.reshape(1, b, t, HEADS, LANES)
    return (yp.reshape(batch, seq, D_MODEL), ys.reshape(dec_batch, dec_seq, D_MODEL),
            head4(dk_p, batch, seq), head4(dv_p, batch, seq), g_p[None],
            head4(mk, batch, n_mem), head4(mv, batch, n_mem),
            head4(dk_s, dec_batch, dec_seq), head4(dv_s, dec_batch, dec_seq), g_s[None])
```

```python
import functools
import math

import jax
import jax.numpy as jnp
from jax import lax
from jax.experimental import pallas as pl
from jax.experimental.pallas import tpu as pltpu

F32 = jnp.float32
BF16 = jnp.bfloat16

D_MODEL = 2048
CHUNK = 64
EPS = 1e-6
NEG_INF = -1e30
GLA_HEADS, GLA_DK, GLA_DV, GLA_GATE_RANK, GLA_GATE_NORM = 4, 128, 256, 16, 16.0
DIFF_HEADS, DIFF_DH, DIFF_DV = 4, 64, 128
MEM_HEADS, MEM_DH = 4, 128
REL_BUCKETS, REL_MAX_DIST = 32, 128
D_FF = 5504
GLA_QK = GLA_HEADS * GLA_DK
GLA_V = GLA_HEADS * GLA_DV
DIFF_QK = DIFF_HEADS * 2 * DIFF_DH
DIFF_V = DIFF_HEADS * DIFF_DV
MEM_W = MEM_HEADS * MEM_DH
GLR_OFF = 2 * GLA_QK + 2 * GLA_V

LANES = 128
HEADS = 4
FF_TILE = 512
N_FF_TILES = -(-D_FF // FF_TILE)
FF_LAST_START = D_FF - FF_TILE
FF_OVERLAP = N_FF_TILES * FF_TILE - D_FF
FFN_TOK_TILE = 1024
TOK_TILE = 512
PROJ_TILE = 256
PROJ_DOT_COLS = 256
Q_BLOCK = 256
MASKED_BUCKET = REL_BUCKETS
MIB = 1024 * 1024
VMEM_CAP_MIB = 60

_W_GQ, _W_GK, _W_GV, _W_GR, _W_GLR = 0, GLA_QK, 2 * GLA_QK, 2 * GLA_QK + GLA_V, GLR_OFF
_W_DQ = GLR_OFF + GLA_GATE_RANK
_W_DK = _W_DQ + DIFF_QK
_W_DV = _W_DK + DIFF_QK
_W_MQ = _W_DV + DIFF_V
IN_WIDTH = _W_MQ + MEM_W


def _dot(a, b):
    return jnp.dot(a, b, preferred_element_type=F32)


def _dot_nt(a, b):
    return lax.dot_general(a, b, (((1,), (1,)), ((), ())), preferred_element_type=F32)


def _dot_tn(a, b):
    return lax.dot_general(a, b, (((0,), (0,)), ((), ())), preferred_element_type=F32)


def _rms(x, gain):
    return x * lax.rsqrt(jnp.mean(x * x, axis=-1, keepdims=True) + EPS) * gain


def _params(sem, vmem_mib=48):
    return pltpu.CompilerParams(dimension_semantics=sem, vmem_limit_bytes=min(vmem_mib, VMEM_CAP_MIB) * MIB)


def _ffn_body(*refs, n_ff, final_norm, n_cast, prenormed):
    x_ref, g_ref, wg_ref, wu_ref, wo_ref = refs[:5]
    refs = refs[5:]
    if final_norm:
        fg_ref, refs = refs[0], refs[1:]
    cast_src, refs = refs[:n_cast], refs[n_cast:]
    o_ref, refs = refs[0], refs[1:]
    cast_dst, refs = refs[:n_cast], refs[n_cast:]
    xn_ref = g_ref if prenormed else refs[0]
    j = pl.program_id(1)

    @pl.when(j == 0)
    def _():
        x = x_ref[...]
        if not prenormed:
            xn_ref[...] = _rms(x, g_ref[...]).astype(BF16)
        o_ref[...] = x

    def tile(lo):
        for src, dst in zip(cast_src, cast_dst):
            dst[...] = src[...].astype(BF16)
        xn = xn_ref[...]
        gate = _dot(xn, wg_ref[:, lo:])
        up = _dot(xn, wu_ref[:, lo:])
        act = (jax.nn.silu(gate) * up).astype(BF16)
        return 0.5 * _dot(act, wo_ref[lo:, :])

    @pl.when(j < n_ff - 1)
    def _():
        o_ref[...] += tile(0)

    @pl.when(j == n_ff - 1)
    def _():
        o = o_ref[...] + tile(FF_OVERLAP)
        o_ref[...] = _rms(o, fg_ref[...]) if final_norm else o


def _ffn(x, gain_or_xn, w_in, w_out, final_gain=None, cast_later=()):
    n = x.shape[0]
    tm = min(FFN_TOK_TILE, n)
    n_ff = N_FF_TILES
    row = pl.BlockSpec((1, D_MODEL), lambda i, j: (0, 0))
    prenormed = gain_or_xn.shape != (1, D_MODEL)
    start = lambda j: jnp.minimum(j * (FF_TILE // LANES), FF_LAST_START // LANES)
    w_in_tile = (pl.Element(D_MODEL), pl.Element(FF_TILE))
    tok = pl.BlockSpec((tm, D_MODEL), lambda i, j: (i, 0))
    in_specs = [
        tok,
        tok if prenormed else row,
        pl.BlockSpec(w_in_tile, lambda i, j: (0, start(j) * LANES)),
        pl.BlockSpec(w_in_tile, lambda i, j: (0, (D_FF // LANES + start(j)) * LANES)),
        pl.BlockSpec((pl.Element(FF_TILE), pl.Element(D_MODEL)), lambda i, j: (start(j) * LANES, 0)),
    ]
    args = [x, gain_or_xn, w_in, w_in, w_out]
    if final_gain is not None:
        in_specs.append(row)
        args.append(final_gain)
    vmem_bytes = (4 * tm * D_MODEL * 4 + (2 if prenormed else 1) * tm * D_MODEL * 2 + 6 * D_MODEL * FF_TILE * 2
                  + 3 * tm * FF_TILE * 4)
    out_specs = [pl.BlockSpec((tm, D_MODEL), lambda i, j: (i, 0))]
    out_shape = [jax.ShapeDtypeStruct((n, D_MODEL), F32)]
    n_steps = (n // tm) * n_ff
    for mat, blk, first in cast_later:
        rows, cols = mat.shape
        assert rows % blk[0] == 0 and cols % blk[1] == 0 and (blk[0] == rows or blk[1] == cols)
        n_blk = (rows // blk[0]) * (cols // blk[1])
        assert first + n_blk <= n_steps
        by_rows = blk[1] == cols

        def index(i, j, first=first, n_blk=n_blk, by_rows=by_rows):
            k = jnp.clip(i * n_ff + j - first, 0, n_blk - 1)
            return (k, 0) if by_rows else (0, k)

        in_specs.append(pl.BlockSpec(blk, index))
        args.append(mat)
        out_specs.append(pl.BlockSpec(blk, index))
        out_shape.append(jax.ShapeDtypeStruct(mat.shape, BF16))
        vmem_bytes += 2 * blk[0] * blk[1] * (4 + 2)
    outs = pl.pallas_call(
        functools.partial(_ffn_body, n_ff=n_ff, final_norm=final_gain is not None, n_cast=len(cast_later),
                          prenormed=prenormed),
        grid=(n // tm, n_ff),
        in_specs=in_specs,
        out_specs=out_specs,
        out_shape=out_shape,
        scratch_shapes=[] if prenormed else [pltpu.VMEM((tm, D_MODEL), BF16)],
        compiler_params=_params(("arbitrary", "arbitrary"), vmem_bytes // MIB + 4),
        name="ffn",
    )(*args)
    return outs[0] if not cast_later else outs


def _group_rms(acc, gain_ref, width, scale, put, c0=0):
    lane = lax.broadcasted_iota(jnp.int32, (1, LANES), 1)
    low = lane < width
    for i in range(acc.shape[1] // LANES):
        c = c0 + i
        cols = slice(c * LANES, (c + 1) * LANES)
        xc = acc[:, i * LANES:(i + 1) * LANES]
        sq = xc * xc
        if width == LANES:
            r = lax.rsqrt(jnp.mean(sq, axis=-1, keepdims=True) + EPS)
        else:
            s_lo = jnp.sum(jnp.where(low, sq, 0.0), axis=-1, keepdims=True)
            s_hi = jnp.sum(jnp.where(low, 0.0, sq), axis=-1, keepdims=True)
            r = jnp.where(low, lax.rsqrt(s_lo / width + EPS), lax.rsqrt(s_hi / width + EPS))
        y = xc * r * gain_ref[:, cols]
        if scale != 1.0:
            y = y * scale
        put(c, y)


def _put_cols(ref):
    def put(c, y):
        ref[:, c * LANES:(c + 1) * LANES] = y.astype(ref.dtype)
    return put


def _put_heads_and_cols(heads_ref, cols_ref):
    def put(c, y):
        heads_ref[:, c, :] = y
        cols_ref[:, c * LANES:(c + 1) * LANES] = y.astype(cols_ref.dtype)
    return put


def _proj_body(x_ref, g_ref, wt_ref, wg2_ref, bg_ref, qn_ref, kn_ref, mqn_ref,
               gq_ref, gk_ref, gv_ref, gr_ref, gg_ref, dq_ref, dk_ref, dv_ref, mq_ref, dkb_ref, dvb_ref, xn_ref):
    xn_ref[...] = _rms(x_ref[...], g_ref[...]).astype(BF16)

    def cols(start, width):
        return _dot_nt(xn_ref[...], wt_ref[start:start + width, :])

    def piece(ref, a, c0):
        return ref.at[:, c0 * LANES:c0 * LANES + a.shape[1]]

    def put_gq(a, c0):
        piece(gq_ref, a, c0)[...] = a * (GLA_DK ** -0.5)

    def put_gk(a, c0):
        piece(gk_ref, a, c0)[...] = a

    def put_gv(a, c0):
        piece(gv_ref, a, c0)[...] = a.astype(BF16)

    def put_gr(a, c0):
        piece(gr_ref, a, c0)[...] = a

    def put_gate(glr, c0):
        z = _dot(glr.astype(BF16), wg2_ref[...]) + bg_ref[...]
        gg_ref[...] = jax.nn.log_sigmoid(z) / GLA_GATE_NORM

    def put_dq(a, c0):
        _group_rms(a, qn_ref, DIFF_DH, DIFF_DH ** -0.5, _put_cols(dq_ref), c0)

    def put_dk(a, c0):
        _group_rms(a, kn_ref, DIFF_DH, 1.0, _put_heads_and_cols(dk_ref, dkb_ref), c0)

    def put_dv(a, c0):
        put = _put_heads_and_cols(dv_ref, dvb_ref)
        for i in range(a.shape[1] // LANES):
            put(c0 + i, a[:, i * LANES:(i + 1) * LANES])

    def put_mq(a, c0):
        _group_rms(a, mqn_ref, MEM_DH, 1.0, _put_cols(mq_ref), c0)

    groups = [(_W_GQ, GLA_QK, put_gq), (_W_GK, GLA_QK, put_gk), (_W_GV, GLA_V, put_gv), (_W_GR, GLA_V, put_gr),
              (_W_GLR, LANES, put_gate), (_W_DQ, DIFF_QK, put_dq), (_W_DK, DIFF_QK, put_dk), (_W_DV, DIFF_V, put_dv),
              (_W_MQ, MEM_W, put_mq)]
    pieces = [(start + off, min(PROJ_DOT_COLS, width - off), finish, off // LANES)
              for start, width, finish in groups for off in range(0, width, PROJ_DOT_COLS)]
    for start, width, finish, c0 in pieces:
        finish(cols(start, width), c0)


def _proj(x, gain, wt, wg2, bg, qn, kn, mqn):
    n = x.shape[0]
    tm = min(PROJ_TILE, n)
    const = lambda shape: pl.BlockSpec(shape, lambda i: (0, 0))
    out = lambda width: pl.BlockSpec((tm, width), lambda i: (i, 0))
    heads = pl.BlockSpec((tm, HEADS, LANES), lambda i: (i, 0, 0))
    shp = lambda width, dt: jax.ShapeDtypeStruct((n, width), dt)
    shp_heads = jax.ShapeDtypeStruct((n, HEADS, LANES), F32)
    return pl.pallas_call(
        _proj_body,
        grid=(n // tm,),
        in_specs=[
            pl.BlockSpec((tm, D_MODEL), lambda i: (i, 0)),
            const((1, D_MODEL)),
            pl.BlockSpec((IN_WIDTH, D_MODEL), lambda i: (0, 0), pipeline_mode=pl.Buffered(1)),
            const((LANES, GLA_QK)),
            const((1, GLA_QK)),
            const((1, DIFF_QK)),
            const((1, DIFF_QK)),
            const((1, MEM_W)),
        ],
        out_specs=[out(GLA_QK), out(GLA_QK), out(GLA_V), out(GLA_V), out(GLA_QK),
                   out(DIFF_QK), heads, heads, out(MEM_W), out(DIFF_QK), out(DIFF_V)],
        out_shape=[shp(GLA_QK, F32), shp(GLA_QK, F32), shp(GLA_V, BF16), shp(GLA_V, F32), shp(GLA_QK, F32),
                   shp(DIFF_QK, BF16), shp_heads, shp_heads, shp(MEM_W, BF16), shp(DIFF_QK, BF16), shp(DIFF_V, BF16)],
        scratch_shapes=[pltpu.VMEM((tm, D_MODEL), BF16)],
        compiler_params=_params(("parallel",)),
        name="proj",
    )(x, gain, wt, wg2, bg, qn, kn, mqn)


def _memkv_body(x_ref, g_ref, w_ref, kn_ref, k_ref, v_ref, kb_ref, vb_ref):
    xn = _rms(x_ref[...], g_ref[...]).astype(BF16)
    _group_rms(_dot(xn, w_ref[:, :MEM_W]), kn_ref, MEM_DH, 1.0, _put_heads_and_cols(k_ref, kb_ref))
    v = _dot(xn, w_ref[:, MEM_W:])
    put_v = _put_heads_and_cols(v_ref, vb_ref)
    for h in range(HEADS):
        put_v(h, v[:, h * LANES:(h + 1) * LANES])


def _memkv(mem, gain, w, kn):
    n = mem.shape[0]
    tm = min(TOK_TILE, n)
    const = lambda shape: pl.BlockSpec(shape, lambda i: (0, 0))
    heads = pl.BlockSpec((tm, HEADS, LANES), lambda i: (i, 0, 0))
    dense = pl.BlockSpec((tm, MEM_W), lambda i: (i, 0))
    return pl.pallas_call(
        _memkv_body,
        grid=(n // tm,),
        in_specs=[pl.BlockSpec((tm, D_MODEL), lambda i: (i, 0)), const((1, D_MODEL)),
                  const((D_MODEL, 2 * MEM_W)), const((1, MEM_W))],
        out_specs=[heads, heads, dense, dense],
        out_shape=[jax.ShapeDtypeStruct((n, HEADS, LANES), F32)] * 2 + [jax.ShapeDtypeStruct((n, MEM_W), BF16)] * 2,
        compiler_params=_params(("parallel",)),
        name="memkv",
    )(mem, gain, w, kn)


def _split3(x):
    hi = x.astype(BF16)
    r1 = x - hi.astype(F32)
    mid = r1.astype(BF16)
    lo = (r1 - mid.astype(F32)).astype(BF16)
    return hi, mid, lo


def _gla_body(*refs, chunk, n_chunks, n_steps, has_state):
    if has_state:
        gq_ref, gk_ref, gg_ref, gv_ref, gr_ref, on_ref, s0_ref, go_ref, st_ref, state = refs
    else:
        gq_ref, gk_ref, gg_ref, gv_ref, gr_ref, on_ref, go_ref, st_ref, state = refs
    t = pl.program_id(1)

    @pl.when(t == 0)
    def _():
        for h in range(GLA_HEADS):
            if has_state:
                state[h] = s0_ref[0, h].T
            else:
                state[h] = jnp.zeros((GLA_DV, GLA_DK), F32)

    row = lax.broadcasted_iota(jnp.int32, (chunk, chunk), 0)
    col = lax.broadcasted_iota(jnp.int32, (chunk, chunk), 1)
    causal = row >= col
    tril = causal.astype(BF16)

    chunks = [slice(c * chunk, (c + 1) * chunk) for c in range(n_chunks)]
    kcs = [slice(h * GLA_DK, (h + 1) * GLA_DK) for h in range(GLA_HEADS)]
    vcs = [slice(h * GLA_DV, (h + 1) * GLA_DV) for h in range(GLA_HEADS)]

    bs = []
    for rows in chunks:
        g_hi, g_mid, g_lo = _split3(gg_ref[rows, :])
        bs.append(_dot(tril, g_hi) + _dot(tril, g_mid) + _dot(tril, g_lo))

    qes, kes, kds, decays = [], [], [], []
    for rows, b in zip(chunks, bs):
        b_last = b[chunk - 1:chunk, :]
        q = gq_ref[rows, :]
        k = gk_ref[rows, :]
        qes.append((q * jnp.exp(b)).astype(BF16))
        kes.append((k * jnp.exp(-b)).astype(BF16))
        kds.append((k * jnp.exp(b_last - b)).astype(BF16))
        decays.append(jnp.exp(b_last))

    a_s = [[jnp.where(causal, _dot_nt(qe[:, kc], ke[:, kc]), 0.0).astype(BF16) for kc in kcs]
           for qe, ke in zip(qes, kes)]
    incs = [[_dot_tn(gv_ref[rows, vc], kd[:, kc]) for kc, vc in zip(kcs, vcs)] for rows, kd in zip(chunks, kds)]

    s_in = []
    s_cur = [state[h] for h in range(GLA_HEADS)]
    for c in range(n_chunks):
        s_in.append([s.astype(BF16) for s in s_cur])
        s_cur = [s * decays[c][:, kc] + inc for s, kc, inc in zip(s_cur, kcs, incs[c])]
    for h in range(GLA_HEADS):
        state[h] = s_cur[h]

    for c, rows in enumerate(chunks):
        for h, (kc, vc) in enumerate(zip(kcs, vcs)):
            o = _dot_nt(qes[c][:, kc], s_in[c][h]) + _dot(a_s[c][h], gv_ref[rows, vc])
            go_ref[rows, vc] = (_rms(o, on_ref[...]) * jax.nn.silu(gr_ref[rows, vc])).astype(BF16)

    @pl.when(t == n_steps - 1)
    def _():
        for h in range(GLA_HEADS):
            st_ref[0, h] = state[h].T


def _gla(gq, gk, gg, gv, gr, onorm, s0, batch, seq, chunk):
    tt = min(TOK_TILE, seq)
    n_steps = seq // tt
    tok = lambda width: pl.BlockSpec((tt, width), lambda b, t: (b * n_steps + t, 0))
    st_spec = pl.BlockSpec((1, GLA_HEADS, GLA_DK, GLA_DV), lambda b, t: (b, 0, 0, 0))
    in_specs = [tok(GLA_QK), tok(GLA_QK), tok(GLA_QK), tok(GLA_V), tok(GLA_V),
                pl.BlockSpec((1, GLA_DV), lambda b, t: (0, 0))]
    args = [gq, gk, gg, gv, gr, onorm]
    if s0 is not None:
        in_specs.append(st_spec)
        args.append(s0)
    return pl.pallas_call(
        functools.partial(_gla_body, chunk=chunk, n_chunks=tt // chunk, n_steps=n_steps, has_state=s0 is not None),
        grid=(batch, n_steps),
        in_specs=in_specs,
        out_specs=[tok(GLA_V), st_spec],
        out_shape=[jax.ShapeDtypeStruct((batch * seq, GLA_V), BF16),
                   jax.ShapeDtypeStruct((batch, GLA_HEADS, GLA_DK, GLA_DV), F32)],
        scratch_shapes=[pltpu.VMEM((GLA_HEADS, GLA_DV, GLA_DK), F32)],
        compiler_params=_params(("parallel", "arbitrary")),
        name="gla",
    )(*args)


def _t5_bucket(rel):
    nb = REL_BUCKETS // 2
    max_exact = nb // 2
    ret = jnp.where(rel > 0, nb, 0)
    n = jnp.abs(rel)
    nf = jnp.maximum(n, 1).astype(F32)
    large = max_exact + (jnp.log(nf / max_exact) / math.log(REL_MAX_DIST / max_exact)
                         * (nb - max_exact)).astype(jnp.int32)
    large = jnp.minimum(large, nb - 1)
    return ret + jnp.where(n < max_exact, n, large)


def _bucket_tile(q_pos, k_pos):
    visible = (k_pos[None, :] // CHUNK) <= (q_pos[:, None] // CHUNK)
    return jnp.where(visible, _t5_bucket(k_pos[None, :] - q_pos[:, None]), MASKED_BUCKET).astype(jnp.int32)


def _bias_from_buckets(idx, tab_ref, head):
    def step(bk, acc):
        return jnp.where(idx == bk, tab_ref[bk * DIFF_HEADS + head], acc)
    return lax.fori_loop(0, REL_BUCKETS, step, jnp.full(idx.shape, NEG_INF, F32))


def _lambda(lam_ref, lam_init):
    l = lam_ref[...]
    return (jnp.exp(jnp.sum(l[0:1] * l[1:2], axis=-1, keepdims=True))
            - jnp.exp(jnp.sum(l[2:3] * l[3:4], axis=-1, keepdims=True)) + lam_init)


def _comp_masks(q):
    lane = lax.broadcasted_iota(jnp.int32, q.shape, 1)
    zero = jnp.zeros_like(q)
    return jnp.where(lane < DIFF_DH, q, zero), jnp.where(lane < DIFF_DH, zero, q)


def _scores(q_c, parts):
    return [_dot_nt(q_c, k) + bias if jnp.ndim(bias) == 2 else _dot_nt(q_c, k) for k, _, bias in parts]


def _softmax_pv(scores, parts):
    shifts = [0.0 if jnp.ndim(bias) == 2 else bias for _, _, bias in parts]
    m = None
    for s, shift in zip(scores, shifts):
        part_max = s.max(axis=-1, keepdims=True) + shift
        m = part_max if m is None else jnp.maximum(m, part_max)
    l = 0.0
    o = 0.0
    for s, shift, (_, v, _) in zip(scores, shifts, parts):
        e = jnp.exp(s - (m - shift))
        l = l + e.sum(axis=-1, keepdims=True)
        o = o + _dot(e.astype(BF16), v)
    return o * (1.0 / l)


def _diff_finish(o0, o1, lam, on_ref, lam_init, out_dtype):
    o = o0 - lam * o1
    return (_rms(o, on_ref[...]) * (1.0 - lam_init)).astype(out_dtype)


def _diff_prompt_body(tab_ref, q_ref, kb, vb, idx_ref, lam_ref, on_ref, o_ref, bias, *, seq, lam_init):
    b = pl.program_id(0)
    h = pl.program_id(1)

    @pl.when(b == 0)
    def _():
        for t in range(2):
            bias[h, t] = _bias_from_buckets(idx_ref[t], tab_ref, h)

    lam = _lambda(lam_ref, lam_init)
    far_bias = tab_ref[(REL_BUCKETS // 2 - 1) * DIFF_HEADS + h]

    def key_parts(i):
        near0 = max(i - 1, 0) * Q_BLOCK
        near = slice(near0, near0 + 2 * Q_BLOCK)
        parts = [(kb[near, :], vb[near, :], bias[h, min(i, 1)])]
        if near0 > 0:
            parts.append((kb[0:near0, :], vb[0:near0, :], far_bias))
        return parts

    items = [(i, c) for i in range(seq // Q_BLOCK) for c in range(2)]
    comps = {}

    def scores_of(item):
        i, c = item
        if i not in comps:
            comps[i] = _comp_masks(q_ref[i * Q_BLOCK:(i + 1) * Q_BLOCK, :])
        return _scores(comps[i][c], key_parts(i))

    outs = {}
    ahead = scores_of(items[0])
    for n, (i, c) in enumerate(items):
        cur = ahead
        if n + 1 < len(items):
            ahead = scores_of(items[n + 1])
        outs[c] = _softmax_pv(cur, key_parts(i))
        if c == 1:
            o_ref[i * Q_BLOCK:(i + 1) * Q_BLOCK, :] = _diff_finish(outs[0], outs[1], lam, on_ref, lam_init,
                                                                    o_ref.dtype)


def _diff_prompt(table, dq, dk, dv, lam_p, onorm, batch, seq, lam_init):
    q_pos = jnp.arange(Q_BLOCK)
    idx = jnp.stack([_bucket_tile(q_pos, jnp.arange(2 * Q_BLOCK)),
                     _bucket_tile(q_pos + Q_BLOCK, jnp.arange(2 * Q_BLOCK))])
    head = lambda: pl.BlockSpec((seq, LANES), lambda b, h: (b, h))
    return pl.pallas_call(
        functools.partial(_diff_prompt_body, seq=seq, lam_init=lam_init),
        grid=(batch, DIFF_HEADS),
        in_specs=[
            pl.BlockSpec(memory_space=pltpu.SMEM),
            head(), head(), head(),
            pl.BlockSpec((2, Q_BLOCK, 2 * Q_BLOCK), lambda b, h: (0, 0, 0)),
            pl.BlockSpec((4, DIFF_DH), lambda b, h: (0, 0)),
            pl.BlockSpec((1, DIFF_DV), lambda b, h: (0, 0)),
        ],
        out_specs=head(),
        out_shape=jax.ShapeDtypeStruct((batch * seq, DIFF_V), BF16),
        scratch_shapes=[pltpu.VMEM((DIFF_HEADS, 2, Q_BLOCK, 2 * Q_BLOCK), F32)],
        compiler_params=_params(("arbitrary", "arbitrary")),
        name="diff_prompt",
    )(table, dq, dk, dv, idx, lam_p, onorm)


def _diff_sample_body(tab_ref, q_ref, k_ref, v_ref, ck_ref, cv_ref, idxc_ref, idxn_ref, lam_ref, on_ref, o_ref,
                      bias_c, bias_n, *, seq, lam_init):
    groups = [(h, c) for h in range(DIFF_HEADS) for c in range(2)]

    @pl.when(pl.program_id(0) == 0)
    def _():
        for g, (h, _) in enumerate(groups):
            bias_c[g * seq:(g + 1) * seq, :] = _bias_from_buckets(idxc_ref[h], tab_ref, h)
        for h in range(DIFF_HEADS):
            bias_n[h] = _bias_from_buckets(idxn_ref[...], tab_ref, h)

    lam = _lambda(lam_ref, lam_init)
    head_cols = [slice(h * LANES, (h + 1) * LANES) for h in range(DIFF_HEADS)]
    q_groups = [q_c for cols in head_cols for q_c in _comp_masks(q_ref[:, cols])]
    s_old = _dot_nt(jnp.concatenate(q_groups, axis=0), ck_ref[...].astype(BF16)) + bias_c[...]
    s_new = jnp.concatenate([_dot_nt(q_g, k_ref[:, head_cols[h]]) + bias_n[h]
                             for q_g, (h, _) in zip(q_groups, groups)], axis=0)
    m = jnp.maximum(s_old.max(axis=-1, keepdims=True), s_new.max(axis=-1, keepdims=True))
    e_old = jnp.exp(s_old - m)
    e_new = jnp.exp(s_new - m)
    l = e_old.sum(axis=-1, keepdims=True) + e_new.sum(axis=-1, keepdims=True)
    o_old = _dot(e_old.astype(BF16), cv_ref[...].astype(BF16))
    e_new = e_new.astype(BF16)
    o_new = jnp.concatenate([_dot(e_new[g * seq:(g + 1) * seq, :], v_ref[:, head_cols[h]])
                             for g, (h, _) in enumerate(groups)], axis=0)
    o = (o_old + o_new) * (1.0 / l)
    for h, cols in enumerate(head_cols):
        o0 = o[(2 * h) * seq:(2 * h + 1) * seq, :]
        o1 = o[(2 * h + 1) * seq:(2 * h + 2) * seq, :]
        o_ref[:, cols] = _diff_finish(o0, o1, lam, on_ref, lam_init, o_ref.dtype)


def _diff_sample(table, dq, dk, dv, cache_k, cache_v, lam_p, onorm, batch, seq, past, lam_init):
    q_pos = past + jnp.arange(seq)
    idx_n = _bucket_tile(q_pos, past + jnp.arange(seq))
    idx_rows = jnp.repeat(_bucket_tile(q_pos, jnp.arange(past)), HEADS, axis=1)
    row_head = jnp.arange(past * HEADS) % HEADS
    idx_c = jnp.where(row_head[None, None, :] == jnp.arange(HEADS)[:, None, None], idx_rows[None], MASKED_BUCKET)
    new_q = pl.BlockSpec((seq, DIFF_QK), lambda b: (b, 0))
    old = lambda: pl.BlockSpec((past * HEADS, LANES), lambda b: (b, 0))
    const = lambda shape: pl.BlockSpec(shape, lambda b: (0,) * len(shape))
    n_groups = 2 * DIFF_HEADS
    return pl.pallas_call(
        functools.partial(_diff_sample_body, seq=seq, lam_init=lam_init),
        grid=(batch,),
        in_specs=[pl.BlockSpec(memory_space=pltpu.SMEM), new_q, new_q, new_q, old(), old(),
                  const((HEADS, seq, past * HEADS)), const((seq, seq)), const((4, DIFF_DH)), const((1, DIFF_DV))],
        out_specs=new_q,
        out_shape=jax.ShapeDtypeStruct((batch * seq, DIFF_V), BF16),
        scratch_shapes=[pltpu.VMEM((n_groups * seq, past * HEADS), F32), pltpu.VMEM((DIFF_HEADS, seq, seq), F32)],
        compiler_params=_params(("arbitrary",)),
        name="diff_sample",
    )(table, dq, dk, dv, cache_k, cache_v, idx_c, idx_n, lam_p, onorm)


def _memattn_body(q_ref, k_ref, v_ref, o_ref, *, per_head):
    head_cols = [slice(h * MEM_DH, (h + 1) * MEM_DH) for h in range(MEM_HEADS)]
    if per_head:
        ks = [k_ref[:, h, :].astype(BF16) for h in range(MEM_HEADS)]
        vs = [v_ref[:, h, :].astype(BF16) for h in range(MEM_HEADS)]
    else:
        ks = [k_ref[:, cols] for cols in head_cols]
        vs = [v_ref[:, cols] for cols in head_cols]
    scores = [_dot_nt(q_ref[:, cols], k) * (MEM_DH ** -0.5) for cols, k in zip(head_cols, ks)]
    probs = []
    for s in scores:
        e = jnp.exp(s - s.max(axis=-1, keepdims=True))
        probs.append((e * (1.0 / e.sum(axis=-1, keepdims=True))).astype(BF16))
    for cols, p, v in zip(head_cols, probs, vs):
        o_ref[:, cols] = _dot(p, v).astype(o_ref.dtype)


def _memattn(mq, mk, mv, batch, seq, n_mem):
    tq = min(TOK_TILE, seq)
    nq = seq // tq
    q_spec = pl.BlockSpec((tq, MEM_W), lambda b, i: (b * nq + i, 0))
    per_head = mk.ndim == 3
    if per_head:
        kv_spec = pl.BlockSpec((n_mem, HEADS, LANES), lambda b, i: (b, 0, 0))
    else:
        kv_spec = pl.BlockSpec((n_mem, MEM_W), lambda b, i: (b, 0))
    return pl.pallas_call(
        functools.partial(_memattn_body, per_head=per_head),
        grid=(batch, nq),
        in_specs=[q_spec, kv_spec, kv_spec],
        out_specs=q_spec,
        out_shape=jax.ShapeDtypeStruct((batch * seq, MEM_W), BF16),
        compiler_params=_params(("parallel", "arbitrary")),
        name="memattn",
    )(mq, mk, mv)


def _outproj_body(x_ref, g_ref, d_ref, m_ref, wg_ref, wd_ref, wm_ref, ng_ref, o_ref, on_ref):
    o = (x_ref[...] + _dot(g_ref[...], wg_ref[...]) + _dot(d_ref[...], wd_ref[...])
         + _dot(m_ref[...], wm_ref[...]))
    o_ref[...] = o
    on_ref[...] = _rms(o, ng_ref[...]).astype(BF16)


def _outproj(x, g, d, m, wo, next_gain):
    n = x.shape[0]
    tm = min(TOK_TILE, n)
    tok = lambda width: pl.BlockSpec((tm, width), lambda i: (i, 0))
    w_rows = lambda rows, blk: pl.BlockSpec((rows, D_MODEL), lambda i: (blk, 0))
    return pl.pallas_call(
        _outproj_body,
        grid=(n // tm,),
        in_specs=[tok(D_MODEL), tok(GLA_V), tok(DIFF_V), tok(MEM_W),
                  w_rows(GLA_V, 0), w_rows(DIFF_V, GLA_V // DIFF_V), w_rows(MEM_W, (GLA_V + DIFF_V) // MEM_W),
                  pl.BlockSpec((1, D_MODEL), lambda i: (0, 0))],
        out_specs=[tok(D_MODEL), tok(D_MODEL)],
        out_shape=[jax.ShapeDtypeStruct((n, D_MODEL), F32), jax.ShapeDtypeStruct((n, D_MODEL), BF16)],
        compiler_params=_params(("parallel",)),
        name="outproj",
    )(x, g, d, m, wo, wo, wo, next_gain)


def _row(v):
    return v.reshape(1, -1).astype(F32)


def kernel(x_prompt, x_sample, mem_prompt, cache_diff_k, cache_diff_v, state_gla, cache_mem_k, cache_mem_v, rel_bias_table, norm_ffn1, w_ffn1_in, w_ffn1_out, norm_mix, w_in, w_gla_g2, b_gla_g, gla_out_norm, diff_q_norm, diff_k_norm, diff_lambda, diff_out_norm, mem_norm, w_mem_kv, mem_q_norm, mem_k_norm, w_o, norm_ffn2, w_ffn2_in, w_ffn2_out, norm_final):
    depth = norm_ffn1.shape[0]
    assert depth == 1, "single-layer step"
    layer = 0
    batch, seq, _ = x_prompt.shape
    dec_batch, dec_seq, _ = x_sample.shape
    past = cache_diff_k.shape[2]
    n_mem = mem_prompt.shape[1]
    lam_init = 0.8 - 0.6 * math.exp(-0.3 * layer)

    ffn1 = (w_ffn1_in[layer].astype(BF16), w_ffn1_out[layer].astype(BF16))
    w_g2 = jnp.pad(w_gla_g2[layer].astype(BF16), ((0, LANES - GLA_GATE_RANK), (0, 0)))
    b_g = _row(b_gla_g[layer])
    qn = _row(jnp.tile(diff_q_norm[layer], DIFF_QK // DIFF_DH))
    kn = _row(jnp.tile(diff_k_norm[layer], DIFF_QK // DIFF_DH))
    mqn = _row(jnp.tile(mem_q_norm[layer], MEM_HEADS))
    mkn = _row(jnp.tile(mem_k_norm[layer], MEM_HEADS))
    wo = w_o[layer].astype(BF16)
    table = rel_bias_table.astype(F32).reshape(-1)
    lam_p = diff_lambda[layer].astype(F32)
    gla_on = _row(gla_out_norm[layer])
    diff_on = _row(diff_out_norm[layer])
    per_head = lambda a: a.reshape(-1, HEADS, LANES)

    mk, mv, mk_b, mv_b = _memkv(mem_prompt.reshape(batch * n_mem, D_MODEL), _row(mem_norm[layer]),
                                w_mem_kv[layer].astype(BF16), mkn)

    n1 = _row(norm_ffn1[layer])
    xs1 = _ffn(x_sample.reshape(dec_batch * dec_seq, D_MODEL), n1, *ffn1)
    n_in_blocks = w_ffn2_in.shape[2] // LANES
    xp1, *ffn2 = _ffn(x_prompt.reshape(batch * seq, D_MODEL), n1, *ffn1,
                      cast_later=[(w_ffn2_in[layer], (D_MODEL, LANES), 0),
                                  (w_ffn2_out[layer], (D_FF // n_in_blocks, D_MODEL), n_in_blocks)])
    w_t = jnp.swapaxes(w_in[layer], 0, 1).astype(BF16)

    def layer_fn(x, b, t, chunk, s0, diff_fn, mem_k, mem_v):
        gq, gk, gv, gr, gg, dq, dk, dv, mq, dk_b, dv_b = _proj(x, _row(norm_mix[layer]), w_t, w_g2, b_g,
                                                               qn, kn, mqn)
        g_out, g_state = _gla(gq, gk, gg, gv, gr, gla_on, s0, b, t, chunk)
        d_out = diff_fn(dq, dk_b, dv_b)
        m_out = _memattn(mq, mem_k, mem_v, b, t, n_mem)
        x, xn = _outproj(x, g_out, d_out, m_out, wo, _row(norm_ffn2[layer]))
        x = _ffn(x, xn, *ffn2, final_gain=_row(norm_final[layer]))
        return x, dk, dv, g_state

    yp, dk_p, dv_p, g_p = layer_fn(
        xp1, batch, seq, CHUNK, None,
        lambda dq, dk, dv: _diff_prompt(table, dq, dk, dv, lam_p, diff_on, batch, seq, lam_init), mk_b, mv_b)
    ys, dk_s, dv_s, g_s = layer_fn(
        xs1, dec_batch, dec_seq, dec_seq, state_gla[layer],
        lambda dq, dk, dv: _diff_sample(table, dq, dk, dv, cache_diff_k[layer].reshape(-1, LANES),
                                        cache_diff_v[layer].reshape(-1, LANES),
                                        lam_p, diff_on, dec_batch, dec_seq, past, lam_init),
        per_head(cache_mem_k[layer]), per_head(cache_mem_v[layer]))

    head4 = lambda a, b, t: a.reshape(1, b, t, HEADS, LANES)
    return (yp.reshape(batch, seq, D_MODEL), ys.reshape(dec_batch, dec_seq, D_MODEL),
            head4(dk_p, batch, seq), head4(dv_p, batch, seq), g_p[None],
            head4(mk, batch, n_mem), head4(mv, batch, n_mem),
            head4(dk_s, dec_batch, dec_seq), head4(dv_s, dec_batch, dec_seq), g_s[None])
```

```python
import functools
import math

import jax
import jax.numpy as jnp
from jax import lax
from jax.experimental import pallas as pl
from jax.experimental.pallas import tpu as pltpu

F32 = jnp.float32
BF16 = jnp.bfloat16

D_MODEL = 2048
CHUNK = 64
EPS = 1e-6
NEG_INF = -1e30
GLA_HEADS, GLA_DK, GLA_DV, GLA_GATE_RANK, GLA_GATE_NORM = 4, 128, 256, 16, 16.0
DIFF_HEADS, DIFF_DH, DIFF_DV = 4, 64, 128
MEM_HEADS, MEM_DH = 4, 128
REL_BUCKETS, REL_MAX_DIST = 32, 128
D_FF = 5504
GLA_QK = GLA_HEADS * GLA_DK
GLA_V = GLA_HEADS * GLA_DV
DIFF_QK = DIFF_HEADS * 2 * DIFF_DH
DIFF_V = DIFF_HEADS * DIFF_DV
MEM_W = MEM_HEADS * MEM_DH
GLR_OFF = 2 * GLA_QK + 2 * GLA_V

LANES = 128
HEADS = 4
FF_TILE = 512
N_FF_TILES = -(-D_FF // FF_TILE)
FF_LAST_START = D_FF - FF_TILE
FF_OVERLAP = N_FF_TILES * FF_TILE - D_FF
FFN_TOK_TILE = 1024
TOK_TILE = 512
PROJ_TILE = 256
PROJ_DOT_COLS = 256
Q_BLOCK = 256
MASKED_BUCKET = REL_BUCKETS
MIB = 1024 * 1024
VMEM_CAP_MIB = 60
CAST_ROWS_W_IN = 48
CAST_ROWS_W_O = 16

_W_GQ, _W_GK, _W_GV, _W_GR, _W_GLR = 0, GLA_QK, 2 * GLA_QK, 2 * GLA_QK + GLA_V, GLR_OFF
_W_DQ = GLR_OFF + GLA_GATE_RANK
_W_DK = _W_DQ + DIFF_QK
_W_DV = _W_DK + DIFF_QK
_W_MQ = _W_DV + DIFF_V
IN_WIDTH = _W_MQ + MEM_W


def _dot(a, b):
    return jnp.dot(a, b, preferred_element_type=F32)


def _dot_nt(a, b):
    return lax.dot_general(a, b, (((1,), (1,)), ((), ())), preferred_element_type=F32)


def _dot_tn(a, b):
    return lax.dot_general(a, b, (((0,), (0,)), ((), ())), preferred_element_type=F32)


def _rms(x, gain):
    return x * lax.rsqrt(jnp.mean(x * x, axis=-1, keepdims=True) + EPS) * gain


def _params(sem, vmem_mib=48, flags=None):
    return pltpu.CompilerParams(dimension_semantics=sem, vmem_limit_bytes=min(vmem_mib, VMEM_CAP_MIB) * MIB,
                                flags=flags)


def _ffn_body(*refs, n_ff, final_norm, n_cast, prenormed):
    x_ref, g_ref, wg_ref, wu_ref, wo_ref = refs[:5]
    refs = refs[5:]
    if final_norm:
        fg_ref, refs = refs[0], refs[1:]
    cast_src, refs = refs[:n_cast], refs[n_cast:]
    o_ref, refs = refs[0], refs[1:]
    cast_dst, refs = refs[:n_cast], refs[n_cast:]
    xn_ref = g_ref if prenormed else refs[0]
    j = pl.program_id(1)

    @pl.when(j == 0)
    def _():
        x = x_ref[...]
        if not prenormed:
            xn_ref[...] = _rms(x, g_ref[...]).astype(BF16)
        o_ref[...] = x

    def tile(lo):
        for src, dst in zip(cast_src, cast_dst):
            dst[...] = src[...].astype(BF16)
        xn = xn_ref[...]
        gate = _dot(xn, wg_ref[:, lo:])
        up = _dot(xn, wu_ref[:, lo:])
        act = (jax.nn.silu(gate) * up).astype(BF16)
        return 0.5 * _dot(act, wo_ref[lo:, :])

    @pl.when(j < n_ff - 1)
    def _():
        o_ref[...] += tile(0)

    @pl.when(j == n_ff - 1)
    def _():
        o = o_ref[...] + tile(FF_OVERLAP)
        o_ref[...] = _rms(o, fg_ref[...]) if final_norm else o


def _ffn(x, gain_or_xn, w_in, w_out, final_gain=None, cast_later=()):
    n = x.shape[0]
    tm = min(FFN_TOK_TILE, n)
    n_ff = N_FF_TILES
    row = pl.BlockSpec((1, D_MODEL), lambda i, j: (0, 0))
    prenormed = gain_or_xn.shape != (1, D_MODEL)
    start = lambda j: jnp.minimum(j * (FF_TILE // LANES), FF_LAST_START // LANES)
    w_in_tile = (pl.Element(D_MODEL), pl.Element(FF_TILE))
    tok = pl.BlockSpec((tm, D_MODEL), lambda i, j: (i, 0))
    in_specs = [
        tok,
        tok if prenormed else row,
        pl.BlockSpec(w_in_tile, lambda i, j: (0, start(j) * LANES)),
        pl.BlockSpec(w_in_tile, lambda i, j: (0, (D_FF // LANES + start(j)) * LANES)),
        pl.BlockSpec((pl.Element(FF_TILE), pl.Element(D_MODEL)), lambda i, j: (start(j) * LANES, 0)),
    ]
    args = [x, gain_or_xn, w_in, w_in, w_out]
    if final_gain is not None:
        in_specs.append(row)
        args.append(final_gain)
    vmem_bytes = (4 * tm * D_MODEL * 4 + (2 if prenormed else 1) * tm * D_MODEL * 2 + 6 * D_MODEL * FF_TILE * 2
                  + 3 * tm * FF_TILE * 4)
    out_specs = [pl.BlockSpec((tm, D_MODEL), lambda i, j: (i, 0))]
    out_shape = [jax.ShapeDtypeStruct((n, D_MODEL), F32)]
    n_steps = (n // tm) * n_ff
    for mat, blk, first in cast_later:
        rows, cols = mat.shape
        assert rows % blk[0] == 0 and cols % blk[1] == 0 and (blk[0] == rows or blk[1] == cols)
        n_blk = (rows // blk[0]) * (cols // blk[1])
        assert first + n_blk <= n_steps
        by_rows = blk[1] == cols

        def index(i, j, first=first, n_blk=n_blk, by_rows=by_rows):
            k = jnp.clip(i * n_ff + j - first, 0, n_blk - 1)
            return (k, 0) if by_rows else (0, k)

        in_specs.append(pl.BlockSpec(blk, index))
        args.append(mat)
        out_specs.append(pl.BlockSpec(blk, index))
        out_shape.append(jax.ShapeDtypeStruct(mat.shape, BF16))
        vmem_bytes += 2 * blk[0] * blk[1] * (4 + 2)
    outs = pl.pallas_call(
        functools.partial(_ffn_body, n_ff=n_ff, final_norm=final_gain is not None, n_cast=len(cast_later),
                          prenormed=prenormed),
        grid=(n // tm, n_ff),
        in_specs=in_specs,
        out_specs=out_specs,
        out_shape=out_shape,
        scratch_shapes=[] if prenormed else [pltpu.VMEM((tm, D_MODEL), BF16)],
        compiler_params=_params(("arbitrary", "arbitrary"), vmem_bytes // MIB + 4),
        name="ffn",
    )(*args)
    return outs[0] if not cast_later else outs


def _group_rms(acc, gain_ref, width, scale, put, c0=0):
    lane = lax.broadcasted_iota(jnp.int32, (1, LANES), 1)
    low = lane < width
    for i in range(acc.shape[1] // LANES):
        c = c0 + i
        cols = slice(c * LANES, (c + 1) * LANES)
        xc = acc[:, i * LANES:(i + 1) * LANES]
        sq = xc * xc
        if width == LANES:
            r = lax.rsqrt(jnp.mean(sq, axis=-1, keepdims=True) + EPS)
        else:
            s_lo = jnp.sum(jnp.where(low, sq, 0.0), axis=-1, keepdims=True)
            s_hi = jnp.sum(jnp.where(low, 0.0, sq), axis=-1, keepdims=True)
            r = jnp.where(low, lax.rsqrt(s_lo / width + EPS), lax.rsqrt(s_hi / width + EPS))
        y = xc * r * gain_ref[:, cols]
        if scale != 1.0:
            y = y * scale
        put(c, y)


def _put_cols(ref):
    def put(c, y):
        ref[:, c * LANES:(c + 1) * LANES] = y.astype(ref.dtype)
    return put


def _put_heads_and_cols(heads_ref, cols_ref):
    def put(c, y):
        heads_ref[:, c, :] = y
        cols_ref[:, c * LANES:(c + 1) * LANES] = y.astype(cols_ref.dtype)
    return put


def _proj_body(x_ref, g_ref, wt_ref, wg2_ref, bg_ref, qn_ref, kn_ref, mqn_ref,
               gq_ref, gk_ref, gv_ref, gr_ref, gg_ref, dq_ref, dk_ref, dv_ref, mq_ref, dkb_ref, dvb_ref, xn_ref):
    xn_ref[...] = _rms(x_ref[...], g_ref[...]).astype(BF16)

    def cols(start, width):
        return _dot_nt(xn_ref[...], wt_ref[start:start + width, :])

    def piece(ref, a, c0):
        return ref.at[:, c0 * LANES:c0 * LANES + a.shape[1]]

    def put_gq(a, c0):
        piece(gq_ref, a, c0)[...] = a * (GLA_DK ** -0.5)

    def put_gk(a, c0):
        piece(gk_ref, a, c0)[...] = a

    def put_gv(a, c0):
        piece(gv_ref, a, c0)[...] = a.astype(BF16)

    def put_gr(a, c0):
        piece(gr_ref, a, c0)[...] = a

    def put_gate(glr, c0):
        z = _dot(glr.astype(BF16), wg2_ref[...]) + bg_ref[...]
        gg_ref[...] = jax.nn.log_sigmoid(z) / GLA_GATE_NORM

    def put_dq(a, c0):
        _group_rms(a, qn_ref, DIFF_DH, DIFF_DH ** -0.5, _put_cols(dq_ref), c0)

    def put_dk(a, c0):
        _group_rms(a, kn_ref, DIFF_DH, 1.0, _put_heads_and_cols(dk_ref, dkb_ref), c0)

    def put_dv(a, c0):
        put = _put_heads_and_cols(dv_ref, dvb_ref)
        for i in range(a.shape[1] // LANES):
            put(c0 + i, a[:, i * LANES:(i + 1) * LANES])

    def put_mq(a, c0):
        _group_rms(a, mqn_ref, MEM_DH, 1.0, _put_cols(mq_ref), c0)

    groups = [(_W_GQ, GLA_QK, put_gq), (_W_GK, GLA_QK, put_gk), (_W_GV, GLA_V, put_gv), (_W_GR, GLA_V, put_gr),
              (_W_GLR, LANES, put_gate), (_W_DQ, DIFF_QK, put_dq), (_W_DK, DIFF_QK, put_dk), (_W_DV, DIFF_V, put_dv),
              (_W_MQ, MEM_W, put_mq)]
    pieces = [(start + off, min(PROJ_DOT_COLS, width - off), finish, off // LANES)
              for start, width, finish in groups for off in range(0, width, PROJ_DOT_COLS)]
    for start, width, finish, c0 in pieces:
        finish(cols(start, width), c0)


def _proj(x, gain, wt, wg2, bg, qn, kn, mqn):
    n = x.shape[0]
    tm = min(PROJ_TILE, n)
    const = lambda shape: pl.BlockSpec(shape, lambda i: (0, 0))
    out = lambda width: pl.BlockSpec((tm, width), lambda i: (i, 0))
    heads = pl.BlockSpec((tm, HEADS, LANES), lambda i: (i, 0, 0))
    shp = lambda width, dt: jax.ShapeDtypeStruct((n, width), dt)
    shp_heads = jax.ShapeDtypeStruct((n, HEADS, LANES), F32)
    return pl.pallas_call(
        _proj_body,
        grid=(n // tm,),
        in_specs=[
            pl.BlockSpec((tm, D_MODEL), lambda i: (i, 0)),
            const((1, D_MODEL)),
            pl.BlockSpec((IN_WIDTH, D_MODEL), lambda i: (0, 0), pipeline_mode=pl.Buffered(1)),
            const((LANES, GLA_QK)),
            const((1, GLA_QK)),
            const((1, DIFF_QK)),
            const((1, DIFF_QK)),
            const((1, MEM_W)),
        ],
        out_specs=[out(GLA_QK), out(GLA_QK), out(GLA_V), out(GLA_V), out(GLA_QK),
                   out(DIFF_QK), heads, heads, out(MEM_W), out(DIFF_QK), out(DIFF_V)],
        out_shape=[shp(GLA_QK, F32), shp(GLA_QK, F32), shp(GLA_V, BF16), shp(GLA_V, F32), shp(GLA_QK, F32),
                   shp(DIFF_QK, BF16), shp_heads, shp_heads, shp(MEM_W, BF16), shp(DIFF_QK, BF16), shp(DIFF_V, BF16)],
        scratch_shapes=[pltpu.VMEM((tm, D_MODEL), BF16)],
        compiler_params=_params(("parallel",)),
        name="proj",
    )(x, gain, wt, wg2, bg, qn, kn, mqn)


def _memkv_body(x_ref, g_ref, w_ref, kn_ref, k_ref, v_ref, kb_ref, vb_ref):
    xn = _rms(x_ref[...], g_ref[...]).astype(BF16)
    _group_rms(_dot(xn, w_ref[:, :MEM_W]), kn_ref, MEM_DH, 1.0, _put_heads_and_cols(k_ref, kb_ref))
    v = _dot(xn, w_ref[:, MEM_W:])
    put_v = _put_heads_and_cols(v_ref, vb_ref)
    for h in range(HEADS):
        put_v(h, v[:, h * LANES:(h + 1) * LANES])


def _memkv(mem, gain, w, kn):
    n = mem.shape[0]
    tm = min(TOK_TILE, n)
    const = lambda shape: pl.BlockSpec(shape, lambda i: (0, 0))
    heads = pl.BlockSpec((tm, HEADS, LANES), lambda i: (i, 0, 0))
    dense = pl.BlockSpec((tm, MEM_W), lambda i: (i, 0))
    return pl.pallas_call(
        _memkv_body,
        grid=(n // tm,),
        in_specs=[pl.BlockSpec((tm, D_MODEL), lambda i: (i, 0)), const((1, D_MODEL)),
                  const((D_MODEL, 2 * MEM_W)), const((1, MEM_W))],
        out_specs=[heads, heads, dense, dense],
        out_shape=[jax.ShapeDtypeStruct((n, HEADS, LANES), F32)] * 2 + [jax.ShapeDtypeStruct((n, MEM_W), BF16)] * 2,
        compiler_params=_params(("parallel",)),
        name="memkv",
    )(mem, gain, w, kn)


def _split3(x):
    hi = x.astype(BF16)
    r1 = x - hi.astype(F32)
    mid = r1.astype(BF16)
    lo = (r1 - mid.astype(F32)).astype(BF16)
    return hi, mid, lo


def _gla_body(*refs, chunk, n_chunks, n_steps, has_state):
    if has_state:
        gq_ref, gk_ref, gg_ref, gv_ref, gr_ref, on_ref, s0_ref, go_ref, st_ref, state = refs
    else:
        gq_ref, gk_ref, gg_ref, gv_ref, gr_ref, on_ref, go_ref, st_ref, state = refs
    t = pl.program_id(1)

    @pl.when(t == 0)
    def _():
        for h in range(GLA_HEADS):
            if has_state:
                state[h] = s0_ref[0, h].T
            else:
                state[h] = jnp.zeros((GLA_DV, GLA_DK), F32)

    row = lax.broadcasted_iota(jnp.int32, (chunk, chunk), 0)
    col = lax.broadcasted_iota(jnp.int32, (chunk, chunk), 1)
    causal = row >= col
    tril = causal.astype(BF16)

    chunks = [slice(c * chunk, (c + 1) * chunk) for c in range(n_chunks)]
    kcs = [slice(h * GLA_DK, (h + 1) * GLA_DK) for h in range(GLA_HEADS)]
    vcs = [slice(h * GLA_DV, (h + 1) * GLA_DV) for h in range(GLA_HEADS)]

    bs = []
    for rows in chunks:
        g_hi, g_mid, g_lo = _split3(gg_ref[rows, :])
        bs.append(_dot(tril, g_hi) + _dot(tril, g_mid) + _dot(tril, g_lo))

    qes, kes, kds, decays = [], [], [], []
    for rows, b in zip(chunks, bs):
        b_last = b[chunk - 1:chunk, :]
        q = gq_ref[rows, :]
        k = gk_ref[rows, :]
        qes.append((q * jnp.exp(b)).astype(BF16))
        kes.append((k * jnp.exp(-b)).astype(BF16))
        kds.append((k * jnp.exp(b_last - b)).astype(BF16))
        decays.append(jnp.exp(b_last))

    a_s = [[jnp.where(causal, _dot_nt(qe[:, kc], ke[:, kc]), 0.0).astype(BF16) for kc in kcs]
           for qe, ke in zip(qes, kes)]
    incs = [[_dot_tn(gv_ref[rows, vc], kd[:, kc]) for kc, vc in zip(kcs, vcs)] for rows, kd in zip(chunks, kds)]

    s_in = []
    s_cur = [state[h] for h in range(GLA_HEADS)]
    for c in range(n_chunks):
        s_in.append([s.astype(BF16) for s in s_cur])
        s_cur = [s * decays[c][:, kc] + inc for s, kc, inc in zip(s_cur, kcs, incs[c])]
    for h in range(GLA_HEADS):
        state[h] = s_cur[h]

    for c, rows in enumerate(chunks):
        for h, (kc, vc) in enumerate(zip(kcs, vcs)):
            o = _dot_nt(qes[c][:, kc], s_in[c][h]) + _dot(a_s[c][h], gv_ref[rows, vc])
            go_ref[rows, vc] = (_rms(o, on_ref[...]) * jax.nn.silu(gr_ref[rows, vc])).astype(BF16)

    @pl.when(t == n_steps - 1)
    def _():
        for h in range(GLA_HEADS):
            st_ref[0, h] = state[h].T


def _gla(gq, gk, gg, gv, gr, onorm, s0, batch, seq, chunk):
    tt = min(TOK_TILE, seq)
    n_steps = seq // tt
    tok = lambda width: pl.BlockSpec((tt, width), lambda b, t: (b * n_steps + t, 0))
    st_spec = pl.BlockSpec((1, GLA_HEADS, GLA_DK, GLA_DV), lambda b, t: (b, 0, 0, 0))
    in_specs = [tok(GLA_QK), tok(GLA_QK), tok(GLA_QK), tok(GLA_V), tok(GLA_V),
                pl.BlockSpec((1, GLA_DV), lambda b, t: (0, 0))]
    args = [gq, gk, gg, gv, gr, onorm]
    if s0 is not None:
        in_specs.append(st_spec)
        args.append(s0)
    return pl.pallas_call(
        functools.partial(_gla_body, chunk=chunk, n_chunks=tt // chunk, n_steps=n_steps, has_state=s0 is not None),
        grid=(batch, n_steps),
        in_specs=in_specs,
        out_specs=[tok(GLA_V), st_spec],
        out_shape=[jax.ShapeDtypeStruct((batch * seq, GLA_V), BF16),
                   jax.ShapeDtypeStruct((batch, GLA_HEADS, GLA_DK, GLA_DV), F32)],
        scratch_shapes=[pltpu.VMEM((GLA_HEADS, GLA_DV, GLA_DK), F32)],
        compiler_params=_params(("parallel", "arbitrary")),
        name="gla",
    )(*args)


def _t5_bucket(rel):
    nb = REL_BUCKETS // 2
    max_exact = nb // 2
    ret = jnp.where(rel > 0, nb, 0)
    n = jnp.abs(rel)
    nf = jnp.maximum(n, 1).astype(F32)
    large = max_exact + (jnp.log(nf / max_exact) / math.log(REL_MAX_DIST / max_exact)
                         * (nb - max_exact)).astype(jnp.int32)
    large = jnp.minimum(large, nb - 1)
    return ret + jnp.where(n < max_exact, n, large)


def _bucket_tile(q_pos, k_pos):
    visible = (k_pos[None, :] // CHUNK) <= (q_pos[:, None] // CHUNK)
    return jnp.where(visible, _t5_bucket(k_pos[None, :] - q_pos[:, None]), MASKED_BUCKET).astype(jnp.int32)


def _bias_from_buckets(idx, tab_ref, head):
    def step(bk, acc):
        return jnp.where(idx == bk, tab_ref[bk * DIFF_HEADS + head], acc)
    return lax.fori_loop(0, REL_BUCKETS, step, jnp.full(idx.shape, NEG_INF, F32))


def _lambda(lam_ref, lam_init):
    l = lam_ref[...]
    return (jnp.exp(jnp.sum(l[0:1] * l[1:2], axis=-1, keepdims=True))
            - jnp.exp(jnp.sum(l[2:3] * l[3:4], axis=-1, keepdims=True)) + lam_init)


def _comp_masks(q):
    lane = lax.broadcasted_iota(jnp.int32, q.shape, 1)
    zero = jnp.zeros_like(q)
    return jnp.where(lane < DIFF_DH, q, zero), jnp.where(lane < DIFF_DH, zero, q)


def _diff_finish(o0, o1, lam, on_ref, lam_init, out_dtype):
    o = o0 - lam * o1
    return (_rms(o, on_ref[...]) * (1.0 - lam_init)).astype(out_dtype)


def _scores(q_c, parts):
    return [_dot_nt(q_c, k) + bias if jnp.ndim(bias) == 2 else _dot_nt(q_c, k) for k, _, bias in parts]


def _softmax_pv(scores, parts):
    shifts = [0.0 if jnp.ndim(bias) == 2 else bias for _, _, bias in parts]
    m = None
    for s, shift in zip(scores, shifts):
        part_max = s.max(axis=-1, keepdims=True) + shift
        m = part_max if m is None else jnp.maximum(m, part_max)
    ol = 0.0
    for s, shift, (_, v1, _) in zip(scores, shifts, parts):
        ol = ol + _dot(jnp.exp(s - (m - shift)).astype(BF16), v1)
    return ol[:, :DIFF_DV] * (1.0 / ol[:, DIFF_DV:])


def _diff_prompt_body(tab_ref, q_ref, kb, vb, idx_ref, lam_ref, on_ref, o_ref, bias, v1, *, seq, lam_init):
    b = pl.program_id(0)
    h = pl.program_id(1)

    @pl.when(b == 0)
    def _():
        for t in range(2):
            bias[h, t] = _bias_from_buckets(idx_ref[t], tab_ref, h)

    v1[:, :DIFF_DV] = vb[...]
    v1[:, DIFF_DV:] = jnp.ones((seq, LANES), BF16)
    lam = _lambda(lam_ref, lam_init)
    far_bias = tab_ref[(REL_BUCKETS // 2 - 1) * DIFF_HEADS + h]

    def key_parts(i):
        near0 = max(i - 1, 0) * Q_BLOCK
        near = slice(near0, near0 + 2 * Q_BLOCK)
        parts = [(kb[near, :], v1[near, :], bias[h, min(i, 1)])]
        if near0 > 0:
            parts.append((kb[0:near0, :], v1[0:near0, :], far_bias))
        return parts

    items = [(i, c) for i in range(seq // Q_BLOCK) for c in range(2)]
    comps = {}

    def scores_of(item):
        i, c = item
        if i not in comps:
            comps[i] = _comp_masks(q_ref[i * Q_BLOCK:(i + 1) * Q_BLOCK, :])
        return _scores(comps[i][c], key_parts(i))

    outs = {}
    ahead = scores_of(items[0])
    for n, (i, c) in enumerate(items):
        cur = ahead
        if n + 1 < len(items):
            ahead = scores_of(items[n + 1])
        outs[c] = _softmax_pv(cur, key_parts(i))
        if c == 1:
            o_ref[i * Q_BLOCK:(i + 1) * Q_BLOCK, :] = _diff_finish(outs[0], outs[1], lam, on_ref, lam_init,
                                                                    o_ref.dtype)


def _diff_prompt(table, dq, dk, dv, lam_p, onorm, batch, seq, lam_init):
    q_pos = jnp.arange(Q_BLOCK)
    idx = jnp.stack([_bucket_tile(q_pos, jnp.arange(2 * Q_BLOCK)),
                     _bucket_tile(q_pos + Q_BLOCK, jnp.arange(2 * Q_BLOCK))])
    head = lambda: pl.BlockSpec((seq, LANES), lambda b, h: (b, h))
    return pl.pallas_call(
        functools.partial(_diff_prompt_body, seq=seq, lam_init=lam_init),
        grid=(batch, DIFF_HEADS),
        in_specs=[
            pl.BlockSpec(memory_space=pltpu.SMEM),
            head(), head(), head(),
            pl.BlockSpec((2, Q_BLOCK, 2 * Q_BLOCK), lambda b, h: (0, 0, 0)),
            pl.BlockSpec((4, DIFF_DH), lambda b, h: (0, 0)),
            pl.BlockSpec((1, DIFF_DV), lambda b, h: (0, 0)),
        ],
        out_specs=head(),
        out_shape=jax.ShapeDtypeStruct((batch * seq, DIFF_V), BF16),
        scratch_shapes=[pltpu.VMEM((DIFF_HEADS, 2, Q_BLOCK, 2 * Q_BLOCK), F32),
                        pltpu.VMEM((seq, DIFF_DV + LANES), BF16)],
        compiler_params=_params(("arbitrary", "arbitrary")),
        name="diff_prompt",
    )(table, dq, dk, dv, idx, lam_p, onorm)


def _diff_sample_body(tab_ref, q_ref, k_ref, v_ref, ck_ref, cv_ref, idxc_ref, idxn_ref, lam_ref, on_ref, o_ref,
                      bias_c, bias_n, *, seq, lam_init):
    groups = [(h, c) for h in range(DIFF_HEADS) for c in range(2)]

    @pl.when(pl.program_id(0) == 0)
    def _():
        for g, (h, _) in enumerate(groups):
            bias_c[g * seq:(g + 1) * seq, :] = _bias_from_buckets(idxc_ref[h], tab_ref, h)
        for h in range(DIFF_HEADS):
            bias_n[h] = _bias_from_buckets(idxn_ref[...], tab_ref, h)

    lam = _lambda(lam_ref, lam_init)
    head_cols = [slice(h * LANES, (h + 1) * LANES) for h in range(DIFF_HEADS)]
    q_groups = [q_c for cols in head_cols for q_c in _comp_masks(q_ref[:, cols])]
    s_old = _dot_nt(jnp.concatenate(q_groups, axis=0), ck_ref[...].astype(BF16)) + bias_c[...]
    s_new = jnp.concatenate([_dot_nt(q_g, k_ref[:, head_cols[h]]) + bias_n[h]
                             for q_g, (h, _) in zip(q_groups, groups)], axis=0)
    m = jnp.maximum(s_old.max(axis=-1, keepdims=True), s_new.max(axis=-1, keepdims=True))
    e_old = jnp.exp(s_old - m)
    e_new = jnp.exp(s_new - m)
    l = e_old.sum(axis=-1, keepdims=True) + e_new.sum(axis=-1, keepdims=True)
    o_old = _dot(e_old.astype(BF16), cv_ref[...].astype(BF16))
    e_new = e_new.astype(BF16)
    o_new = jnp.concatenate([_dot(e_new[g * seq:(g + 1) * seq, :], v_ref[:, head_cols[h]])
                             for g, (h, _) in enumerate(groups)], axis=0)
    o = (o_old + o_new) * (1.0 / l)
    for h, cols in enumerate(head_cols):
        o0 = o[(2 * h) * seq:(2 * h + 1) * seq, :]
        o1 = o[(2 * h + 1) * seq:(2 * h + 2) * seq, :]
        o_ref[:, cols] = _diff_finish(o0, o1, lam, on_ref, lam_init, o_ref.dtype)


def _diff_sample(table, dq, dk, dv, cache_k, cache_v, lam_p, onorm, batch, seq, past, lam_init):
    q_pos = past + jnp.arange(seq)
    idx_n = _bucket_tile(q_pos, past + jnp.arange(seq))
    idx_rows = jnp.repeat(_bucket_tile(q_pos, jnp.arange(past)), HEADS, axis=1)
    row_head = jnp.arange(past * HEADS) % HEADS
    idx_c = jnp.where(row_head[None, None, :] == jnp.arange(HEADS)[:, None, None], idx_rows[None], MASKED_BUCKET)
    new_q = pl.BlockSpec((seq, DIFF_QK), lambda b: (b, 0))
    old = lambda: pl.BlockSpec((past * HEADS, LANES), lambda b: (b, 0))
    const = lambda shape: pl.BlockSpec(shape, lambda b: (0,) * len(shape))
    n_groups = 2 * DIFF_HEADS
    return pl.pallas_call(
        functools.partial(_diff_sample_body, seq=seq, lam_init=lam_init),
        grid=(batch,),
        in_specs=[pl.BlockSpec(memory_space=pltpu.SMEM), new_q, new_q, new_q, old(), old(),
                  const((HEADS, seq, past * HEADS)), const((seq, seq)), const((4, DIFF_DH)), const((1, DIFF_DV))],
        out_specs=new_q,
        out_shape=jax.ShapeDtypeStruct((batch * seq, DIFF_V), BF16),
        scratch_shapes=[pltpu.VMEM((n_groups * seq, past * HEADS), F32), pltpu.VMEM((DIFF_HEADS, seq, seq), F32)],
        compiler_params=_params(("arbitrary",)),
        name="diff_sample",
    )(table, dq, dk, dv, cache_k, cache_v, idx_c, idx_n, lam_p, onorm)


def _memattn_body(q_ref, k_ref, v_ref, o_ref, *, per_head):
    head_cols = [slice(h * MEM_DH, (h + 1) * MEM_DH) for h in range(MEM_HEADS)]
    if per_head:
        ks = [k_ref[:, h, :].astype(BF16) for h in range(MEM_HEADS)]
        vs = [v_ref[:, h, :].astype(BF16) for h in range(MEM_HEADS)]
    else:
        ks = [k_ref[:, cols] for cols in head_cols]
        vs = [v_ref[:, cols] for cols in head_cols]
    scores = [_dot_nt(q_ref[:, cols], k) * (MEM_DH ** -0.5) for cols, k in zip(head_cols, ks)]
    probs = []
    for s in scores:
        e = jnp.exp(s - s.max(axis=-1, keepdims=True))
        probs.append((e * (1.0 / e.sum(axis=-1, keepdims=True))).astype(BF16))
    for cols, p, v in zip(head_cols, probs, vs):
        o_ref[:, cols] = _dot(p, v).astype(o_ref.dtype)


def _memattn(mq, mk, mv, batch, seq, n_mem):
    tq = min(TOK_TILE, seq)
    nq = seq // tq
    q_spec = pl.BlockSpec((tq, MEM_W), lambda b, i: (b * nq + i, 0))
    per_head = mk.ndim == 3
    if per_head:
        kv_spec = pl.BlockSpec((n_mem, HEADS, LANES), lambda b, i: (b, 0, 0))
    else:
        kv_spec = pl.BlockSpec((n_mem, MEM_W), lambda b, i: (b, 0))
    return pl.pallas_call(
        functools.partial(_memattn_body, per_head=per_head),
        grid=(batch, nq),
        in_specs=[q_spec, kv_spec, kv_spec],
        out_specs=q_spec,
        out_shape=jax.ShapeDtypeStruct((batch * seq, MEM_W), BF16),
        compiler_params=_params(("parallel", "arbitrary")),
        name="memattn",
    )(mq, mk, mv)


def _outproj_body(x_ref, g_ref, d_ref, m_ref, wg_ref, wd_ref, wm_ref, ng_ref, o_ref, on_ref):
    o = (x_ref[...] + _dot(g_ref[...], wg_ref[...]) + _dot(d_ref[...], wd_ref[...])
         + _dot(m_ref[...], wm_ref[...]))
    o_ref[...] = o
    on_ref[...] = _rms(o, ng_ref[...]).astype(BF16)


def _outproj(x, g, d, m, wo, next_gain):
    n = x.shape[0]
    tm = min(TOK_TILE, n)
    tok = lambda width: pl.BlockSpec((tm, width), lambda i: (i, 0))
    w_rows = lambda rows, blk: pl.BlockSpec((rows, D_MODEL), lambda i: (blk, 0))
    return pl.pallas_call(
        _outproj_body,
        grid=(n // tm,),
        in_specs=[tok(D_MODEL), tok(GLA_V), tok(DIFF_V), tok(MEM_W),
                  w_rows(GLA_V, 0), w_rows(DIFF_V, GLA_V // DIFF_V), w_rows(MEM_W, (GLA_V + DIFF_V) // MEM_W),
                  pl.BlockSpec((1, D_MODEL), lambda i: (0, 0))],
        out_specs=[tok(D_MODEL), tok(D_MODEL)],
        out_shape=[jax.ShapeDtypeStruct((n, D_MODEL), F32), jax.ShapeDtypeStruct((n, D_MODEL), BF16)],
        compiler_params=_params(("parallel",)),
        name="outproj",
    )(x, g, d, m, wo, wo, wo, next_gain)


def _row(v):
    return v.reshape(1, -1).astype(F32)


def kernel(x_prompt, x_sample, mem_prompt, cache_diff_k, cache_diff_v, state_gla, cache_mem_k, cache_mem_v, rel_bias_table, norm_ffn1, w_ffn1_in, w_ffn1_out, norm_mix, w_in, w_gla_g2, b_gla_g, gla_out_norm, diff_q_norm, diff_k_norm, diff_lambda, diff_out_norm, mem_norm, w_mem_kv, mem_q_norm, mem_k_norm, w_o, norm_ffn2, w_ffn2_in, w_ffn2_out, norm_final):
    depth = norm_ffn1.shape[0]
    assert depth == 1, "single-layer step"
    layer = 0
    batch, seq, _ = x_prompt.shape
    dec_batch, dec_seq, _ = x_sample.shape
    past = cache_diff_k.shape[2]
    n_mem = mem_prompt.shape[1]
    lam_init = 0.8 - 0.6 * math.exp(-0.3 * layer)

    ffn1 = (w_ffn1_in[layer].astype(BF16), w_ffn1_out[layer].astype(BF16))
    w_g2 = jnp.pad(w_gla_g2[layer].astype(BF16), ((0, LANES - GLA_GATE_RANK), (0, 0)))
    b_g = _row(b_gla_g[layer])
    qn = _row(jnp.tile(diff_q_norm[layer], DIFF_QK // DIFF_DH))
    kn = _row(jnp.tile(diff_k_norm[layer], DIFF_QK // DIFF_DH))
    mqn = _row(jnp.tile(mem_q_norm[layer], MEM_HEADS))
    mkn = _row(jnp.tile(mem_k_norm[layer], MEM_HEADS))
    table = rel_bias_table.astype(F32).reshape(-1)
    lam_p = diff_lambda[layer].astype(F32)
    gla_on = _row(gla_out_norm[layer])
    diff_on = _row(diff_out_norm[layer])
    per_head = lambda a: a.reshape(-1, HEADS, LANES)

    mk, mv, mk_b, mv_b = _memkv(mem_prompt.reshape(batch * n_mem, D_MODEL), _row(mem_norm[layer]),
                                w_mem_kv[layer].astype(BF16), mkn)

    n1 = _row(norm_ffn1[layer])
    n_in_blocks = w_ffn2_in.shape[2] // LANES
    xp1, w2_in, w2_out, w_t, wo = _ffn(
        x_prompt.reshape(batch * seq, D_MODEL), n1, *ffn1,
        cast_later=[(w_ffn2_in[layer], (D_MODEL, LANES), 0),
                    (w_ffn2_out[layer], (D_FF // n_in_blocks, D_MODEL), n_in_blocks),
                    (jnp.swapaxes(w_in[layer], 0, 1), (CAST_ROWS_W_IN, D_MODEL), 0),
                    (w_o[layer], (CAST_ROWS_W_O, D_MODEL), 0)])
    ffn2 = (w2_in, w2_out)
    xs1 = _ffn(x_sample.reshape(dec_batch * dec_seq, D_MODEL), n1, *ffn1)

    def layer_fn(x, b, t, chunk, s0, diff_fn, mem_k, mem_v):
        gq, gk, gv, gr, gg, dq, dk, dv, mq, dk_b, dv_b = _proj(x, _row(norm_mix[layer]), w_t, w_g2, b_g,
                                                               qn, kn, mqn)
        g_out, g_state = _gla(gq, gk, gg, gv, gr, gla_on, s0, b, t, chunk)
        d_out = diff_fn(dq, dk_b, dv_b)
        m_out = _memattn(mq, mem_k, mem_v, b, t, n_mem)
        x, xn = _outproj(x, g_out, d_out, m_out, wo, _row(norm_ffn2[layer]))
        x = _ffn(x, xn, *ffn2, final_gain=_row(norm_final[layer]))
        return x, dk, dv, g_state

    yp, dk_p, dv_p, g_p = layer_fn(
        xp1, batch, seq, CHUNK, None,
        lambda dq, dk, dv: _diff_prompt(table, dq, dk, dv, lam_p, diff_on, batch, seq, lam_init), mk_b, mv_b)
    ys, dk_s, dv_s, g_s = layer_fn(
        xs1, dec_batch, dec_seq, dec_seq, state_gla[layer],
        lambda dq, dk, dv: _diff_sample(table, dq, dk, dv, cache_diff_k[layer].reshape(-1, LANES),
                                        cache_diff_v[layer].reshape(-1, LANES),
                                        lam_p, diff_on, dec_batch, dec_seq, past, lam_init),
        per_head(cache_mem_k[layer]), per_head(cache_mem_v[layer]))

    head4 = lambda a, b, t: a.reshape(1, b, t, HEADS, LANES)
    return (yp.reshape(batch, seq, D_MODEL), ys.reshape(dec_batch, dec_seq, D_MODEL),
            head4(dk_p, batch, seq), head4(dv_p, batch, seq), g_p[None],
            head4(mk, batch, n_mem), head4(mv, batch, n_mem),
            head4(dk_s, dec_batch, dec_seq), head4(dv_s, dec_batch, dec_seq), g_s[None])
```

```python
import functools
import math

import jax
import jax.numpy as jnp
from jax import lax
from jax.experimental import pallas as pl
from jax.experimental.pallas import tpu as pltpu

F32 = jnp.float32
BF16 = jnp.bfloat16

D_MODEL = 2048
CHUNK = 64
EPS = 1e-6
NEG_INF = -1e30
GLA_HEADS, GLA_DK, GLA_DV, GLA_GATE_RANK, GLA_GATE_NORM = 4, 128, 256, 16, 16.0
DIFF_HEADS, DIFF_DH, DIFF_DV = 4, 64, 128
MEM_HEADS, MEM_DH = 4, 128
REL_BUCKETS, REL_MAX_DIST = 32, 128
D_FF = 5504
GLA_QK = GLA_HEADS * GLA_DK
GLA_V = GLA_HEADS * GLA_DV
DIFF_QK = DIFF_HEADS * 2 * DIFF_DH
DIFF_V = DIFF_HEADS * DIFF_DV
MEM_W = MEM_HEADS * MEM_DH
GLR_OFF = 2 * GLA_QK + 2 * GLA_V

LANES = 128
HEADS = 4
FF_TILE = 512
N_FF_TILES = -(-D_FF // FF_TILE)
FF_LAST_START = D_FF - FF_TILE
FF_OVERLAP = N_FF_TILES * FF_TILE - D_FF
FFN_TOK_TILE = 1024
TOK_TILE = 512
PROJ_TILE = 256
PROJ_DOT_COLS = 256
Q_BLOCK = 256
MASKED_BUCKET = REL_BUCKETS
LOOKAHEAD = 2
SCORE_SLOTS = LOOKAHEAD + 1
ONES_ROWS = 16
MIB = 1024 * 1024
VMEM_CAP_MIB = 60
CAST_ROWS_W_IN = 48
CAST_ROWS_W_O = 16

_W_GQ, _W_GK, _W_GV, _W_GR, _W_GLR = 0, GLA_QK, 2 * GLA_QK, 2 * GLA_QK + GLA_V, GLR_OFF
_W_DQ = GLR_OFF + GLA_GATE_RANK
_W_DK = _W_DQ + DIFF_QK
_W_DV = _W_DK + DIFF_QK
_W_MQ = _W_DV + DIFF_V
IN_WIDTH = _W_MQ + MEM_W


def _dot(a, b):
    return jnp.dot(a, b, preferred_element_type=F32)


def _dot_nt(a, b):
    return lax.dot_general(a, b, (((1,), (1,)), ((), ())), preferred_element_type=F32)


def _dot_tn(a, b):
    return lax.dot_general(a, b, (((0,), (0,)), ((), ())), preferred_element_type=F32)


def _rms(x, gain):
    return x * lax.rsqrt(jnp.mean(x * x, axis=-1, keepdims=True) + EPS) * gain


def _params(sem, vmem_mib=48, flags=None):
    return pltpu.CompilerParams(dimension_semantics=sem, vmem_limit_bytes=min(vmem_mib, VMEM_CAP_MIB) * MIB,
                                flags=flags)


def _ffn_body(*refs, n_ff, final_norm, n_cast, prenormed):
    x_ref, g_ref, wg_ref, wu_ref, wo_ref = refs[:5]
    refs = refs[5:]
    if final_norm:
        fg_ref, refs = refs[0], refs[1:]
    cast_src, refs = refs[:n_cast], refs[n_cast:]
    o_ref, refs = refs[0], refs[1:]
    cast_dst, refs = refs[:n_cast], refs[n_cast:]
    xn_ref = g_ref if prenormed else refs[0]
    j = pl.program_id(1)

    @pl.when(j == 0)
    def _():
        x = x_ref[...]
        if not prenormed:
            xn_ref[...] = _rms(x, g_ref[...]).astype(BF16)
        o_ref[...] = x

    def tile(lo):
        for src, dst in zip(cast_src, cast_dst):
            dst[...] = src[...].astype(BF16)
        xn = xn_ref[...]
        gate = _dot(xn, wg_ref[:, lo:])
        up = _dot(xn, wu_ref[:, lo:])
        act = (jax.nn.silu(gate) * up).astype(BF16)
        return 0.5 * _dot(act, wo_ref[lo:, :])

    @pl.when(j < n_ff - 1)
    def _():
        o_ref[...] += tile(0)

    @pl.when(j == n_ff - 1)
    def _():
        o = o_ref[...] + tile(FF_OVERLAP)
        o_ref[...] = _rms(o, fg_ref[...]) if final_norm else o


def _ffn(x, gain_or_xn, w_in, w_out, final_gain=None, cast_later=()):
    n = x.shape[0]
    tm = min(FFN_TOK_TILE, n)
    n_ff = N_FF_TILES
    row = pl.BlockSpec((1, D_MODEL), lambda i, j: (0, 0))
    prenormed = gain_or_xn.shape != (1, D_MODEL)
    start = lambda j: jnp.minimum(j * (FF_TILE // LANES), FF_LAST_START // LANES)
    w_in_tile = (pl.Element(D_MODEL), pl.Element(FF_TILE))
    tok = pl.BlockSpec((tm, D_MODEL), lambda i, j: (i, 0))
    in_specs = [
        tok,
        tok if prenormed else row,
        pl.BlockSpec(w_in_tile, lambda i, j: (0, start(j) * LANES)),
        pl.BlockSpec(w_in_tile, lambda i, j: (0, (D_FF // LANES + start(j)) * LANES)),
        pl.BlockSpec((pl.Element(FF_TILE), pl.Element(D_MODEL)), lambda i, j: (start(j) * LANES, 0)),
    ]
    args = [x, gain_or_xn, w_in, w_in, w_out]
    if final_gain is not None:
        in_specs.append(row)
        args.append(final_gain)
    vmem_bytes = (4 * tm * D_MODEL * 4 + (2 if prenormed else 1) * tm * D_MODEL * 2 + 6 * D_MODEL * FF_TILE * 2
                  + 3 * tm * FF_TILE * 4)
    out_specs = [pl.BlockSpec((tm, D_MODEL), lambda i, j: (i, 0))]
    out_shape = [jax.ShapeDtypeStruct((n, D_MODEL), F32)]
    n_steps = (n // tm) * n_ff
    for mat, blk, first in cast_later:
        rows, cols = mat.shape
        assert rows % blk[0] == 0 and cols % blk[1] == 0 and (blk[0] == rows or blk[1] == cols)
        n_blk = (rows // blk[0]) * (cols // blk[1])
        assert first + n_blk <= n_steps
        by_rows = blk[1] == cols

        def index(i, j, first=first, n_blk=n_blk, by_rows=by_rows):
            k = jnp.clip(i * n_ff + j - first, 0, n_blk - 1)
            return (k, 0) if by_rows else (0, k)

        in_specs.append(pl.BlockSpec(blk, index))
        args.append(mat)
        out_specs.append(pl.BlockSpec(blk, index))
        out_shape.append(jax.ShapeDtypeStruct(mat.shape, BF16))
        vmem_bytes += 2 * blk[0] * blk[1] * (4 + 2)
    outs = pl.pallas_call(
        functools.partial(_ffn_body, n_ff=n_ff, final_norm=final_gain is not None, n_cast=len(cast_later),
                          prenormed=prenormed),
        grid=(n // tm, n_ff),
        in_specs=in_specs,
        out_specs=out_specs,
        out_shape=out_shape,
        scratch_shapes=[] if prenormed else [pltpu.VMEM((tm, D_MODEL), BF16)],
        compiler_params=_params(("arbitrary", "arbitrary"), vmem_bytes // MIB + 4),
        name="ffn",
    )(*args)
    return outs[0] if not cast_later else outs


def _group_rms(acc, gain_ref, width, scale, put, c0=0):
    lane = lax.broadcasted_iota(jnp.int32, (1, LANES), 1)
    low = lane < width
    for i in range(acc.shape[1] // LANES):
        c = c0 + i
        cols = slice(c * LANES, (c + 1) * LANES)
        xc = acc[:, i * LANES:(i + 1) * LANES]
        sq = xc * xc
        if width == LANES:
            r = lax.rsqrt(jnp.mean(sq, axis=-1, keepdims=True) + EPS)
        else:
            s_lo = jnp.sum(jnp.where(low, sq, 0.0), axis=-1, keepdims=True)
            s_hi = jnp.sum(jnp.where(low, 0.0, sq), axis=-1, keepdims=True)
            r = jnp.where(low, lax.rsqrt(s_lo / width + EPS), lax.rsqrt(s_hi / width + EPS))
        y = xc * r * gain_ref[:, cols]
        if scale != 1.0:
            y = y * scale
        put(c, y)


def _put_cols(ref):
    def put(c, y):
        ref[:, c * LANES:(c + 1) * LANES] = y.astype(ref.dtype)
    return put


def _put_heads_and_cols(heads_ref, cols_ref):
    def put(c, y):
        heads_ref[:, c, :] = y
        cols_ref[:, c * LANES:(c + 1) * LANES] = y.astype(cols_ref.dtype)
    return put


def _proj_body(x_ref, g_ref, wt_ref, wg2_ref, bg_ref, qn_ref, kn_ref, mqn_ref,
               gq_ref, gk_ref, gv_ref, gr_ref, gg_ref, dq_ref, dk_ref, dv_ref, mq_ref, dkb_ref, dvb_ref, xn_ref):
    xn_ref[...] = _rms(x_ref[...], g_ref[...]).astype(BF16)

    def cols(start, width):
        return _dot_nt(xn_ref[...], wt_ref[start:start + width, :])

    def piece(ref, a, c0):
        return ref.at[:, c0 * LANES:c0 * LANES + a.shape[1]]

    def put_gq(a, c0):
        piece(gq_ref, a, c0)[...] = a * (GLA_DK ** -0.5)

    def put_gk(a, c0):
        piece(gk_ref, a, c0)[...] = a

    def put_gv(a, c0):
        piece(gv_ref, a, c0)[...] = a.astype(BF16)

    def put_gr(a, c0):
        piece(gr_ref, a, c0)[...] = a

    def put_gate(glr, c0):
        z = _dot(glr.astype(BF16), wg2_ref[...]) + bg_ref[...]
        gg_ref[...] = jax.nn.log_sigmoid(z) / GLA_GATE_NORM

    def put_dq(a, c0):
        _group_rms(a, qn_ref, DIFF_DH, DIFF_DH ** -0.5, _put_cols(dq_ref), c0)

    def put_dk(a, c0):
        _group_rms(a, kn_ref, DIFF_DH, 1.0, _put_heads_and_cols(dk_ref, dkb_ref), c0)

    def put_dv(a, c0):
        put = _put_heads_and_cols(dv_ref, dvb_ref)
        for i in range(a.shape[1] // LANES):
            put(c0 + i, a[:, i * LANES:(i + 1) * LANES])

    def put_mq(a, c0):
        _group_rms(a, mqn_ref, MEM_DH, 1.0, _put_cols(mq_ref), c0)

    groups = [(_W_GQ, GLA_QK, put_gq), (_W_GK, GLA_QK, put_gk), (_W_GV, GLA_V, put_gv), (_W_GR, GLA_V, put_gr),
              (_W_GLR, LANES, put_gate), (_W_DQ, DIFF_QK, put_dq), (_W_DK, DIFF_QK, put_dk), (_W_DV, DIFF_V, put_dv),
              (_W_MQ, MEM_W, put_mq)]
    pieces = [(start + off, min(PROJ_DOT_COLS, width - off), finish, off // LANES)
              for start, width, finish in groups for off in range(0, width, PROJ_DOT_COLS)]
    for start, width, finish, c0 in pieces:
        finish(cols(start, width), c0)


def _proj(x, gain, wt, wg2, bg, qn, kn, mqn):
    n = x.shape[0]
    tm = min(PROJ_TILE, n)
    const = lambda shape: pl.BlockSpec(shape, lambda i: (0, 0))
    out = lambda width: pl.BlockSpec((tm, width), lambda i: (i, 0))
    heads = pl.BlockSpec((tm, HEADS, LANES), lambda i: (i, 0, 0))
    shp = lambda width, dt: jax.ShapeDtypeStruct((n, width), dt)
    shp_heads = jax.ShapeDtypeStruct((n, HEADS, LANES), F32)
    return pl.pallas_call(
        _proj_body,
        grid=(n // tm,),
        in_specs=[
            pl.BlockSpec((tm, D_MODEL), lambda i: (i, 0)),
            const((1, D_MODEL)),
            pl.BlockSpec((IN_WIDTH, D_MODEL), lambda i: (0, 0), pipeline_mode=pl.Buffered(1)),
            const((LANES, GLA_QK)),
            const((1, GLA_QK)),
            const((1, DIFF_QK)),
            const((1, DIFF_QK)),
            const((1, MEM_W)),
        ],
        out_specs=[out(GLA_QK), out(GLA_QK), out(GLA_V), out(GLA_V), out(GLA_QK),
                   out(DIFF_QK), heads, heads, out(MEM_W), out(DIFF_QK), out(DIFF_V)],
        out_shape=[shp(GLA_QK, F32), shp(GLA_QK, F32), shp(GLA_V, BF16), shp(GLA_V, F32), shp(GLA_QK, F32),
                   shp(DIFF_QK, BF16), shp_heads, shp_heads, shp(MEM_W, BF16), shp(DIFF_QK, BF16), shp(DIFF_V, BF16)],
        scratch_shapes=[pltpu.VMEM((tm, D_MODEL), BF16)],
        compiler_params=_params(("parallel",)),
        name="proj",
    )(x, gain, wt, wg2, bg, qn, kn, mqn)


def _memkv_body(x_ref, g_ref, w_ref, kn_ref, k_ref, v_ref, kb_ref, vb_ref):
    xn = _rms(x_ref[...], g_ref[...]).astype(BF16)
    _group_rms(_dot(xn, w_ref[:, :MEM_W]), kn_ref, MEM_DH, 1.0, _put_heads_and_cols(k_ref, kb_ref))
    v = _dot(xn, w_ref[:, MEM_W:])
    put_v = _put_heads_and_cols(v_ref, vb_ref)
    for h in range(HEADS):
        put_v(h, v[:, h * LANES:(h + 1) * LANES])


def _memkv(mem, gain, w, kn):
    n = mem.shape[0]
    tm = min(TOK_TILE, n)
    const = lambda shape: pl.BlockSpec(shape, lambda i: (0, 0))
    heads = pl.BlockSpec((tm, HEADS, LANES), lambda i: (i, 0, 0))
    dense = pl.BlockSpec((tm, MEM_W), lambda i: (i, 0))
    return pl.pallas_call(
        _memkv_body,
        grid=(n // tm,),
        in_specs=[pl.BlockSpec((tm, D_MODEL), lambda i: (i, 0)), const((1, D_MODEL)),
                  const((D_MODEL, 2 * MEM_W)), const((1, MEM_W))],
        out_specs=[heads, heads, dense, dense],
        out_shape=[jax.ShapeDtypeStruct((n, HEADS, LANES), F32)] * 2 + [jax.ShapeDtypeStruct((n, MEM_W), BF16)] * 2,
        compiler_params=_params(("parallel",)),
        name="memkv",
    )(mem, gain, w, kn)


def _split3(x):
    hi = x.astype(BF16)
    r1 = x - hi.astype(F32)
    mid = r1.astype(BF16)
    lo = (r1 - mid.astype(F32)).astype(BF16)
    return hi, mid, lo


def _gla_body(*refs, chunk, n_chunks, n_steps, has_state):
    if has_state:
        gq_ref, gk_ref, gg_ref, gv_ref, gr_ref, on_ref, s0_ref, go_ref, st_ref, state = refs
    else:
        gq_ref, gk_ref, gg_ref, gv_ref, gr_ref, on_ref, go_ref, st_ref, state = refs
    t = pl.program_id(1)

    @pl.when(t == 0)
    def _():
        for h in range(GLA_HEADS):
            if has_state:
                state[h] = s0_ref[0, h].T
            else:
                state[h] = jnp.zeros((GLA_DV, GLA_DK), F32)

    row = lax.broadcasted_iota(jnp.int32, (chunk, chunk), 0)
    col = lax.broadcasted_iota(jnp.int32, (chunk, chunk), 1)
    causal = row >= col
    tril = causal.astype(BF16)

    chunks = [slice(c * chunk, (c + 1) * chunk) for c in range(n_chunks)]
    kcs = [slice(h * GLA_DK, (h + 1) * GLA_DK) for h in range(GLA_HEADS)]
    vcs = [slice(h * GLA_DV, (h + 1) * GLA_DV) for h in range(GLA_HEADS)]

    bs = []
    for rows in chunks:
        g_hi, g_mid, g_lo = _split3(gg_ref[rows, :])
        bs.append(_dot(tril, g_hi) + _dot(tril, g_mid) + _dot(tril, g_lo))

    qes, kes, kds, decays = [], [], [], []
    for rows, b in zip(chunks, bs):
        b_last = b[chunk - 1:chunk, :]
        q = gq_ref[rows, :]
        k = gk_ref[rows, :]
        qes.append((q * jnp.exp(b)).astype(BF16))
        kes.append((k * jnp.exp(-b)).astype(BF16))
        kds.append((k * jnp.exp(b_last - b)).astype(BF16))
        decays.append(jnp.exp(b_last))

    a_s = [[jnp.where(causal, _dot_nt(qe[:, kc], ke[:, kc]), 0.0).astype(BF16) for kc in kcs]
           for qe, ke in zip(qes, kes)]
    incs = [[_dot_tn(gv_ref[rows, vc], kd[:, kc]) for kc, vc in zip(kcs, vcs)] for rows, kd in zip(chunks, kds)]

    s_in = []
    s_cur = [state[h] for h in range(GLA_HEADS)]
    for c in range(n_chunks):
        s_in.append([s.astype(BF16) for s in s_cur])
        s_cur = [s * decays[c][:, kc] + inc for s, kc, inc in zip(s_cur, kcs, incs[c])]
    for h in range(GLA_HEADS):
        state[h] = s_cur[h]

    for c, rows in enumerate(chunks):
        for h, (kc, vc) in enumerate(zip(kcs, vcs)):
            o = _dot_nt(qes[c][:, kc], s_in[c][h]) + _dot(a_s[c][h], gv_ref[rows, vc])
            go_ref[rows, vc] = (_rms(o, on_ref[...]) * jax.nn.silu(gr_ref[rows, vc])).astype(BF16)

    @pl.when(t == n_steps - 1)
    def _():
        for h in range(GLA_HEADS):
            st_ref[0, h] = state[h].T


def _gla(gq, gk, gg, gv, gr, onorm, s0, batch, seq, chunk):
    tt = min(TOK_TILE, seq)
    n_steps = seq // tt
    tok = lambda width: pl.BlockSpec((tt, width), lambda b, t: (b * n_steps + t, 0))
    st_spec = pl.BlockSpec((1, GLA_HEADS, GLA_DK, GLA_DV), lambda b, t: (b, 0, 0, 0))
    in_specs = [tok(GLA_QK), tok(GLA_QK), tok(GLA_QK), tok(GLA_V), tok(GLA_V),
                pl.BlockSpec((1, GLA_DV), lambda b, t: (0, 0))]
    args = [gq, gk, gg, gv, gr, onorm]
    if s0 is not None:
        in_specs.append(st_spec)
        args.append(s0)
    return pl.pallas_call(
        functools.partial(_gla_body, chunk=chunk, n_chunks=tt // chunk, n_steps=n_steps, has_state=s0 is not None),
        grid=(batch, n_steps),
        in_specs=in_specs,
        out_specs=[tok(GLA_V), st_spec],
        out_shape=[jax.ShapeDtypeStruct((batch * seq, GLA_V), BF16),
                   jax.ShapeDtypeStruct((batch, GLA_HEADS, GLA_DK, GLA_DV), F32)],
        scratch_shapes=[pltpu.VMEM((GLA_HEADS, GLA_DV, GLA_DK), F32)],
        compiler_params=_params(("parallel", "arbitrary")),
        name="gla",
    )(*args)


def _t5_bucket(rel):
    nb = REL_BUCKETS // 2
    max_exact = nb // 2
    ret = jnp.where(rel > 0, nb, 0)
    n = jnp.abs(rel)
    nf = jnp.maximum(n, 1).astype(F32)
    large = max_exact + (jnp.log(nf / max_exact) / math.log(REL_MAX_DIST / max_exact)
                         * (nb - max_exact)).astype(jnp.int32)
    large = jnp.minimum(large, nb - 1)
    return ret + jnp.where(n < max_exact, n, large)


def _bucket_tile(q_pos, k_pos):
    visible = (k_pos[None, :] // CHUNK) <= (q_pos[:, None] // CHUNK)
    return jnp.where(visible, _t5_bucket(k_pos[None, :] - q_pos[:, None]), MASKED_BUCKET).astype(jnp.int32)


def _bias_from_buckets(idx, tab_ref, head):
    def step(bk, acc):
        return jnp.where(idx == bk, tab_ref[bk * DIFF_HEADS + head], acc)
    return lax.fori_loop(0, REL_BUCKETS, step, jnp.full(idx.shape, NEG_INF, F32))


def _lambda(lam_ref, lam_init):
    l = lam_ref[...]
    return (jnp.exp(jnp.sum(l[0:1] * l[1:2], axis=-1, keepdims=True))
            - jnp.exp(jnp.sum(l[2:3] * l[3:4], axis=-1, keepdims=True)) + lam_init)


def _comp_masks(q):
    lane = lax.broadcasted_iota(jnp.int32, q.shape, 1)
    zero = jnp.zeros_like(q)
    return jnp.where(lane < DIFF_DH, q, zero), jnp.where(lane < DIFF_DH, zero, q)


def _diff_finish(o0, o1, lam, on_ref, lam_init, out_dtype):
    o = o0 - lam * o1
    return (_rms(o, on_ref[...]) * (1.0 - lam_init)).astype(out_dtype)


def _diff_prompt_body(tab_ref, q_ref, kb, vb, idx_ref, lam_ref, on_ref, o_ref, bias, v1t, sc, es, *, seq, lam_init):
    b = pl.program_id(0)
    h = pl.program_id(1)

    @pl.when(b == 0)
    def _():
        for t in range(2):
            bias[h, t] = _bias_from_buckets(idx_ref[t], tab_ref, h)

    v1t[:DIFF_DV, :] = vb[...].T
    v1t[DIFF_DV:, :] = jnp.ones((ONES_ROWS, seq), BF16)
    lam = _lambda(lam_ref, lam_init)
    far_bias = tab_ref[(REL_BUCKETS // 2 - 1) * DIFF_HEADS + h]

    items = [(i, c) for i in range(seq // Q_BLOCK) for c in range(2)]
    comps = {}

    def write_scores(n):
        i, c = items[n]
        if i not in comps:
            comps[i] = _comp_masks(q_ref[i * Q_BLOCK:(i + 1) * Q_BLOCK, :])
        w = (i + 1) * Q_BLOCK
        sc[n % SCORE_SLOTS, 0:w, :] = _dot_nt(kb[0:w, :], comps[i][c])

    def softmax_pv(n):
        i, _ = items[n]
        w = (i + 1) * Q_BLOCK
        far = max(i - 1, 0) * Q_BLOCK
        s_ref, e_ref = sc.at[n % SCORE_SLOTS], es.at[n % 2]
        near = s_ref[far:w, :] + bias[h, min(i, 1), 0:w - far, :]
        m = near.max(axis=0, keepdims=True)
        if far:
            m = jnp.maximum(m, s_ref[0:far, :].max(axis=0, keepdims=True) + far_bias)
            e_ref[0:far, :] = jnp.exp(s_ref[0:far, :] - (m - far_bias)).astype(BF16)
        e_ref[far:w, :] = jnp.exp(near - m).astype(BF16)
        ol = _dot(v1t[:, 0:w], e_ref[0:w, :])
        return ol[:DIFF_DV, :] * (1.0 / ol[DIFF_DV:DIFF_DV + 1, :])

    for n in range(min(LOOKAHEAD, len(items))):
        write_scores(n)
    outs = {}
    for n, (i, c) in enumerate(items):
        if n + LOOKAHEAD < len(items):
            write_scores(n + LOOKAHEAD)
        outs[c] = softmax_pv(n)
        if c == 1:
            o = outs[0] - lam * outs[1]
            o = o * lax.rsqrt(jnp.mean(o * o, axis=0, keepdims=True) + EPS) * on_ref[...] * (1.0 - lam_init)
            o_ref[i * Q_BLOCK:(i + 1) * Q_BLOCK, :] = o.T.astype(o_ref.dtype)


def _diff_prompt(table, dq, dk, dv, lam_p, onorm, batch, seq, lam_init):
    q_pos = jnp.arange(Q_BLOCK)
    idx = jnp.stack([_bucket_tile(q_pos, jnp.arange(2 * Q_BLOCK)).T,
                     _bucket_tile(q_pos + Q_BLOCK, jnp.arange(2 * Q_BLOCK)).T])
    head = lambda: pl.BlockSpec((seq, LANES), lambda b, h: (b, h))
    return pl.pallas_call(
        functools.partial(_diff_prompt_body, seq=seq, lam_init=lam_init),
        grid=(batch, DIFF_HEADS),
        in_specs=[
            pl.BlockSpec(memory_space=pltpu.SMEM),
            head(), head(), head(),
            pl.BlockSpec((2, 2 * Q_BLOCK, Q_BLOCK), lambda b, h: (0, 0, 0)),
            pl.BlockSpec((4, DIFF_DH), lambda b, h: (0, 0)),
            pl.BlockSpec((DIFF_DV, 1), lambda b, h: (0, 0)),
        ],
        out_specs=head(),
        out_shape=jax.ShapeDtypeStruct((batch * seq, DIFF_V), BF16),
        scratch_shapes=[pltpu.VMEM((DIFF_HEADS, 2, 2 * Q_BLOCK, Q_BLOCK), F32),
                        pltpu.VMEM((DIFF_DV + ONES_ROWS, seq), BF16),
                        pltpu.VMEM((SCORE_SLOTS, seq, Q_BLOCK), F32),
                        pltpu.VMEM((2, seq, Q_BLOCK), BF16)],
        compiler_params=_params(("arbitrary", "arbitrary")),
        name="diff_prompt",
    )(table, dq, dk, dv, idx, lam_p, onorm.reshape(DIFF_DV, 1))


def _diff_sample_body(tab_ref, q_ref, k_ref, v_ref, ck_ref, cv_ref, idxc_ref, idxn_ref, lam_ref, on_ref, o_ref,
                      bias_c, bias_n, *, seq, lam_init):
    groups = [(h, c) for h in range(DIFF_HEADS) for c in range(2)]

    @pl.when(pl.program_id(0) == 0)
    def _():
        for g, (h, _) in enumerate(groups):
            bias_c[g * seq:(g + 1) * seq, :] = _bias_from_buckets(idxc_ref[h], tab_ref, h)
        for h in range(DIFF_HEADS):
            bias_n[h] = _bias_from_buckets(idxn_ref[...], tab_ref, h)

    lam = _lambda(lam_ref, lam_init)
    head_cols = [slice(h * LANES, (h + 1) * LANES) for h in range(DIFF_HEADS)]
    q_groups = [q_c for cols in head_cols for q_c in _comp_masks(q_ref[:, cols])]
    s_old = _dot_nt(jnp.concatenate(q_groups, axis=0), ck_ref[...].astype(BF16)) + bias_c[...]
    s_new = jnp.concatenate([_dot_nt(q_g, k_ref[:, head_cols[h]]) + bias_n[h]
                             for q_g, (h, _) in zip(q_groups, groups)], axis=0)
    m = jnp.maximum(s_old.max(axis=-1, keepdims=True), s_new.max(axis=-1, keepdims=True))
    e_old = jnp.exp(s_old - m)
    e_new = jnp.exp(s_new - m)
    l = e_old.sum(axis=-1, keepdims=True) + e_new.sum(axis=-1, keepdims=True)
    o_old = _dot(e_old.astype(BF16), cv_ref[...].astype(BF16))
    e_new = e_new.astype(BF16)
    o_new = jnp.concatenate([_dot(e_new[g * seq:(g + 1) * seq, :], v_ref[:, head_cols[h]])
                             for g, (h, _) in enumerate(groups)], axis=0)
    o = (o_old + o_new) * (1.0 / l)
    for h, cols in enumerate(head_cols):
        o0 = o[(2 * h) * seq:(2 * h + 1) * seq, :]
        o1 = o[(2 * h + 1) * seq:(2 * h + 2) * seq, :]
        o_ref[:, cols] = _diff_finish(o0, o1, lam, on_ref, lam_init, o_ref.dtype)


def _diff_sample(table, dq, dk, dv, cache_k, cache_v, lam_p, onorm, batch, seq, past, lam_init):
    q_pos = past + jnp.arange(seq)
    idx_n = _bucket_tile(q_pos, past + jnp.arange(seq))
    idx_rows = jnp.repeat(_bucket_tile(q_pos, jnp.arange(past)), HEADS, axis=1)
    row_head = jnp.arange(past * HEADS) % HEADS
    idx_c = jnp.where(row_head[None, None, :] == jnp.arange(HEADS)[:, None, None], idx_rows[None], MASKED_BUCKET)
    new_q = pl.BlockSpec((seq, DIFF_QK), lambda b: (b, 0))
    old = lambda: pl.BlockSpec((past * HEADS, LANES), lambda b: (b, 0))
    const = lambda shape: pl.BlockSpec(shape, lambda b: (0,) * len(shape))
    n_groups = 2 * DIFF_HEADS
    return pl.pallas_call(
        functools.partial(_diff_sample_body, seq=seq, lam_init=lam_init),
        grid=(batch,),
        in_specs=[pl.BlockSpec(memory_space=pltpu.SMEM), new_q, new_q, new_q, old(), old(),
                  const((HEADS, seq, past * HEADS)), const((seq, seq)), const((4, DIFF_DH)), const((1, DIFF_DV))],
        out_specs=new_q,
        out_shape=jax.ShapeDtypeStruct((batch * seq, DIFF_V), BF16),
        scratch_shapes=[pltpu.VMEM((n_groups * seq, past * HEADS), F32), pltpu.VMEM((DIFF_HEADS, seq, seq), F32)],
        compiler_params=_params(("arbitrary",)),
        name="diff_sample",
    )(table, dq, dk, dv, cache_k, cache_v, idx_c, idx_n, lam_p, onorm)


def _memattn_body(q_ref, k_ref, v_ref, o_ref, *, per_head):
    head_cols = [slice(h * MEM_DH, (h + 1) * MEM_DH) for h in range(MEM_HEADS)]
    if per_head:
        ks = [k_ref[:, h, :].astype(BF16) for h in range(MEM_HEADS)]
        vs = [v_ref[:, h, :].astype(BF16) for h in range(MEM_HEADS)]
    else:
        ks = [k_ref[:, cols] for cols in head_cols]
        vs = [v_ref[:, cols] for cols in head_cols]
    scores = [_dot_nt(q_ref[:, cols], k) * (MEM_DH ** -0.5) for cols, k in zip(head_cols, ks)]
    probs = []
    for s in scores:
        e = jnp.exp(s - s.max(axis=-1, keepdims=True))
        probs.append((e * (1.0 / e.sum(axis=-1, keepdims=True))).astype(BF16))
    for cols, p, v in zip(head_cols, probs, vs):
        o_ref[:, cols] = _dot(p, v).astype(o_ref.dtype)


def _memattn(mq, mk, mv, batch, seq, n_mem):
    tq = min(TOK_TILE, seq)
    nq = seq // tq
    q_spec = pl.BlockSpec((tq, MEM_W), lambda b, i: (b * nq + i, 0))
    per_head = mk.ndim == 3
    if per_head:
        kv_spec = pl.BlockSpec((n_mem, HEADS, LANES), lambda b, i: (b, 0, 0))
    else:
        kv_spec = pl.BlockSpec((n_mem, MEM_W), lambda b, i: (b, 0))
    return pl.pallas_call(
        functools.partial(_memattn_body, per_head=per_head),
        grid=(batch, nq),
        in_specs=[q_spec, kv_spec, kv_spec],
        out_specs=q_spec,
        out_shape=jax.ShapeDtypeStruct((batch * seq, MEM_W), BF16),
        compiler_params=_params(("parallel", "arbitrary")),
        name="memattn",
    )(mq, mk, mv)


def _outproj_body(x_ref, g_ref, d_ref, m_ref, wg_ref, wd_ref, wm_ref, ng_ref, o_ref, on_ref):
    o = (x_ref[...] + _dot(g_ref[...], wg_ref[...]) + _dot(d_ref[...], wd_ref[...])
         + _dot(m_ref[...], wm_ref[...]))
    o_ref[...] = o
    on_ref[...] = _rms(o, ng_ref[...]).astype(BF16)


def _outproj(x, g, d, m, wo, next_gain):
    n = x.shape[0]
    tm = min(TOK_TILE, n)
    tok = lambda width: pl.BlockSpec((tm, width), lambda i: (i, 0))
    w_rows = lambda rows, blk: pl.BlockSpec((rows, D_MODEL), lambda i: (blk, 0))
    return pl.pallas_call(
        _outproj_body,
        grid=(n // tm,),
        in_specs=[tok(D_MODEL), tok(GLA_V), tok(DIFF_V), tok(MEM_W),
                  w_rows(GLA_V, 0), w_rows(DIFF_V, GLA_V // DIFF_V), w_rows(MEM_W, (GLA_V + DIFF_V) // MEM_W),
                  pl.BlockSpec((1, D_MODEL), lambda i: (0, 0))],
        out_specs=[tok(D_MODEL), tok(D_MODEL)],
        out_shape=[jax.ShapeDtypeStruct((n, D_MODEL), F32), jax.ShapeDtypeStruct((n, D_MODEL), BF16)],
        compiler_params=_params(("parallel",)),
        name="outproj",
    )(x, g, d, m, wo, wo, wo, next_gain)


def _row(v):
    return v.reshape(1, -1).astype(F32)


def kernel(x_prompt, x_sample, mem_prompt, cache_diff_k, cache_diff_v, state_gla, cache_mem_k, cache_mem_v, rel_bias_table, norm_ffn1, w_ffn1_in, w_ffn1_out, norm_mix, w_in, w_gla_g2, b_gla_g, gla_out_norm, diff_q_norm, diff_k_norm, diff_lambda, diff_out_norm, mem_norm, w_mem_kv, mem_q_norm, mem_k_norm, w_o, norm_ffn2, w_ffn2_in, w_ffn2_out, norm_final):
    depth = norm_ffn1.shape[0]
    assert depth == 1, "single-layer step"
    layer = 0
    batch, seq, _ = x_prompt.shape
    dec_batch, dec_seq, _ = x_sample.shape
    past = cache_diff_k.shape[2]
    n_mem = mem_prompt.shape[1]
    lam_init = 0.8 - 0.6 * math.exp(-0.3 * layer)

    ffn1 = (w_ffn1_in[layer].astype(BF16), w_ffn1_out[layer].astype(BF16))
    w_g2 = jnp.pad(w_gla_g2[layer].astype(BF16), ((0, LANES - GLA_GATE_RANK), (0, 0)))
    b_g = _row(b_gla_g[layer])
    qn = _row(jnp.tile(diff_q_norm[layer], DIFF_QK // DIFF_DH))
    kn = _row(jnp.tile(diff_k_norm[layer], DIFF_QK // DIFF_DH))
    mqn = _row(jnp.tile(mem_q_norm[layer], MEM_HEADS))
    mkn = _row(jnp.tile(mem_k_norm[layer], MEM_HEADS))
    table = rel_bias_table.astype(F32).reshape(-1)
    lam_p = diff_lambda[layer].astype(F32)
    gla_on = _row(gla_out_norm[layer])
    diff_on = _row(diff_out_norm[layer])
    per_head = lambda a: a.reshape(-1, HEADS, LANES)

    mk, mv, mk_b, mv_b = _memkv(mem_prompt.reshape(batch * n_mem, D_MODEL), _row(mem_norm[layer]),
                                w_mem_kv[layer].astype(BF16), mkn)

    n1 = _row(norm_ffn1[layer])
    n_in_blocks = w_ffn2_in.shape[2] // LANES
    xp1, w2_in, w2_out, w_t, wo = _ffn(
        x_prompt.reshape(batch * seq, D_MODEL), n1, *ffn1,
        cast_later=[(w_ffn2_in[layer], (D_MODEL, LANES), 0),
                    (w_ffn2_out[layer], (D_FF // n_in_blocks, D_MODEL), n_in_blocks),
                    (jnp.swapaxes(w_in[layer], 0, 1), (CAST_ROWS_W_IN, D_MODEL), 0),
                    (w_o[layer], (CAST_ROWS_W_O, D_MODEL), 0)])
    ffn2 = (w2_in, w2_out)
    xs1 = _ffn(x_sample.reshape(dec_batch * dec_seq, D_MODEL), n1, *ffn1)

    def layer_fn(x, b, t, chunk, s0, diff_fn, mem_k, mem_v):
        gq, gk, gv, gr, gg, dq, dk, dv, mq, dk_b, dv_b = _proj(x, _row(norm_mix[layer]), w_t, w_g2, b_g,
                                                               qn, kn, mqn)
        g_out, g_state = _gla(gq, gk, gg, gv, gr, gla_on, s0, b, t, chunk)
        d_out = diff_fn(dq, dk_b, dv_b)
        m_out = _memattn(mq, mem_k, mem_v, b, t, n_mem)
        x, xn = _outproj(x, g_out, d_out, m_out, wo, _row(norm_ffn2[layer]))
        x = _ffn(x, xn, *ffn2, final_gain=_row(norm_final[layer]))
        return x, dk, dv, g_state

    yp, dk_p, dv_p, g_p = layer_fn(
        xp1, batch, seq, CHUNK, None,
        lambda dq, dk, dv: _diff_prompt(table, dq, dk, dv, lam_p, diff_on, batch, seq, lam_init), mk_b, mv_b)
    ys, dk_s, dv_s, g_s = layer_fn(
        xs1, dec_batch, dec_seq, dec_seq, state_gla[layer],
        lambda dq, dk, dv: _diff_sample(table, dq, dk, dv, cache_diff_k[layer].reshape(-1, LANES),
                                        cache_diff_v[layer].reshape(-1, LANES),
                                        lam_p, diff_on, dec_batch, dec_seq, past, lam_init),
        per_head(cache_mem_k[layer]), per_head(cache_mem_v[layer]))

    head4 = lambda a, b, t: a.reshape(1, b, t, HEADS, LANES)
    return (yp.reshape(batch, seq, D_MODEL), ys.reshape(dec_batch, dec_seq, D_MODEL),
            head4(dk_p, batch, seq), head4(dv_p, batch, seq), g_p[None],
            head4(mk, batch, n_mem), head4(mv, batch, n_mem),
            head4(dk_s, dec_batch, dec_seq), head4(dv_s, dec_batch, dec_seq), g_s[None])
```

```python
import functools
import math

import jax
import jax.numpy as jnp
from jax import lax
from jax.experimental import pallas as pl
from jax.experimental.pallas import tpu as pltpu

F32 = jnp.float32
BF16 = jnp.bfloat16

D_MODEL = 2048
CHUNK = 64
EPS = 1e-6
NEG_INF = -1e30
GLA_HEADS, GLA_DK, GLA_DV, GLA_GATE_RANK, GLA_GATE_NORM = 4, 128, 256, 16, 16.0
DIFF_HEADS, DIFF_DH, DIFF_DV = 4, 64, 128
MEM_HEADS, MEM_DH = 4, 128
REL_BUCKETS, REL_MAX_DIST = 32, 128
D_FF = 5504
GLA_QK = GLA_HEADS * GLA_DK
GLA_V = GLA_HEADS * GLA_DV
DIFF_QK = DIFF_HEADS * 2 * DIFF_DH
DIFF_V = DIFF_HEADS * DIFF_DV
MEM_W = MEM_HEADS * MEM_DH
GLR_OFF = 2 * GLA_QK + 2 * GLA_V

LANES = 128
HEADS = 4
FF_TILE = 512
N_FF_TILES = -(-D_FF // FF_TILE)
FF_LAST_START = D_FF - FF_TILE
FF_OVERLAP = N_FF_TILES * FF_TILE - D_FF
FFN_TOK_TILE = 1024
TOK_TILE = 512
GLA_TILE = 1024
MEMATTN_TILE = 2048
PROJ_TILE = 256
Q_BLOCK = 256
MASKED_BUCKET = REL_BUCKETS
MIB = 1024 * 1024
VMEM_CAP_MIB = 60
CAST_ROWS_W_IN = 48
CAST_ROWS_W_O = 16

_W_GQ, _W_GK, _W_GV, _W_GR, _W_GLR = 0, GLA_QK, 2 * GLA_QK, 2 * GLA_QK + GLA_V, GLR_OFF
_W_DQ = GLR_OFF + GLA_GATE_RANK
_W_DK = _W_DQ + DIFF_QK
_W_DV = _W_DK + DIFF_QK
_W_MQ = _W_DV + DIFF_V
IN_WIDTH = _W_MQ + MEM_W


def _dot(a, b):
    return jnp.dot(a, b, preferred_element_type=F32)


def _dot_nt(a, b):
    return lax.dot_general(a, b, (((1,), (1,)), ((), ())), preferred_element_type=F32)


def _dot_tn(a, b):
    return lax.dot_general(a, b, (((0,), (0,)), ((), ())), preferred_element_type=F32)


def _rms(x, gain):
    return x * lax.rsqrt(jnp.mean(x * x, axis=-1, keepdims=True) + EPS) * gain


def _params(sem, vmem_mib=48):
    return pltpu.CompilerParams(dimension_semantics=sem, vmem_limit_bytes=min(vmem_mib, VMEM_CAP_MIB) * MIB)


def _ffn_body(*refs, n_ff, final_norm, n_cast, prenormed):
    x_ref, g_ref, wg_ref, wu_ref, wo_ref = refs[:5]
    refs = refs[5:]
    if final_norm:
        fg_ref, refs = refs[0], refs[1:]
    cast_src, refs = refs[:n_cast], refs[n_cast:]
    o_ref, refs = refs[0], refs[1:]
    cast_dst, refs = refs[:n_cast], refs[n_cast:]
    xn_ref = g_ref if prenormed else refs[0]
    j = pl.program_id(1)

    @pl.when(j == 0)
    def _():
        x = x_ref[...]
        if not prenormed:
            xn_ref[...] = _rms(x, g_ref[...]).astype(BF16)
        o_ref[...] = x

    def tile(lo):
        for src, dst in zip(cast_src, cast_dst):
            dst[...] = src[...].astype(BF16)
        xn = xn_ref[...]
        gate = _dot(xn, wg_ref[:, lo:])
        up = _dot(xn, wu_ref[:, lo:])
        act = (jax.nn.silu(gate) * up).astype(BF16)
        return 0.5 * _dot(act, wo_ref[lo:, :])

    @pl.when(j < n_ff - 1)
    def _():
        o_ref[...] += tile(0)

    @pl.when(j == n_ff - 1)
    def _():
        o = o_ref[...] + tile(FF_OVERLAP)
        o_ref[...] = _rms(o, fg_ref[...]) if final_norm else o


def _ffn(x, gain_or_xn, w_in, w_out, final_gain=None, cast_later=()):
    n = x.shape[0]
    tm = min(FFN_TOK_TILE, n)
    n_ff = N_FF_TILES
    row = pl.BlockSpec((1, D_MODEL), lambda i, j: (0, 0))
    prenormed = gain_or_xn.shape != (1, D_MODEL)
    start = lambda j: jnp.minimum(j * (FF_TILE // LANES), FF_LAST_START // LANES)
    w_in_tile = (pl.Element(D_MODEL), pl.Element(FF_TILE))
    tok = pl.BlockSpec((tm, D_MODEL), lambda i, j: (i, 0))
    in_specs = [
        tok,
        tok if prenormed else row,
        pl.BlockSpec(w_in_tile, lambda i, j: (0, start(j) * LANES)),
        pl.BlockSpec(w_in_tile, lambda i, j: (0, (D_FF // LANES + start(j)) * LANES)),
        pl.BlockSpec((pl.Element(FF_TILE), pl.Element(D_MODEL)), lambda i, j: (start(j) * LANES, 0)),
    ]
    args = [x, gain_or_xn, w_in, w_in, w_out]
    if final_gain is not None:
        in_specs.append(row)
        args.append(final_gain)
    vmem_bytes = (4 * tm * D_MODEL * 4 + (2 if prenormed else 1) * tm * D_MODEL * 2 + 6 * D_MODEL * FF_TILE * 2
                  + 3 * tm * FF_TILE * 4)
    out_specs = [pl.BlockSpec((tm, D_MODEL), lambda i, j: (i, 0))]
    out_shape = [jax.ShapeDtypeStruct((n, D_MODEL), F32)]
    n_steps = (n // tm) * n_ff
    for mat, blk, first in cast_later:
        rows, cols = mat.shape
        assert rows % blk[0] == 0 and cols % blk[1] == 0 and (blk[0] == rows or blk[1] == cols)
        n_blk = (rows // blk[0]) * (cols // blk[1])
        assert first + n_blk <= n_steps
        by_rows = blk[1] == cols

        def index(i, j, first=first, n_blk=n_blk, by_rows=by_rows):
            k = jnp.clip(i * n_ff + j - first, 0, n_blk - 1)
            return (k, 0) if by_rows else (0, k)

        in_specs.append(pl.BlockSpec(blk, index))
        args.append(mat)
        out_specs.append(pl.BlockSpec(blk, index))
        out_shape.append(jax.ShapeDtypeStruct(mat.shape, BF16))
        vmem_bytes += 2 * blk[0] * blk[1] * (4 + 2)
    outs = pl.pallas_call(
        functools.partial(_ffn_body, n_ff=n_ff, final_norm=final_gain is not None, n_cast=len(cast_later),
                          prenormed=prenormed),
        grid=(n // tm, n_ff),
        in_specs=in_specs,
        out_specs=out_specs,
        out_shape=out_shape,
        scratch_shapes=[] if prenormed else [pltpu.VMEM((tm, D_MODEL), BF16)],
        compiler_params=_params(("arbitrary", "arbitrary"), vmem_bytes // MIB + 4),
        name="ffn",
    )(*args)
    return outs[0] if not cast_later else outs


def _group_rms(acc, gain_ref, width, scale, put):
    lane = lax.broadcasted_iota(jnp.int32, (1, LANES), 1)
    low = lane < width
    for c in range(acc.shape[1] // LANES):
        cols = slice(c * LANES, (c + 1) * LANES)
        xc = acc[:, cols]
        sq = xc * xc
        if width == LANES:
            r = lax.rsqrt(jnp.mean(sq, axis=-1, keepdims=True) + EPS)
        else:
            s_lo = jnp.sum(jnp.where(low, sq, 0.0), axis=-1, keepdims=True)
            s_hi = jnp.sum(jnp.where(low, 0.0, sq), axis=-1, keepdims=True)
            r = jnp.where(low, lax.rsqrt(s_lo / width + EPS), lax.rsqrt(s_hi / width + EPS))
        y = xc * r * gain_ref[:, cols]
        if scale != 1.0:
            y = y * scale
        put(c, y)


def _put_cols(ref):
    def put(c, y):
        ref[:, c * LANES:(c + 1) * LANES] = y.astype(ref.dtype)
    return put


def _put_heads_and_cols(heads_ref, cols_ref):
    def put(c, y):
        heads_ref[:, c, :] = y
        cols_ref[:, c * LANES:(c + 1) * LANES] = y.astype(cols_ref.dtype)
    return put


def _proj_body(x_ref, g_ref, wt_ref, wg2_ref, bg_ref, qn_ref, kn_ref, mqn_ref,
               gq_ref, gk_ref, gv_ref, gr_ref, gg_ref, dq_ref, dk_ref, dv_ref, mq_ref, dkb_ref, dvb_ref, xn_ref):
    xn_ref[...] = _rms(x_ref[...], g_ref[...]).astype(BF16)

    def cols(start, width):
        return _dot_nt(xn_ref[...], wt_ref[start:start + width, :])

    gq_ref[...] = cols(_W_GQ, GLA_QK) * (GLA_DK ** -0.5)
    gk_ref[...] = cols(_W_GK, GLA_QK)
    gv_ref[...] = cols(_W_GV, GLA_V).astype(BF16)
    gr_ref[...] = cols(_W_GR, GLA_V)
    glr = cols(_W_GLR, LANES)
    z = _dot(glr.astype(BF16), wg2_ref[...]) + bg_ref[...]
    gg_ref[...] = jax.nn.log_sigmoid(z) / GLA_GATE_NORM
    _group_rms(cols(_W_DQ, DIFF_QK), qn_ref, DIFF_DH, DIFF_DH ** -0.5, _put_cols(dq_ref))
    _group_rms(cols(_W_DK, DIFF_QK), kn_ref, DIFF_DH, 1.0, _put_heads_and_cols(dk_ref, dkb_ref))
    dv = cols(_W_DV, DIFF_V)
    put_v = _put_heads_and_cols(dv_ref, dvb_ref)
    for h in range(HEADS):
        put_v(h, dv[:, h * LANES:(h + 1) * LANES])
    _group_rms(cols(_W_MQ, MEM_W), mqn_ref, MEM_DH, 1.0, _put_cols(mq_ref))


def _proj(x, gain, wt, wg2, bg, qn, kn, mqn):
    n = x.shape[0]
    tm = min(PROJ_TILE, n)
    const = lambda shape: pl.BlockSpec(shape, lambda i: (0, 0))
    out = lambda width: pl.BlockSpec((tm, width), lambda i: (i, 0))
    heads = pl.BlockSpec((tm, HEADS, LANES), lambda i: (i, 0, 0))
    shp = lambda width, dt: jax.ShapeDtypeStruct((n, width), dt)
    shp_heads = jax.ShapeDtypeStruct((n, HEADS, LANES), F32)
    return pl.pallas_call(
        _proj_body,
        grid=(n // tm,),
        in_specs=[
            pl.BlockSpec((tm, D_MODEL), lambda i: (i, 0)),
            const((1, D_MODEL)),
            pl.BlockSpec((IN_WIDTH, D_MODEL), lambda i: (0, 0), pipeline_mode=pl.Buffered(1)),
            const((LANES, GLA_QK)),
            const((1, GLA_QK)),
            const((1, DIFF_QK)),
            const((1, DIFF_QK)),
            const((1, MEM_W)),
        ],
        out_specs=[out(GLA_QK), out(GLA_QK), out(GLA_V), out(GLA_V), out(GLA_QK),
                   out(DIFF_QK), heads, heads, out(MEM_W), out(DIFF_QK), out(DIFF_V)],
        out_shape=[shp(GLA_QK, F32), shp(GLA_QK, F32), shp(GLA_V, BF16), shp(GLA_V, F32), shp(GLA_QK, F32),
                   shp(DIFF_QK, BF16), shp_heads, shp_heads, shp(MEM_W, BF16), shp(DIFF_QK, BF16), shp(DIFF_V, BF16)],
        scratch_shapes=[pltpu.VMEM((tm, D_MODEL), BF16)],
        compiler_params=_params(("parallel",)),
        name="proj",
    )(x, gain, wt, wg2, bg, qn, kn, mqn)


def _memkv_body(x_ref, g_ref, w_ref, kn_ref, k_ref, v_ref, kb_ref, vb_ref):
    xn = _rms(x_ref[...], g_ref[...]).astype(BF16)
    _group_rms(_dot(xn, w_ref[:, :MEM_W]), kn_ref, MEM_DH, 1.0, _put_heads_and_cols(k_ref, kb_ref))
    v = _dot(xn, w_ref[:, MEM_W:])
    put_v = _put_heads_and_cols(v_ref, vb_ref)
    for h in range(HEADS):
        put_v(h, v[:, h * LANES:(h + 1) * LANES])


def _memkv(mem, gain, w, kn):
    n = mem.shape[0]
    tm = min(TOK_TILE, n)
    const = lambda shape: pl.BlockSpec(shape, lambda i: (0, 0))
    heads = pl.BlockSpec((tm, HEADS, LANES), lambda i: (i, 0, 0))
    dense = pl.BlockSpec((tm, MEM_W), lambda i: (i, 0))
    return pl.pallas_call(
        _memkv_body,
        grid=(n // tm,),
        in_specs=[pl.BlockSpec((tm, D_MODEL), lambda i: (i, 0)), const((1, D_MODEL)),
                  const((D_MODEL, 2 * MEM_W)), const((1, MEM_W))],
        out_specs=[heads, heads, dense, dense],
        out_shape=[jax.ShapeDtypeStruct((n, HEADS, LANES), F32)] * 2 + [jax.ShapeDtypeStruct((n, MEM_W), BF16)] * 2,
        compiler_params=_params(("parallel",)),
        name="memkv",
    )(mem, gain, w, kn)


def _split3(x):
    hi = x.astype(BF16)
    r1 = x - hi.astype(F32)
    mid = r1.astype(BF16)
    lo = (r1 - mid.astype(F32)).astype(BF16)
    return hi, mid, lo


def _gla_body(*refs, chunk, n_chunks, n_steps, has_state):
    if has_state:
        gq_ref, gk_ref, gg_ref, gv_ref, gr_ref, on_ref, s0_ref, go_ref, st_ref, state = refs
    else:
        gq_ref, gk_ref, gg_ref, gv_ref, gr_ref, on_ref, go_ref, st_ref, state = refs
    t = pl.program_id(1)

    @pl.when(t == 0)
    def _():
        for h in range(GLA_HEADS):
            if has_state:
                state[h] = s0_ref[0, h].T
            else:
                state[h] = jnp.zeros((GLA_DV, GLA_DK), F32)

    row = lax.broadcasted_iota(jnp.int32, (chunk, chunk), 0)
    col = lax.broadcasted_iota(jnp.int32, (chunk, chunk), 1)
    causal = row >= col
    tril = causal.astype(BF16)

    chunks = [slice(c * chunk, (c + 1) * chunk) for c in range(n_chunks)]
    kcs = [slice(h * GLA_DK, (h + 1) * GLA_DK) for h in range(GLA_HEADS)]
    vcs = [slice(h * GLA_DV, (h + 1) * GLA_DV) for h in range(GLA_HEADS)]

    bs = []
    for rows in chunks:
        g_hi, g_mid, g_lo = _split3(gg_ref[rows, :])
        bs.append(_dot(tril, g_hi) + _dot(tril, g_mid) + _dot(tril, g_lo))

    qes, kes, kds, decays = [], [], [], []
    for rows, b in zip(chunks, bs):
        b_last = b[chunk - 1:chunk, :]
        q = gq_ref[rows, :]
        k = gk_ref[rows, :]
        qes.append((q * jnp.exp(b)).astype(BF16))
        kes.append((k * jnp.exp(-b)).astype(BF16))
        kds.append((k * jnp.exp(b_last - b)).astype(BF16))
        decays.append(jnp.exp(b_last))

    a_s = [[jnp.where(causal, _dot_nt(qe[:, kc], ke[:, kc]), 0.0).astype(BF16) for kc in kcs]
           for qe, ke in zip(qes, kes)]
    incs = [[_dot_tn(gv_ref[rows, vc], kd[:, kc]) for kc, vc in zip(kcs, vcs)] for rows, kd in zip(chunks, kds)]

    s_in = []
    s_cur = [state[h] for h in range(GLA_HEADS)]
    for c in range(n_chunks):
        s_in.append([s.astype(BF16) for s in s_cur])
        s_cur = [s * decays[c][:, kc] + inc for s, kc, inc in zip(s_cur, kcs, incs[c])]
    for h in range(GLA_HEADS):
        state[h] = s_cur[h]

    for c, rows in enumerate(chunks):
        for h, (kc, vc) in enumerate(zip(kcs, vcs)):
            o = _dot_nt(qes[c][:, kc], s_in[c][h]) + _dot(a_s[c][h], gv_ref[rows, vc])
            go_ref[rows, vc] = (_rms(o, on_ref[...]) * jax.nn.silu(gr_ref[rows, vc])).astype(BF16)

    @pl.when(t == n_steps - 1)
    def _():
        for h in range(GLA_HEADS):
            st_ref[0, h] = state[h].T


def _gla(gq, gk, gg, gv, gr, onorm, s0, batch, seq, chunk):
    tt = min(GLA_TILE, seq)
    n_steps = seq // tt
    tok = lambda width: pl.BlockSpec((tt, width), lambda b, t: (b * n_steps + t, 0))
    st_spec = pl.BlockSpec((1, GLA_HEADS, GLA_DK, GLA_DV), lambda b, t: (b, 0, 0, 0))
    in_specs = [tok(GLA_QK), tok(GLA_QK), tok(GLA_QK), tok(GLA_V), tok(GLA_V),
                pl.BlockSpec((1, GLA_DV), lambda b, t: (0, 0))]
    args = [gq, gk, gg, gv, gr, onorm]
    if s0 is not None:
        in_specs.append(st_spec)
        args.append(s0)
    return pl.pallas_call(
        functools.partial(_gla_body, chunk=chunk, n_chunks=tt // chunk, n_steps=n_steps, has_state=s0 is not None),
        grid=(batch, n_steps),
        in_specs=in_specs,
        out_specs=[tok(GLA_V), st_spec],
        out_shape=[jax.ShapeDtypeStruct((batch * seq, GLA_V), BF16),
                   jax.ShapeDtypeStruct((batch, GLA_HEADS, GLA_DK, GLA_DV), F32)],
        scratch_shapes=[pltpu.VMEM((GLA_HEADS, GLA_DV, GLA_DK), F32)],
        compiler_params=_params(("parallel", "arbitrary")),
        name="gla",
    )(*args)


def _t5_bucket(rel):
    nb = REL_BUCKETS // 2
    max_exact = nb // 2
    ret = jnp.where(rel > 0, nb, 0)
    n = jnp.abs(rel)
    nf = jnp.maximum(n, 1).astype(F32)
    large = max_exact + (jnp.log(nf / max_exact) / math.log(REL_MAX_DIST / max_exact)
                         * (nb - max_exact)).astype(jnp.int32)
    large = jnp.minimum(large, nb - 1)
    return ret + jnp.where(n < max_exact, n, large)


def _bucket_tile(q_pos, k_pos):
    visible = (k_pos[None, :] // CHUNK) <= (q_pos[:, None] // CHUNK)
    return jnp.where(visible, _t5_bucket(k_pos[None, :] - q_pos[:, None]), MASKED_BUCKET).astype(jnp.int32)


def _bias_from_buckets(idx, tab_ref, head):
    def step(bk, acc):
        return jnp.where(idx == bk, tab_ref[bk * DIFF_HEADS + head], acc)
    return lax.fori_loop(0, REL_BUCKETS, step, jnp.full(idx.shape, NEG_INF, F32))


def _lambda(lam_ref, lam_init):
    l = lam_ref[...]
    return (jnp.exp(jnp.sum(l[0:1] * l[1:2], axis=-1, keepdims=True))
            - jnp.exp(jnp.sum(l[2:3] * l[3:4], axis=-1, keepdims=True)) + lam_init)


def _comp_masks(q):
    lane = lax.broadcasted_iota(jnp.int32, q.shape, 1)
    zero = jnp.zeros_like(q)
    return jnp.where(lane < DIFF_DH, q, zero), jnp.where(lane < DIFF_DH, zero, q)


def _diff_finish(o0, o1, lam, on_ref, lam_init, out_dtype):
    o = o0 - lam * o1
    return (_rms(o, on_ref[...]) * (1.0 - lam_init)).astype(out_dtype)


def _scores(q_c, parts):
    return [_dot_nt(q_c, k) + bias if jnp.ndim(bias) == 2 else _dot_nt(q_c, k) for k, _, bias in parts]


def _softmax_pv(scores, parts):
    shifts = [0.0 if jnp.ndim(bias) == 2 else bias for _, _, bias in parts]
    m = None
    for s, shift in zip(scores, shifts):
        part_max = s.max(axis=-1, keepdims=True) + shift
        m = part_max if m is None else jnp.maximum(m, part_max)
    ol = 0.0
    for s, shift, (_, v1, _) in zip(scores, shifts, parts):
        ol = ol + _dot(jnp.exp(s - (m - shift)).astype(BF16), v1)
    return ol[:, :DIFF_DV] * (1.0 / ol[:, DIFF_DV:])


def _diff_prompt_body(tab_ref, q_ref, kb, vb, idx_ref, lam_ref, on_ref, o_ref, bias, v1, *, seq, lam_init):
    b = pl.program_id(0)
    h = pl.program_id(1)

    @pl.when(b == 0)
    def _():
        for t in range(2):
            bias[h, t] = _bias_from_buckets(idx_ref[t], tab_ref, h)

    v1[:, :DIFF_DV] = vb[...]
    v1[:, DIFF_DV:] = jnp.ones((seq, LANES), BF16)
    lam = _lambda(lam_ref, lam_init)
    far_bias = tab_ref[(REL_BUCKETS // 2 - 1) * DIFF_HEADS + h]

    def key_parts(i):
        near0 = max(i - 1, 0) * Q_BLOCK
        near = slice(near0, near0 + 2 * Q_BLOCK)
        parts = [(kb[near, :], v1[near, :], bias[h, min(i, 1)])]
        if near0 > 0:
            parts.append((kb[0:near0, :], v1[0:near0, :], far_bias))
        return parts

    items = [(i, c) for i in range(seq // Q_BLOCK) for c in range(2)]
    comps = {}

    def scores_of(item):
        i, c = item
        if i not in comps:
            comps[i] = _comp_masks(q_ref[i * Q_BLOCK:(i + 1) * Q_BLOCK, :])
        return _scores(comps[i][c], key_parts(i))

    outs = {}
    ahead = scores_of(items[0])
    for n, (i, c) in enumerate(items):
        cur = ahead
        if n + 1 < len(items):
            ahead = scores_of(items[n + 1])
        outs[c] = _softmax_pv(cur, key_parts(i))
        if c == 1:
            o_ref[i * Q_BLOCK:(i + 1) * Q_BLOCK, :] = _diff_finish(outs[0], outs[1], lam, on_ref, lam_init,
                                                                    o_ref.dtype)


def _diff_prompt(table, dq, dk, dv, lam_p, onorm, batch, seq, lam_init):
    q_pos = jnp.arange(Q_BLOCK)
    idx = jnp.stack([_bucket_tile(q_pos, jnp.arange(2 * Q_BLOCK)),
                     _bucket_tile(q_pos + Q_BLOCK, jnp.arange(2 * Q_BLOCK))])
    head = lambda: pl.BlockSpec((seq, LANES), lambda b, h: (b, h))
    return pl.pallas_call(
        functools.partial(_diff_prompt_body, seq=seq, lam_init=lam_init),
        grid=(batch, DIFF_HEADS),
        in_specs=[
            pl.BlockSpec(memory_space=pltpu.SMEM),
            head(), head(), head(),
            pl.BlockSpec((2, Q_BLOCK, 2 * Q_BLOCK), lambda b, h: (0, 0, 0)),
            pl.BlockSpec((4, DIFF_DH), lambda b, h: (0, 0)),
            pl.BlockSpec((1, DIFF_DV), lambda b, h: (0, 0)),
        ],
        out_specs=head(),
        out_shape=jax.ShapeDtypeStruct((batch * seq, DIFF_V), BF16),
        scratch_shapes=[pltpu.VMEM((DIFF_HEADS, 2, Q_BLOCK, 2 * Q_BLOCK), F32),
                        pltpu.VMEM((seq, DIFF_DV + LANES), BF16)],
        compiler_params=_params(("arbitrary", "arbitrary")),
        name="diff_prompt",
    )(table, dq, dk, dv, idx, lam_p, onorm)


def _diff_sample_body(tab_ref, q_ref, k_ref, v_ref, ck_ref, cv_ref, idxc_ref, idxn_ref, lam_ref, on_ref, o_ref,
                      bias_c, bias_n, *, seq, lam_init):
    groups = [(h, c) for h in range(DIFF_HEADS) for c in range(2)]

    @pl.when(pl.program_id(0) == 0)
    def _():
        for g, (h, _) in enumerate(groups):
            bias_c[g * seq:(g + 1) * seq, :] = _bias_from_buckets(idxc_ref[h], tab_ref, h)
        for h in range(DIFF_HEADS):
            bias_n[h] = _bias_from_buckets(idxn_ref[...], tab_ref, h)

    lam = _lambda(lam_ref, lam_init)
    head_cols = [slice(h * LANES, (h + 1) * LANES) for h in range(DIFF_HEADS)]
    q_groups = [q_c for cols in head_cols for q_c in _comp_masks(q_ref[:, cols])]
    s_old = _dot_nt(jnp.concatenate(q_groups, axis=0), ck_ref[...].astype(BF16)) + bias_c[...]
    s_new = jnp.concatenate([_dot_nt(q_g, k_ref[:, head_cols[h]]) + bias_n[h]
                             for q_g, (h, _) in zip(q_groups, groups)], axis=0)
    m = jnp.maximum(s_old.max(axis=-1, keepdims=True), s_new.max(axis=-1, keepdims=True))
    e_old = jnp.exp(s_old - m)
    e_new = jnp.exp(s_new - m)
    l = e_old.sum(axis=-1, keepdims=True) + e_new.sum(axis=-1, keepdims=True)
    o_old = _dot(e_old.astype(BF16), cv_ref[...].astype(BF16))
    e_new = e_new.astype(BF16)
    o_new = jnp.concatenate([_dot(e_new[g * seq:(g + 1) * seq, :], v_ref[:, head_cols[h]])
                             for g, (h, _) in enumerate(groups)], axis=0)
    o = (o_old + o_new) * (1.0 / l)
    for h, cols in enumerate(head_cols):
        o0 = o[(2 * h) * seq:(2 * h + 1) * seq, :]
        o1 = o[(2 * h + 1) * seq:(2 * h + 2) * seq, :]
        o_ref[:, cols] = _diff_finish(o0, o1, lam, on_ref, lam_init, o_ref.dtype)


def _diff_sample(table, dq, dk, dv, cache_k, cache_v, lam_p, onorm, batch, seq, past, lam_init):
    q_pos = past + jnp.arange(seq)
    idx_n = _bucket_tile(q_pos, past + jnp.arange(seq))
    idx_rows = jnp.repeat(_bucket_tile(q_pos, jnp.arange(past)), HEADS, axis=1)
    row_head = jnp.arange(past * HEADS) % HEADS
    idx_c = jnp.where(row_head[None, None, :] == jnp.arange(HEADS)[:, None, None], idx_rows[None], MASKED_BUCKET)
    new_q = pl.BlockSpec((seq, DIFF_QK), lambda b: (b, 0))
    old = lambda: pl.BlockSpec((past * HEADS, LANES), lambda b: (b, 0))
    const = lambda shape: pl.BlockSpec(shape, lambda b: (0,) * len(shape))
    n_groups = 2 * DIFF_HEADS
    return pl.pallas_call(
        functools.partial(_diff_sample_body, seq=seq, lam_init=lam_init),
        grid=(batch,),
        in_specs=[pl.BlockSpec(memory_space=pltpu.SMEM), new_q, new_q, new_q, old(), old(),
                  const((HEADS, seq, past * HEADS)), const((seq, seq)), const((4, DIFF_DH)), const((1, DIFF_DV))],
        out_specs=new_q,
        out_shape=jax.ShapeDtypeStruct((batch * seq, DIFF_V), BF16),
        scratch_shapes=[pltpu.VMEM((n_groups * seq, past * HEADS), F32), pltpu.VMEM((DIFF_HEADS, seq, seq), F32)],
        compiler_params=_params(("arbitrary",)),
        name="diff_sample",
    )(table, dq, dk, dv, cache_k, cache_v, idx_c, idx_n, lam_p, onorm)


def _memattn_body(q_ref, k_ref, v_ref, o_ref, *, per_head):
    head_cols = [slice(h * MEM_DH, (h + 1) * MEM_DH) for h in range(MEM_HEADS)]
    if per_head:
        ks = [k_ref[:, h, :].astype(BF16) for h in range(MEM_HEADS)]
        vs = [v_ref[:, h, :].astype(BF16) for h in range(MEM_HEADS)]
    else:
        ks = [k_ref[:, cols] for cols in head_cols]
        vs = [v_ref[:, cols] for cols in head_cols]
    scores = [_dot_nt(q_ref[:, cols], k) * (MEM_DH ** -0.5) for cols, k in zip(head_cols, ks)]
    probs = []
    for s in scores:
        e = jnp.exp(s - s.max(axis=-1, keepdims=True))
        probs.append((e * (1.0 / e.sum(axis=-1, keepdims=True))).astype(BF16))
    for cols, p, v in zip(head_cols, probs, vs):
        o_ref[:, cols] = _dot(p, v).astype(o_ref.dtype)


def _memattn(mq, mk, mv, batch, seq, n_mem):
    tq = min(MEMATTN_TILE, seq)
    nq = seq // tq
    q_spec = pl.BlockSpec((tq, MEM_W), lambda b, i: (b * nq + i, 0))
    per_head = mk.ndim == 3
    if per_head:
        kv_spec = pl.BlockSpec((n_mem, HEADS, LANES), lambda b, i: (b, 0, 0))
    else:
        kv_spec = pl.BlockSpec((n_mem, MEM_W), lambda b, i: (b, 0))
    return pl.pallas_call(
        functools.partial(_memattn_body, per_head=per_head),
        grid=(batch, nq),
        in_specs=[q_spec, kv_spec, kv_spec],
        out_specs=q_spec,
        out_shape=jax.ShapeDtypeStruct((batch * seq, MEM_W), BF16),
        compiler_params=_params(("parallel", "arbitrary")),
        name="memattn",
    )(mq, mk, mv)


def _outproj_body(x_ref, g_ref, d_ref, m_ref, wg_ref, wd_ref, wm_ref, ng_ref, o_ref, on_ref):
    o = (x_ref[...] + _dot(g_ref[...], wg_ref[...]) + _dot(d_ref[...], wd_ref[...])
         + _dot(m_ref[...], wm_ref[...]))
    o_ref[...] = o
    on_ref[...] = _rms(o, ng_ref[...]).astype(BF16)


def _outproj(x, g, d, m, wo, next_gain):
    n = x.shape[0]
    tm = min(TOK_TILE, n)
    tok = lambda width: pl.BlockSpec((tm, width), lambda i: (i, 0))
    w_rows = lambda rows, blk: pl.BlockSpec((rows, D_MODEL), lambda i: (blk, 0))
    return pl.pallas_call(
        _outproj_body,
        grid=(n // tm,),
        in_specs=[tok(D_MODEL), tok(GLA_V), tok(DIFF_V), tok(MEM_W),
                  w_rows(GLA_V, 0), w_rows(DIFF_V, GLA_V // DIFF_V), w_rows(MEM_W, (GLA_V + DIFF_V) // MEM_W),
                  pl.BlockSpec((1, D_MODEL), lambda i: (0, 0))],
        out_specs=[tok(D_MODEL), tok(D_MODEL)],
        out_shape=[jax.ShapeDtypeStruct((n, D_MODEL), F32), jax.ShapeDtypeStruct((n, D_MODEL), BF16)],
        compiler_params=_params(("parallel",)),
        name="outproj",
    )(x, g, d, m, wo, wo, wo, next_gain)


def _row(v):
    return v.reshape(1, -1).astype(F32)


def kernel(x_prompt, x_sample, mem_prompt, cache_diff_k, cache_diff_v, state_gla, cache_mem_k, cache_mem_v, rel_bias_table, norm_ffn1, w_ffn1_in, w_ffn1_out, norm_mix, w_in, w_gla_g2, b_gla_g, gla_out_norm, diff_q_norm, diff_k_norm, diff_lambda, diff_out_norm, mem_norm, w_mem_kv, mem_q_norm, mem_k_norm, w_o, norm_ffn2, w_ffn2_in, w_ffn2_out, norm_final):
    depth = norm_ffn1.shape[0]
    assert depth == 1, "single-layer step"
    layer = 0
    batch, seq, _ = x_prompt.shape
    dec_batch, dec_seq, _ = x_sample.shape
    past = cache_diff_k.shape[2]
    n_mem = mem_prompt.shape[1]
    lam_init = 0.8 - 0.6 * math.exp(-0.3 * layer)

    ffn1 = (w_ffn1_in[layer].astype(BF16), w_ffn1_out[layer].astype(BF16))
    w_g2 = jnp.pad(w_gla_g2[layer].astype(BF16), ((0, LANES - GLA_GATE_RANK), (0, 0)))
    b_g = _row(b_gla_g[layer])
    qn = _row(jnp.tile(diff_q_norm[layer], DIFF_QK // DIFF_DH))
    kn = _row(jnp.tile(diff_k_norm[layer], DIFF_QK // DIFF_DH))
    mqn = _row(jnp.tile(mem_q_norm[layer], MEM_HEADS))
    mkn = _row(jnp.tile(mem_k_norm[layer], MEM_HEADS))
    table = rel_bias_table.astype(F32).reshape(-1)
    lam_p = diff_lambda[layer].astype(F32)
    gla_on = _row(gla_out_norm[layer])
    diff_on = _row(diff_out_norm[layer])
    per_head = lambda a: a.reshape(-1, HEADS, LANES)

    mk, mv, mk_b, mv_b = _memkv(mem_prompt.reshape(batch * n_mem, D_MODEL), _row(mem_norm[layer]),
                                w_mem_kv[layer].astype(BF16), mkn)

    n1 = _row(norm_ffn1[layer])
    n_in_blocks = w_ffn2_in.shape[2] // LANES
    xp1, w2_in, w2_out, w_t, wo = _ffn(
        x_prompt.reshape(batch * seq, D_MODEL), n1, *ffn1,
        cast_later=[(w_ffn2_in[layer], (D_MODEL, LANES), 0),
                    (w_ffn2_out[layer], (D_FF // n_in_blocks, D_MODEL), n_in_blocks),
                    (jnp.swapaxes(w_in[layer], 0, 1), (CAST_ROWS_W_IN, D_MODEL), 0),
                    (w_o[layer], (CAST_ROWS_W_O, D_MODEL), 0)])
    ffn2 = (w2_in, w2_out)
    xs1 = _ffn(x_sample.reshape(dec_batch * dec_seq, D_MODEL), n1, *ffn1)

    def layer_fn(x, b, t, chunk, s0, diff_fn, mem_k, mem_v):
        gq, gk, gv, gr, gg, dq, dk, dv, mq, dk_b, dv_b = _proj(x, _row(norm_mix[layer]), w_t, w_g2, b_g,
                                                               qn, kn, mqn)
        g_out, g_state = _gla(gq, gk, gg, gv, gr, gla_on, s0, b, t, chunk)
        d_out = diff_fn(dq, dk_b, dv_b)
        m_out = _memattn(mq, mem_k, mem_v, b, t, n_mem)
        x, xn = _outproj(x, g_out, d_out, m_out, wo, _row(norm_ffn2[layer]))
        x = _ffn(x, xn, *ffn2, final_gain=_row(norm_final[layer]))
        return x, dk, dv, g_state

    yp, dk_p, dv_p, g_p = layer_fn(
        xp1, batch, seq, CHUNK, None,
        lambda dq, dk, dv: _diff_prompt(table, dq, dk, dv, lam_p, diff_on, batch, seq, lam_init), mk_b, mv_b)
    ys, dk_s, dv_s, g_s = layer_fn(
        xs1, dec_batch, dec_seq, dec_seq, state_gla[layer],
        lambda dq, dk, dv: _diff_sample(table, dq, dk, dv, cache_diff_k[layer].reshape(-1, LANES),
                                        cache_diff_v[layer].reshape(-1, LANES),
                                        lam_p, diff_on, dec_batch, dec_seq, past, lam_init),
        per_head(cache_mem_k[layer]), per_head(cache_mem_v[layer]))

    head4 = lambda a, b, t: a.reshape(1, b, t, HEADS, LANES)
    return (yp.reshape(batch, seq, D_MODEL), ys.reshape(dec_batch, dec_seq, D_MODEL),
            head4(dk_p, batch, seq), head4(dv_p, batch, seq), g_p[None],
            head4(mk, batch, n_mem), head4(mv, batch, n_mem),
            head4(dk_s, dec_batch, dec_seq), head4(dv_s, dec_batch, dec_seq), g_s[None])
```

```python
import functools
import math

import jax
import jax.numpy as jnp
from jax import lax
from jax.experimental import pallas as pl
from jax.experimental.pallas import tpu as pltpu

F32 = jnp.float32
BF16 = jnp.bfloat16

D_MODEL = 2048
CHUNK = 64
EPS = 1e-6
NEG_INF = -1e30
GLA_HEADS, GLA_DK, GLA_DV, GLA_GATE_RANK, GLA_GATE_NORM = 4, 128, 256, 16, 16.0
DIFF_HEADS, DIFF_DH, DIFF_DV = 4, 64, 128
MEM_HEADS, MEM_DH = 4, 128
REL_BUCKETS, REL_MAX_DIST = 32, 128
D_FF = 5504
GLA_QK = GLA_HEADS * GLA_DK
GLA_V = GLA_HEADS * GLA_DV
DIFF_QK = DIFF_HEADS * 2 * DIFF_DH
DIFF_V = DIFF_HEADS * DIFF_DV
MEM_W = MEM_HEADS * MEM_DH
GLR_OFF = 2 * GLA_QK + 2 * GLA_V

LANES = 128
HEADS = 4
FF_TILE = 512
FF_TILE_F32 = 256
FFN_TOK_TILE = 1024
TOK_TILE = 512
GLA_TILE = 1024
MEMATTN_TILE = 2048
PROJ_TILE = 256
Q_BLOCK = 256
MASKED_BUCKET = REL_BUCKETS
MIB = 1024 * 1024
VMEM_CAP_MIB = 60
CAST_ROWS_W_IN = 48
CAST_ROWS_W_O = 16

_W_GQ, _W_GK, _W_GV, _W_GR, _W_GLR = 0, GLA_QK, 2 * GLA_QK, 2 * GLA_QK + GLA_V, GLR_OFF
_W_DQ = GLR_OFF + GLA_GATE_RANK
_W_DK = _W_DQ + DIFF_QK
_W_DV = _W_DK + DIFF_QK
_W_MQ = _W_DV + DIFF_V
IN_WIDTH = _W_MQ + MEM_W


def _dot(a, b):
    return jnp.dot(a, b, preferred_element_type=F32)


def _dot_nt(a, b):
    return lax.dot_general(a, b, (((1,), (1,)), ((), ())), preferred_element_type=F32)


def _dot_tn(a, b):
    return lax.dot_general(a, b, (((0,), (0,)), ((), ())), preferred_element_type=F32)


def _rms(x, gain):
    return x * lax.rsqrt(jnp.mean(x * x, axis=-1, keepdims=True) + EPS) * gain


def _params(sem, vmem_mib=48):
    return pltpu.CompilerParams(dimension_semantics=sem, vmem_limit_bytes=min(vmem_mib, VMEM_CAP_MIB) * MIB)


def _ffn_body(*refs, n_ff, overlap, final_norm, n_cast, prenormed, emit_bf16):
    x_ref, g_ref, wg_ref, wu_ref, wo_ref = refs[:5]
    refs = refs[5:]
    if final_norm:
        fg_ref, refs = refs[0], refs[1:]
    cast_src, refs = refs[:n_cast], refs[n_cast:]
    o_ref, refs = refs[0], refs[1:]
    cast_dst, refs = refs[:n_cast], refs[n_cast:]
    if emit_bf16:
        (wg_out, wu_out, wo_out), refs = refs[:3], refs[3:]
    xn_ref = g_ref if prenormed else refs[0]
    j = pl.program_id(1)

    @pl.when(j == 0)
    def _():
        x = x_ref[...]
        if not prenormed:
            xn_ref[...] = _rms(x, g_ref[...]).astype(BF16)
        o_ref[...] = x

    def tile(lo):
        for src, dst in zip(cast_src, cast_dst):
            dst[...] = src[...].astype(BF16)
        if emit_bf16:
            wg_out[...] = wg_ref[...].astype(BF16)
            wu_out[...] = wu_ref[...].astype(BF16)
            wo_out[...] = wo_ref[...].astype(BF16)
            wg, wu, wo = wg_out, wu_out, wo_out
        else:
            wg, wu, wo = wg_ref, wu_ref, wo_ref
        xn = xn_ref[...]
        gate = _dot(xn, wg[:, lo:])
        up = _dot(xn, wu[:, lo:])
        act = (jax.nn.silu(gate) * up).astype(BF16)
        return 0.5 * _dot(act, wo[lo:, :])

    @pl.when(j < n_ff - 1)
    def _():
        o_ref[...] += tile(0)

    @pl.when(j == n_ff - 1)
    def _():
        o = o_ref[...] + tile(overlap)
        o_ref[...] = _rms(o, fg_ref[...]) if final_norm else o


def _ffn(x, gain_or_xn, w_gate, w_up, w_out, *, up_col0, final_gain=None, cast_later=(), ff_tile=FF_TILE,
         emit_bf16=False):
    n = x.shape[0]
    tm = min(FFN_TOK_TILE, n)
    n_ff = -(-D_FF // ff_tile)
    row = pl.BlockSpec((1, D_MODEL), lambda i, j: (0, 0))
    prenormed = gain_or_xn.shape != (1, D_MODEL)
    start = lambda j: jnp.minimum(j * (ff_tile // LANES), (D_FF - ff_tile) // LANES)
    w_in_tile = (pl.Element(D_MODEL), pl.Element(ff_tile))
    w_out_tile = (pl.Element(ff_tile), pl.Element(D_MODEL))
    gate_index = lambda i, j: (0, start(j) * LANES)
    out_index = lambda i, j: (start(j) * LANES, 0)
    tok = pl.BlockSpec((tm, D_MODEL), lambda i, j: (i, 0))
    in_specs = [
        tok,
        tok if prenormed else row,
        pl.BlockSpec(w_in_tile, gate_index),
        pl.BlockSpec(w_in_tile, lambda i, j: (0, (up_col0 // LANES + start(j)) * LANES)),
        pl.BlockSpec(w_out_tile, out_index),
    ]
    args = [x, gain_or_xn, w_gate, w_up, w_out]
    if final_gain is not None:
        in_specs.append(row)
        args.append(final_gain)
    w_tile_bytes = D_MODEL * ff_tile * ((4 + 2) if emit_bf16 else 2)
    vmem_bytes = (4 * tm * D_MODEL * 4 + (2 if prenormed else 1) * tm * D_MODEL * 2 + 6 * w_tile_bytes
                  + 3 * tm * ff_tile * 4)
    out_specs = [pl.BlockSpec((tm, D_MODEL), lambda i, j: (i, 0))]
    out_shape = [jax.ShapeDtypeStruct((n, D_MODEL), F32)]
    n_steps = (n // tm) * n_ff
    for mat, blk, first in cast_later:
        rows, cols = mat.shape
        assert rows % blk[0] == 0 and cols % blk[1] == 0 and (blk[0] == rows or blk[1] == cols)
        n_blk = (rows // blk[0]) * (cols // blk[1])
        assert first + n_blk <= n_steps
        by_rows = blk[1] == cols

        def index(i, j, first=first, n_blk=n_blk, by_rows=by_rows):
            k = jnp.clip(i * n_ff + j - first, 0, n_blk - 1)
            return (k, 0) if by_rows else (0, k)

        in_specs.append(pl.BlockSpec(blk, index))
        args.append(mat)
        out_specs.append(pl.BlockSpec(blk, index))
        out_shape.append(jax.ShapeDtypeStruct(mat.shape, BF16))
        vmem_bytes += 2 * blk[0] * blk[1] * (4 + 2)
    if emit_bf16:
        out_specs += [pl.BlockSpec(w_in_tile, gate_index), pl.BlockSpec(w_in_tile, gate_index),
                      pl.BlockSpec(w_out_tile, out_index)]
        out_shape += [jax.ShapeDtypeStruct((D_MODEL, D_FF), BF16)] * 2 + [jax.ShapeDtypeStruct((D_FF, D_MODEL), BF16)]
    outs = pl.pallas_call(
        functools.partial(_ffn_body, n_ff=n_ff, overlap=n_ff * ff_tile - D_FF, final_norm=final_gain is not None,
                          n_cast=len(cast_later), prenormed=prenormed, emit_bf16=emit_bf16),
        grid=(n // tm, n_ff),
        in_specs=in_specs,
        out_specs=out_specs,
        out_shape=out_shape,
        scratch_shapes=[] if prenormed else [pltpu.VMEM((tm, D_MODEL), BF16)],
        compiler_params=_params(("arbitrary", "arbitrary"), vmem_bytes // MIB + 4),
        name="ffn",
    )(*args)
    return outs[0] if len(outs) == 1 else outs


def _group_rms(acc, gain_ref, width, scale, put):
    lane = lax.broadcasted_iota(jnp.int32, (1, LANES), 1)
    low = lane < width
    for c in range(acc.shape[1] // LANES):
        cols = slice(c * LANES, (c + 1) * LANES)
        xc = acc[:, cols]
        sq = xc * xc
        if width == LANES:
            r = lax.rsqrt(jnp.mean(sq, axis=-1, keepdims=True) + EPS)
        else:
            s_lo = jnp.sum(jnp.where(low, sq, 0.0), axis=-1, keepdims=True)
            s_hi = jnp.sum(jnp.where(low, 0.0, sq), axis=-1, keepdims=True)
            r = jnp.where(low, lax.rsqrt(s_lo / width + EPS), lax.rsqrt(s_hi / width + EPS))
        y = xc * r * gain_ref[:, cols]
        if scale != 1.0:
            y = y * scale
        put(c, y)


def _put_cols(ref):
    def put(c, y):
        ref[:, c * LANES:(c + 1) * LANES] = y.astype(ref.dtype)
    return put


def _put_heads_and_cols(heads_ref, cols_ref):
    def put(c, y):
        heads_ref[:, c, :] = y
        cols_ref[:, c * LANES:(c + 1) * LANES] = y.astype(cols_ref.dtype)
    return put


def _proj_body(x_ref, g_ref, wt_ref, wg2_ref, bg_ref, qn_ref, kn_ref, mqn_ref,
               gq_ref, gk_ref, gv_ref, gr_ref, gg_ref, dq_ref, dk_ref, dv_ref, mq_ref, dkb_ref, dvb_ref, xn_ref):
    xn_ref[...] = _rms(x_ref[...], g_ref[...]).astype(BF16)

    def cols(start, width):
        return _dot_nt(xn_ref[...], wt_ref[start:start + width, :])

    gq_ref[...] = cols(_W_GQ, GLA_QK) * (GLA_DK ** -0.5)
    gk_ref[...] = cols(_W_GK, GLA_QK)
    gv_ref[...] = cols(_W_GV, GLA_V).astype(BF16)
    gr_ref[...] = cols(_W_GR, GLA_V)
    glr = cols(_W_GLR, LANES)
    z = _dot(glr.astype(BF16), wg2_ref[...]) + bg_ref[...]
    gg_ref[...] = jax.nn.log_sigmoid(z) / GLA_GATE_NORM
    _group_rms(cols(_W_DQ, DIFF_QK), qn_ref, DIFF_DH, DIFF_DH ** -0.5, _put_cols(dq_ref))
    _group_rms(cols(_W_DK, DIFF_QK), kn_ref, DIFF_DH, 1.0, _put_heads_and_cols(dk_ref, dkb_ref))
    dv = cols(_W_DV, DIFF_V)
    put_v = _put_heads_and_cols(dv_ref, dvb_ref)
    for h in range(HEADS):
        put_v(h, dv[:, h * LANES:(h + 1) * LANES])
    _group_rms(cols(_W_MQ, MEM_W), mqn_ref, MEM_DH, 1.0, _put_cols(mq_ref))


def _proj(x, gain, wt, wg2, bg, qn, kn, mqn):
    n = x.shape[0]
    tm = min(PROJ_TILE, n)
    const = lambda shape: pl.BlockSpec(shape, lambda i: (0, 0))
    out = lambda width: pl.BlockSpec((tm, width), lambda i: (i, 0))
    heads = pl.BlockSpec((tm, HEADS, LANES), lambda i: (i, 0, 0))
    shp = lambda width, dt: jax.ShapeDtypeStruct((n, width), dt)
    shp_heads = jax.ShapeDtypeStruct((n, HEADS, LANES), F32)
    return pl.pallas_call(
        _proj_body,
        grid=(n // tm,),
        in_specs=[
            pl.BlockSpec((tm, D_MODEL), lambda i: (i, 0)),
            const((1, D_MODEL)),
            pl.BlockSpec((IN_WIDTH, D_MODEL), lambda i: (0, 0), pipeline_mode=pl.Buffered(1)),
            const((LANES, GLA_QK)),
            const((1, GLA_QK)),
            const((1, DIFF_QK)),
            const((1, DIFF_QK)),
            const((1, MEM_W)),
        ],
        out_specs=[out(GLA_QK), out(GLA_QK), out(GLA_V), out(GLA_V), out(GLA_QK),
                   out(DIFF_QK), heads, heads, out(MEM_W), out(DIFF_QK), out(DIFF_V)],
        out_shape=[shp(GLA_QK, F32), shp(GLA_QK, F32), shp(GLA_V, BF16), shp(GLA_V, F32), shp(GLA_QK, F32),
                   shp(DIFF_QK, BF16), shp_heads, shp_heads, shp(MEM_W, BF16), shp(DIFF_QK, BF16), shp(DIFF_V, BF16)],
        scratch_shapes=[pltpu.VMEM((tm, D_MODEL), BF16)],
        compiler_params=_params(("parallel",)),
        name="proj",
    )(x, gain, wt, wg2, bg, qn, kn, mqn)


def _memkv_body(x_ref, g_ref, w_ref, kn_ref, k_ref, v_ref, kb_ref, vb_ref):
    xn = _rms(x_ref[...], g_ref[...]).astype(BF16)
    _group_rms(_dot(xn, w_ref[:, :MEM_W]), kn_ref, MEM_DH, 1.0, _put_heads_and_cols(k_ref, kb_ref))
    v = _dot(xn, w_ref[:, MEM_W:])
    put_v = _put_heads_and_cols(v_ref, vb_ref)
    for h in range(HEADS):
        put_v(h, v[:, h * LANES:(h + 1) * LANES])


def _memkv(mem, gain, w, kn):
    n = mem.shape[0]
    tm = min(TOK_TILE, n)
    const = lambda shape: pl.BlockSpec(shape, lambda i: (0, 0))
    heads = pl.BlockSpec((tm, HEADS, LANES), lambda i: (i, 0, 0))
    dense = pl.BlockSpec((tm, MEM_W), lambda i: (i, 0))
    return pl.pallas_call(
        _memkv_body,
        grid=(n // tm,),
        in_specs=[pl.BlockSpec((tm, D_MODEL), lambda i: (i, 0)), const((1, D_MODEL)),
                  const((D_MODEL, 2 * MEM_W)), const((1, MEM_W))],
        out_specs=[heads, heads, dense, dense],
        out_shape=[jax.ShapeDtypeStruct((n, HEADS, LANES), F32)] * 2 + [jax.ShapeDtypeStruct((n, MEM_W), BF16)] * 2,
        compiler_params=_params(("parallel",)),
        name="memkv",
    )(mem, gain, w, kn)


def _split3(x):
    hi = x.astype(BF16)
    r1 = x - hi.astype(F32)
    mid = r1.astype(BF16)
    lo = (r1 - mid.astype(F32)).astype(BF16)
    return hi, mid, lo


def _gla_body(*refs, chunk, n_chunks, n_steps, has_state):
    if has_state:
        gq_ref, gk_ref, gg_ref, gv_ref, gr_ref, on_ref, s0_ref, go_ref, st_ref, state = refs
    else:
        gq_ref, gk_ref, gg_ref, gv_ref, gr_ref, on_ref, go_ref, st_ref, state = refs
    t = pl.program_id(1)

    @pl.when(t == 0)
    def _():
        for h in range(GLA_HEADS):
            if has_state:
                state[h] = s0_ref[0, h].T
            else:
                state[h] = jnp.zeros((GLA_DV, GLA_DK), F32)

    row = lax.broadcasted_iota(jnp.int32, (chunk, chunk), 0)
    col = lax.broadcasted_iota(jnp.int32, (chunk, chunk), 1)
    causal = row >= col
    tril = causal.astype(BF16)

    chunks = [slice(c * chunk, (c + 1) * chunk) for c in range(n_chunks)]
    kcs = [slice(h * GLA_DK, (h + 1) * GLA_DK) for h in range(GLA_HEADS)]
    vcs = [slice(h * GLA_DV, (h + 1) * GLA_DV) for h in range(GLA_HEADS)]

    bs = []
    for rows in chunks:
        g_hi, g_mid, g_lo = _split3(gg_ref[rows, :])
        bs.append(_dot(tril, g_hi) + _dot(tril, g_mid) + _dot(tril, g_lo))

    qes, kes, kds, decays = [], [], [], []
    for rows, b in zip(chunks, bs):
        b_last = b[chunk - 1:chunk, :]
        q = gq_ref[rows, :]
        k = gk_ref[rows, :]
        qes.append((q * jnp.exp(b)).astype(BF16))
        kes.append((k * jnp.exp(-b)).astype(BF16))
        kds.append((k * jnp.exp(b_last - b)).astype(BF16))
        decays.append(jnp.exp(b_last))

    a_s = [[jnp.where(causal, _dot_nt(qe[:, kc], ke[:, kc]), 0.0).astype(BF16) for kc in kcs]
           for qe, ke in zip(qes, kes)]
    incs = [[_dot_tn(gv_ref[rows, vc], kd[:, kc]) for kc, vc in zip(kcs, vcs)] for rows, kd in zip(chunks, kds)]

    s_in = []
    s_cur = [state[h] for h in range(GLA_HEADS)]
    for c in range(n_chunks):
        s_in.append([s.astype(BF16) for s in s_cur])
        s_cur = [s * decays[c][:, kc] + inc for s, kc, inc in zip(s_cur, kcs, incs[c])]
    for h in range(GLA_HEADS):
        state[h] = s_cur[h]

    for c, rows in enumerate(chunks):
        for h, (kc, vc) in enumerate(zip(kcs, vcs)):
            o = _dot_nt(qes[c][:, kc], s_in[c][h]) + _dot(a_s[c][h], gv_ref[rows, vc])
            go_ref[rows, vc] = (_rms(o, on_ref[...]) * jax.nn.silu(gr_ref[rows, vc])).astype(BF16)

    @pl.when(t == n_steps - 1)
    def _():
        for h in range(GLA_HEADS):
            st_ref[0, h] = state[h].T


def _gla(gq, gk, gg, gv, gr, onorm, s0, batch, seq, chunk):
    tt = min(GLA_TILE, seq)
    n_steps = seq // tt
    tok = lambda width: pl.BlockSpec((tt, width), lambda b, t: (b * n_steps + t, 0))
    st_spec = pl.BlockSpec((1, GLA_HEADS, GLA_DK, GLA_DV), lambda b, t: (b, 0, 0, 0))
    in_specs = [tok(GLA_QK), tok(GLA_QK), tok(GLA_QK), tok(GLA_V), tok(GLA_V),
                pl.BlockSpec((1, GLA_DV), lambda b, t: (0, 0))]
    args = [gq, gk, gg, gv, gr, onorm]
    if s0 is not None:
        in_specs.append(st_spec)
        args.append(s0)
    return pl.pallas_call(
        functools.partial(_gla_body, chunk=chunk, n_chunks=tt // chunk, n_steps=n_steps, has_state=s0 is not None),
        grid=(batch, n_steps),
        in_specs=in_specs,
        out_specs=[tok(GLA_V), st_spec],
        out_shape=[jax.ShapeDtypeStruct((batch * seq, GLA_V), BF16),
                   jax.ShapeDtypeStruct((batch, GLA_HEADS, GLA_DK, GLA_DV), F32)],
        scratch_shapes=[pltpu.VMEM((GLA_HEADS, GLA_DV, GLA_DK), F32)],
        compiler_params=_params(("parallel", "arbitrary")),
        name="gla",
    )(*args)


def _t5_bucket(rel):
    nb = REL_BUCKETS // 2
    max_exact = nb // 2
    ret = jnp.where(rel > 0, nb, 0)
    n = jnp.abs(rel)
    nf = jnp.maximum(n, 1).astype(F32)
    large = max_exact + (jnp.log(nf / max_exact) / math.log(REL_MAX_DIST / max_exact)
                         * (nb - max_exact)).astype(jnp.int32)
    large = jnp.minimum(large, nb - 1)
    return ret + jnp.where(n < max_exact, n, large)


def _bucket_tile(q_pos, k_pos):
    visible = (k_pos[None, :] // CHUNK) <= (q_pos[:, None] // CHUNK)
    return jnp.where(visible, _t5_bucket(k_pos[None, :] - q_pos[:, None]), MASKED_BUCKET).astype(jnp.int32)


def _bias_from_buckets(idx, tab_ref, head):
    def step(bk, acc):
        return jnp.where(idx == bk, tab_ref[bk * DIFF_HEADS + head], acc)
    return lax.fori_loop(0, REL_BUCKETS, step, jnp.full(idx.shape, NEG_INF, F32))


def _lambda(lam_ref, lam_init):
    l = lam_ref[...]
    return (jnp.exp(jnp.sum(l[0:1] * l[1:2], axis=-1, keepdims=True))
            - jnp.exp(jnp.sum(l[2:3] * l[3:4], axis=-1, keepdims=True)) + lam_init)


def _comp_masks(q):
    lane = lax.broadcasted_iota(jnp.int32, q.shape, 1)
    zero = jnp.zeros_like(q)
    return jnp.where(lane < DIFF_DH, q, zero), jnp.where(lane < DIFF_DH, zero, q)


def _diff_finish(o0, o1, lam, on_ref, lam_init, out_dtype):
    o = o0 - lam * o1
    return (_rms(o, on_ref[...]) * (1.0 - lam_init)).astype(out_dtype)


def _scores(q_c, parts):
    return [_dot_nt(q_c, k) + bias if jnp.ndim(bias) == 2 else _dot_nt(q_c, k) for k, _, bias in parts]


def _softmax_pv(scores, parts):
    shifts = [0.0 if jnp.ndim(bias) == 2 else bias for _, _, bias in parts]
    m = None
    for s, shift in zip(scores, shifts):
        part_max = s.max(axis=-1, keepdims=True) + shift
        m = part_max if m is None else jnp.maximum(m, part_max)
    ol = 0.0
    for s, shift, (_, v1, _) in zip(scores, shifts, parts):
        ol = ol + _dot(jnp.exp(s - (m - shift)).astype(BF16), v1)
    return ol[:, :DIFF_DV] * (1.0 / ol[:, DIFF_DV:])


def _diff_prompt_body(tab_ref, q_ref, kb, vb, idx_ref, lam_ref, on_ref, o_ref, bias, v1, *, seq, lam_init):
    b = pl.program_id(0)
    h = pl.program_id(1)

    @pl.when(b == 0)
    def _():
        for t in range(2):
            bias[h, t] = _bias_from_buckets(idx_ref[t], tab_ref, h)

    v1[:, :DIFF_DV] = vb[...]
    v1[:, DIFF_DV:] = jnp.ones((seq, LANES), BF16)
    lam = _lambda(lam_ref, lam_init)
    far_bias = tab_ref[(REL_BUCKETS // 2 - 1) * DIFF_HEADS + h]

    def key_parts(i):
        near0 = max(i - 1, 0) * Q_BLOCK
        near = slice(near0, near0 + 2 * Q_BLOCK)
        parts = [(kb[near, :], v1[near, :], bias[h, min(i, 1)])]
        if near0 > 0:
            parts.append((kb[0:near0, :], v1[0:near0, :], far_bias))
        return parts

    items = [(i, c) for i in range(seq // Q_BLOCK) for c in range(2)]
    comps = {}

    def scores_of(item):
        i, c = item
        if i not in comps:
            comps[i] = _comp_masks(q_ref[i * Q_BLOCK:(i + 1) * Q_BLOCK, :])
        return _scores(comps[i][c], key_parts(i))

    outs = {}
    ahead = scores_of(items[0])
    for n, (i, c) in enumerate(items):
        cur = ahead
        if n + 1 < len(items):
            ahead = scores_of(items[n + 1])
        outs[c] = _softmax_pv(cur, key_parts(i))
        if c == 1:
            o_ref[i * Q_BLOCK:(i + 1) * Q_BLOCK, :] = _diff_finish(outs[0], outs[1], lam, on_ref, lam_init,
                                                                    o_ref.dtype)


def _diff_prompt(table, dq, dk, dv, lam_p, onorm, batch, seq, lam_init):
    q_pos = jnp.arange(Q_BLOCK)
    idx = jnp.stack([_bucket_tile(q_pos, jnp.arange(2 * Q_BLOCK)),
                     _bucket_tile(q_pos + Q_BLOCK, jnp.arange(2 * Q_BLOCK))])
    head = lambda: pl.BlockSpec((seq, LANES), lambda b, h: (b, h))
    return pl.pallas_call(
        functools.partial(_diff_prompt_body, seq=seq, lam_init=lam_init),
        grid=(batch, DIFF_HEADS),
        in_specs=[
            pl.BlockSpec(memory_space=pltpu.SMEM),
            head(), head(), head(),
            pl.BlockSpec((2, Q_BLOCK, 2 * Q_BLOCK), lambda b, h: (0, 0, 0)),
            pl.BlockSpec((4, DIFF_DH), lambda b, h: (0, 0)),
            pl.BlockSpec((1, DIFF_DV), lambda b, h: (0, 0)),
        ],
        out_specs=head(),
        out_shape=jax.ShapeDtypeStruct((batch * seq, DIFF_V), BF16),
        scratch_shapes=[pltpu.VMEM((DIFF_HEADS, 2, Q_BLOCK, 2 * Q_BLOCK), F32),
                        pltpu.VMEM((seq, DIFF_DV + LANES), BF16)],
        compiler_params=_params(("arbitrary", "arbitrary")),
        name="diff_prompt",
    )(table, dq, dk, dv, idx, lam_p, onorm)


def _diff_sample_body(tab_ref, q_ref, k_ref, v_ref, ck_ref, cv_ref, idxc_ref, idxn_ref, lam_ref, on_ref, o_ref,
                      bias_c, bias_n, *, seq, lam_init):
    groups = [(h, c) for h in range(DIFF_HEADS) for c in range(2)]

    @pl.when(pl.program_id(0) == 0)
    def _():
        for h in range(DIFF_HEADS):
            tile = _bias_from_buckets(idxc_ref[h], tab_ref, h)
            for g in (2 * h, 2 * h + 1):
                bias_c[g * seq:(g + 1) * seq, :] = tile
            bias_n[h] = _bias_from_buckets(idxn_ref[...], tab_ref, h)

    lam = _lambda(lam_ref, lam_init)
    head_cols = [slice(h * LANES, (h + 1) * LANES) for h in range(DIFF_HEADS)]
    q_groups = [q_c for cols in head_cols for q_c in _comp_masks(q_ref[:, cols])]
    s_old = _dot_nt(jnp.concatenate(q_groups, axis=0), ck_ref[...].astype(BF16)) + bias_c[...]
    s_new = jnp.concatenate([_dot_nt(q_g, k_ref[:, head_cols[h]]) + bias_n[h]
                             for q_g, (h, _) in zip(q_groups, groups)], axis=0)
    m = jnp.maximum(s_old.max(axis=-1, keepdims=True), s_new.max(axis=-1, keepdims=True))
    e_old = jnp.exp(s_old - m)
    e_new = jnp.exp(s_new - m)
    l = e_old.sum(axis=-1, keepdims=True) + e_new.sum(axis=-1, keepdims=True)
    o_old = _dot(e_old.astype(BF16), cv_ref[...].astype(BF16))
    e_new = e_new.astype(BF16)
    o_new = jnp.concatenate([_dot(e_new[g * seq:(g + 1) * seq, :], v_ref[:, head_cols[h]])
                             for g, (h, _) in enumerate(groups)], axis=0)
    o = (o_old + o_new) * (1.0 / l)
    for h, cols in enumerate(head_cols):
        o0 = o[(2 * h) * seq:(2 * h + 1) * seq, :]
        o1 = o[(2 * h + 1) * seq:(2 * h + 2) * seq, :]
        o_ref[:, cols] = _diff_finish(o0, o1, lam, on_ref, lam_init, o_ref.dtype)


def _diff_sample(table, dq, dk, dv, cache_k, cache_v, lam_p, onorm, batch, seq, past, lam_init):
    q_pos = past + jnp.arange(seq)
    idx_n = _bucket_tile(q_pos, past + jnp.arange(seq))
    idx_rows = jnp.repeat(_bucket_tile(q_pos, jnp.arange(past)), HEADS, axis=1)
    row_head = jnp.arange(past * HEADS) % HEADS
    idx_c = jnp.where(row_head[None, None, :] == jnp.arange(HEADS)[:, None, None], idx_rows[None], MASKED_BUCKET)
    new_q = pl.BlockSpec((seq, DIFF_QK), lambda b: (b, 0))
    old = lambda: pl.BlockSpec((past * HEADS, LANES), lambda b: (b, 0))
    const = lambda shape: pl.BlockSpec(shape, lambda b: (0,) * len(shape))
    n_groups = 2 * DIFF_HEADS
    return pl.pallas_call(
        functools.partial(_diff_sample_body, seq=seq, lam_init=lam_init),
        grid=(batch,),
        in_specs=[pl.BlockSpec(memory_space=pltpu.SMEM), new_q, new_q, new_q, old(), old(),
                  const((HEADS, seq, past * HEADS)), const((seq, seq)), const((4, DIFF_DH)), const((1, DIFF_DV))],
        out_specs=new_q,
        out_shape=jax.ShapeDtypeStruct((batch * seq, DIFF_V), BF16),
        scratch_shapes=[pltpu.VMEM((n_groups * seq, past * HEADS), F32), pltpu.VMEM((DIFF_HEADS, seq, seq), F32)],
        compiler_params=_params(("arbitrary",)),
        name="diff_sample",
    )(table, dq, dk, dv, cache_k, cache_v, idx_c, idx_n, lam_p, onorm)


def _memattn_body(q_ref, k_ref, v_ref, o_ref, *, per_head):
    head_cols = [slice(h * MEM_DH, (h + 1) * MEM_DH) for h in range(MEM_HEADS)]
    if per_head:
        ks = [k_ref[:, h, :].astype(BF16) for h in range(MEM_HEADS)]
        vs = [v_ref[:, h, :].astype(BF16) for h in range(MEM_HEADS)]
    else:
        ks = [k_ref[:, cols] for cols in head_cols]
        vs = [v_ref[:, cols] for cols in head_cols]
    scores = [_dot_nt(q_ref[:, cols], k) * (MEM_DH ** -0.5) for cols, k in zip(head_cols, ks)]
    probs = []
    for s in scores:
        e = jnp.exp(s - s.max(axis=-1, keepdims=True))
        probs.append((e * (1.0 / e.sum(axis=-1, keepdims=True))).astype(BF16))
    for cols, p, v in zip(head_cols, probs, vs):
        o_ref[:, cols] = _dot(p, v).astype(o_ref.dtype)


def _memattn(mq, mk, mv, batch, seq, n_mem):
    tq = min(MEMATTN_TILE, seq)
    nq = seq // tq
    q_spec = pl.BlockSpec((tq, MEM_W), lambda b, i: (b * nq + i, 0))
    per_head = mk.ndim == 3
    if per_head:
        kv_spec = pl.BlockSpec((n_mem, HEADS, LANES), lambda b, i: (b, 0, 0))
    else:
        kv_spec = pl.BlockSpec((n_mem, MEM_W), lambda b, i: (b, 0))
    return pl.pallas_call(
        functools.partial(_memattn_body, per_head=per_head),
        grid=(batch, nq),
        in_specs=[q_spec, kv_spec, kv_spec],
        out_specs=q_spec,
        out_shape=jax.ShapeDtypeStruct((batch * seq, MEM_W), BF16),
        compiler_params=_params(("parallel", "arbitrary")),
        name="memattn",
    )(mq, mk, mv)


def _outproj_body(x_ref, g_ref, d_ref, m_ref, wg_ref, wd_ref, wm_ref, ng_ref, o_ref, on_ref):
    o = (x_ref[...] + _dot(g_ref[...], wg_ref[...]) + _dot(d_ref[...], wd_ref[...])
         + _dot(m_ref[...], wm_ref[...]))
    o_ref[...] = o
    on_ref[...] = _rms(o, ng_ref[...]).astype(BF16)


def _outproj(x, g, d, m, wo, next_gain):
    n = x.shape[0]
    tm = min(TOK_TILE, n)
    tok = lambda width: pl.BlockSpec((tm, width), lambda i: (i, 0))
    w_rows = lambda rows, blk: pl.BlockSpec((rows, D_MODEL), lambda i: (blk, 0))
    return pl.pallas_call(
        _outproj_body,
        grid=(n // tm,),
        in_specs=[tok(D_MODEL), tok(GLA_V), tok(DIFF_V), tok(MEM_W),
                  w_rows(GLA_V, 0), w_rows(DIFF_V, GLA_V // DIFF_V), w_rows(MEM_W, (GLA_V + DIFF_V) // MEM_W),
                  pl.BlockSpec((1, D_MODEL), lambda i: (0, 0))],
        out_specs=[tok(D_MODEL), tok(D_MODEL)],
        out_shape=[jax.ShapeDtypeStruct((n, D_MODEL), F32), jax.ShapeDtypeStruct((n, D_MODEL), BF16)],
        compiler_params=_params(("parallel",)),
        name="outproj",
    )(x, g, d, m, wo, wo, wo, next_gain)


def _row(v):
    return v.reshape(1, -1).astype(F32)


def kernel(x_prompt, x_sample, mem_prompt, cache_diff_k, cache_diff_v, state_gla, cache_mem_k, cache_mem_v, rel_bias_table, norm_ffn1, w_ffn1_in, w_ffn1_out, norm_mix, w_in, w_gla_g2, b_gla_g, gla_out_norm, diff_q_norm, diff_k_norm, diff_lambda, diff_out_norm, mem_norm, w_mem_kv, mem_q_norm, mem_k_norm, w_o, norm_ffn2, w_ffn2_in, w_ffn2_out, norm_final):
    depth = norm_ffn1.shape[0]
    assert depth == 1, "single-layer step"
    layer = 0
    batch, seq, _ = x_prompt.shape
    dec_batch, dec_seq, _ = x_sample.shape
    past = cache_diff_k.shape[2]
    n_mem = mem_prompt.shape[1]
    lam_init = 0.8 - 0.6 * math.exp(-0.3 * layer)

    w_g2 =jnp.pad(w_gla_g2[layer].astype(BF16), ((0, LANES - GLA_GATE_RANK), (0, 0)))
    b_g = _row(b_gla_g[layer])
    qn = _row(jnp.tile(diff_q_norm[layer], DIFF_QK // DIFF_DH))
    kn = _row(jnp.tile(diff_k_norm[layer], DIFF_QK // DIFF_DH))
    mqn = _row(jnp.tile(mem_q_norm[layer], MEM_HEADS))
    mkn = _row(jnp.tile(mem_k_norm[layer], MEM_HEADS))
    table = rel_bias_table.astype(F32).reshape(-1)
    lam_p = diff_lambda[layer].astype(F32)
    gla_on = _row(gla_out_norm[layer])
    diff_on = _row(diff_out_norm[layer])
    per_head = lambda a: a.reshape(-1, HEADS, LANES)

    mk, mv, mk_b, mv_b = _memkv(mem_prompt.reshape(batch * n_mem, D_MODEL), _row(mem_norm[layer]),
                                w_mem_kv[layer].astype(BF16), mkn)

    n1 = _row(norm_ffn1[layer])
    xs1, *ffn1 = _ffn(x_sample.reshape(dec_batch * dec_seq, D_MODEL), n1, w_ffn1_in[layer], w_ffn1_in[layer],
                      w_ffn1_out[layer], up_col0=D_FF, ff_tile=FF_TILE_F32, emit_bf16=True)
    n_in_blocks = w_ffn2_in.shape[2] // LANES
    xp1, w2_in, w2_out, w_t, wo = _ffn(
        x_prompt.reshape(batch * seq, D_MODEL), n1, *ffn1, up_col0=0,
        cast_later=[(w_ffn2_in[layer], (D_MODEL, LANES), 0),
                    (w_ffn2_out[layer], (D_FF // n_in_blocks, D_MODEL), n_in_blocks),
                    (jnp.swapaxes(w_in[layer], 0, 1), (CAST_ROWS_W_IN, D_MODEL), 0),
                    (w_o[layer], (CAST_ROWS_W_O, D_MODEL), 0)])
    ffn2 = (w2_in, w2_in, w2_out)

    def layer_fn(x, b, t, chunk, s0, diff_fn, mem_k, mem_v):
        gq, gk, gv, gr, gg, dq, dk, dv, mq, dk_b, dv_b = _proj(x, _row(norm_mix[layer]), w_t, w_g2, b_g,
                                                               qn, kn, mqn)
        g_out, g_state = _gla(gq, gk, gg, gv, gr, gla_on, s0, b, t, chunk)
        d_out = diff_fn(dq, dk_b, dv_b)
        m_out = _memattn(mq, mem_k, mem_v, b, t, n_mem)
        x, xn = _outproj(x, g_out, d_out, m_out, wo, _row(norm_ffn2[layer]))
        x = _ffn(x, xn, *ffn2, up_col0=D_FF, final_gain=_row(norm_final[layer]))
        return x, dk, dv, g_state

    yp, dk_p, dv_p, g_p = layer_fn(
        xp1, batch, seq, CHUNK, None,
        lambda dq, dk, dv: _diff_prompt(table, dq, dk, dv, lam_p, diff_on, batch, seq, lam_init), mk_b, mv_b)
    ys, dk_s, dv_s, g_s = layer_fn(
        xs1, dec_batch, dec_seq, dec_seq, state_gla[layer],
        lambda dq, dk, dv: _diff_sample(table, dq, dk, dv, cache_diff_k[layer].reshape(-1, LANES),
                                        cache_diff_v[layer].reshape(-1, LANES),
                                        lam_p, diff_on, dec_batch, dec_seq, past, lam_init),
        per_head(cache_mem_k[layer]), per_head(cache_mem_v[layer]))

    head4 = lambda a, b, t: a.reshape(1, b, t, HEADS, LANES)
    return (yp.reshape(batch, seq, D_MODEL), ys.reshape(dec_batch, dec_seq, D_MODEL),
            head4(dk_p, batch, seq), head4(dv_p, batch, seq), g_p[None],
            head4(mk, batch, n_mem), head4(mv, batch, n_mem),
            head4(dk_s, dec_batch, dec_seq), head4(dv_s, dec_batch, dec_seq), g_s[None])
```

```python
import functools
import math

import jax
import jax.numpy as jnp
from jax import lax
from jax.experimental import pallas as pl
from jax.experimental.pallas import tpu as pltpu

F32 = jnp.float32
BF16 = jnp.bfloat16

D_MODEL = 2048
CHUNK = 64
EPS = 1e-6
NEG_INF = -1e30
GLA_HEADS, GLA_DK, GLA_DV, GLA_GATE_RANK, GLA_GATE_NORM = 4, 128, 256, 16, 16.0
DIFF_HEADS, DIFF_DH, DIFF_DV = 4, 64, 128
MEM_HEADS, MEM_DH = 4, 128
REL_BUCKETS, REL_MAX_DIST = 32, 128
D_FF = 5504
GLA_QK = GLA_HEADS * GLA_DK
GLA_V = GLA_HEADS * GLA_DV
DIFF_QK = DIFF_HEADS * 2 * DIFF_DH
DIFF_V = DIFF_HEADS * DIFF_DV
MEM_W = MEM_HEADS * MEM_DH
GLR_OFF = 2 * GLA_QK + 2 * GLA_V

LANES = 128
HEADS = 4
FF_TILE = 512
FF_TILE_F32 = 256
FFN_TOK_TILE = 1024
TOK_TILE = 512
GLA_TILE = 1024
MEMATTN_TILE = 2048
PROJ_TILE = 256
Q_BLOCK = 256
MASKED_BUCKET = REL_BUCKETS
MIB = 1024 * 1024
VMEM_CAP_MIB = 60
CAST_ROWS_W_IN = 48
CAST_ROWS_W_O = 16

_W_GQ, _W_GK, _W_GV, _W_GR, _W_GLR = 0, GLA_QK, 2 * GLA_QK, 2 * GLA_QK + GLA_V, GLR_OFF
_W_DQ = GLR_OFF + GLA_GATE_RANK
_W_DK = _W_DQ + DIFF_QK
_W_DV = _W_DK + DIFF_QK
_W_MQ = _W_DV + DIFF_V
IN_WIDTH = _W_MQ + MEM_W


def _dot(a, b):
    return jnp.dot(a, b, preferred_element_type=F32)


def _dot_nt(a, b):
    return lax.dot_general(a, b, (((1,), (1,)), ((), ())), preferred_element_type=F32)


def _dot_tn(a, b):
    return lax.dot_general(a, b, (((0,), (0,)), ((), ())), preferred_element_type=F32)


def _rms(x, gain):
    return x * lax.rsqrt(jnp.mean(x * x, axis=-1, keepdims=True) + EPS) * gain


def _params(sem, vmem_mib=48):
    return pltpu.CompilerParams(dimension_semantics=sem, vmem_limit_bytes=min(vmem_mib, VMEM_CAP_MIB) * MIB)


def _ffn_body(*refs, n_ff, overlap, final_norm, n_cast, prenormed, emit_bf16):
    x_ref, g_ref, wg_ref, wu_ref, wo_ref = refs[:5]
    refs = refs[5:]
    if final_norm:
        fg_ref, refs = refs[0], refs[1:]
    cast_src, refs = refs[:n_cast], refs[n_cast:]
    o_ref, refs = refs[0], refs[1:]
    cast_dst, refs = refs[:n_cast], refs[n_cast:]
    if emit_bf16:
        (wg_out, wu_out, wo_out), refs = refs[:3], refs[3:]
    xn_ref = g_ref if prenormed else refs[0]
    j = pl.program_id(1)

    @pl.when(j == 0)
    def _():
        x = x_ref[...]
        if not prenormed:
            xn_ref[...] = _rms(x, g_ref[...]).astype(BF16)
        o_ref[...] = x

    def tile(lo):
        for src, dst in zip(cast_src, cast_dst):
            dst[...] = src[...].astype(BF16)
        xn = xn_ref[...]
        if emit_bf16:
            def front(a, axis):
                a = a.astype(BF16)
                if lo == 0:
                    return a
                return jnp.concatenate([lax.slice_in_dim(a, lo, None, axis=axis),
                                        lax.slice_in_dim(a, 0, lo, axis=axis)], axis=axis)
            wg_out[...] = front(wg_ref[...], 1)
            wu_out[...] = front(wu_ref[...], 1)
            wo_out[...] = front(wo_ref[...], 0)
            live = wg_out.shape[1] - lo
            gate = _dot(xn, wg_out[:, :live])
            up = _dot(xn, wu_out[:, :live])
            act = (jax.nn.silu(gate) * up).astype(BF16)
            return 0.5 * _dot(act, wo_out[:live, :])
        gate = _dot(xn, wg_ref[:, lo:])
        up = _dot(xn, wu_ref[:, lo:])
        act = (jax.nn.silu(gate) * up).astype(BF16)
        return 0.5 * _dot(act, wo_ref[lo:, :])

    @pl.when(j < n_ff - 1)
    def _():
        o_ref[...] += tile(0)

    @pl.when(j == n_ff - 1)
    def _():
        o = o_ref[...] + tile(overlap)
        o_ref[...] = _rms(o, fg_ref[...]) if final_norm else o


def _ffn(x, gain_or_xn, w_gate, w_up, w_out, *, up_col0, final_gain=None, cast_later=(), ff_tile=FF_TILE,
         emit_bf16=False):
    n = x.shape[0]
    tm = min(FFN_TOK_TILE, n)
    n_ff = -(-D_FF // ff_tile)
    row = pl.BlockSpec((1, D_MODEL), lambda i, j: (0, 0))
    prenormed = gain_or_xn.shape != (1, D_MODEL)
    start = lambda j: jnp.minimum(j * (ff_tile // LANES), (D_FF - ff_tile) // LANES)
    w_in_tile = (pl.Element(D_MODEL), pl.Element(ff_tile))
    w_out_tile = (pl.Element(ff_tile), pl.Element(D_MODEL))
    gate_index = lambda i, j: (0, start(j) * LANES)
    out_index = lambda i, j: (start(j) * LANES, 0)
    tok = pl.BlockSpec((tm, D_MODEL), lambda i, j: (i, 0))
    in_specs = [
        tok,
        tok if prenormed else row,
        pl.BlockSpec(w_in_tile, gate_index),
        pl.BlockSpec(w_in_tile, lambda i, j: (0, (up_col0 // LANES + start(j)) * LANES)),
        pl.BlockSpec(w_out_tile, out_index),
    ]
    args = [x, gain_or_xn, w_gate, w_up, w_out]
    if final_gain is not None:
        in_specs.append(row)
        args.append(final_gain)
    w_tile_bytes = D_MODEL * ff_tile * ((4 + 2) if emit_bf16 else 2)
    vmem_bytes = (4 * tm * D_MODEL * 4 + (2 if prenormed else 1) * tm * D_MODEL * 2 + 6 * w_tile_bytes
                  + 3 * tm * ff_tile * 4)
    out_specs = [pl.BlockSpec((tm, D_MODEL), lambda i, j: (i, 0))]
    out_shape = [jax.ShapeDtypeStruct((n, D_MODEL), F32)]
    n_steps = (n // tm) * n_ff
    for mat, blk, first in cast_later:
        rows, cols = mat.shape
        assert rows % blk[0] == 0 and cols % blk[1] == 0 and (blk[0] == rows or blk[1] == cols)
        n_blk = (rows // blk[0]) * (cols // blk[1])
        assert first + n_blk <= n_steps
        by_rows = blk[1] == cols

        def index(i, j, first=first, n_blk=n_blk, by_rows=by_rows):
            k = jnp.clip(i * n_ff + j - first, 0, n_blk - 1)
            return (k, 0) if by_rows else (0, k)

        in_specs.append(pl.BlockSpec(blk, index))
        args.append(mat)
        out_specs.append(pl.BlockSpec(blk, index))
        out_shape.append(jax.ShapeDtypeStruct(mat.shape, BF16))
        vmem_bytes += 2 * blk[0] * blk[1] * (4 + 2)
    if emit_bf16:
        d_ff_pad = n_ff * ff_tile
        cols_j = pl.BlockSpec((D_MODEL, ff_tile), lambda i, j: (0, j))
        out_specs += [cols_j, cols_j, pl.BlockSpec((ff_tile, D_MODEL), lambda i, j: (j, 0))]
        out_shape += ([jax.ShapeDtypeStruct((D_MODEL, d_ff_pad), BF16)] * 2
                      + [jax.ShapeDtypeStruct((d_ff_pad, D_MODEL), BF16)])
    outs = pl.pallas_call(
        functools.partial(_ffn_body, n_ff=n_ff, overlap=n_ff * ff_tile - D_FF, final_norm=final_gain is not None,
                          n_cast=len(cast_later), prenormed=prenormed, emit_bf16=emit_bf16),
        grid=(n // tm, n_ff),
        in_specs=in_specs,
        out_specs=out_specs,
        out_shape=out_shape,
        scratch_shapes=[] if prenormed else [pltpu.VMEM((tm, D_MODEL), BF16)],
        compiler_params=_params(("arbitrary", "arbitrary"), vmem_bytes // MIB + 4),
        name="ffn",
    )(*args)
    return outs[0] if len(outs) == 1 else outs


def _group_rms(acc, gain_ref, width, scale, put):
    lane = lax.broadcasted_iota(jnp.int32, (1, LANES), 1)
    low = lane < width
    for c in range(acc.shape[1] // LANES):
        cols = slice(c * LANES, (c + 1) * LANES)
        xc = acc[:, cols]
        sq = xc * xc
        if width == LANES:
            r = lax.rsqrt(jnp.mean(sq, axis=-1, keepdims=True) + EPS)
        else:
            s_lo = jnp.sum(jnp.where(low, sq, 0.0), axis=-1, keepdims=True)
            s_hi = jnp.sum(jnp.where(low, 0.0, sq), axis=-1, keepdims=True)
            r = jnp.where(low, lax.rsqrt(s_lo / width + EPS), lax.rsqrt(s_hi / width + EPS))
        y = xc * r * gain_ref[:, cols]
        if scale != 1.0:
            y = y * scale
        put(c, y)


def _put_cols(ref):
    def put(c, y):
        ref[:, c * LANES:(c + 1) * LANES] = y.astype(ref.dtype)
    return put


def _put_heads_and_cols(heads_ref, cols_ref):
    def put(c, y):
        heads_ref[:, c, :] = y
        cols_ref[:, c * LANES:(c + 1) * LANES] = y.astype(cols_ref.dtype)
    return put


def _proj_body(x_ref, g_ref, wt_ref, wg2_ref, bg_ref, qn_ref, kn_ref, mqn_ref,
               gq_ref, gk_ref, gv_ref, gr_ref, gg_ref, dq_ref, dk_ref, dv_ref, mq_ref, dkb_ref, dvb_ref, xn_ref):
    xn_ref[...] = _rms(x_ref[...], g_ref[...]).astype(BF16)

    def cols(start, width):
        return _dot_nt(xn_ref[...], wt_ref[start:start + width, :])

    gq_ref[...] = cols(_W_GQ, GLA_QK) * (GLA_DK ** -0.5)
    gk_ref[...] = cols(_W_GK, GLA_QK)
    gv_ref[...] = cols(_W_GV, GLA_V).astype(BF16)
    gr_ref[...] = cols(_W_GR, GLA_V)
    glr = cols(_W_GLR, LANES)
    z = _dot(glr.astype(BF16), wg2_ref[...]) + bg_ref[...]
    gg_ref[...] = jax.nn.log_sigmoid(z) / GLA_GATE_NORM
    _group_rms(cols(_W_DQ, DIFF_QK), qn_ref, DIFF_DH, DIFF_DH ** -0.5, _put_cols(dq_ref))
    _group_rms(cols(_W_DK, DIFF_QK), kn_ref, DIFF_DH, 1.0, _put_heads_and_cols(dk_ref, dkb_ref))
    dv = cols(_W_DV, DIFF_V)
    put_v = _put_heads_and_cols(dv_ref, dvb_ref)
    for h in range(HEADS):
        put_v(h, dv[:, h * LANES:(h + 1) * LANES])
    _group_rms(cols(_W_MQ, MEM_W), mqn_ref, MEM_DH, 1.0, _put_cols(mq_ref))


def _proj(x, gain, wt, wg2, bg, qn, kn, mqn):
    n = x.shape[0]
    tm = min(PROJ_TILE, n)
    const = lambda shape: pl.BlockSpec(shape, lambda i: (0, 0))
    out = lambda width: pl.BlockSpec((tm, width), lambda i: (i, 0))
    heads = pl.BlockSpec((tm, HEADS, LANES), lambda i: (i, 0, 0))
    shp = lambda width, dt: jax.ShapeDtypeStruct((n, width), dt)
    shp_heads = jax.ShapeDtypeStruct((n, HEADS, LANES), F32)
    return pl.pallas_call(
        _proj_body,
        grid=(n // tm,),
        in_specs=[
            pl.BlockSpec((tm, D_MODEL), lambda i: (i, 0)),
            const((1, D_MODEL)),
            pl.BlockSpec((IN_WIDTH, D_MODEL), lambda i: (0, 0), pipeline_mode=pl.Buffered(1)),
            const((LANES, GLA_QK)),
            const((1, GLA_QK)),
            const((1, DIFF_QK)),
            const((1, DIFF_QK)),
            const((1, MEM_W)),
        ],
        out_specs=[out(GLA_QK), out(GLA_QK), out(GLA_V), out(GLA_V), out(GLA_QK),
                   out(DIFF_QK), heads, heads, out(MEM_W), out(DIFF_QK), out(DIFF_V)],
        out_shape=[shp(GLA_QK, F32), shp(GLA_QK, F32), shp(GLA_V, BF16), shp(GLA_V, F32), shp(GLA_QK, F32),
                   shp(DIFF_QK, BF16), shp_heads, shp_heads, shp(MEM_W, BF16), shp(DIFF_QK, BF16), shp(DIFF_V, BF16)],
        scratch_shapes=[pltpu.VMEM((tm, D_MODEL), BF16)],
        compiler_params=_params(("parallel",)),
        name="proj",
    )(x, gain, wt, wg2, bg, qn, kn, mqn)


def _memkv_body(x_ref, g_ref, w_ref, kn_ref, k_ref, v_ref, kb_ref, vb_ref):
    xn = _rms(x_ref[...], g_ref[...]).astype(BF16)
    _group_rms(_dot(xn, w_ref[:, :MEM_W]), kn_ref, MEM_DH, 1.0, _put_heads_and_cols(k_ref, kb_ref))
    v = _dot(xn, w_ref[:, MEM_W:])
    put_v = _put_heads_and_cols(v_ref, vb_ref)
    for h in range(HEADS):
        put_v(h, v[:, h * LANES:(h + 1) * LANES])


def _memkv(mem, gain, w, kn):
    n = mem.shape[0]
    tm = min(TOK_TILE, n)
    const = lambda shape: pl.BlockSpec(shape, lambda i: (0, 0))
    heads = pl.BlockSpec((tm, HEADS, LANES), lambda i: (i, 0, 0))
    dense = pl.BlockSpec((tm, MEM_W), lambda i: (i, 0))
    return pl.pallas_call(
        _memkv_body,
        grid=(n // tm,),
        in_specs=[pl.BlockSpec((tm, D_MODEL), lambda i: (i, 0)), const((1, D_MODEL)),
                  const((D_MODEL, 2 * MEM_W)), const((1, MEM_W))],
        out_specs=[heads, heads, dense, dense],
        out_shape=[jax.ShapeDtypeStruct((n, HEADS, LANES), F32)] * 2 + [jax.ShapeDtypeStruct((n, MEM_W), BF16)] * 2,
        compiler_params=_params(("parallel",)),
        name="memkv",
    )(mem, gain, w, kn)


def _split3(x):
    hi = x.astype(BF16)
    r1 = x - hi.astype(F32)
    mid = r1.astype(BF16)
    lo = (r1 - mid.astype(F32)).astype(BF16)
    return hi, mid, lo


def _gla_body(*refs, chunk, n_chunks, n_steps, has_state):
    if has_state:
        gq_ref, gk_ref, gg_ref, gv_ref, gr_ref, on_ref, s0_ref, go_ref, st_ref, state = refs
    else:
        gq_ref, gk_ref, gg_ref, gv_ref, gr_ref, on_ref, go_ref, st_ref, state = refs
    t = pl.program_id(1)

    @pl.when(t == 0)
    def _():
        for h in range(GLA_HEADS):
            if has_state:
                state[h] = s0_ref[0, h].T
            else:
                state[h] = jnp.zeros((GLA_DV, GLA_DK), F32)

    row = lax.broadcasted_iota(jnp.int32, (chunk, chunk), 0)
    col = lax.broadcasted_iota(jnp.int32, (chunk, chunk), 1)
    causal = row >= col
    tril = causal.astype(BF16)

    chunks = [slice(c * chunk, (c + 1) * chunk) for c in range(n_chunks)]
    kcs = [slice(h * GLA_DK, (h + 1) * GLA_DK) for h in range(GLA_HEADS)]
    vcs = [slice(h * GLA_DV, (h + 1) * GLA_DV) for h in range(GLA_HEADS)]

    bs = []
    for rows in chunks:
        g_hi, g_mid, g_lo = _split3(gg_ref[rows, :])
        bs.append(_dot(tril, g_hi) + _dot(tril, g_mid) + _dot(tril, g_lo))

    qes, kes, kds, decays = [], [], [], []
    for rows, b in zip(chunks, bs):
        b_last = b[chunk - 1:chunk, :]
        q = gq_ref[rows, :]
        k = gk_ref[rows, :]
        qes.append((q * jnp.exp(b)).astype(BF16))
        kes.append((k * jnp.exp(-b)).astype(BF16))
        kds.append((k * jnp.exp(b_last - b)).astype(BF16))
        decays.append(jnp.exp(b_last))

    a_s = [[jnp.where(causal, _dot_nt(qe[:, kc], ke[:, kc]), 0.0).astype(BF16) for kc in kcs]
           for qe, ke in zip(qes, kes)]
    incs = [[_dot_tn(gv_ref[rows, vc], kd[:, kc]) for kc, vc in zip(kcs, vcs)] for rows, kd in zip(chunks, kds)]

    s_in = []
    s_cur = [state[h] for h in range(GLA_HEADS)]
    for c in range(n_chunks):
        s_in.append([s.astype(BF16) for s in s_cur])
        s_cur = [s * decays[c][:, kc] + inc for s, kc, inc in zip(s_cur, kcs, incs[c])]
    for h in range(GLA_HEADS):
        state[h] = s_cur[h]

    for c, rows in enumerate(chunks):
        for h, (kc, vc) in enumerate(zip(kcs, vcs)):
            o = _dot_nt(qes[c][:, kc], s_in[c][h]) + _dot(a_s[c][h], gv_ref[rows, vc])
            go_ref[rows, vc] = (_rms(o, on_ref[...]) * jax.nn.silu(gr_ref[rows, vc])).astype(BF16)

    @pl.when(t == n_steps - 1)
    def _():
        for h in range(GLA_HEADS):
            st_ref[0, h] = state[h].T


def _gla(gq, gk, gg, gv, gr, onorm, s0, batch, seq, chunk):
    tt = min(GLA_TILE, seq)
    n_steps = seq // tt
    tok = lambda width: pl.BlockSpec((tt, width), lambda b, t: (b * n_steps + t, 0))
    st_spec = pl.BlockSpec((1, GLA_HEADS, GLA_DK, GLA_DV), lambda b, t: (b, 0, 0, 0))
    in_specs = [tok(GLA_QK), tok(GLA_QK), tok(GLA_QK), tok(GLA_V), tok(GLA_V),
                pl.BlockSpec((1, GLA_DV), lambda b, t: (0, 0))]
    args = [gq, gk, gg, gv, gr, onorm]
    if s0 is not None:
        in_specs.append(st_spec)
        args.append(s0)
    return pl.pallas_call(
        functools.partial(_gla_body, chunk=chunk, n_chunks=tt // chunk, n_steps=n_steps, has_state=s0 is not None),
        grid=(batch, n_steps),
        in_specs=in_specs,
        out_specs=[tok(GLA_V), st_spec],
        out_shape=[jax.ShapeDtypeStruct((batch * seq, GLA_V), BF16),
                   jax.ShapeDtypeStruct((batch, GLA_HEADS, GLA_DK, GLA_DV), F32)],
        scratch_shapes=[pltpu.VMEM((GLA_HEADS, GLA_DV, GLA_DK), F32)],
        compiler_params=_params(("parallel", "arbitrary")),
        name="gla",
    )(*args)


def _t5_bucket(rel):
    nb = REL_BUCKETS // 2
    max_exact = nb // 2
    ret = jnp.where(rel > 0, nb, 0)
    n = jnp.abs(rel)
    nf = jnp.maximum(n, 1).astype(F32)
    large = max_exact + (jnp.log(nf / max_exact) / math.log(REL_MAX_DIST / max_exact)
                         * (nb - max_exact)).astype(jnp.int32)
    large = jnp.minimum(large, nb - 1)
    return ret + jnp.where(n < max_exact, n, large)


def _bucket_tile(q_pos, k_pos):
    visible = (k_pos[None, :] // CHUNK) <= (q_pos[:, None] // CHUNK)
    return jnp.where(visible, _t5_bucket(k_pos[None, :] - q_pos[:, None]), MASKED_BUCKET).astype(jnp.int32)


def _bias_from_buckets(idx, tab_ref, head):
    def step(bk, acc):
        return jnp.where(idx == bk, tab_ref[bk * DIFF_HEADS + head], acc)
    return lax.fori_loop(0, REL_BUCKETS, step, jnp.full(idx.shape, NEG_INF, F32))


def _lambda(lam_ref, lam_init):
    l = lam_ref[...]
    return (jnp.exp(jnp.sum(l[0:1] * l[1:2], axis=-1, keepdims=True))
            - jnp.exp(jnp.sum(l[2:3] * l[3:4], axis=-1, keepdims=True)) + lam_init)


def _comp_masks(q):
    lane = lax.broadcasted_iota(jnp.int32, q.shape, 1)
    zero = jnp.zeros_like(q)
    return jnp.where(lane < DIFF_DH, q, zero), jnp.where(lane < DIFF_DH, zero, q)


def _diff_finish(o0, o1, lam, on_ref, lam_init, out_dtype):
    o = o0 - lam * o1
    return (_rms(o, on_ref[...]) * (1.0 - lam_init)).astype(out_dtype)


def _scores(q_c, parts):
    return [_dot_nt(q_c, k) + bias if jnp.ndim(bias) == 2 else _dot_nt(q_c, k) for k, _, bias in parts]


def _softmax_pv(scores, parts):
    shifts = [0.0 if jnp.ndim(bias) == 2 else bias for _, _, bias in parts]
    m = None
    for s, shift in zip(scores, shifts):
        part_max = s.max(axis=-1, keepdims=True) + shift
        m = part_max if m is None else jnp.maximum(m, part_max)
    ol = 0.0
    for s, shift, (_, v1, _) in zip(scores, shifts, parts):
        ol = ol + _dot(jnp.exp(s - (m - shift)).astype(BF16), v1)
    return ol[:, :DIFF_DV] * (1.0 / ol[:, DIFF_DV:])


def _diff_prompt_body(tab_ref, q_ref, kb, vb, idx_ref, lam_ref, on_ref, o_ref, bias, v1, *, seq, lam_init):
    b = pl.program_id(0)
    h = pl.program_id(1)

    @pl.when(b == 0)
    def _():
        for t in range(2):
            bias[h, t] = _bias_from_buckets(idx_ref[t], tab_ref, h)

    v1[:, :DIFF_DV] = vb[...]
    v1[:, DIFF_DV:] = jnp.ones((seq, LANES), BF16)
    lam = _lambda(lam_ref, lam_init)
    far_bias = tab_ref[(REL_BUCKETS // 2 - 1) * DIFF_HEADS + h]

    def key_parts(i):
        near0 = max(i - 1, 0) * Q_BLOCK
        near = slice(near0, near0 + 2 * Q_BLOCK)
        parts = [(kb[near, :], v1[near, :], bias[h, min(i, 1)])]
        if near0 > 0:
            parts.append((kb[0:near0, :], v1[0:near0, :], far_bias))
        return parts

    items = [(i, c) for i in range(seq // Q_BLOCK) for c in range(2)]
    comps = {}

    def scores_of(item):
        i, c = item
        if i not in comps:
            comps[i] = _comp_masks(q_ref[i * Q_BLOCK:(i + 1) * Q_BLOCK, :])
        return _scores(comps[i][c], key_parts(i))

    outs = {}
    ahead = scores_of(items[0])
    for n, (i, c) in enumerate(items):
        cur = ahead
        if n + 1 < len(items):
            ahead = scores_of(items[n + 1])
        outs[c] = _softmax_pv(cur, key_parts(i))
        if c == 1:
            o_ref[i * Q_BLOCK:(i + 1) * Q_BLOCK, :] = _diff_finish(outs[0], outs[1], lam, on_ref, lam_init,
                                                                    o_ref.dtype)


def _diff_prompt(table, dq, dk, dv, lam_p, onorm, batch, seq, lam_init):
    q_pos = jnp.arange(Q_BLOCK)
    idx = jnp.stack([_bucket_tile(q_pos, jnp.arange(2 * Q_BLOCK)),
                     _bucket_tile(q_pos + Q_BLOCK, jnp.arange(2 * Q_BLOCK))])
    head = lambda: pl.BlockSpec((seq, LANES), lambda b, h: (b, h))
    return pl.pallas_call(
        functools.partial(_diff_prompt_body, seq=seq, lam_init=lam_init),
        grid=(batch, DIFF_HEADS),
        in_specs=[
            pl.BlockSpec(memory_space=pltpu.SMEM),
            head(), head(), head(),
            pl.BlockSpec((2, Q_BLOCK, 2 * Q_BLOCK), lambda b, h: (0, 0, 0)),
            pl.BlockSpec((4, DIFF_DH), lambda b, h: (0, 0)),
            pl.BlockSpec((1, DIFF_DV), lambda b, h: (0, 0)),
        ],
        out_specs=head(),
        out_shape=jax.ShapeDtypeStruct((batch * seq, DIFF_V), BF16),
        scratch_shapes=[pltpu.VMEM((DIFF_HEADS, 2, Q_BLOCK, 2 * Q_BLOCK), F32),
                        pltpu.VMEM((seq, DIFF_DV + LANES), BF16)],
        compiler_params=_params(("arbitrary", "arbitrary")),
        name="diff_prompt",
    )(table, dq, dk, dv, idx, lam_p, onorm)


def _diff_sample_body(tab_ref, q_ref, k_ref, v_ref, ck_ref, cv_ref, idxc_ref, idxn_ref, lam_ref, on_ref, o_ref,
                      bias_c, bias_n, *, seq, lam_init):
    groups = [(h, c) for h in range(DIFF_HEADS) for c in range(2)]

    @pl.when(pl.program_id(0) == 0)
    def _():
        for h in range(DIFF_HEADS):
            tile = _bias_from_buckets(idxc_ref[h], tab_ref, h)
            for g in (2 * h, 2 * h + 1):
                bias_c[g * seq:(g + 1) * seq, :] = tile
            bias_n[h] = _bias_from_buckets(idxn_ref[...], tab_ref, h)

    lam = _lambda(lam_ref, lam_init)
    head_cols = [slice(h * LANES, (h + 1) * LANES) for h in range(DIFF_HEADS)]
    q_groups = [q_c for cols in head_cols for q_c in _comp_masks(q_ref[:, cols])]
    s_old = _dot_nt(jnp.concatenate(q_groups, axis=0), ck_ref[...].astype(BF16)) + bias_c[...]
    s_new = jnp.concatenate([_dot_nt(q_g, k_ref[:, head_cols[h]]) + bias_n[h]
                             for q_g, (h, _) in zip(q_groups, groups)], axis=0)
    m = jnp.maximum(s_old.max(axis=-1, keepdims=True), s_new.max(axis=-1, keepdims=True))
    e_old = jnp.exp(s_old - m)
    e_new = jnp.exp(s_new - m)
    l = e_old.sum(axis=-1, keepdims=True) + e_new.sum(axis=-1, keepdims=True)
    o_old = _dot(e_old.astype(BF16), cv_ref[...].astype(BF16))
    e_new = e_new.astype(BF16)
    o_new = jnp.concatenate([_dot(e_new[g * seq:(g + 1) * seq, :], v_ref[:, head_cols[h]])
                             for g, (h, _) in enumerate(groups)], axis=0)
    o = (o_old + o_new) * (1.0 / l)
    for h, cols in enumerate(head_cols):
        o0 = o[(2 * h) * seq:(2 * h + 1) * seq, :]
        o1 = o[(2 * h + 1) * seq:(2 * h + 2) * seq, :]
        o_ref[:, cols] = _diff_finish(o0, o1, lam, on_ref, lam_init, o_ref.dtype)


def _diff_sample(table, dq, dk, dv, cache_k, cache_v, lam_p, onorm, batch, seq, past, lam_init):
    q_pos = past + jnp.arange(seq)
    idx_n = _bucket_tile(q_pos, past + jnp.arange(seq))
    idx_rows = jnp.repeat(_bucket_tile(q_pos, jnp.arange(past)), HEADS, axis=1)
    row_head = jnp.arange(past * HEADS) % HEADS
    idx_c = jnp.where(row_head[None, None, :] == jnp.arange(HEADS)[:, None, None], idx_rows[None], MASKED_BUCKET)
    new_q = pl.BlockSpec((seq, DIFF_QK), lambda b: (b, 0))
    old = lambda: pl.BlockSpec((past * HEADS, LANES), lambda b: (b, 0))
    const = lambda shape: pl.BlockSpec(shape, lambda b: (0,) * len(shape))
    n_groups = 2 * DIFF_HEADS
    return pl.pallas_call(
        functools.partial(_diff_sample_body, seq=seq, lam_init=lam_init),
        grid=(batch,),
        in_specs=[pl.BlockSpec(memory_space=pltpu.SMEM), new_q, new_q, new_q, old(), old(),
                  const((HEADS, seq, past * HEADS)), const((seq, seq)), const((4, DIFF_DH)), const((1, DIFF_DV))],
        out_specs=new_q,
        out_shape=jax.ShapeDtypeStruct((batch * seq, DIFF_V), BF16),
        scratch_shapes=[pltpu.VMEM((n_groups * seq, past * HEADS), F32), pltpu.VMEM((DIFF_HEADS, seq, seq), F32)],
        compiler_params=_params(("arbitrary",)),
        name="diff_sample",
    )(table, dq, dk, dv, cache_k, cache_v, idx_c, idx_n, lam_p, onorm)


def _memattn_body(q_ref, k_ref, v_ref, o_ref, *, per_head):
    head_cols = [slice(h * MEM_DH, (h + 1) * MEM_DH) for h in range(MEM_HEADS)]
    if per_head:
        ks = [k_ref[:, h, :].astype(BF16) for h in range(MEM_HEADS)]
        vs = [v_ref[:, h, :].astype(BF16) for h in range(MEM_HEADS)]
    else:
        ks = [k_ref[:, cols] for cols in head_cols]
        vs = [v_ref[:, cols] for cols in head_cols]
    scores = [_dot_nt(q_ref[:, cols], k) * (MEM_DH ** -0.5) for cols, k in zip(head_cols, ks)]
    probs = []
    for s in scores:
        e = jnp.exp(s - s.max(axis=-1, keepdims=True))
        probs.append((e * (1.0 / e.sum(axis=-1, keepdims=True))).astype(BF16))
    for cols, p, v in zip(head_cols, probs, vs):
        o_ref[:, cols] = _dot(p, v).astype(o_ref.dtype)


def _memattn(mq, mk, mv, batch, seq, n_mem):
    tq = min(MEMATTN_TILE, seq)
    nq = seq // tq
    q_spec = pl.BlockSpec((tq, MEM_W), lambda b, i: (b * nq + i, 0))
    per_head = mk.ndim == 3
    if per_head:
        kv_spec = pl.BlockSpec((n_mem, HEADS, LANES), lambda b, i: (b, 0, 0))
    else:
        kv_spec = pl.BlockSpec((n_mem, MEM_W), lambda b, i: (b, 0))
    return pl.pallas_call(
        functools.partial(_memattn_body, per_head=per_head),
        grid=(batch, nq),
        in_specs=[q_spec, kv_spec, kv_spec],
        out_specs=q_spec,
        out_shape=jax.ShapeDtypeStruct((batch * seq, MEM_W), BF16),
        compiler_params=_params(("parallel", "arbitrary")),
        name="memattn",
    )(mq, mk, mv)


def _outproj_body(x_ref, g_ref, d_ref, m_ref, wg_ref, wd_ref, wm_ref, ng_ref, o_ref, on_ref):
    o = (x_ref[...] + _dot(g_ref[...], wg_ref[...]) + _dot(d_ref[...], wd_ref[...])
         + _dot(m_ref[...], wm_ref[...]))
    o_ref[...] = o
    on_ref[...] = _rms(o, ng_ref[...]).astype(BF16)


def _outproj(x, g, d, m, wo, next_gain):
    n = x.shape[0]
    tm = min(TOK_TILE, n)
    tok = lambda width: pl.BlockSpec((tm, width), lambda i: (i, 0))
    w_rows = lambda rows, blk: pl.BlockSpec((rows, D_MODEL), lambda i: (blk, 0))
    return pl.pallas_call(
        _outproj_body,
        grid=(n // tm,),
        in_specs=[tok(D_MODEL), tok(GLA_V), tok(DIFF_V), tok(MEM_W),
                  w_rows(GLA_V, 0), w_rows(DIFF_V, GLA_V // DIFF_V), w_rows(MEM_W, (GLA_V + DIFF_V) // MEM_W),
                  pl.BlockSpec((1, D_MODEL), lambda i: (0, 0))],
        out_specs=[tok(D_MODEL), tok(D_MODEL)],
        out_shape=[jax.ShapeDtypeStruct((n, D_MODEL), F32), jax.ShapeDtypeStruct((n, D_MODEL), BF16)],
        compiler_params=_params(("parallel",)),
        name="outproj",
    )(x, g, d, m, wo, wo, wo, next_gain)


def _row(v):
    return v.reshape(1, -1).astype(F32)


def kernel(x_prompt, x_sample, mem_prompt, cache_diff_k, cache_diff_v, state_gla, cache_mem_k, cache_mem_v, rel_bias_table, norm_ffn1, w_ffn1_in, w_ffn1_out, norm_mix, w_in, w_gla_g2, b_gla_g, gla_out_norm, diff_q_norm, diff_k_norm, diff_lambda, diff_out_norm, mem_norm, w_mem_kv, mem_q_norm, mem_k_norm, w_o, norm_ffn2, w_ffn2_in, w_ffn2_out, norm_final):
    depth = norm_ffn1.shape[0]
    assert depth == 1, "single-layer step"
    layer = 0
    batch, seq, _ = x_prompt.shape
    dec_batch, dec_seq, _ = x_sample.shape
    past = cache_diff_k.shape[2]
    n_mem = mem_prompt.shape[1]
    lam_init = 0.8 - 0.6 * math.exp(-0.3 * layer)

    w_g2 =jnp.pad(w_gla_g2[layer].astype(BF16), ((0, LANES - GLA_GATE_RANK), (0, 0)))
    b_g = _row(b_gla_g[layer])
    qn = _row(jnp.tile(diff_q_norm[layer], DIFF_QK // DIFF_DH))
    kn = _row(jnp.tile(diff_k_norm[layer], DIFF_QK // DIFF_DH))
    mqn = _row(jnp.tile(mem_q_norm[layer], MEM_HEADS))
    mkn = _row(jnp.tile(mem_k_norm[layer], MEM_HEADS))
    table = rel_bias_table.astype(F32).reshape(-1)
    lam_p = diff_lambda[layer].astype(F32)
    gla_on = _row(gla_out_norm[layer])
    diff_on = _row(diff_out_norm[layer])
    per_head = lambda a: a.reshape(-1, HEADS, LANES)

    mk, mv, mk_b, mv_b = _memkv(mem_prompt.reshape(batch * n_mem, D_MODEL), _row(mem_norm[layer]),
                                w_mem_kv[layer].astype(BF16), mkn)

    n1 = _row(norm_ffn1[layer])
    xs1, *ffn1 = _ffn(x_sample.reshape(dec_batch * dec_seq, D_MODEL), n1, w_ffn1_in[layer], w_ffn1_in[layer],
                      w_ffn1_out[layer], up_col0=D_FF, ff_tile=FF_TILE_F32, emit_bf16=True)
    n_in_blocks = w_ffn2_in.shape[2] // LANES
    xp1, w2_in, w2_out, w_t, wo = _ffn(
        x_prompt.reshape(batch * seq, D_MODEL), n1, *ffn1, up_col0=0,
        cast_later=[(w_ffn2_in[layer], (D_MODEL, LANES), 0),
                    (w_ffn2_out[layer], (D_FF // n_in_blocks, D_MODEL), n_in_blocks),
                    (jnp.swapaxes(w_in[layer], 0, 1), (CAST_ROWS_W_IN, D_MODEL), 0),
                    (w_o[layer], (CAST_ROWS_W_O, D_MODEL), 0)])
    ffn2 = (w2_in, w2_in, w2_out)

    def layer_fn(x, b, t, chunk, s0, diff_fn, mem_k, mem_v):
        gq, gk, gv, gr, gg, dq, dk, dv, mq, dk_b, dv_b = _proj(x, _row(norm_mix[layer]), w_t, w_g2, b_g,
                                                               qn, kn, mqn)
        g_out, g_state = _gla(gq, gk, gg, gv, gr, gla_on, s0, b, t, chunk)
        d_out = diff_fn(dq, dk_b, dv_b)
        m_out = _memattn(mq, mem_k, mem_v, b, t, n_mem)
        x, xn = _outproj(x, g_out, d_out, m_out, wo, _row(norm_ffn2[layer]))
        x = _ffn(x, xn, *ffn2, up_col0=D_FF, final_gain=_row(norm_final[layer]))
        return x, dk, dv, g_state

    yp, dk_p, dv_p, g_p = layer_fn(
        xp1, batch, seq, CHUNK, None,
        lambda dq, dk, dv: _diff_prompt(table, dq, dk, dv, lam_p, diff_on, batch, seq, lam_init), mk_b, mv_b)
    ys, dk_s, dv_s, g_s = layer_fn(
        xs1, dec_batch, dec_seq, dec_seq, state_gla[layer],
        lambda dq, dk, dv: _diff_sample(table, dq, dk, dv, cache_diff_k[layer].reshape(-1, LANES),
                                        cache_diff_v[layer].reshape(-1, LANES),
                                        lam_p, diff_on, dec_batch, dec_seq, past, lam_init),
        per_head(cache_mem_k[layer]), per_head(cache_mem_v[layer]))

    head4 = lambda a, b, t: a.reshape(1, b, t, HEADS, LANES)
    return (yp.reshape(batch, seq, D_MODEL), ys.reshape(dec_batch, dec_seq, D_MODEL),
            head4(dk_p, batch, seq), head4(dv_p, batch, seq), g_p[None],
            head4(mk, batch, n_mem), head4(mv, batch, n_mem),
            head4(dk_s, dec_batch, dec_seq), head4(dv_s, dec_batch, dec_seq), g_s[None])
```

```python
import functools
import math

import jax
import jax.numpy as jnp
from jax import lax
from jax.experimental import pallas as pl
from jax.experimental.pallas import tpu as pltpu

F32 = jnp.float32
BF16 = jnp.bfloat16

D_MODEL = 2048
CHUNK = 64
EPS = 1e-6
NEG_INF = -1e30
GLA_HEADS, GLA_DK, GLA_DV, GLA_GATE_RANK, GLA_GATE_NORM = 4, 128, 256, 16, 16.0
DIFF_HEADS, DIFF_DH, DIFF_DV = 4, 64, 128
MEM_HEADS, MEM_DH = 4, 128
REL_BUCKETS, REL_MAX_DIST = 32, 128
D_FF = 5504
GLA_QK = GLA_HEADS * GLA_DK
GLA_V = GLA_HEADS * GLA_DV
DIFF_QK = DIFF_HEADS * 2 * DIFF_DH
DIFF_V = DIFF_HEADS * DIFF_DV
MEM_W = MEM_HEADS * MEM_DH
GLR_OFF = 2 * GLA_QK + 2 * GLA_V

LANES = 128
HEADS = 4
FF_TILE = 512
FF_TILE_F32 = 256
FFN_TOK_TILE = 1024
TOK_TILE = 512
GLA_TILE = 1024
MEMATTN_TILE = 2048
PROJ_TILE = 256
Q_BLOCK = 256
MASKED_BUCKET = REL_BUCKETS
MIB = 1024 * 1024
VMEM_CAP_MIB = 60
CAST_ROWS_W_IN = 48
CAST_ROWS_W_O = 16

_W_GQ, _W_GK, _W_GV, _W_GR, _W_GLR = 0, GLA_QK, 2 * GLA_QK, 2 * GLA_QK + GLA_V, GLR_OFF
_W_DQ = GLR_OFF + GLA_GATE_RANK
_W_DK = _W_DQ + DIFF_QK
_W_DV = _W_DK + DIFF_QK
_W_MQ = _W_DV + DIFF_V
IN_WIDTH = _W_MQ + MEM_W


def _dot(a, b):
    return jnp.dot(a, b, preferred_element_type=F32)


def _dot_nt(a, b):
    return lax.dot_general(a, b, (((1,), (1,)), ((), ())), preferred_element_type=F32)


def _dot_tn(a, b):
    return lax.dot_general(a, b, (((0,), (0,)), ((), ())), preferred_element_type=F32)


def _rms(x, gain):
    return x * lax.rsqrt(jnp.mean(x * x, axis=-1, keepdims=True) + EPS) * gain


def _params(sem, vmem_mib=48):
    return pltpu.CompilerParams(dimension_semantics=sem, vmem_limit_bytes=min(vmem_mib, VMEM_CAP_MIB) * MIB)


def _ffn_body(*refs, n_ff, overlap, final_norm, n_cast, prenormed, emit_bf16):
    x_ref, g_ref, wg_ref, wu_ref, wo_ref = refs[:5]
    refs = refs[5:]
    if final_norm:
        fg_ref, refs = refs[0], refs[1:]
    cast_src, refs = refs[:n_cast], refs[n_cast:]
    o_ref, refs = refs[0], refs[1:]
    cast_dst, refs = refs[:n_cast], refs[n_cast:]
    if emit_bf16:
        (wg_out, wu_out, wo_out), refs = refs[:3], refs[3:]
    xn_ref = g_ref if prenormed else refs[0]
    j = pl.program_id(1)

    def tile(lo):
        for src, dst in zip(cast_src, cast_dst):
            dst[...] = src[...].astype(BF16)
        xn = xn_ref[...]
        if emit_bf16:
            def front(a, axis):
                a = a.astype(BF16)
                if lo == 0:
                    return a
                return jnp.concatenate([lax.slice_in_dim(a, lo, None, axis=axis),
                                        lax.slice_in_dim(a, 0, lo, axis=axis)], axis=axis)
            wg_out[...] = front(wg_ref[...], 1)
            wu_out[...] = front(wu_ref[...], 1)
            wo_out[...] = front(wo_ref[...], 0)
            live = wg_out.shape[1] - lo
            gate = _dot(xn, wg_out[:, :live])
            up = _dot(xn, wu_out[:, :live])
            act = (jax.nn.silu(gate) * up).astype(BF16)
            return 0.5 * _dot(act, wo_out[:live, :])
        gate = _dot(xn, wg_ref[:, lo:])
        up = _dot(xn, wu_ref[:, lo:])
        act = (jax.nn.silu(gate) * up).astype(BF16)
        return 0.5 * _dot(act, wo_ref[lo:, :])

    @pl.when(j == 0)
    def _():
        if not prenormed:
            xn_ref[...] = _rms(x_ref[...], g_ref[...]).astype(BF16)
        o_ref[...] = x_ref[...] + tile(0)

    @pl.when((j > 0) & (j < n_ff - 1))
    def _():
        o_ref[...] += tile(0)

    @pl.when(j == n_ff - 1)
    def _():
        o = o_ref[...] + tile(overlap)
        o_ref[...] = _rms(o, fg_ref[...]) if final_norm else o


def _ffn(x, gain_or_xn, w_gate, w_up, w_out, *, up_col0, final_gain=None, cast_later=(), ff_tile=FF_TILE,
         emit_bf16=False):
    n = x.shape[0]
    tm = min(FFN_TOK_TILE, n)
    n_ff = -(-D_FF // ff_tile)
    row = pl.BlockSpec((1, D_MODEL), lambda i, j: (0, 0))
    prenormed = gain_or_xn.shape != (1, D_MODEL)
    start = lambda j: jnp.minimum(j * (ff_tile // LANES), (D_FF - ff_tile) // LANES)
    w_in_tile = (pl.Element(D_MODEL), pl.Element(ff_tile))
    w_out_tile = (pl.Element(ff_tile), pl.Element(D_MODEL))
    gate_index = lambda i, j: (0, start(j) * LANES)
    out_index = lambda i, j: (start(j) * LANES, 0)
    tok = pl.BlockSpec((tm, D_MODEL), lambda i, j: (i, 0))
    in_specs = [
        tok,
        tok if prenormed else row,
        pl.BlockSpec(w_in_tile, gate_index),
        pl.BlockSpec(w_in_tile, lambda i, j: (0, (up_col0 // LANES + start(j)) * LANES)),
        pl.BlockSpec(w_out_tile, out_index),
    ]
    args = [x, gain_or_xn, w_gate, w_up, w_out]
    if final_gain is not None:
        in_specs.append(row)
        args.append(final_gain)
    w_tile_bytes = D_MODEL * ff_tile * ((4 + 2) if emit_bf16 else 2)
    vmem_bytes = (4 * tm * D_MODEL * 4 + (2 if prenormed else 1) * tm * D_MODEL * 2 + 6 * w_tile_bytes
                  + 3 * tm * ff_tile * 4)
    out_specs = [pl.BlockSpec((tm, D_MODEL), lambda i, j: (i, 0))]
    out_shape = [jax.ShapeDtypeStruct((n, D_MODEL), F32)]
    n_steps = (n // tm) * n_ff
    for mat, blk, first in cast_later:
        rows, cols = mat.shape
        assert rows % blk[0] == 0 and cols % blk[1] == 0 and (blk[0] == rows or blk[1] == cols)
        n_blk = (rows // blk[0]) * (cols // blk[1])
        assert first + n_blk <= n_steps
        by_rows = blk[1] == cols

        def index(i, j, first=first, n_blk=n_blk, by_rows=by_rows):
            k = jnp.clip(i * n_ff + j - first, 0, n_blk - 1)
            return (k, 0) if by_rows else (0, k)

        in_specs.append(pl.BlockSpec(blk, index))
        args.append(mat)
        out_specs.append(pl.BlockSpec(blk, index))
        out_shape.append(jax.ShapeDtypeStruct(mat.shape, BF16))
        vmem_bytes += 2 * blk[0] * blk[1] * (4 + 2)
    if emit_bf16:
        d_ff_pad = n_ff * ff_tile
        cols_j = pl.BlockSpec((D_MODEL, ff_tile), lambda i, j: (0, j))
        out_specs += [cols_j, cols_j, pl.BlockSpec((ff_tile, D_MODEL), lambda i, j: (j, 0))]
        out_shape += ([jax.ShapeDtypeStruct((D_MODEL, d_ff_pad), BF16)] * 2
                      + [jax.ShapeDtypeStruct((d_ff_pad, D_MODEL), BF16)])
    outs = pl.pallas_call(
        functools.partial(_ffn_body, n_ff=n_ff, overlap=n_ff * ff_tile - D_FF, final_norm=final_gain is not None,
                          n_cast=len(cast_later), prenormed=prenormed, emit_bf16=emit_bf16),
        grid=(n // tm, n_ff),
        in_specs=in_specs,
        out_specs=out_specs,
        out_shape=out_shape,
        scratch_shapes=[] if prenormed else [pltpu.VMEM((tm, D_MODEL), BF16)],
        compiler_params=_params(("arbitrary", "arbitrary"), vmem_bytes // MIB + 4),
        name="ffn",
    )(*args)
    return outs[0] if len(outs) == 1 else outs


def _group_rms(acc, gain_ref, width, scale, put):
    lane = lax.broadcasted_iota(jnp.int32, (1, LANES), 1)
    low = lane < width
    for c in range(acc.shape[1] // LANES):
        cols = slice(c * LANES, (c + 1) * LANES)
        xc = acc[:, cols]
        sq = xc * xc
        if width == LANES:
            r = lax.rsqrt(jnp.mean(sq, axis=-1, keepdims=True) + EPS)
        else:
            s_lo = jnp.sum(jnp.where(low, sq, 0.0), axis=-1, keepdims=True)
            s_hi = jnp.sum(jnp.where(low, 0.0, sq), axis=-1, keepdims=True)
            r = jnp.where(low, lax.rsqrt(s_lo / width + EPS), lax.rsqrt(s_hi / width + EPS))
        y = xc * r * gain_ref[:, cols]
        if scale != 1.0:
            y = y * scale
        put(c, y)


def _put_cols(ref):
    def put(c, y):
        ref[:, c * LANES:(c + 1) * LANES] = y.astype(ref.dtype)
    return put


def _put_heads_and_cols(heads_ref, cols_ref):
    def put(c, y):
        heads_ref[:, c, :] = y
        cols_ref[:, c * LANES:(c + 1) * LANES] = y.astype(cols_ref.dtype)
    return put


def _proj_body(x_ref, g_ref, wt_ref, wg2_ref, bg_ref, qn_ref, kn_ref, mqn_ref,
               gq_ref, gk_ref, gv_ref, gr_ref, gg_ref, dq_ref, dk_ref, dv_ref, mq_ref, dkb_ref, dvb_ref, xn_ref):
    xn_ref[...] = _rms(x_ref[...], g_ref[...]).astype(BF16)

    def cols(start, width):
        return _dot_nt(xn_ref[...], wt_ref[start:start + width, :])

    gq_ref[...] = cols(_W_GQ, GLA_QK) * (GLA_DK ** -0.5)
    gk_ref[...] = cols(_W_GK, GLA_QK)
    gv_ref[...] = cols(_W_GV, GLA_V).astype(BF16)
    gr_ref[...] = cols(_W_GR, GLA_V)
    glr = cols(_W_GLR, LANES)
    z = _dot(glr.astype(BF16), wg2_ref[...]) + bg_ref[...]
    gg_ref[...] = jax.nn.log_sigmoid(z) / GLA_GATE_NORM
    _group_rms(cols(_W_DQ, DIFF_QK), qn_ref, DIFF_DH, DIFF_DH ** -0.5, _put_cols(dq_ref))
    _group_rms(cols(_W_DK, DIFF_QK), kn_ref, DIFF_DH, 1.0, _put_heads_and_cols(dk_ref, dkb_ref))
    dv = cols(_W_DV, DIFF_V)
    put_v = _put_heads_and_cols(dv_ref, dvb_ref)
    for h in range(HEADS):
        put_v(h, dv[:, h * LANES:(h + 1) * LANES])
    _group_rms(cols(_W_MQ, MEM_W), mqn_ref, MEM_DH, 1.0, _put_cols(mq_ref))


def _proj(x, gain, wt, wg2, bg, qn, kn, mqn):
    n = x.shape[0]
    tm = min(PROJ_TILE, n)
    const = lambda shape: pl.BlockSpec(shape, lambda i: (0, 0))
    out = lambda width: pl.BlockSpec((tm, width), lambda i: (i, 0))
    heads = pl.BlockSpec((tm, HEADS, LANES), lambda i: (i, 0, 0))
    shp = lambda width, dt: jax.ShapeDtypeStruct((n, width), dt)
    shp_heads = jax.ShapeDtypeStruct((n, HEADS, LANES), F32)
    return pl.pallas_call(
        _proj_body,
        grid=(n // tm,),
        in_specs=[
            pl.BlockSpec((tm, D_MODEL), lambda i: (i, 0)),
            const((1, D_MODEL)),
            pl.BlockSpec((IN_WIDTH, D_MODEL), lambda i: (0, 0), pipeline_mode=pl.Buffered(1)),
            const((LANES, GLA_QK)),
            const((1, GLA_QK)),
            const((1, DIFF_QK)),
            const((1, DIFF_QK)),
            const((1, MEM_W)),
        ],
        out_specs=[out(GLA_QK), out(GLA_QK), out(GLA_V), out(GLA_V), out(GLA_QK),
                   out(DIFF_QK), heads, heads, out(MEM_W), out(DIFF_QK), out(DIFF_V)],
        out_shape=[shp(GLA_QK, F32), shp(GLA_QK, F32), shp(GLA_V, BF16), shp(GLA_V, F32), shp(GLA_QK, F32),
                   shp(DIFF_QK, BF16), shp_heads, shp_heads, shp(MEM_W, BF16), shp(DIFF_QK, BF16), shp(DIFF_V, BF16)],
        scratch_shapes=[pltpu.VMEM((tm, D_MODEL), BF16)],
        compiler_params=_params(("parallel",)),
        name="proj",
    )(x, gain, wt, wg2, bg, qn, kn, mqn)


def _memkv_body(x_ref, g_ref, w_ref, kn_ref, k_ref, v_ref, kb_ref, vb_ref):
    xn = _rms(x_ref[...], g_ref[...]).astype(BF16)
    _group_rms(_dot(xn, w_ref[:, :MEM_W]), kn_ref, MEM_DH, 1.0, _put_heads_and_cols(k_ref, kb_ref))
    v = _dot(xn, w_ref[:, MEM_W:])
    put_v = _put_heads_and_cols(v_ref, vb_ref)
    for h in range(HEADS):
        put_v(h, v[:, h * LANES:(h + 1) * LANES])


def _memkv(mem, gain, w, kn):
    n = mem.shape[0]
    tm = min(TOK_TILE, n)
    const = lambda shape: pl.BlockSpec(shape, lambda i: (0, 0))
    heads = pl.BlockSpec((tm, HEADS, LANES), lambda i: (i, 0, 0))
    dense = pl.BlockSpec((tm, MEM_W), lambda i: (i, 0))
    return pl.pallas_call(
        _memkv_body,
        grid=(n // tm,),
        in_specs=[pl.BlockSpec((tm, D_MODEL), lambda i: (i, 0)), const((1, D_MODEL)),
                  const((D_MODEL, 2 * MEM_W)), const((1, MEM_W))],
        out_specs=[heads, heads, dense, dense],
        out_shape=[jax.ShapeDtypeStruct((n, HEADS, LANES), F32)] * 2 + [jax.ShapeDtypeStruct((n, MEM_W), BF16)] * 2,
        compiler_params=_params(("parallel",)),
        name="memkv",
    )(mem, gain, w, kn)


def _split3(x):
    hi = x.astype(BF16)
    r1 = x - hi.astype(F32)
    mid = r1.astype(BF16)
    lo = (r1 - mid.astype(F32)).astype(BF16)
    return hi, mid, lo


def _gla_body(*refs, chunk, n_chunks, n_steps, has_state):
    if has_state:
        gq_ref, gk_ref, gg_ref, gv_ref, gr_ref, on_ref, s0_ref, go_ref, st_ref, state = refs
    else:
        gq_ref, gk_ref, gg_ref, gv_ref, gr_ref, on_ref, go_ref, st_ref, state = refs
    t = pl.program_id(1)

    @pl.when(t == 0)
    def _():
        for h in range(GLA_HEADS):
            if has_state:
                state[h] = s0_ref[0, h].T
            else:
                state[h] = jnp.zeros((GLA_DV, GLA_DK), F32)

    row = lax.broadcasted_iota(jnp.int32, (chunk, chunk), 0)
    col = lax.broadcasted_iota(jnp.int32, (chunk, chunk), 1)
    causal = row >= col
    tril = causal.astype(BF16)

    chunks = [slice(c * chunk, (c + 1) * chunk) for c in range(n_chunks)]
    kcs = [slice(h * GLA_DK, (h + 1) * GLA_DK) for h in range(GLA_HEADS)]
    vcs = [slice(h * GLA_DV, (h + 1) * GLA_DV) for h in range(GLA_HEADS)]

    bs = []
    for rows in chunks:
        g_hi, g_mid, g_lo = _split3(gg_ref[rows, :])
        bs.append(_dot(tril, g_hi) + _dot(tril, g_mid) + _dot(tril, g_lo))

    qes, kes, kds, decays = [], [], [], []
    for rows, b in zip(chunks, bs):
        b_last = b[chunk - 1:chunk, :]
        q = gq_ref[rows, :]
        k = gk_ref[rows, :]
        qes.append((q * jnp.exp(b)).astype(BF16))
        kes.append((k * jnp.exp(-b)).astype(BF16))
        kds.append((k * jnp.exp(b_last - b)).astype(BF16))
        decays.append(jnp.exp(b_last))

    a_s = [[jnp.where(causal, _dot_nt(qe[:, kc], ke[:, kc]), 0.0).astype(BF16) for kc in kcs]
           for qe, ke in zip(qes, kes)]
    incs = [[_dot_tn(gv_ref[rows, vc], kd[:, kc]) for kc, vc in zip(kcs, vcs)] for rows, kd in zip(chunks, kds)]

    s_in = []
    s_cur = [state[h] for h in range(GLA_HEADS)]
    for c in range(n_chunks):
        s_in.append([s.astype(BF16) for s in s_cur])
        s_cur = [s * decays[c][:, kc] + inc for s, kc, inc in zip(s_cur, kcs, incs[c])]
    for h in range(GLA_HEADS):
        state[h] = s_cur[h]

    for c, rows in enumerate(chunks):
        for h, (kc, vc) in enumerate(zip(kcs, vcs)):
            o = _dot_nt(qes[c][:, kc], s_in[c][h]) + _dot(a_s[c][h], gv_ref[rows, vc])
            go_ref[rows, vc] = (_rms(o, on_ref[...]) * jax.nn.silu(gr_ref[rows, vc])).astype(BF16)

    @pl.when(t == n_steps - 1)
    def _():
        for h in range(GLA_HEADS):
            st_ref[0, h] = state[h].T


def _gla(gq, gk, gg, gv, gr, onorm, s0, batch, seq, chunk):
    tt = min(GLA_TILE, seq)
    n_steps = seq // tt
    tok = lambda width: pl.BlockSpec((tt, width), lambda b, t: (b * n_steps + t, 0))
    st_spec = pl.BlockSpec((1, GLA_HEADS, GLA_DK, GLA_DV), lambda b, t: (b, 0, 0, 0))
    in_specs = [tok(GLA_QK), tok(GLA_QK), tok(GLA_QK), tok(GLA_V), tok(GLA_V),
                pl.BlockSpec((1, GLA_DV), lambda b, t: (0, 0))]
    args = [gq, gk, gg, gv, gr, onorm]
    if s0 is not None:
        in_specs.append(st_spec)
        args.append(s0)
    return pl.pallas_call(
        functools.partial(_gla_body, chunk=chunk, n_chunks=tt // chunk, n_steps=n_steps, has_state=s0 is not None),
        grid=(batch, n_steps),
        in_specs=in_specs,
        out_specs=[tok(GLA_V), st_spec],
        out_shape=[jax.ShapeDtypeStruct((batch * seq, GLA_V), BF16),
                   jax.ShapeDtypeStruct((batch, GLA_HEADS, GLA_DK, GLA_DV), F32)],
        scratch_shapes=[pltpu.VMEM((GLA_HEADS, GLA_DV, GLA_DK), F32)],
        compiler_params=_params(("parallel", "arbitrary")),
        name="gla",
    )(*args)


def _t5_bucket(rel):
    nb = REL_BUCKETS // 2
    max_exact = nb // 2
    ret = jnp.where(rel > 0, nb, 0)
    n = jnp.abs(rel)
    nf = jnp.maximum(n, 1).astype(F32)
    large = max_exact + (jnp.log(nf / max_exact) / math.log(REL_MAX_DIST / max_exact)
                         * (nb - max_exact)).astype(jnp.int32)
    large = jnp.minimum(large, nb - 1)
    return ret + jnp.where(n < max_exact, n, large)


def _bucket_tile(q_pos, k_pos):
    visible = (k_pos[None, :] // CHUNK) <= (q_pos[:, None] // CHUNK)
    return jnp.where(visible, _t5_bucket(k_pos[None, :] - q_pos[:, None]), MASKED_BUCKET).astype(jnp.int32)


def _bias_from_buckets(idx, tab_ref, head):
    def step(bk, acc):
        return jnp.where(idx == bk, tab_ref[bk * DIFF_HEADS + head], acc)
    return lax.fori_loop(0, REL_BUCKETS, step, jnp.full(idx.shape, NEG_INF, F32))


def _lambda(lam_ref, lam_init):
    l = lam_ref[...]
    return (jnp.exp(jnp.sum(l[0:1] * l[1:2], axis=-1, keepdims=True))
            - jnp.exp(jnp.sum(l[2:3] * l[3:4], axis=-1, keepdims=True)) + lam_init)


def _comp_masks(q):
    lane = lax.broadcasted_iota(jnp.int32, q.shape, 1)
    zero = jnp.zeros_like(q)
    return jnp.where(lane < DIFF_DH, q, zero), jnp.where(lane < DIFF_DH, zero, q)


def _diff_finish(o0, o1, lam, on_ref, lam_init, out_dtype):
    o = o0 - lam * o1
    return (_rms(o, on_ref[...]) * (1.0 - lam_init)).astype(out_dtype)


def _scores(q_c, parts):
    return [_dot_nt(q_c, k) + bias if jnp.ndim(bias) == 2 else _dot_nt(q_c, k) for k, _, bias in parts]


def _softmax_pv(scores, parts):
    shifts = [0.0 if jnp.ndim(bias) == 2 else bias for _, _, bias in parts]
    m = None
    for s, shift in zip(scores, shifts):
        part_max = s.max(axis=-1, keepdims=True) + shift
        m = part_max if m is None else jnp.maximum(m, part_max)
    ol = 0.0
    for s, shift, (_, v1, _) in zip(scores, shifts, parts):
        ol = ol + _dot(jnp.exp(s - (m - shift)).astype(BF16), v1)
    return ol[:, :DIFF_DV] * (1.0 / ol[:, DIFF_DV:])


def _diff_prompt_body(tab_ref, q_ref, kb, vb, idx_ref, lam_ref, on_ref, o_ref, bias, v1, *, seq, lam_init):
    b = pl.program_id(0)
    h = pl.program_id(1)

    @pl.when(b == 0)
    def _():
        for t in range(2):
            bias[h, t] = _bias_from_buckets(idx_ref[t], tab_ref, h)

    v1[:, :DIFF_DV] = vb[...]
    v1[:, DIFF_DV:] = jnp.ones((seq, LANES), BF16)
    lam = _lambda(lam_ref, lam_init)
    far_bias = tab_ref[(REL_BUCKETS // 2 - 1) * DIFF_HEADS + h]

    def key_parts(i):
        near0 = max(i - 1, 0) * Q_BLOCK
        near = slice(near0, near0 + 2 * Q_BLOCK)
        parts = [(kb[near, :], v1[near, :], bias[h, min(i, 1)])]
        if near0 > 0:
            parts.append((kb[0:near0, :], v1[0:near0, :], far_bias))
        return parts

    items = [(i, c) for i in range(seq // Q_BLOCK) for c in range(2)]
    comps = {}

    def scores_of(item):
        i, c = item
        if i not in comps:
            comps[i] = _comp_masks(q_ref[i * Q_BLOCK:(i + 1) * Q_BLOCK, :])
        return _scores(comps[i][c], key_parts(i))

    outs = {}
    ahead = scores_of(items[0])
    for n, (i, c) in enumerate(items):
        cur = ahead
        if n + 1 < len(items):
            ahead = scores_of(items[n + 1])
        outs[c] = _softmax_pv(cur, key_parts(i))
        if c == 1:
            o_ref[i * Q_BLOCK:(i + 1) * Q_BLOCK, :] = _diff_finish(outs[0], outs[1], lam, on_ref, lam_init,
                                                                    o_ref.dtype)


def _diff_prompt(table, dq, dk, dv, lam_p, onorm, batch, seq, lam_init):
    q_pos = jnp.arange(Q_BLOCK)
    idx = jnp.stack([_bucket_tile(q_pos, jnp.arange(2 * Q_BLOCK)),
                     _bucket_tile(q_pos + Q_BLOCK, jnp.arange(2 * Q_BLOCK))])
    head = lambda: pl.BlockSpec((seq, LANES), lambda b, h: (b, h))
    return pl.pallas_call(
        functools.partial(_diff_prompt_body, seq=seq, lam_init=lam_init),
        grid=(batch, DIFF_HEADS),
        in_specs=[
            pl.BlockSpec(memory_space=pltpu.SMEM),
            head(), head(), head(),
            pl.BlockSpec((2, Q_BLOCK, 2 * Q_BLOCK), lambda b, h: (0, 0, 0)),
            pl.BlockSpec((4, DIFF_DH), lambda b, h: (0, 0)),
            pl.BlockSpec((1, DIFF_DV), lambda b, h: (0, 0)),
        ],
        out_specs=head(),
        out_shape=jax.ShapeDtypeStruct((batch * seq, DIFF_V), BF16),
        scratch_shapes=[pltpu.VMEM((DIFF_HEADS, 2, Q_BLOCK, 2 * Q_BLOCK), F32),
                        pltpu.VMEM((seq, DIFF_DV + LANES), BF16)],
        compiler_params=_params(("arbitrary", "arbitrary")),
        name="diff_prompt",
    )(table, dq, dk, dv, idx, lam_p, onorm)


def _diff_sample_body(tab_ref, q_ref, k_ref, v_ref, ck_ref, cv_ref, idxc_ref, idxn_ref, lam_ref, on_ref, o_ref,
                      bias_c, bias_n, *, seq, lam_init):
    groups = [(h, c) for h in range(DIFF_HEADS) for c in range(2)]

    @pl.when(pl.program_id(0) == 0)
    def _():
        for h in range(DIFF_HEADS):
            tile = _bias_from_buckets(idxc_ref[h], tab_ref, h)
            for g in (2 * h, 2 * h + 1):
                bias_c[g * seq:(g + 1) * seq, :] = tile
            bias_n[h] = _bias_from_buckets(idxn_ref[...], tab_ref, h)

    lam = _lambda(lam_ref, lam_init)
    head_cols = [slice(h * LANES, (h + 1) * LANES) for h in range(DIFF_HEADS)]
    q_groups = [q_c for cols in head_cols for q_c in _comp_masks(q_ref[:, cols])]
    s_old = _dot_nt(jnp.concatenate(q_groups, axis=0), ck_ref[...].astype(BF16)) + bias_c[...]
    s_new = jnp.concatenate([_dot_nt(q_g, k_ref[:, head_cols[h]]) + bias_n[h]
                             for q_g, (h, _) in zip(q_groups, groups)], axis=0)
    m = jnp.maximum(s_old.max(axis=-1, keepdims=True), s_new.max(axis=-1, keepdims=True))
    e_old = jnp.exp(s_old - m)
    e_new = jnp.exp(s_new - m)
    l = e_old.sum(axis=-1, keepdims=True) + e_new.sum(axis=-1, keepdims=True)
    o_old = _dot(e_old.astype(BF16), cv_ref[...].astype(BF16))
    e_new = e_new.astype(BF16)
    o_new = jnp.concatenate([_dot(e_new[g * seq:(g + 1) * seq, :], v_ref[:, head_cols[h]])
                             for g, (h, _) in enumerate(groups)], axis=0)
    o = (o_old + o_new) * (1.0 / l)
    for h, cols in enumerate(head_cols):
        o0 = o[(2 * h) * seq:(2 * h + 1) * seq, :]
        o1 = o[(2 * h + 1) * seq:(2 * h + 2) * seq, :]
        o_ref[:, cols] = _diff_finish(o0, o1, lam, on_ref, lam_init, o_ref.dtype)


def _diff_sample(table, dq, dk, dv, cache_k, cache_v, lam_p, onorm, batch, seq, past, lam_init):
    q_pos = past + jnp.arange(seq)
    idx_n = _bucket_tile(q_pos, past + jnp.arange(seq))
    idx_rows = jnp.repeat(_bucket_tile(q_pos, jnp.arange(past)), HEADS, axis=1)
    row_head = jnp.arange(past * HEADS) % HEADS
    idx_c = jnp.where(row_head[None, None, :] == jnp.arange(HEADS)[:, None, None], idx_rows[None], MASKED_BUCKET)
    new_q = pl.BlockSpec((seq, DIFF_QK), lambda b: (b, 0))
    old = lambda: pl.BlockSpec((past * HEADS, LANES), lambda b: (b, 0))
    const = lambda shape: pl.BlockSpec(shape, lambda b: (0,) * len(shape))
    n_groups = 2 * DIFF_HEADS
    return pl.pallas_call(
        functools.partial(_diff_sample_body, seq=seq, lam_init=lam_init),
        grid=(batch,),
        in_specs=[pl.BlockSpec(memory_space=pltpu.SMEM), new_q, new_q, new_q, old(), old(),
                  const((HEADS, seq, past * HEADS)), const((seq, seq)), const((4, DIFF_DH)), const((1, DIFF_DV))],
        out_specs=new_q,
        out_shape=jax.ShapeDtypeStruct((batch * seq, DIFF_V), BF16),
        scratch_shapes=[pltpu.VMEM((n_groups * seq, past * HEADS), F32), pltpu.VMEM((DIFF_HEADS, seq, seq), F32)],
        compiler_params=_params(("arbitrary",)),
        name="diff_sample",
    )(table, dq, dk, dv, cache_k, cache_v, idx_c, idx_n, lam_p, onorm)


def _memattn_body(q_ref, k_ref, v_ref, o_ref, *, per_head):
    head_cols = [slice(h * MEM_DH, (h + 1) * MEM_DH) for h in range(MEM_HEADS)]
    if per_head:
        ks = [k_ref[:, h, :].astype(BF16) for h in range(MEM_HEADS)]
        vs = [v_ref[:, h, :].astype(BF16) for h in range(MEM_HEADS)]
    else:
        ks = [k_ref[:, cols] for cols in head_cols]
        vs = [v_ref[:, cols] for cols in head_cols]
    scores = [_dot_nt(q_ref[:, cols], k) * (MEM_DH ** -0.5) for cols, k in zip(head_cols, ks)]
    probs = []
    for s in scores:
        e = jnp.exp(s - s.max(axis=-1, keepdims=True))
        probs.append((e * (1.0 / e.sum(axis=-1, keepdims=True))).astype(BF16))
    for cols, p, v in zip(head_cols, probs, vs):
        o_ref[:, cols] = _dot(p, v).astype(o_ref.dtype)


def _memattn(mq, mk, mv, batch, seq, n_mem):
    tq = min(MEMATTN_TILE, seq)
    nq = seq // tq
    q_spec = pl.BlockSpec((tq, MEM_W), lambda b, i: (b * nq + i, 0))
    per_head = mk.ndim == 3
    if per_head:
        kv_spec = pl.BlockSpec((n_mem, HEADS, LANES), lambda b, i: (b, 0, 0))
    else:
        kv_spec = pl.BlockSpec((n_mem, MEM_W), lambda b, i: (b, 0))
    return pl.pallas_call(
        functools.partial(_memattn_body, per_head=per_head),
        grid=(batch, nq),
        in_specs=[q_spec, kv_spec, kv_spec],
        out_specs=q_spec,
        out_shape=jax.ShapeDtypeStruct((batch * seq, MEM_W), BF16),
        compiler_params=_params(("parallel", "arbitrary")),
        name="memattn",
    )(mq, mk, mv)


def _outproj_body(x_ref, g_ref, d_ref, m_ref, wg_ref, wd_ref, wm_ref, ng_ref, o_ref, on_ref):
    o = (x_ref[...] + _dot(g_ref[...], wg_ref[...]) + _dot(d_ref[...], wd_ref[...])
         + _dot(m_ref[...], wm_ref[...]))
    o_ref[...] = o
    on_ref[...] = _rms(o, ng_ref[...]).astype(BF16)


def _outproj(x, g, d, m, wo, next_gain):
    n = x.shape[0]
    tm = min(TOK_TILE, n)
    tok = lambda width: pl.BlockSpec((tm, width), lambda i: (i, 0))
    w_rows = lambda rows, blk: pl.BlockSpec((rows, D_MODEL), lambda i: (blk, 0))
    return pl.pallas_call(
        _outproj_body,
        grid=(n // tm,),
        in_specs=[tok(D_MODEL), tok(GLA_V), tok(DIFF_V), tok(MEM_W),
                  w_rows(GLA_V, 0), w_rows(DIFF_V, GLA_V // DIFF_V), w_rows(MEM_W, (GLA_V + DIFF_V) // MEM_W),
                  pl.BlockSpec((1, D_MODEL), lambda i: (0, 0))],
        out_specs=[tok(D_MODEL), tok(D_MODEL)],
        out_shape=[jax.ShapeDtypeStruct((n, D_MODEL), F32), jax.ShapeDtypeStruct((n, D_MODEL), BF16)],
        compiler_params=_params(("parallel",)),
        name="outproj",
    )(x, g, d, m, wo, wo, wo, next_gain)


def _row(v):
    return v.reshape(1, -1).astype(F32)


def kernel(x_prompt, x_sample, mem_prompt, cache_diff_k, cache_diff_v, state_gla, cache_mem_k, cache_mem_v, rel_bias_table, norm_ffn1, w_ffn1_in, w_ffn1_out, norm_mix, w_in, w_gla_g2, b_gla_g, gla_out_norm, diff_q_norm, diff_k_norm, diff_lambda, diff_out_norm, mem_norm, w_mem_kv, mem_q_norm, mem_k_norm, w_o, norm_ffn2, w_ffn2_in, w_ffn2_out, norm_final):
    depth = norm_ffn1.shape[0]
    assert depth == 1, "single-layer step"
    layer = 0
    batch, seq, _ = x_prompt.shape
    dec_batch, dec_seq, _ = x_sample.shape
    past = cache_diff_k.shape[2]
    n_mem = mem_prompt.shape[1]
    lam_init = 0.8 - 0.6 * math.exp(-0.3 * layer)

    w_g2 =jnp.pad(w_gla_g2[layer].astype(BF16), ((0, LANES - GLA_GATE_RANK), (0, 0)))
    b_g = _row(b_gla_g[layer])
    qn = _row(jnp.tile(diff_q_norm[layer], DIFF_QK // DIFF_DH))
    kn = _row(jnp.tile(diff_k_norm[layer], DIFF_QK // DIFF_DH))
    mqn = _row(jnp.tile(mem_q_norm[layer], MEM_HEADS))
    mkn = _row(jnp.tile(mem_k_norm[layer], MEM_HEADS))
    table = rel_bias_table.astype(F32).reshape(-1)
    lam_p = diff_lambda[layer].astype(F32)
    gla_on = _row(gla_out_norm[layer])
    diff_on = _row(diff_out_norm[layer])
    per_head = lambda a: a.reshape(-1, HEADS, LANES)

    mk, mv, mk_b, mv_b = _memkv(mem_prompt.reshape(batch * n_mem, D_MODEL), _row(mem_norm[layer]),
                                w_mem_kv[layer].astype(BF16), mkn)

    n1 = _row(norm_ffn1[layer])
    xs1, *ffn1 = _ffn(x_sample.reshape(dec_batch * dec_seq, D_MODEL), n1, w_ffn1_in[layer], w_ffn1_in[layer],
                      w_ffn1_out[layer], up_col0=D_FF, ff_tile=FF_TILE_F32, emit_bf16=True)
    n_in_blocks = w_ffn2_in.shape[2] // LANES
    xp1, w2_in, w2_out, w_t, wo = _ffn(
        x_prompt.reshape(batch * seq, D_MODEL), n1, *ffn1, up_col0=0,
        cast_later=[(w_ffn2_in[layer], (D_MODEL, LANES), 0),
                    (w_ffn2_out[layer], (D_FF // n_in_blocks, D_MODEL), n_in_blocks),
                    (jnp.swapaxes(w_in[layer], 0, 1), (CAST_ROWS_W_IN, D_MODEL), 0),
                    (w_o[layer], (CAST_ROWS_W_O, D_MODEL), 0)])
    ffn2 = (w2_in, w2_in, w2_out)

    def layer_fn(x, b, t, chunk, s0, diff_fn, mem_k, mem_v):
        gq, gk, gv, gr, gg, dq, dk, dv, mq, dk_b, dv_b = _proj(x, _row(norm_mix[layer]), w_t, w_g2, b_g,
                                                               qn, kn, mqn)
        g_out, g_state = _gla(gq, gk, gg, gv, gr, gla_on, s0, b, t, chunk)
        d_out = diff_fn(dq, dk_b, dv_b)
        m_out = _memattn(mq, mem_k, mem_v, b, t, n_mem)
        x, xn = _outproj(x, g_out, d_out, m_out, wo, _row(norm_ffn2[layer]))
        x = _ffn(x, xn, *ffn2, up_col0=D_FF, final_gain=_row(norm_final[layer]))
        return x, dk, dv, g_state

    yp, dk_p, dv_p, g_p = layer_fn(
        xp1, batch, seq, CHUNK, None,
        lambda dq, dk, dv: _diff_prompt(table, dq, dk, dv, lam_p, diff_on, batch, seq, lam_init), mk_b, mv_b)
    ys, dk_s, dv_s, g_s = layer_fn(
        xs1, dec_batch, dec_seq, dec_seq, state_gla[layer],
        lambda dq, dk, dv: _diff_sample(table, dq, dk, dv, cache_diff_k[layer].reshape(-1, LANES),
                                        cache_diff_v[layer].reshape(-1, LANES),
                                        lam_p, diff_on, dec_batch, dec_seq, past, lam_init),
        per_head(cache_mem_k[layer]), per_head(cache_mem_v[layer]))

    head4 = lambda a, b, t: a.reshape(1, b, t, HEADS, LANES)
    return (yp.reshape(batch, seq, D_MODEL), ys.reshape(dec_batch, dec_seq, D_MODEL),
            head4(dk_p, batch, seq), head4(dv_p, batch, seq), g_p[None],
            head4(mk, batch, n_mem), head4(mv, batch, n_mem),
            head4(dk_s, dec_batch, dec_seq), head4(dv_s, dec_batch, dec_seq), g_s[None])
```

```python
import functools
import math

import jax
import jax.numpy as jnp
from jax import lax
from jax.experimental import pallas as pl
from jax.experimental.pallas import tpu as pltpu

F32 = jnp.float32
BF16 = jnp.bfloat16

D_MODEL = 2048
CHUNK = 64
EPS = 1e-6
NEG_INF = -1e30
GLA_HEADS, GLA_DK, GLA_DV, GLA_GATE_RANK, GLA_GATE_NORM = 4, 128, 256, 16, 16.0
DIFF_HEADS, DIFF_DH, DIFF_DV = 4, 64, 128
MEM_HEADS, MEM_DH = 4, 128
REL_BUCKETS, REL_MAX_DIST = 32, 128
D_FF = 5504
GLA_QK = GLA_HEADS * GLA_DK
GLA_V = GLA_HEADS * GLA_DV
DIFF_QK = DIFF_HEADS * 2 * DIFF_DH
DIFF_V = DIFF_HEADS * DIFF_DV
MEM_W = MEM_HEADS * MEM_DH
GLR_OFF = 2 * GLA_QK + 2 * GLA_V

LANES = 128
HEADS = 4
FF_TILE = 512
FF_TILE_F32 = 256
FFN_TOK_TILE = 1024
TOK_TILE = 512
GLA_TILE = 1024
MEMATTN_TILE = 2048
PROJ_TILE = 256
Q_BLOCK = 256
MASKED_BUCKET = REL_BUCKETS
MIB = 1024 * 1024
VMEM_CAP_MIB = 60
CAST_ROWS_W_IN = 48
CAST_ROWS_W_O = 16

_W_GQ, _W_GK, _W_GV, _W_GR, _W_GLR = 0, GLA_QK, 2 * GLA_QK, 2 * GLA_QK + GLA_V, GLR_OFF
_W_DQ = GLR_OFF + GLA_GATE_RANK
_W_DK = _W_DQ + DIFF_QK
_W_DV = _W_DK + DIFF_QK
_W_MQ = _W_DV + DIFF_V
IN_WIDTH = _W_MQ + MEM_W


def _dot(a, b):
    return jnp.dot(a, b, preferred_element_type=F32)


def _dot_nt(a, b):
    return lax.dot_general(a, b, (((1,), (1,)), ((), ())), preferred_element_type=F32)


def _dot_tn(a, b):
    return lax.dot_general(a, b, (((0,), (0,)), ((), ())), preferred_element_type=F32)


def _rms(x, gain):
    return x * lax.rsqrt(jnp.mean(x * x, axis=-1, keepdims=True) + EPS) * gain


def _params(sem, vmem_mib=48):
    return pltpu.CompilerParams(dimension_semantics=sem, vmem_limit_bytes=min(vmem_mib, VMEM_CAP_MIB) * MIB)


def _ffn_body(*refs, n_ff, overlap, final_norm, n_cast, emit_bf16):
    x_ref, g_ref, wg_ref, wu_ref, wo_ref = refs[:5]
    refs = refs[5:]
    if final_norm:
        fg_ref, refs = refs[0], refs[1:]
    cast_src, refs = refs[:n_cast], refs[n_cast:]
    o_ref, refs = refs[0], refs[1:]
    cast_dst, refs = refs[:n_cast], refs[n_cast:]
    if emit_bf16:
        (wg_out, wu_out, wo_out), refs = refs[:3], refs[3:]
    (xn_ref,) = refs
    j = pl.program_id(1)

    def tile(lo):
        for src, dst in zip(cast_src, cast_dst):
            dst[...] = src[...].astype(BF16)
        xn = xn_ref[...]
        if emit_bf16:
            def front(a, axis):
                a = a.astype(BF16)
                if lo == 0:
                    return a
                return jnp.concatenate([lax.slice_in_dim(a, lo, None, axis=axis),
                                        lax.slice_in_dim(a, 0, lo, axis=axis)], axis=axis)
            wg_out[...] = front(wg_ref[...], 1)
            wu_out[...] = front(wu_ref[...], 1)
            wo_out[...] = front(wo_ref[...], 0)
            live = wg_out.shape[1] - lo
            gate = _dot(xn, wg_out[:, :live])
            up = _dot(xn, wu_out[:, :live])
            act = (jax.nn.silu(gate) * up).astype(BF16)
            return 0.5 * _dot(act, wo_out[:live, :])
        gate = _dot(xn, wg_ref[:, lo:])
        up = _dot(xn, wu_ref[:, lo:])
        act = (jax.nn.silu(gate) * up).astype(BF16)
        return 0.5 * _dot(act, wo_ref[lo:, :])

    @pl.when(j == 0)
    def _():
        xn_ref[...] = _rms(x_ref[...], g_ref[...]).astype(BF16)
        o_ref[...] = x_ref[...] + tile(0)

    @pl.when((j > 0) & (j < n_ff - 1))
    def _():
        o_ref[...] += tile(0)

    @pl.when(j == n_ff - 1)
    def _():
        o = o_ref[...] + tile(overlap)
        o_ref[...] = _rms(o, fg_ref[...]) if final_norm else o


def _ffn(x, gain, w_gate, w_up, w_out, *, up_col0, final_gain=None, cast_later=(), ff_tile=FF_TILE,
         emit_bf16=False):
    n = x.shape[0]
    tm = min(FFN_TOK_TILE, n)
    n_ff = -(-D_FF // ff_tile)
    row = pl.BlockSpec((1, D_MODEL), lambda i, j: (0, 0))
    start = lambda j: jnp.minimum(j * (ff_tile // LANES), (D_FF - ff_tile) // LANES)
    w_in_tile = (pl.Element(D_MODEL), pl.Element(ff_tile))
    w_out_tile = (pl.Element(ff_tile), pl.Element(D_MODEL))
    gate_index = lambda i, j: (0, start(j) * LANES)
    out_index = lambda i, j: (start(j) * LANES, 0)
    tok = pl.BlockSpec((tm, D_MODEL), lambda i, j: (i, 0))
    in_specs = [
        tok,
        row,
        pl.BlockSpec(w_in_tile, gate_index),
        pl.BlockSpec(w_in_tile, lambda i, j: (0, (up_col0 // LANES + start(j)) * LANES)),
        pl.BlockSpec(w_out_tile, out_index),
    ]
    args = [x, gain, w_gate, w_up, w_out]
    if final_gain is not None:
        in_specs.append(row)
        args.append(final_gain)
    w_tile_bytes = D_MODEL * ff_tile * ((4 + 2) if emit_bf16 else 2)
    vmem_bytes = 4 * tm * D_MODEL * 4 + tm * D_MODEL * 2 + 6 * w_tile_bytes + 3 * tm * ff_tile * 4
    out_specs = [pl.BlockSpec((tm, D_MODEL), lambda i, j: (i, 0))]
    out_shape = [jax.ShapeDtypeStruct((n, D_MODEL), F32)]
    n_steps = (n // tm) * n_ff
    for mat, blk, first in cast_later:
        rows, cols = mat.shape
        assert rows % blk[0] == 0 and cols % blk[1] == 0 and (blk[0] == rows or blk[1] == cols)
        n_blk = (rows // blk[0]) * (cols // blk[1])
        assert first + n_blk <= n_steps
        by_rows = blk[1] == cols

        def index(i, j, first=first, n_blk=n_blk, by_rows=by_rows):
            k = jnp.clip(i * n_ff + j - first, 0, n_blk - 1)
            return (k, 0) if by_rows else (0, k)

        in_specs.append(pl.BlockSpec(blk, index))
        args.append(mat)
        out_specs.append(pl.BlockSpec(blk, index))
        out_shape.append(jax.ShapeDtypeStruct(mat.shape, BF16))
        vmem_bytes += 2 * blk[0] * blk[1] * (4 + 2)
    if emit_bf16:
        d_ff_pad = n_ff * ff_tile
        cols_j = pl.BlockSpec((D_MODEL, ff_tile), lambda i, j: (0, j))
        out_specs += [cols_j, cols_j, pl.BlockSpec((ff_tile, D_MODEL), lambda i, j: (j, 0))]
        out_shape += ([jax.ShapeDtypeStruct((D_MODEL, d_ff_pad), BF16)] * 2
                      + [jax.ShapeDtypeStruct((d_ff_pad, D_MODEL), BF16)])
    outs = pl.pallas_call(
        functools.partial(_ffn_body, n_ff=n_ff, overlap=n_ff * ff_tile - D_FF, final_norm=final_gain is not None,
                          n_cast=len(cast_later), emit_bf16=emit_bf16),
        grid=(n // tm, n_ff),
        in_specs=in_specs,
        out_specs=out_specs,
        out_shape=out_shape,
        scratch_shapes=[pltpu.VMEM((tm, D_MODEL), BF16)],
        compiler_params=_params(("arbitrary", "arbitrary"), vmem_bytes // MIB + 4),
        name="ffn",
    )(*args)
    return outs[0] if len(outs) == 1 else outs


def _group_rms(acc, gain_ref, width, scale, put):
    lane = lax.broadcasted_iota(jnp.int32, (1, LANES), 1)
    low = lane < width
    for c in range(acc.shape[1] // LANES):
        cols = slice(c * LANES, (c + 1) * LANES)
        xc = acc[:, cols]
        sq = xc * xc
        if width == LANES:
            r = lax.rsqrt(jnp.mean(sq, axis=-1, keepdims=True) + EPS)
        else:
            s_lo = jnp.sum(jnp.where(low, sq, 0.0), axis=-1, keepdims=True)
            s_hi = jnp.sum(jnp.where(low, 0.0, sq), axis=-1, keepdims=True)
            r = jnp.where(low, lax.rsqrt(s_lo / width + EPS), lax.rsqrt(s_hi / width + EPS))
        y = xc * r * gain_ref[:, cols]
        if scale != 1.0:
            y = y * scale
        put(c, y)


def _put_cols(ref):
    def put(c, y):
        ref[:, c * LANES:(c + 1) * LANES] = y.astype(ref.dtype)
    return put


def _put_heads_and_cols(heads_ref, cols_ref):
    def put(c, y):
        heads_ref[:, c, :] = y
        cols_ref[:, c * LANES:(c + 1) * LANES] = y.astype(cols_ref.dtype)
    return put


def _proj_body(x_ref, g_ref, wt_ref, wg2_ref, bg_ref, qn_ref, kn_ref, mqn_ref,
               gq_ref, gk_ref, gv_ref, gr_ref, gg_ref, dq_ref, dk_ref, dv_ref, mq_ref, dkb_ref, dvb_ref, xn_ref):
    xn_ref[...] = _rms(x_ref[...], g_ref[...]).astype(BF16)

    def cols(start, width):
        return _dot_nt(xn_ref[...], wt_ref[start:start + width, :])

    gq_ref[...] = cols(_W_GQ, GLA_QK) * (GLA_DK ** -0.5)
    gk_ref[...] = cols(_W_GK, GLA_QK)
    gv_ref[...] = cols(_W_GV, GLA_V).astype(BF16)
    gr_ref[...] = cols(_W_GR, GLA_V)
    glr = cols(_W_GLR, LANES)
    z = _dot(glr.astype(BF16), wg2_ref[...]) + bg_ref[...]
    gg_ref[...] = jax.nn.log_sigmoid(z) / GLA_GATE_NORM
    _group_rms(cols(_W_DQ, DIFF_QK), qn_ref, DIFF_DH, DIFF_DH ** -0.5, _put_cols(dq_ref))
    _group_rms(cols(_W_DK, DIFF_QK), kn_ref, DIFF_DH, 1.0, _put_heads_and_cols(dk_ref, dkb_ref))
    dv = cols(_W_DV, DIFF_V)
    put_v = _put_heads_and_cols(dv_ref, dvb_ref)
    for h in range(HEADS):
        put_v(h, dv[:, h * LANES:(h + 1) * LANES])
    _group_rms(cols(_W_MQ, MEM_W), mqn_ref, MEM_DH, 1.0, _put_cols(mq_ref))


def _proj(x, gain, wt, wg2, bg, qn, kn, mqn):
    n = x.shape[0]
    tm = min(PROJ_TILE, n)
    const = lambda shape: pl.BlockSpec(shape, lambda i: (0, 0))
    out = lambda width: pl.BlockSpec((tm, width), lambda i: (i, 0))
    heads = pl.BlockSpec((tm, HEADS, LANES), lambda i: (i, 0, 0))
    shp = lambda width, dt: jax.ShapeDtypeStruct((n, width), dt)
    shp_heads = jax.ShapeDtypeStruct((n, HEADS, LANES), F32)
    return pl.pallas_call(
        _proj_body,
        grid=(n // tm,),
        in_specs=[
            pl.BlockSpec((tm, D_MODEL), lambda i: (i, 0)),
            const((1, D_MODEL)),
            pl.BlockSpec((IN_WIDTH, D_MODEL), lambda i: (0, 0), pipeline_mode=pl.Buffered(1)),
            const((LANES, GLA_QK)),
            const((1, GLA_QK)),
            const((1, DIFF_QK)),
            const((1, DIFF_QK)),
            const((1, MEM_W)),
        ],
        out_specs=[out(GLA_QK), out(GLA_QK), out(GLA_V), out(GLA_V), out(GLA_QK),
                   out(DIFF_QK), heads, heads, out(MEM_W), out(DIFF_QK), out(DIFF_V)],
        out_shape=[shp(GLA_QK, F32), shp(GLA_QK, F32), shp(GLA_V, BF16), shp(GLA_V, F32), shp(GLA_QK, F32),
                   shp(DIFF_QK, BF16), shp_heads, shp_heads, shp(MEM_W, BF16), shp(DIFF_QK, BF16), shp(DIFF_V, BF16)],
        scratch_shapes=[pltpu.VMEM((tm, D_MODEL), BF16)],
        compiler_params=_params(("parallel",)),
        name="proj",
    )(x, gain, wt, wg2, bg, qn, kn, mqn)


def _memkv_body(x_ref, g_ref, w_ref, kn_ref, k_ref, v_ref, kb_ref, vb_ref):
    xn = _rms(x_ref[...], g_ref[...]).astype(BF16)
    _group_rms(_dot(xn, w_ref[:, :MEM_W]), kn_ref, MEM_DH, 1.0, _put_heads_and_cols(k_ref, kb_ref))
    v = _dot(xn, w_ref[:, MEM_W:])
    put_v = _put_heads_and_cols(v_ref, vb_ref)
    for h in range(HEADS):
        put_v(h, v[:, h * LANES:(h + 1) * LANES])


def _memkv(mem, gain, w, kn):
    n = mem.shape[0]
    tm = min(TOK_TILE, n)
    const = lambda shape: pl.BlockSpec(shape, lambda i: (0, 0))
    heads = pl.BlockSpec((tm, HEADS, LANES), lambda i: (i, 0, 0))
    dense = pl.BlockSpec((tm, MEM_W), lambda i: (i, 0))
    return pl.pallas_call(
        _memkv_body,
        grid=(n // tm,),
        in_specs=[pl.BlockSpec((tm, D_MODEL), lambda i: (i, 0)), const((1, D_MODEL)),
                  const((D_MODEL, 2 * MEM_W)), const((1, MEM_W))],
        out_specs=[heads, heads, dense, dense],
        out_shape=[jax.ShapeDtypeStruct((n, HEADS, LANES), F32)] * 2 + [jax.ShapeDtypeStruct((n, MEM_W), BF16)] * 2,
        compiler_params=_params(("parallel",)),
        name="memkv",
    )(mem, gain, w, kn)


def _split3(x):
    hi = x.astype(BF16)
    r1 = x - hi.astype(F32)
    mid = r1.astype(BF16)
    lo = (r1 - mid.astype(F32)).astype(BF16)
    return hi, mid, lo


def _gla_body(*refs, chunk, n_chunks, n_steps, has_state):
    if has_state:
        gq_ref, gk_ref, gg_ref, gv_ref, gr_ref, on_ref, s0_ref, go_ref, st_ref, state = refs
    else:
        gq_ref, gk_ref, gg_ref, gv_ref, gr_ref, on_ref, go_ref, st_ref, state = refs
    t = pl.program_id(1)

    @pl.when(t == 0)
    def _():
        for h in range(GLA_HEADS):
            if has_state:
                state[h] = s0_ref[0, h].T
            else:
                state[h] = jnp.zeros((GLA_DV, GLA_DK), F32)

    row = lax.broadcasted_iota(jnp.int32, (chunk, chunk), 0)
    col = lax.broadcasted_iota(jnp.int32, (chunk, chunk), 1)
    causal = row >= col
    tril = causal.astype(BF16)

    chunks = [slice(c * chunk, (c + 1) * chunk) for c in range(n_chunks)]
    kcs = [slice(h * GLA_DK, (h + 1) * GLA_DK) for h in range(GLA_HEADS)]
    vcs = [slice(h * GLA_DV, (h + 1) * GLA_DV) for h in range(GLA_HEADS)]

    bs = []
    for rows in chunks:
        g_hi, g_mid, g_lo = _split3(gg_ref[rows, :])
        bs.append(_dot(tril, g_hi) + _dot(tril, g_mid) + _dot(tril, g_lo))

    qes, kes, kds, decays = [], [], [], []
    for rows, b in zip(chunks, bs):
        b_last = b[chunk - 1:chunk, :]
        q = gq_ref[rows, :]
        k = gk_ref[rows, :]
        qes.append((q * jnp.exp(b)).astype(BF16))
        kes.append((k * jnp.exp(-b)).astype(BF16))
        kds.append((k * jnp.exp(b_last - b)).astype(BF16))
        decays.append(jnp.exp(b_last))

    a_s = [[jnp.where(causal, _dot_nt(qe[:, kc], ke[:, kc]), 0.0).astype(BF16) for kc in kcs]
           for qe, ke in zip(qes, kes)]
    incs = [[_dot_tn(gv_ref[rows, vc], kd[:, kc]) for kc, vc in zip(kcs, vcs)] for rows, kd in zip(chunks, kds)]

    s_in = []
    s_cur = [state[h] for h in range(GLA_HEADS)]
    for c in range(n_chunks):
        s_in.append([s.astype(BF16) for s in s_cur])
        s_cur = [s * decays[c][:, kc] + inc for s, kc, inc in zip(s_cur, kcs, incs[c])]
    for h in range(GLA_HEADS):
        state[h] = s_cur[h]

    for c, rows in enumerate(chunks):
        for h, (kc, vc) in enumerate(zip(kcs, vcs)):
            o = _dot_nt(qes[c][:, kc], s_in[c][h]) + _dot(a_s[c][h], gv_ref[rows, vc])
            go_ref[rows, vc] = (_rms(o, on_ref[...]) * jax.nn.silu(gr_ref[rows, vc])).astype(BF16)

    @pl.when(t == n_steps - 1)
    def _():
        for h in range(GLA_HEADS):
            st_ref[0, h] = state[h].T


def _gla(gq, gk, gg, gv, gr, onorm, s0, batch, seq, chunk):
    tt = min(GLA_TILE, seq)
    n_steps = seq // tt
    tok = lambda width: pl.BlockSpec((tt, width), lambda b, t: (b * n_steps + t, 0))
    st_spec = pl.BlockSpec((1, GLA_HEADS, GLA_DK, GLA_DV), lambda b, t: (b, 0, 0, 0))
    in_specs = [tok(GLA_QK), tok(GLA_QK), tok(GLA_QK), tok(GLA_V), tok(GLA_V),
                pl.BlockSpec((1, GLA_DV), lambda b, t: (0, 0))]
    args = [gq, gk, gg, gv, gr, onorm]
    if s0 is not None:
        in_specs.append(st_spec)
        args.append(s0)
    return pl.pallas_call(
        functools.partial(_gla_body, chunk=chunk, n_chunks=tt // chunk, n_steps=n_steps, has_state=s0 is not None),
        grid=(batch, n_steps),
        in_specs=in_specs,
        out_specs=[tok(GLA_V), st_spec],
        out_shape=[jax.ShapeDtypeStruct((batch * seq, GLA_V), BF16),
                   jax.ShapeDtypeStruct((batch, GLA_HEADS, GLA_DK, GLA_DV), F32)],
        scratch_shapes=[pltpu.VMEM((GLA_HEADS, GLA_DV, GLA_DK), F32)],
        compiler_params=_params(("parallel", "arbitrary")),
        name="gla",
    )(*args)


def _t5_bucket(rel):
    nb = REL_BUCKETS // 2
    max_exact = nb // 2
    ret = jnp.where(rel > 0, nb, 0)
    n = jnp.abs(rel)
    nf = jnp.maximum(n, 1).astype(F32)
    large = max_exact + (jnp.log(nf / max_exact) / math.log(REL_MAX_DIST / max_exact)
                         * (nb - max_exact)).astype(jnp.int32)
    large = jnp.minimum(large, nb - 1)
    return ret + jnp.where(n < max_exact, n, large)


def _bucket_tile(q_pos, k_pos):
    visible = (k_pos[None, :] // CHUNK) <= (q_pos[:, None] // CHUNK)
    return jnp.where(visible, _t5_bucket(k_pos[None, :] - q_pos[:, None]), MASKED_BUCKET).astype(jnp.int32)


def _bias_from_buckets(idx, tab_ref, head):
    def step(bk, acc):
        return jnp.where(idx == bk, tab_ref[bk * DIFF_HEADS + head], acc)
    return lax.fori_loop(0, REL_BUCKETS, step, jnp.full(idx.shape, NEG_INF, F32))


def _lambda(lam_ref, lam_init):
    l = lam_ref[...]
    return (jnp.exp(jnp.sum(l[0:1] * l[1:2], axis=-1, keepdims=True))
            - jnp.exp(jnp.sum(l[2:3] * l[3:4], axis=-1, keepdims=True)) + lam_init)


def _comp_masks(q):
    lane = lax.broadcasted_iota(jnp.int32, q.shape, 1)
    zero = jnp.zeros_like(q)
    return jnp.where(lane < DIFF_DH, q, zero), jnp.where(lane < DIFF_DH, zero, q)


def _diff_finish(o0, o1, lam, on_ref, lam_init, out_dtype):
    o = o0 - lam * o1
    return (_rms(o, on_ref[...]) * (1.0 - lam_init)).astype(out_dtype)


def _scores(q_c, parts):
    return [_dot_nt(q_c, k) + bias if jnp.ndim(bias) == 2 else _dot_nt(q_c, k) for k, _, bias in parts]


def _softmax_pv(scores, parts):
    shifts = [0.0 if jnp.ndim(bias) == 2 else bias for _, _, bias in parts]
    m = None
    for s, shift in zip(scores, shifts):
        part_max = s.max(axis=-1, keepdims=True) + shift
        m = part_max if m is None else jnp.maximum(m, part_max)
    ol = 0.0
    for s, shift, (_, v1, _) in zip(scores, shifts, parts):
        ol = ol + _dot(jnp.exp(s - (m - shift)).astype(BF16), v1)
    return ol[:, :DIFF_DV] * (1.0 / ol[:, DIFF_DV:])


def _diff_prompt_body(tab_ref, q_ref, kb, vb, idx_ref, lam_ref, on_ref, o_ref, bias, v1, *, seq, lam_init):
    b = pl.program_id(0)
    h = pl.program_id(1)

    @pl.when(b == 0)
    def _():
        for t in range(2):
            bias[h, t] = _bias_from_buckets(idx_ref[t], tab_ref, h)

    v1[:, :DIFF_DV] = vb[...]
    v1[:, DIFF_DV:] = jnp.ones((seq, LANES), BF16)
    lam = _lambda(lam_ref, lam_init)
    far_bias = tab_ref[(REL_BUCKETS // 2 - 1) * DIFF_HEADS + h]

    def key_parts(i):
        near0 = max(i - 1, 0) * Q_BLOCK
        near = slice(near0, near0 + 2 * Q_BLOCK)
        parts = [(kb[near, :], v1[near, :], bias[h, min(i, 1)])]
        if near0 > 0:
            parts.append((kb[0:near0, :], v1[0:near0, :], far_bias))
        return parts

    items = [(i, c) for i in range(seq // Q_BLOCK) for c in range(2)]
    comps = {}

    def scores_of(item):
        i, c = item
        if i not in comps:
            comps[i] = _comp_masks(q_ref[i * Q_BLOCK:(i + 1) * Q_BLOCK, :])
        return _scores(comps[i][c], key_parts(i))

    outs = {}
    ahead = scores_of(items[0])
    for n, (i, c) in enumerate(items):
        cur = ahead
        if n + 1 < len(items):
            ahead = scores_of(items[n + 1])
        outs[c] = _softmax_pv(cur, key_parts(i))
        if c == 1:
            o_ref[i * Q_BLOCK:(i + 1) * Q_BLOCK, :] = _diff_finish(outs[0], outs[1], lam, on_ref, lam_init,
                                                                    o_ref.dtype)


def _diff_prompt(table, dq, dk, dv, lam_p, onorm, batch, seq, lam_init):
    q_pos = jnp.arange(Q_BLOCK)
    idx = jnp.stack([_bucket_tile(q_pos, jnp.arange(2 * Q_BLOCK)),
                     _bucket_tile(q_pos + Q_BLOCK, jnp.arange(2 * Q_BLOCK))])
    head = lambda: pl.BlockSpec((seq, LANES), lambda b, h: (b, h))
    return pl.pallas_call(
        functools.partial(_diff_prompt_body, seq=seq, lam_init=lam_init),
        grid=(batch, DIFF_HEADS),
        in_specs=[
            pl.BlockSpec(memory_space=pltpu.SMEM),
            head(), head(), head(),
            pl.BlockSpec((2, Q_BLOCK, 2 * Q_BLOCK), lambda b, h: (0, 0, 0)),
            pl.BlockSpec((4, DIFF_DH), lambda b, h: (0, 0)),
            pl.BlockSpec((1, DIFF_DV), lambda b, h: (0, 0)),
        ],
        out_specs=head(),
        out_shape=jax.ShapeDtypeStruct((batch * seq, DIFF_V), BF16),
        scratch_shapes=[pltpu.VMEM((DIFF_HEADS, 2, Q_BLOCK, 2 * Q_BLOCK), F32),
                        pltpu.VMEM((seq, DIFF_DV + LANES), BF16)],
        compiler_params=_params(("arbitrary", "arbitrary")),
        name="diff_prompt",
    )(table, dq, dk, dv, idx, lam_p, onorm)


def _diff_sample_body(tab_ref, q_ref, k_ref, v_ref, ck_ref, cv_ref, idxc_ref, idxn_ref, lam_ref, on_ref, o_ref,
                      bias_c, bias_n, *, seq, lam_init):
    groups = [(h, c) for h in range(DIFF_HEADS) for c in range(2)]

    @pl.when(pl.program_id(0) == 0)
    def _():
        for h in range(DIFF_HEADS):
            tile = _bias_from_buckets(idxc_ref[h], tab_ref, h)
            for g in (2 * h, 2 * h + 1):
                bias_c[g * seq:(g + 1) * seq, :] = tile
            bias_n[h] = _bias_from_buckets(idxn_ref[...], tab_ref, h)

    lam = _lambda(lam_ref, lam_init)
    head_cols = [slice(h * LANES, (h + 1) * LANES) for h in range(DIFF_HEADS)]
    q_groups = [q_c for cols in head_cols for q_c in _comp_masks(q_ref[:, cols])]
    s_old = _dot_nt(jnp.concatenate(q_groups, axis=0), ck_ref[...].astype(BF16)) + bias_c[...]
    s_new = jnp.concatenate([_dot_nt(q_g, k_ref[:, head_cols[h]]) + bias_n[h]
                             for q_g, (h, _) in zip(q_groups, groups)], axis=0)
    m = jnp.maximum(s_old.max(axis=-1, keepdims=True), s_new.max(axis=-1, keepdims=True))
    e_old = jnp.exp(s_old - m)
    e_new = jnp.exp(s_new - m)
    l = e_old.sum(axis=-1, keepdims=True) + e_new.sum(axis=-1, keepdims=True)
    o_old = _dot(e_old.astype(BF16), cv_ref[...].astype(BF16))
    e_new = e_new.astype(BF16)
    o_new = jnp.concatenate([_dot(e_new[g * seq:(g + 1) * seq, :], v_ref[:, head_cols[h]])
                             for g, (h, _) in enumerate(groups)], axis=0)
    o = (o_old + o_new) * (1.0 / l)
    for h, cols in enumerate(head_cols):
        o0 = o[(2 * h) * seq:(2 * h + 1) * seq, :]
        o1 = o[(2 * h + 1) * seq:(2 * h + 2) * seq, :]
        o_ref[:, cols] = _diff_finish(o0, o1, lam, on_ref, lam_init, o_ref.dtype)


def _diff_sample(table, dq, dk, dv, cache_k, cache_v, lam_p, onorm, batch, seq, past, lam_init):
    q_pos = past + jnp.arange(seq)
    idx_n = _bucket_tile(q_pos, past + jnp.arange(seq))
    idx_rows = jnp.repeat(_bucket_tile(q_pos, jnp.arange(past)), HEADS, axis=1)
    row_head = jnp.arange(past * HEADS) % HEADS
    idx_c = jnp.where(row_head[None, None, :] == jnp.arange(HEADS)[:, None, None], idx_rows[None], MASKED_BUCKET)
    new_q = pl.BlockSpec((seq, DIFF_QK), lambda b: (b, 0))
    old = lambda: pl.BlockSpec((past * HEADS, LANES), lambda b: (b, 0))
    const = lambda shape: pl.BlockSpec(shape, lambda b: (0,) * len(shape))
    n_groups = 2 * DIFF_HEADS
    return pl.pallas_call(
        functools.partial(_diff_sample_body, seq=seq, lam_init=lam_init),
        grid=(batch,),
        in_specs=[pl.BlockSpec(memory_space=pltpu.SMEM), new_q, new_q, new_q, old(), old(),
                  const((HEADS, seq, past * HEADS)), const((seq, seq)), const((4, DIFF_DH)), const((1, DIFF_DV))],
        out_specs=new_q,
        out_shape=jax.ShapeDtypeStruct((batch * seq, DIFF_V), BF16),
        scratch_shapes=[pltpu.VMEM((n_groups * seq, past * HEADS), F32), pltpu.VMEM((DIFF_HEADS, seq, seq), F32)],
        compiler_params=_params(("arbitrary",)),
        name="diff_sample",
    )(table, dq, dk, dv, cache_k, cache_v, idx_c, idx_n, lam_p, onorm)


def _memattn_body(q_ref, k_ref, v_ref, o_ref, *, per_head):
    head_cols = [slice(h * MEM_DH, (h + 1) * MEM_DH) for h in range(MEM_HEADS)]
    if per_head:
        ks = [k_ref[:, h, :].astype(BF16) for h in range(MEM_HEADS)]
        vs = [v_ref[:, h, :].astype(BF16) for h in range(MEM_HEADS)]
    else:
        ks = [k_ref[:, cols] for cols in head_cols]
        vs = [v_ref[:, cols] for cols in head_cols]
    scores = [_dot_nt(q_ref[:, cols], k) * (MEM_DH ** -0.5) for cols, k in zip(head_cols, ks)]
    probs = []
    for s in scores:
        e = jnp.exp(s - s.max(axis=-1, keepdims=True))
        probs.append((e * (1.0 / e.sum(axis=-1, keepdims=True))).astype(BF16))
    for cols, p, v in zip(head_cols, probs, vs):
        o_ref[:, cols] = _dot(p, v).astype(o_ref.dtype)


def _memattn(mq, mk, mv, batch, seq, n_mem):
    tq = min(MEMATTN_TILE, seq)
    nq = seq // tq
    q_spec = pl.BlockSpec((tq, MEM_W), lambda b, i: (b * nq + i, 0))
    per_head = mk.ndim == 3
    if per_head:
        kv_spec = pl.BlockSpec((n_mem, HEADS, LANES), lambda b, i: (b, 0, 0))
    else:
        kv_spec = pl.BlockSpec((n_mem, MEM_W), lambda b, i: (b, 0))
    return pl.pallas_call(
        functools.partial(_memattn_body, per_head=per_head),
        grid=(batch, nq),
        in_specs=[q_spec, kv_spec, kv_spec],
        out_specs=q_spec,
        out_shape=jax.ShapeDtypeStruct((batch * seq, MEM_W), BF16),
        compiler_params=_params(("parallel", "arbitrary")),
        name="memattn",
    )(mq, mk, mv)


def _outproj_body(x_ref, g_ref, d_ref, m_ref, wg_ref, wd_ref, wm_ref, o_ref):
    o_ref[...] = (x_ref[...] + _dot(g_ref[...], wg_ref[...]) + _dot(d_ref[...], wd_ref[...])
                  + _dot(m_ref[...], wm_ref[...]))


def _outproj(x, g, d, m, wo):
    n = x.shape[0]
    tm = min(TOK_TILE, n)
    tok = lambda width: pl.BlockSpec((tm, width), lambda i: (i, 0))
    w_rows = lambda rows, blk: pl.BlockSpec((rows, D_MODEL), lambda i: (blk, 0))
    return pl.pallas_call(
        _outproj_body,
        grid=(n // tm,),
        in_specs=[tok(D_MODEL), tok(GLA_V), tok(DIFF_V), tok(MEM_W),
                  w_rows(GLA_V, 0), w_rows(DIFF_V, GLA_V // DIFF_V), w_rows(MEM_W, (GLA_V + DIFF_V) // MEM_W)],
        out_specs=tok(D_MODEL),
        out_shape=jax.ShapeDtypeStruct((n, D_MODEL), F32),
        compiler_params=_params(("parallel",)),
        name="outproj",
    )(x, g, d, m, wo, wo, wo)


def _row(v):
    return v.reshape(1, -1).astype(F32)


def kernel(x_prompt, x_sample, mem_prompt, cache_diff_k, cache_diff_v, state_gla, cache_mem_k, cache_mem_v, rel_bias_table, norm_ffn1, w_ffn1_in, w_ffn1_out, norm_mix, w_in, w_gla_g2, b_gla_g, gla_out_norm, diff_q_norm, diff_k_norm, diff_lambda, diff_out_norm, mem_norm, w_mem_kv, mem_q_norm, mem_k_norm, w_o, norm_ffn2, w_ffn2_in, w_ffn2_out, norm_final):
    depth = norm_ffn1.shape[0]
    assert depth == 1, "single-layer step"
    layer = 0
    batch, seq, _ = x_prompt.shape
    dec_batch, dec_seq, _ = x_sample.shape
    past = cache_diff_k.shape[2]
    n_mem = mem_prompt.shape[1]
    lam_init = 0.8 - 0.6 * math.exp(-0.3 * layer)

    w_g2 =jnp.pad(w_gla_g2[layer].astype(BF16), ((0, LANES - GLA_GATE_RANK), (0, 0)))
    b_g = _row(b_gla_g[layer])
    qn = _row(jnp.tile(diff_q_norm[layer], DIFF_QK // DIFF_DH))
    kn = _row(jnp.tile(diff_k_norm[layer], DIFF_QK // DIFF_DH))
    mqn = _row(jnp.tile(mem_q_norm[layer], MEM_HEADS))
    mkn = _row(jnp.tile(mem_k_norm[layer], MEM_HEADS))
    table = rel_bias_table.astype(F32).reshape(-1)
    lam_p = diff_lambda[layer].astype(F32)
    gla_on = _row(gla_out_norm[layer])
    diff_on = _row(diff_out_norm[layer])
    per_head = lambda a: a.reshape(-1, HEADS, LANES)

    mk, mv, mk_b, mv_b = _memkv(mem_prompt.reshape(batch * n_mem, D_MODEL), _row(mem_norm[layer]),
                                w_mem_kv[layer].astype(BF16), mkn)

    n1 = _row(norm_ffn1[layer])
    xs1, *ffn1 = _ffn(x_sample.reshape(dec_batch * dec_seq, D_MODEL), n1, w_ffn1_in[layer], w_ffn1_in[layer],
                      w_ffn1_out[layer], up_col0=D_FF, ff_tile=FF_TILE_F32, emit_bf16=True)
    n_in_blocks = w_ffn2_in.shape[2] // LANES
    xp1, w2_in, w2_out, w_t, wo = _ffn(
        x_prompt.reshape(batch * seq, D_MODEL), n1, *ffn1, up_col0=0,
        cast_later=[(w_ffn2_in[layer], (D_MODEL, LANES), 0),
                    (w_ffn2_out[layer], (D_FF // n_in_blocks, D_MODEL), n_in_blocks),
                    (jnp.swapaxes(w_in[layer], 0, 1), (CAST_ROWS_W_IN, D_MODEL), 0),
                    (w_o[layer], (CAST_ROWS_W_O, D_MODEL), 0)])
    ffn2 = (w2_in, w2_in, w2_out)

    def layer_fn(x, b, t, chunk, s0, diff_fn, mem_k, mem_v):
        gq, gk, gv, gr, gg, dq, dk, dv, mq, dk_b, dv_b = _proj(x, _row(norm_mix[layer]), w_t, w_g2, b_g,
                                                               qn, kn, mqn)
        g_out, g_state = _gla(gq, gk, gg, gv, gr, gla_on, s0, b, t, chunk)
        d_out = diff_fn(dq, dk_b, dv_b)
        m_out = _memattn(mq, mem_k, mem_v, b, t, n_mem)
        x = _outproj(x, g_out, d_out, m_out, wo)
        x = _ffn(x, _row(norm_ffn2[layer]), *ffn2, up_col0=D_FF, final_gain=_row(norm_final[layer]))
        return x, dk, dv, g_state

    yp, dk_p, dv_p, g_p = layer_fn(
        xp1, batch, seq, CHUNK, None,
        lambda dq, dk, dv: _diff_prompt(table, dq, dk, dv, lam_p, diff_on, batch, seq, lam_init), mk_b, mv_b)
    ys, dk_s, dv_s, g_s = layer_fn(
        xs1, dec_batch, dec_seq, dec_seq, state_gla[layer],
        lambda dq, dk, dv: _diff_sample(table, dq, dk, dv, cache_diff_k[layer].reshape(-1, LANES),
                                        cache_diff_v[layer].reshape(-1, LANES),
                                        lam_p, diff_on, dec_batch, dec_seq, past, lam_init),
        per_head(cache_mem_k[layer]), per_head(cache_mem_v[layer]))

    head4 = lambda a, b, t: a.reshape(1, b, t, HEADS, LANES)
    return (yp.reshape(batch, seq, D_MODEL), ys.reshape(dec_batch, dec_seq, D_MODEL),
            head4(dk_p, batch, seq), head4(dv_p, batch, seq), g_p[None],
            head4(mk, batch, n_mem), head4(mv, batch, n_mem),
            head4(dk_s, dec_batch, dec_seq), head4(dv_s, dec_batch, dec_seq), g_s[None])
```

```python
import functools
import math

import jax
import jax.numpy as jnp
from jax import lax
from jax.experimental import pallas as pl
from jax.experimental.pallas import tpu as pltpu

F32 = jnp.float32
BF16 = jnp.bfloat16

D_MODEL = 2048
CHUNK = 64
EPS = 1e-6
NEG_INF = -1e30
GLA_HEADS, GLA_DK, GLA_DV, GLA_GATE_RANK, GLA_GATE_NORM = 4, 128, 256, 16, 16.0
DIFF_HEADS, DIFF_DH, DIFF_DV = 4, 64, 128
MEM_HEADS, MEM_DH = 4, 128
REL_BUCKETS, REL_MAX_DIST = 32, 128
D_FF = 5504
GLA_QK = GLA_HEADS * GLA_DK
GLA_V = GLA_HEADS * GLA_DV
DIFF_QK = DIFF_HEADS * 2 * DIFF_DH
DIFF_V = DIFF_HEADS * DIFF_DV
MEM_W = MEM_HEADS * MEM_DH
GLR_OFF = 2 * GLA_QK + 2 * GLA_V

LANES = 128
HEADS = 4
FF_TILE = 512
FF_TILE_F32 = 256
FFN_TOK_TILE = 1024
TOK_TILE = 512
GLA_TILE = 1024
MEMATTN_TILE = 2048
PROJ_TILE = 256
Q_BLOCK = 256
MASKED_BUCKET = REL_BUCKETS
MIB = 1024 * 1024
VMEM_CAP_MIB = 60
CAST_ROWS_W_IN = 48
CAST_ROWS_W_O = 16

_W_GQ, _W_GK, _W_GV, _W_GR, _W_GLR = 0, GLA_QK, 2 * GLA_QK, 2 * GLA_QK + GLA_V, GLR_OFF
_W_DQ = GLR_OFF + GLA_GATE_RANK
_W_DK = _W_DQ + DIFF_QK
_W_DV = _W_DK + DIFF_QK
_W_MQ = _W_DV + DIFF_V
IN_WIDTH = _W_MQ + MEM_W


def _dot(a, b):
    return jnp.dot(a, b, preferred_element_type=F32)


def _dot_nt(a, b):
    return lax.dot_general(a, b, (((1,), (1,)), ((), ())), preferred_element_type=F32)


def _dot_tn(a, b):
    return lax.dot_general(a, b, (((0,), (0,)), ((), ())), preferred_element_type=F32)


def _rms(x, gain):
    return x * lax.rsqrt(jnp.mean(x * x, axis=-1, keepdims=True) + EPS) * gain


def _params(sem, vmem_mib=48):
    return pltpu.CompilerParams(dimension_semantics=sem, vmem_limit_bytes=min(vmem_mib, VMEM_CAP_MIB) * MIB)


def _ffn_body(*refs, n_ff, overlap, final_norm, n_cast, emit_bf16):
    x_ref, g_ref, wg_ref, wu_ref, wo_ref = refs[:5]
    refs = refs[5:]
    if final_norm:
        fg_ref, refs = refs[0], refs[1:]
    cast_src, refs = refs[:n_cast], refs[n_cast:]
    o_ref, refs = refs[0], refs[1:]
    cast_dst, refs = refs[:n_cast], refs[n_cast:]
    if emit_bf16:
        (wg_out, wu_out, wo_out), refs = refs[:3], refs[3:]
    (xn_ref,) = refs
    j = pl.program_id(1)

    def tile(lo):
        for src, dst in zip(cast_src, cast_dst):
            dst[...] = src[...].astype(BF16)
        xn = xn_ref[...]
        if emit_bf16:
            def front(a, axis):
                a = a.astype(BF16)
                if lo == 0:
                    return a
                return jnp.concatenate([lax.slice_in_dim(a, lo, None, axis=axis),
                                        lax.slice_in_dim(a, 0, lo, axis=axis)], axis=axis)
            wg_out[...] = front(wg_ref[...], 1)
            wu_out[...] = front(wu_ref[...], 1)
            wo_out[...] = front(wo_ref[...], 0)
            live = wg_out.shape[1] - lo
            gate = _dot(xn, wg_out[:, :live])
            up = _dot(xn, wu_out[:, :live])
            act = (jax.nn.silu(gate) * up).astype(BF16)
            return 0.5 * _dot(act, wo_out[:live, :])
        gate = _dot(xn, wg_ref[:, lo:])
        up = _dot(xn, wu_ref[:, lo:])
        act = (jax.nn.silu(gate) * up).astype(BF16)
        return 0.5 * _dot(act, wo_ref[lo:, :])

    @pl.when(j == 0)
    def _():
        xn_ref[...] = _rms(x_ref[...], g_ref[...]).astype(BF16)
        o_ref[...] = x_ref[...] + tile(0)

    @pl.when((j > 0) & (j < n_ff - 1))
    def _():
        o_ref[...] += tile(0)

    @pl.when(j == n_ff - 1)
    def _():
        o = o_ref[...] + tile(overlap)
        o_ref[...] = _rms(o, fg_ref[...]) if final_norm else o


def _ffn(x, gain, w_gate, w_up, w_out, *, up_col0, final_gain=None, cast_later=(), ff_tile=FF_TILE,
         emit_bf16=False):
    n = x.shape[0]
    tm = min(FFN_TOK_TILE, n)
    n_ff = -(-D_FF // ff_tile)
    row = pl.BlockSpec((1, D_MODEL), lambda i, j: (0, 0))
    start = lambda j: jnp.minimum(j * (ff_tile // LANES), (D_FF - ff_tile) // LANES)
    w_in_tile = (pl.Element(D_MODEL), pl.Element(ff_tile))
    w_out_tile = (pl.Element(ff_tile), pl.Element(D_MODEL))
    gate_index = lambda i, j: (0, start(j) * LANES)
    out_index = lambda i, j: (start(j) * LANES, 0)
    tok = pl.BlockSpec((tm, D_MODEL), lambda i, j: (i, 0))
    in_specs = [
        tok,
        row,
        pl.BlockSpec(w_in_tile, gate_index),
        pl.BlockSpec(w_in_tile, lambda i, j: (0, (up_col0 // LANES + start(j)) * LANES)),
        pl.BlockSpec(w_out_tile, out_index),
    ]
    args = [x, gain, w_gate, w_up, w_out]
    if final_gain is not None:
        in_specs.append(row)
        args.append(final_gain)
    w_tile_bytes = D_MODEL * ff_tile * ((4 + 2) if emit_bf16 else 2)
    vmem_bytes = 4 * tm * D_MODEL * 4 + tm * D_MODEL * 2 + 6 * w_tile_bytes + 3 * tm * ff_tile * 4
    out_specs = [pl.BlockSpec((tm, D_MODEL), lambda i, j: (i, 0))]
    out_shape = [jax.ShapeDtypeStruct((n, D_MODEL), F32)]
    n_steps = (n // tm) * n_ff
    for mat, blk, first in cast_later:
        rows, cols = mat.shape
        assert rows % blk[0] == 0 and cols % blk[1] == 0 and (blk[0] == rows or blk[1] == cols)
        n_blk = (rows // blk[0]) * (cols // blk[1])
        assert first + n_blk <= n_steps
        by_rows = blk[1] == cols

        def index(i, j, first=first, n_blk=n_blk, by_rows=by_rows):
            k = jnp.clip(i * n_ff + j - first, 0, n_blk - 1)
            return (k, 0) if by_rows else (0, k)

        in_specs.append(pl.BlockSpec(blk, index))
        args.append(mat)
        out_specs.append(pl.BlockSpec(blk, index))
        out_shape.append(jax.ShapeDtypeStruct(mat.shape, BF16))
        vmem_bytes += 2 * blk[0] * blk[1] * (4 + 2)
    if emit_bf16:
        d_ff_pad = n_ff * ff_tile
        cols_j = pl.BlockSpec((D_MODEL, ff_tile), lambda i, j: (0, j))
        out_specs += [cols_j, cols_j, pl.BlockSpec((ff_tile, D_MODEL), lambda i, j: (j, 0))]
        out_shape += ([jax.ShapeDtypeStruct((D_MODEL, d_ff_pad), BF16)] * 2
                      + [jax.ShapeDtypeStruct((d_ff_pad, D_MODEL), BF16)])
    outs = pl.pallas_call(
        functools.partial(_ffn_body, n_ff=n_ff, overlap=n_ff * ff_tile - D_FF, final_norm=final_gain is not None,
                          n_cast=len(cast_later), emit_bf16=emit_bf16),
        grid=(n // tm, n_ff),
        in_specs=in_specs,
        out_specs=out_specs,
        out_shape=out_shape,
        scratch_shapes=[pltpu.VMEM((tm, D_MODEL), BF16)],
        compiler_params=_params(("arbitrary", "arbitrary"), vmem_bytes // MIB + 4),
        name="ffn",
    )(*args)
    return outs[0] if len(outs) == 1 else outs


def _group_rms(acc, gain_ref, width, scale, put):
    lane = lax.broadcasted_iota(jnp.int32, (1, LANES), 1)
    low = lane < width
    for c in range(acc.shape[1] // LANES):
        cols = slice(c * LANES, (c + 1) * LANES)
        xc = acc[:, cols]
        sq = xc * xc
        if width == LANES:
            r = lax.rsqrt(jnp.mean(sq, axis=-1, keepdims=True) + EPS)
        else:
            s_lo = jnp.sum(jnp.where(low, sq, 0.0), axis=-1, keepdims=True)
            s_hi = jnp.sum(jnp.where(low, 0.0, sq), axis=-1, keepdims=True)
            r = jnp.where(low, lax.rsqrt(s_lo / width + EPS), lax.rsqrt(s_hi / width + EPS))
        y = xc * r * gain_ref[:, cols]
        if scale != 1.0:
            y = y * scale
        put(c, y)


def _put_cols(ref):
    def put(c, y):
        ref[:, c * LANES:(c + 1) * LANES] = y.astype(ref.dtype)
    return put


def _put_heads_and_cols(heads_ref, cols_ref):
    def put(c, y):
        heads_ref[:, c, :] = y
        cols_ref[:, c * LANES:(c + 1) * LANES] = y.astype(cols_ref.dtype)
    return put


def _proj_body(x_ref, g_ref, wt_ref, wg2_ref, bg_ref, qn_ref, kn_ref, mqn_ref,
               gq_ref, gk_ref, gv_ref, gr_ref, gg_ref, dq_ref, dk_ref, dv_ref, mq_ref, dkb_ref, dvb_ref, xn_ref):
    xn_ref[...] = _rms(x_ref[...], g_ref[...]).astype(BF16)

    def cols(start, width):
        return _dot_nt(xn_ref[...], wt_ref[start:start + width, :])

    gq_ref[...] = cols(_W_GQ, GLA_QK) * (GLA_DK ** -0.5)
    gk_ref[...] = cols(_W_GK, GLA_QK)
    gv_ref[...] = cols(_W_GV, GLA_V).astype(BF16)
    gr_ref[...] = cols(_W_GR, GLA_V)
    glr = cols(_W_GLR, LANES)
    z = _dot(glr.astype(BF16), wg2_ref[...]) + bg_ref[...]
    gg_ref[...] = jax.nn.log_sigmoid(z) / GLA_GATE_NORM
    _group_rms(cols(_W_DQ, DIFF_QK), qn_ref, DIFF_DH, DIFF_DH ** -0.5, _put_cols(dq_ref))
    _group_rms(cols(_W_DK, DIFF_QK), kn_ref, DIFF_DH, 1.0, _put_heads_and_cols(dk_ref, dkb_ref))
    dv = cols(_W_DV, DIFF_V)
    put_v = _put_heads_and_cols(dv_ref, dvb_ref)
    for h in range(HEADS):
        put_v(h, dv[:, h * LANES:(h + 1) * LANES])
    _group_rms(cols(_W_MQ, MEM_W), mqn_ref, MEM_DH, 1.0, _put_cols(mq_ref))


def _proj(x, gain, wt, wg2, bg, qn, kn, mqn):
    n = x.shape[0]
    tm = min(PROJ_TILE, n)
    const = lambda shape: pl.BlockSpec(shape, lambda i: (0, 0))
    out = lambda width: pl.BlockSpec((tm, width), lambda i: (i, 0))
    heads = pl.BlockSpec((tm, HEADS, LANES), lambda i: (i, 0, 0))
    shp = lambda width, dt: jax.ShapeDtypeStruct((n, width), dt)
    shp_heads = jax.ShapeDtypeStruct((n, HEADS, LANES), F32)
    return pl.pallas_call(
        _proj_body,
        grid=(n // tm,),
        in_specs=[
            pl.BlockSpec((tm, D_MODEL), lambda i: (i, 0)),
            const((1, D_MODEL)),
            pl.BlockSpec((IN_WIDTH, D_MODEL), lambda i: (0, 0), pipeline_mode=pl.Buffered(1)),
            const((LANES, GLA_QK)),
            const((1, GLA_QK)),
            const((1, DIFF_QK)),
            const((1, DIFF_QK)),
            const((1, MEM_W)),
        ],
        out_specs=[out(GLA_QK), out(GLA_QK), out(GLA_V), out(GLA_V), out(GLA_QK),
                   out(DIFF_QK), heads, heads, out(MEM_W), out(DIFF_QK), out(DIFF_V)],
        out_shape=[shp(GLA_QK, F32), shp(GLA_QK, F32), shp(GLA_V, BF16), shp(GLA_V, F32), shp(GLA_QK, F32),
                   shp(DIFF_QK, BF16), shp_heads, shp_heads, shp(MEM_W, BF16), shp(DIFF_QK, BF16), shp(DIFF_V, BF16)],
        scratch_shapes=[pltpu.VMEM((tm, D_MODEL), BF16)],
        compiler_params=_params(("parallel",)),
        name="proj",
    )(x, gain, wt, wg2, bg, qn, kn, mqn)


def _memkv_body(x_ref, g_ref, w_ref, kn_ref, k_ref, v_ref, kb_ref, vb_ref):
    xn = _rms(x_ref[...], g_ref[...]).astype(BF16)
    _group_rms(_dot(xn, w_ref[:, :MEM_W]), kn_ref, MEM_DH, 1.0, _put_heads_and_cols(k_ref, kb_ref))
    v = _dot(xn, w_ref[:, MEM_W:])
    put_v = _put_heads_and_cols(v_ref, vb_ref)
    for h in range(HEADS):
        put_v(h, v[:, h * LANES:(h + 1) * LANES])


def _memkv(mem, gain, w, kn):
    n = mem.shape[0]
    tm = min(TOK_TILE, n)
    const = lambda shape: pl.BlockSpec(shape, lambda i: (0, 0))
    heads = pl.BlockSpec((tm, HEADS, LANES), lambda i: (i, 0, 0))
    dense = pl.BlockSpec((tm, MEM_W), lambda i: (i, 0))
    return pl.pallas_call(
        _memkv_body,
        grid=(n // tm,),
        in_specs=[pl.BlockSpec((tm, D_MODEL), lambda i: (i, 0)), const((1, D_MODEL)),
                  const((D_MODEL, 2 * MEM_W)), const((1, MEM_W))],
        out_specs=[heads, heads, dense, dense],
        out_shape=[jax.ShapeDtypeStruct((n, HEADS, LANES), F32)] * 2 + [jax.ShapeDtypeStruct((n, MEM_W), BF16)] * 2,
        compiler_params=_params(("parallel",)),
        name="memkv",
    )(mem, gain, w, kn)


def _split3(x):
    hi = x.astype(BF16)
    r1 = x - hi.astype(F32)
    mid = r1.astype(BF16)
    lo = (r1 - mid.astype(F32)).astype(BF16)
    return hi, mid, lo


def _gla_body(*refs, chunk, n_chunks, n_steps, has_state):
    if has_state:
        gq_ref, gk_ref, gg_ref, gv_ref, gr_ref, on_ref, s0_ref, go_ref, st_ref, state = refs
    else:
        gq_ref, gk_ref, gg_ref, gv_ref, gr_ref, on_ref, go_ref, st_ref, state = refs
    t = pl.program_id(1)

    @pl.when(t == 0)
    def _():
        for h in range(GLA_HEADS):
            if has_state:
                state[h] = s0_ref[0, h].T
            else:
                state[h] = jnp.zeros((GLA_DV, GLA_DK), F32)

    row = lax.broadcasted_iota(jnp.int32, (chunk, chunk), 0)
    col = lax.broadcasted_iota(jnp.int32, (chunk, chunk), 1)
    causal = row >= col
    tril = causal.astype(BF16)

    chunks = [slice(c * chunk, (c + 1) * chunk) for c in range(n_chunks)]
    kcs = [slice(h * GLA_DK, (h + 1) * GLA_DK) for h in range(GLA_HEADS)]
    vcs = [slice(h * GLA_DV, (h + 1) * GLA_DV) for h in range(GLA_HEADS)]

    bs = []
    for rows in chunks:
        g_hi, g_mid, g_lo = _split3(gg_ref[rows, :])
        bs.append(_dot(tril, g_hi) + _dot(tril, g_mid) + _dot(tril, g_lo))

    qes, kes, kds, decays = [], [], [], []
    for rows, b in zip(chunks, bs):
        b_last = b[chunk - 1:chunk, :]
        q = gq_ref[rows, :]
        k = gk_ref[rows, :]
        qes.append((q * jnp.exp(b)).astype(BF16))
        kes.append((k * jnp.exp(-b)).astype(BF16))
        kds.append((k * jnp.exp(b_last - b)).astype(BF16))
        decays.append(jnp.exp(b_last))

    a_s = [[jnp.where(causal, _dot_nt(qe[:, kc], ke[:, kc]), 0.0).astype(BF16) for kc in kcs]
           for qe, ke in zip(qes, kes)]
    incs = [[_dot_tn(gv_ref[rows, vc], kd[:, kc]) for kc, vc in zip(kcs, vcs)] for rows, kd in zip(chunks, kds)]

    s_in = []
    s_cur = [state[h] for h in range(GLA_HEADS)]
    for c in range(n_chunks):
        s_in.append([s.astype(BF16) for s in s_cur])
        s_cur = [s * decays[c][:, kc] + inc for s, kc, inc in zip(s_cur, kcs, incs[c])]
    for h in range(GLA_HEADS):
        state[h] = s_cur[h]

    for c, rows in enumerate(chunks):
        for h, (kc, vc) in enumerate(zip(kcs, vcs)):
            o = _dot_nt(qes[c][:, kc], s_in[c][h]) + _dot(a_s[c][h], gv_ref[rows, vc])
            go_ref[rows, vc] = (_rms(o, on_ref[...]) * jax.nn.silu(gr_ref[rows, vc])).astype(BF16)

    @pl.when(t == n_steps - 1)
    def _():
        for h in range(GLA_HEADS):
            st_ref[0, h] = state[h].T


def _gla(gq, gk, gg, gv, gr, onorm, s0, batch, seq, chunk):
    tt = min(GLA_TILE, seq)
    n_steps = seq // tt
    tok = lambda width: pl.BlockSpec((tt, width), lambda b, t: (b * n_steps + t, 0))
    st_spec = pl.BlockSpec((1, GLA_HEADS, GLA_DK, GLA_DV), lambda b, t: (b, 0, 0, 0))
    in_specs = [tok(GLA_QK), tok(GLA_QK), tok(GLA_QK), tok(GLA_V), tok(GLA_V),
                pl.BlockSpec((1, GLA_DV), lambda b, t: (0, 0))]
    args = [gq, gk, gg, gv, gr, onorm]
    if s0 is not None:
        in_specs.append(st_spec)
        args.append(s0)
    return pl.pallas_call(
        functools.partial(_gla_body, chunk=chunk, n_chunks=tt // chunk, n_steps=n_steps, has_state=s0 is not None),
        grid=(batch, n_steps),
        in_specs=in_specs,
        out_specs=[tok(GLA_V), st_spec],
        out_shape=[jax.ShapeDtypeStruct((batch * seq, GLA_V), BF16),
                   jax.ShapeDtypeStruct((batch, GLA_HEADS, GLA_DK, GLA_DV), F32)],
        scratch_shapes=[pltpu.VMEM((GLA_HEADS, GLA_DV, GLA_DK), F32)],
        compiler_params=_params(("parallel", "arbitrary")),
        name="gla",
    )(*args)


def _t5_bucket(rel):
    nb = REL_BUCKETS // 2
    max_exact = nb // 2
    ret = jnp.where(rel > 0, nb, 0)
    n = jnp.abs(rel)
    nf = jnp.maximum(n, 1).astype(F32)
    large = max_exact + (jnp.log(nf / max_exact) / math.log(REL_MAX_DIST / max_exact)
                         * (nb - max_exact)).astype(jnp.int32)
    large = jnp.minimum(large, nb - 1)
    return ret + jnp.where(n < max_exact, n, large)


def _bucket_tile(q_pos, k_pos):
    visible = (k_pos[None, :] // CHUNK) <= (q_pos[:, None] // CHUNK)
    return jnp.where(visible, _t5_bucket(k_pos[None, :] - q_pos[:, None]), MASKED_BUCKET).astype(jnp.int32)


def _bias_from_buckets(idx, tab_ref, head):
    def step(bk, acc):
        return jnp.where(idx == bk, tab_ref[bk * DIFF_HEADS + head], acc)
    return lax.fori_loop(0, REL_BUCKETS, step, jnp.full(idx.shape, NEG_INF, F32))


def _lambda(lam_ref, lam_init):
    l = lam_ref[...]
    return (jnp.exp(jnp.sum(l[0:1] * l[1:2], axis=-1, keepdims=True))
            - jnp.exp(jnp.sum(l[2:3] * l[3:4], axis=-1, keepdims=True)) + lam_init)


def _comp_masks(q):
    lane = lax.broadcasted_iota(jnp.int32, q.shape, 1)
    zero = jnp.zeros_like(q)
    return jnp.where(lane < DIFF_DH, q, zero), jnp.where(lane < DIFF_DH, zero, q)


def _diff_finish(o0, o1, lam, on_ref, lam_init, out_dtype):
    o = o0 - lam * o1
    return (_rms(o, on_ref[...]) * (1.0 - lam_init)).astype(out_dtype)


def _scores(q_c, parts):
    return [_dot_nt(q_c, k) + bias if jnp.ndim(bias) == 2 else _dot_nt(q_c, k) for k, _, bias in parts]


def _softmax_pv(scores, parts):
    shifts = [0.0 if jnp.ndim(bias) == 2 else bias for _, _, bias in parts]
    m = None
    for s, shift in zip(scores, shifts):
        part_max = s.max(axis=-1, keepdims=True) + shift
        m = part_max if m is None else jnp.maximum(m, part_max)
    ol = 0.0
    for s, shift, (_, v1, _) in zip(scores, shifts, parts):
        ol = ol + _dot(jnp.exp(s - (m - shift)).astype(BF16), v1)
    return ol[:, :DIFF_DV] * (1.0 / ol[:, DIFF_DV:])


def _diff_prompt_body(tab_ref, q_ref, kb, vb, idx_ref, lam_ref, on_ref, o_ref, bias, v1, *, seq, lam_init):
    b = pl.program_id(0)
    h = pl.program_id(1)

    @pl.when(b == 0)
    def _():
        for t in range(2):
            bias[h, t] = _bias_from_buckets(idx_ref[t], tab_ref, h)

    v1[:, :DIFF_DV] = vb[...]
    v1[:, DIFF_DV:] = jnp.ones((seq, LANES), BF16)
    lam = _lambda(lam_ref, lam_init)
    far_bias = tab_ref[(REL_BUCKETS // 2 - 1) * DIFF_HEADS + h]

    def key_parts(i):
        near0 = max(i - 1, 0) * Q_BLOCK
        near = slice(near0, near0 + 2 * Q_BLOCK)
        parts = [(kb[near, :], v1[near, :], bias[h, min(i, 1)])]
        if near0 > 0:
            parts.append((kb[0:near0, :], v1[0:near0, :], far_bias))
        return parts

    items = [(i, c) for i in range(seq // Q_BLOCK) for c in range(2)]
    comps = {}

    def scores_of(item):
        i, c = item
        if i not in comps:
            comps[i] = _comp_masks(q_ref[i * Q_BLOCK:(i + 1) * Q_BLOCK, :])
        return _scores(comps[i][c], key_parts(i))

    outs = {}
    ahead = scores_of(items[0])
    for n, (i, c) in enumerate(items):
        cur = ahead
        if n + 1 < len(items):
            ahead = scores_of(items[n + 1])
        outs[c] = _softmax_pv(cur, key_parts(i))
        if c == 1:
            o_ref[i * Q_BLOCK:(i + 1) * Q_BLOCK, :] = _diff_finish(outs[0], outs[1], lam, on_ref, lam_init,
                                                                    o_ref.dtype)


def _diff_prompt(table, dq, dk, dv, lam_p, onorm, batch, seq, lam_init):
    q_pos = jnp.arange(Q_BLOCK)
    idx = jnp.stack([_bucket_tile(q_pos, jnp.arange(2 * Q_BLOCK)),
                     _bucket_tile(q_pos + Q_BLOCK, jnp.arange(2 * Q_BLOCK))])
    head = lambda: pl.BlockSpec((seq, LANES), lambda b, h: (b, h))
    return pl.pallas_call(
        functools.partial(_diff_prompt_body, seq=seq, lam_init=lam_init),
        grid=(batch, DIFF_HEADS),
        in_specs=[
            pl.BlockSpec(memory_space=pltpu.SMEM),
            head(), head(), head(),
            pl.BlockSpec((2, Q_BLOCK, 2 * Q_BLOCK), lambda b, h: (0, 0, 0)),
            pl.BlockSpec((4, DIFF_DH), lambda b, h: (0, 0)),
            pl.BlockSpec((1, DIFF_DV), lambda b, h: (0, 0)),
        ],
        out_specs=head(),
        out_shape=jax.ShapeDtypeStruct((batch * seq, DIFF_V), BF16),
        scratch_shapes=[pltpu.VMEM((DIFF_HEADS, 2, Q_BLOCK, 2 * Q_BLOCK), F32),
                        pltpu.VMEM((seq, DIFF_DV + LANES), BF16)],
        compiler_params=_params(("arbitrary", "arbitrary")),
        name="diff_prompt",
    )(table, dq, dk, dv, idx, lam_p, onorm)


def _diff_sample_body(tab_ref, q_ref, k_ref, v_ref, ck_ref, cv_ref, idxc_ref, idxn_ref, lam_ref, on_ref, o_ref,
                      bias_c, bias_n, *, seq, lam_init):
    groups = [(h, c) for h in range(DIFF_HEADS) for c in range(2)]

    @pl.when(pl.program_id(0) == 0)
    def _():
        for h in range(DIFF_HEADS):
            tile = _bias_from_buckets(idxc_ref[h], tab_ref, h)
            for g in (2 * h, 2 * h + 1):
                bias_c[g * seq:(g + 1) * seq, :] = tile
            bias_n[h] = _bias_from_buckets(idxn_ref[...], tab_ref, h)

    lam = _lambda(lam_ref, lam_init)
    head_cols = [slice(h * LANES, (h + 1) * LANES) for h in range(DIFF_HEADS)]
    q_groups = [q_c for cols in head_cols for q_c in _comp_masks(q_ref[:, cols])]
    s_old = _dot_nt(jnp.concatenate(q_groups, axis=0), ck_ref[...].astype(BF16)) + bias_c[...]
    s_new = jnp.concatenate([_dot_nt(q_g, k_ref[:, head_cols[h]]) + bias_n[h]
                             for q_g, (h, _) in zip(q_groups, groups)], axis=0)
    m = jnp.maximum(s_old.max(axis=-1, keepdims=True), s_new.max(axis=-1, keepdims=True))
    e_old = jnp.exp(s_old - m)
    e_new = jnp.exp(s_new - m)
    l = e_old.sum(axis=-1, keepdims=True) + e_new.sum(axis=-1, keepdims=True)
    o_old = _dot(e_old.astype(BF16), cv_ref[...].astype(BF16))
    e_new = e_new.astype(BF16)
    o_new = jnp.concatenate([_dot(e_new[g * seq:(g + 1) * seq, :], v_ref[:, head_cols[h]])
                             for g, (h, _) in enumerate(groups)], axis=0)
    o = (o_old + o_new) * (1.0 / l)
    for h, cols in enumerate(head_cols):
        o0 = o[(2 * h) * seq:(2 * h + 1) * seq, :]
        o1 = o[(2 * h + 1) * seq:(2 * h + 2) * seq, :]
        o_ref[:, cols] = _diff_finish(o0, o1, lam, on_ref, lam_init, o_ref.dtype)


def _diff_sample(table, dq, dk, dv, cache_k, cache_v, lam_p, onorm, batch, seq, past, lam_init):
    q_pos = past + jnp.arange(seq)
    idx_n = _bucket_tile(q_pos, past + jnp.arange(seq))
    idx_rows = jnp.repeat(_bucket_tile(q_pos, jnp.arange(past)), HEADS, axis=1)
    row_head = jnp.arange(past * HEADS) % HEADS
    idx_c = jnp.where(row_head[None, None, :] == jnp.arange(HEADS)[:, None, None], idx_rows[None], MASKED_BUCKET)
    new_q = pl.BlockSpec((seq, DIFF_QK), lambda b: (b, 0))
    old = lambda: pl.BlockSpec((past * HEADS, LANES), lambda b: (b, 0))
    const = lambda shape: pl.BlockSpec(shape, lambda b: (0,) * len(shape))
    n_groups = 2 * DIFF_HEADS
    return pl.pallas_call(
        functools.partial(_diff_sample_body, seq=seq, lam_init=lam_init),
        grid=(batch,),
        in_specs=[pl.BlockSpec(memory_space=pltpu.SMEM), new_q, new_q, new_q, old(), old(),
                  const((HEADS, seq, past * HEADS)), const((seq, seq)), const((4, DIFF_DH)), const((1, DIFF_DV))],
        out_specs=new_q,
        out_shape=jax.ShapeDtypeStruct((batch * seq, DIFF_V), BF16),
        scratch_shapes=[pltpu.VMEM((n_groups * seq, past * HEADS), F32), pltpu.VMEM((DIFF_HEADS, seq, seq), F32)],
        compiler_params=_params(("arbitrary",)),
        name="diff_sample",
    )(table, dq, dk, dv, cache_k, cache_v, idx_c, idx_n, lam_p, onorm)


def _softmax(s):
    e = jnp.exp(s - s.max(axis=-1, keepdims=True))
    return (e * (1.0 / e.sum(axis=-1, keepdims=True))).astype(BF16)


def _memattn_body(q_ref, k_ref, v_ref, o_ref, *, interleaved):
    head_cols = [slice(h * MEM_DH, (h + 1) * MEM_DH) for h in range(MEM_HEADS)]
    scale = MEM_DH ** -0.5
    if interleaved:
        tq = q_ref.shape[0]
        s = _dot_nt(jnp.concatenate([q_ref[:, cols] for cols in head_cols], axis=0), k_ref[...].astype(BF16)) * scale
        q_head = lax.broadcasted_iota(jnp.int32, s.shape, 0) // tq
        k_head = lax.broadcasted_iota(jnp.int32, s.shape, 1) % MEM_HEADS
        o = _dot(_softmax(jnp.where(q_head == k_head, s, NEG_INF)), v_ref[...].astype(BF16))
        for h, cols in enumerate(head_cols):
            o_ref[:, cols] = o[h * tq:(h + 1) * tq, :].astype(o_ref.dtype)
    else:
        scores = [_dot_nt(q_ref[:, cols], k_ref[:, cols]) * scale for cols in head_cols]
        probs = [_softmax(s) for s in scores]
        for cols, p in zip(head_cols, probs):
            o_ref[:, cols] = _dot(p, v_ref[:, cols]).astype(o_ref.dtype)


def _memattn(mq, mk, mv, batch, seq, n_mem):
    tq = min(MEMATTN_TILE, seq)
    nq = seq // tq
    q_spec = pl.BlockSpec((tq, MEM_W), lambda b, i: (b * nq + i, 0))
    interleaved = mk.shape[1] == LANES
    kv_spec = pl.BlockSpec((mk.shape[0] // batch, mk.shape[1]), lambda b, i: (b, 0))
    return pl.pallas_call(
        functools.partial(_memattn_body, interleaved=interleaved),
        grid=(batch, nq),
        in_specs=[q_spec, kv_spec, kv_spec],
        out_specs=q_spec,
        out_shape=jax.ShapeDtypeStruct((batch * seq, MEM_W), BF16),
        compiler_params=_params(("parallel", "arbitrary")),
        name="memattn",
    )(mq, mk, mv)


def _outproj_body(x_ref, g_ref, d_ref, m_ref, wg_ref, wd_ref, wm_ref, o_ref):
    o_ref[...] = (x_ref[...] + _dot(g_ref[...], wg_ref[...]) + _dot(d_ref[...], wd_ref[...])
                  + _dot(m_ref[...], wm_ref[...]))


def _outproj(x, g, d, m, wo):
    n = x.shape[0]
    tm = min(TOK_TILE, n)
    tok = lambda width: pl.BlockSpec((tm, width), lambda i: (i, 0))
    w_rows = lambda rows, blk: pl.BlockSpec((rows, D_MODEL), lambda i: (blk, 0))
    return pl.pallas_call(
        _outproj_body,
        grid=(n // tm,),
        in_specs=[tok(D_MODEL), tok(GLA_V), tok(DIFF_V), tok(MEM_W),
                  w_rows(GLA_V, 0), w_rows(DIFF_V, GLA_V // DIFF_V), w_rows(MEM_W, (GLA_V + DIFF_V) // MEM_W)],
        out_specs=tok(D_MODEL),
        out_shape=jax.ShapeDtypeStruct((n, D_MODEL), F32),
        compiler_params=_params(("parallel",)),
        name="outproj",
    )(x, g, d, m, wo, wo, wo)


def _row(v):
    return v.reshape(1, -1).astype(F32)


def kernel(x_prompt, x_sample, mem_prompt, cache_diff_k, cache_diff_v, state_gla, cache_mem_k, cache_mem_v, rel_bias_table, norm_ffn1, w_ffn1_in, w_ffn1_out, norm_mix, w_in, w_gla_g2, b_gla_g, gla_out_norm, diff_q_norm, diff_k_norm, diff_lambda, diff_out_norm, mem_norm, w_mem_kv, mem_q_norm, mem_k_norm, w_o, norm_ffn2, w_ffn2_in, w_ffn2_out, norm_final):
    depth = norm_ffn1.shape[0]
    assert depth == 1, "single-layer step"
    layer = 0
    batch, seq, _ = x_prompt.shape
    dec_batch, dec_seq, _ = x_sample.shape
    past = cache_diff_k.shape[2]
    n_mem = mem_prompt.shape[1]
    lam_init = 0.8 - 0.6 * math.exp(-0.3 * layer)

    w_g2 =jnp.pad(w_gla_g2[layer].astype(BF16), ((0, LANES - GLA_GATE_RANK), (0, 0)))
    b_g = _row(b_gla_g[layer])
    qn = _row(jnp.tile(diff_q_norm[layer], DIFF_QK // DIFF_DH))
    kn = _row(jnp.tile(diff_k_norm[layer], DIFF_QK // DIFF_DH))
    mqn = _row(jnp.tile(mem_q_norm[layer], MEM_HEADS))
    mkn = _row(jnp.tile(mem_k_norm[layer], MEM_HEADS))
    table = rel_bias_table.astype(F32).reshape(-1)
    lam_p = diff_lambda[layer].astype(F32)
    gla_on = _row(gla_out_norm[layer])
    diff_on = _row(diff_out_norm[layer])

    mk, mv, mk_b, mv_b = _memkv(mem_prompt.reshape(batch * n_mem, D_MODEL), _row(mem_norm[layer]),
                                w_mem_kv[layer].astype(BF16), mkn)

    n1 = _row(norm_ffn1[layer])
    xs1, *ffn1 = _ffn(x_sample.reshape(dec_batch * dec_seq, D_MODEL), n1, w_ffn1_in[layer], w_ffn1_in[layer],
                      w_ffn1_out[layer], up_col0=D_FF, ff_tile=FF_TILE_F32, emit_bf16=True)
    n_in_blocks = w_ffn2_in.shape[2] // LANES
    xp1, w2_in, w2_out, w_t, wo = _ffn(
        x_prompt.reshape(batch * seq, D_MODEL), n1, *ffn1, up_col0=0,
        cast_later=[(w_ffn2_in[layer], (D_MODEL, LANES), 0),
                    (w_ffn2_out[layer], (D_FF // n_in_blocks, D_MODEL), n_in_blocks),
                    (jnp.swapaxes(w_in[layer], 0, 1), (CAST_ROWS_W_IN, D_MODEL), 0),
                    (w_o[layer], (CAST_ROWS_W_O, D_MODEL), 0)])
    ffn2 = (w2_in, w2_in, w2_out)

    def layer_fn(x, b, t, chunk, s0, diff_fn, mem_k, mem_v):
        gq, gk, gv, gr, gg, dq, dk, dv, mq, dk_b, dv_b = _proj(x, _row(norm_mix[layer]), w_t, w_g2, b_g,
                                                               qn, kn, mqn)
        g_out, g_state = _gla(gq, gk, gg, gv, gr, gla_on, s0, b, t, chunk)
        d_out = diff_fn(dq, dk_b, dv_b)
        m_out = _memattn(mq, mem_k, mem_v, b, t, n_mem)
        x = _outproj(x, g_out, d_out, m_out, wo)
        x = _ffn(x, _row(norm_ffn2[layer]), *ffn2, up_col0=D_FF, final_gain=_row(norm_final[layer]))
        return x, dk, dv, g_state

    yp, dk_p, dv_p, g_p = layer_fn(
        xp1, batch, seq, CHUNK, None,
        lambda dq, dk, dv: _diff_prompt(table, dq, dk, dv, lam_p, diff_on, batch, seq, lam_init), mk_b, mv_b)
    ys, dk_s, dv_s, g_s = layer_fn(
        xs1, dec_batch, dec_seq, dec_seq, state_gla[layer],
        lambda dq, dk, dv: _diff_sample(table, dq, dk, dv, cache_diff_k[layer].reshape(-1, LANES),
                                        cache_diff_v[layer].reshape(-1, LANES),
                                        lam_p, diff_on, dec_batch, dec_seq, past, lam_init),
        cache_mem_k[layer].reshape(-1, LANES), cache_mem_v[layer].reshape(-1, LANES))

    head4 = lambda a, b, t: a.reshape(1, b, t, HEADS, LANES)
    return (yp.reshape(batch, seq, D_MODEL), ys.reshape(dec_batch, dec_seq, D_MODEL),
            head4(dk_p, batch, seq), head4(dv_p, batch, seq), g_p[None],
            head4(mk, batch, n_mem), head4(mv, batch, n_mem),
            head4(dk_s, dec_batch, dec_seq), head4(dv_s, dec_batch, dec_seq), g_s[None])
```

```python
import functools
import math

import jax
import jax.numpy as jnp
from jax import lax
from jax.experimental import pallas as pl
from jax.experimental.pallas import tpu as pltpu

F32 = jnp.float32
BF16 = jnp.bfloat16

D_MODEL = 2048
CHUNK = 64
EPS = 1e-6
NEG_INF = -1e30
GLA_HEADS, GLA_DK, GLA_DV, GLA_GATE_RANK, GLA_GATE_NORM = 4, 128, 256, 16, 16.0
DIFF_HEADS, DIFF_DH, DIFF_DV = 4, 64, 128
MEM_HEADS, MEM_DH = 4, 128
REL_BUCKETS, REL_MAX_DIST = 32, 128
D_FF = 5504
GLA_QK = GLA_HEADS * GLA_DK
GLA_V = GLA_HEADS * GLA_DV
DIFF_QK = DIFF_HEADS * 2 * DIFF_DH
DIFF_V = DIFF_HEADS * DIFF_DV
MEM_W = MEM_HEADS * MEM_DH
GLR_OFF = 2 * GLA_QK + 2 * GLA_V

LANES = 128
HEADS = 4
FF_TILE = 512
FF_TILE_F32 = 256
FFN_TOK_TILE = 1024
TOK_TILE = 512
GLA_TILE = 1024
OUTPROJ_TILE = 1024
MEMATTN_TILE = 2048
PROJ_TILE = 256
Q_BLOCK = 256
MASKED_BUCKET = REL_BUCKETS
MIB = 1024 * 1024
VMEM_CAP_MIB = 60
CAST_ROWS_W_IN = 48
CAST_ROWS_W_O = 16

_W_GQ, _W_GK, _W_GV, _W_GR, _W_GLR = 0, GLA_QK, 2 * GLA_QK, 2 * GLA_QK + GLA_V, GLR_OFF
_W_DQ = GLR_OFF + GLA_GATE_RANK
_W_DK = _W_DQ + DIFF_QK
_W_DV = _W_DK + DIFF_QK
_W_MQ = _W_DV + DIFF_V
IN_WIDTH = _W_MQ + MEM_W


def _dot(a, b):
    return jnp.dot(a, b, preferred_element_type=F32)


def _dot_nt(a, b):
    return lax.dot_general(a, b, (((1,), (1,)), ((), ())), preferred_element_type=F32)


def _dot_tn(a, b):
    return lax.dot_general(a, b, (((0,), (0,)), ((), ())), preferred_element_type=F32)


def _rms(x, gain):
    return x * lax.rsqrt(jnp.mean(x * x, axis=-1, keepdims=True) + EPS) * gain


def _params(sem, vmem_mib=48):
    return pltpu.CompilerParams(dimension_semantics=sem, vmem_limit_bytes=min(vmem_mib, VMEM_CAP_MIB) * MIB)


def _ffn_body(*refs, n_ff, overlap, final_norm, n_cast, emit_bf16):
    x_ref, g_ref, wg_ref, wu_ref, wo_ref = refs[:5]
    refs = refs[5:]
    if final_norm:
        fg_ref, refs = refs[0], refs[1:]
    cast_src, refs = refs[:n_cast], refs[n_cast:]
    o_ref, refs = refs[0], refs[1:]
    cast_dst, refs = refs[:n_cast], refs[n_cast:]
    if emit_bf16:
        (wg_out, wu_out, wo_out), refs = refs[:3], refs[3:]
    (xn_ref,) = refs
    j = pl.program_id(1)

    def tile(lo):
        for src, dst in zip(cast_src, cast_dst):
            dst[...] = src[...].astype(BF16)
        xn = xn_ref[...]
        if emit_bf16:
            def front(a, axis):
                a = a.astype(BF16)
                if lo == 0:
                    return a
                return jnp.concatenate([lax.slice_in_dim(a, lo, None, axis=axis),
                                        lax.slice_in_dim(a, 0, lo, axis=axis)], axis=axis)
            wg_out[...] = front(wg_ref[...], 1)
            wu_out[...] = front(wu_ref[...], 1)
            wo_out[...] = front(wo_ref[...], 0)
            live = wg_out.shape[1] - lo
            gate = _dot(xn, wg_out[:, :live])
            up = _dot(xn, wu_out[:, :live])
            act = (jax.nn.silu(gate) * up).astype(BF16)
            return 0.5 * _dot(act, wo_out[:live, :])
        gate = _dot(xn, wg_ref[:, lo:])
        up = _dot(xn, wu_ref[:, lo:])
        act = (jax.nn.silu(gate) * up).astype(BF16)
        return 0.5 * _dot(act, wo_ref[lo:, :])

    @pl.when(j == 0)
    def _():
        xn_ref[...] = _rms(x_ref[...], g_ref[...]).astype(BF16)
        o_ref[...] = x_ref[...] + tile(0)

    @pl.when((j > 0) & (j < n_ff - 1))
    def _():
        o_ref[...] += tile(0)

    @pl.when(j == n_ff - 1)
    def _():
        o = o_ref[...] + tile(overlap)
        o_ref[...] = _rms(o, fg_ref[...]) if final_norm else o


def _ffn(x, gain, w_gate, w_up, w_out, *, up_col0, final_gain=None, cast_later=(), ff_tile=FF_TILE,
         emit_bf16=False):
    n = x.shape[0]
    tm = min(FFN_TOK_TILE, n)
    n_ff = -(-D_FF // ff_tile)
    row = pl.BlockSpec((1, D_MODEL), lambda i, j: (0, 0))
    start = lambda j: jnp.minimum(j * (ff_tile // LANES), (D_FF - ff_tile) // LANES)
    w_in_tile = (pl.Element(D_MODEL), pl.Element(ff_tile))
    w_out_tile = (pl.Element(ff_tile), pl.Element(D_MODEL))
    gate_index = lambda i, j: (0, start(j) * LANES)
    out_index = lambda i, j: (start(j) * LANES, 0)
    tok = pl.BlockSpec((tm, D_MODEL), lambda i, j: (i, 0))
    in_specs = [
        tok,
        row,
        pl.BlockSpec(w_in_tile, gate_index),
        pl.BlockSpec(w_in_tile, lambda i, j: (0, (up_col0 // LANES + start(j)) * LANES)),
        pl.BlockSpec(w_out_tile, out_index),
    ]
    args = [x, gain, w_gate, w_up, w_out]
    if final_gain is not None:
        in_specs.append(row)
        args.append(final_gain)
    w_tile_bytes = D_MODEL * ff_tile * ((4 + 2) if emit_bf16 else 2)
    vmem_bytes = 4 * tm * D_MODEL * 4 + tm * D_MODEL * 2 + 6 * w_tile_bytes + 3 * tm * ff_tile * 4
    out_specs = [pl.BlockSpec((tm, D_MODEL), lambda i, j: (i, 0))]
    out_shape = [jax.ShapeDtypeStruct((n, D_MODEL), F32)]
    n_steps = (n // tm) * n_ff
    for mat, blk, first in cast_later:
        rows, cols = mat.shape
        assert rows % blk[0] == 0 and cols % blk[1] == 0 and (blk[0] == rows or blk[1] == cols)
        n_blk = (rows // blk[0]) * (cols // blk[1])
        assert first + n_blk <= n_steps
        by_rows = blk[1] == cols

        def index(i, j, first=first, n_blk=n_blk, by_rows=by_rows):
            k = jnp.clip(i * n_ff + j - first, 0, n_blk - 1)
            return (k, 0) if by_rows else (0, k)

        in_specs.append(pl.BlockSpec(blk, index))
        args.append(mat)
        out_specs.append(pl.BlockSpec(blk, index))
        out_shape.append(jax.ShapeDtypeStruct(mat.shape, BF16))
        vmem_bytes += 2 * blk[0] * blk[1] * (4 + 2)
    if emit_bf16:
        d_ff_pad = n_ff * ff_tile
        cols_j = pl.BlockSpec((D_MODEL, ff_tile), lambda i, j: (0, j))
        out_specs += [cols_j, cols_j, pl.BlockSpec((ff_tile, D_MODEL), lambda i, j: (j, 0))]
        out_shape += ([jax.ShapeDtypeStruct((D_MODEL, d_ff_pad), BF16)] * 2
                      + [jax.ShapeDtypeStruct((d_ff_pad, D_MODEL), BF16)])
    outs = pl.pallas_call(
        functools.partial(_ffn_body, n_ff=n_ff, overlap=n_ff * ff_tile - D_FF, final_norm=final_gain is not None,
                          n_cast=len(cast_later), emit_bf16=emit_bf16),
        grid=(n // tm, n_ff),
        in_specs=in_specs,
        out_specs=out_specs,
        out_shape=out_shape,
        scratch_shapes=[pltpu.VMEM((tm, D_MODEL), BF16)],
        compiler_params=_params(("arbitrary", "arbitrary"), vmem_bytes // MIB + 4),
        name="ffn",
    )(*args)
    return outs[0] if len(outs) == 1 else outs


def _group_rms(acc, gain_ref, width, scale, put):
    lane = lax.broadcasted_iota(jnp.int32, (1, LANES), 1)
    low = lane < width
    for c in range(acc.shape[1] // LANES):
        cols = slice(c * LANES, (c + 1) * LANES)
        xc = acc[:, cols]
        sq = xc * xc
        if width == LANES:
            r = lax.rsqrt(jnp.mean(sq, axis=-1, keepdims=True) + EPS)
        else:
            s_lo = jnp.sum(jnp.where(low, sq, 0.0), axis=-1, keepdims=True)
            s_hi = jnp.sum(jnp.where(low, 0.0, sq), axis=-1, keepdims=True)
            r = jnp.where(low, lax.rsqrt(s_lo / width + EPS), lax.rsqrt(s_hi / width + EPS))
        y = xc * r * gain_ref[:, cols]
        if scale != 1.0:
            y = y * scale
        put(c, y)


def _put_cols(ref):
    def put(c, y):
        ref[:, c * LANES:(c + 1) * LANES] = y.astype(ref.dtype)
    return put


def _put_heads_and_cols(heads_ref, cols_ref):
    def put(c, y):
        heads_ref[:, c, :] = y
        cols_ref[:, c * LANES:(c + 1) * LANES] = y.astype(cols_ref.dtype)
    return put


def _proj_body(x_ref, g_ref, wt_ref, wg2_ref, bg_ref, qn_ref, kn_ref, mqn_ref,
               gq_ref, gk_ref, gv_ref, gr_ref, gg_ref, dq_ref, dk_ref, dv_ref, mq_ref, dkb_ref, dvb_ref, xn_ref):
    xn_ref[...] = _rms(x_ref[...], g_ref[...]).astype(BF16)

    def cols(start, width):
        return _dot_nt(xn_ref[...], wt_ref[start:start + width, :])

    gq_ref[...] = cols(_W_GQ, GLA_QK) * (GLA_DK ** -0.5)
    gk_ref[...] = cols(_W_GK, GLA_QK)
    gv_ref[...] = cols(_W_GV, GLA_V).astype(BF16)
    gr_ref[...] = cols(_W_GR, GLA_V)
    glr = cols(_W_GLR, LANES)
    z = _dot(glr.astype(BF16), wg2_ref[...]) + bg_ref[...]
    gg_ref[...] = jax.nn.log_sigmoid(z) / GLA_GATE_NORM
    _group_rms(cols(_W_DQ, DIFF_QK), qn_ref, DIFF_DH, DIFF_DH ** -0.5, _put_cols(dq_ref))
    _group_rms(cols(_W_DK, DIFF_QK), kn_ref, DIFF_DH, 1.0, _put_heads_and_cols(dk_ref, dkb_ref))
    dv = cols(_W_DV, DIFF_V)
    put_v = _put_heads_and_cols(dv_ref, dvb_ref)
    for h in range(HEADS):
        put_v(h, dv[:, h * LANES:(h + 1) * LANES])
    _group_rms(cols(_W_MQ, MEM_W), mqn_ref, MEM_DH, 1.0, _put_cols(mq_ref))


def _proj(x, gain, wt, wg2, bg, qn, kn, mqn):
    n = x.shape[0]
    tm = min(PROJ_TILE, n)
    const = lambda shape: pl.BlockSpec(shape, lambda i: (0, 0))
    out = lambda width: pl.BlockSpec((tm, width), lambda i: (i, 0))
    heads = pl.BlockSpec((tm, HEADS, LANES), lambda i: (i, 0, 0))
    shp = lambda width, dt: jax.ShapeDtypeStruct((n, width), dt)
    shp_heads = jax.ShapeDtypeStruct((n, HEADS, LANES), F32)
    return pl.pallas_call(
        _proj_body,
        grid=(n // tm,),
        in_specs=[
            pl.BlockSpec((tm, D_MODEL), lambda i: (i, 0)),
            const((1, D_MODEL)),
            pl.BlockSpec((IN_WIDTH, D_MODEL), lambda i: (0, 0), pipeline_mode=pl.Buffered(1)),
            const((LANES, GLA_QK)),
            const((1, GLA_QK)),
            const((1, DIFF_QK)),
            const((1, DIFF_QK)),
            const((1, MEM_W)),
        ],
        out_specs=[out(GLA_QK), out(GLA_QK), out(GLA_V), out(GLA_V), out(GLA_QK),
                   out(DIFF_QK), heads, heads, out(MEM_W), out(DIFF_QK), out(DIFF_V)],
        out_shape=[shp(GLA_QK, F32), shp(GLA_QK, F32), shp(GLA_V, BF16), shp(GLA_V, F32), shp(GLA_QK, F32),
                   shp(DIFF_QK, BF16), shp_heads, shp_heads, shp(MEM_W, BF16), shp(DIFF_QK, BF16), shp(DIFF_V, BF16)],
        scratch_shapes=[pltpu.VMEM((tm, D_MODEL), BF16)],
        compiler_params=_params(("parallel",)),
        name="proj",
    )(x, gain, wt, wg2, bg, qn, kn, mqn)


def _memkv_body(x_ref, g_ref, w_ref, kn_ref, k_ref, v_ref, kb_ref, vb_ref):
    xn = _rms(x_ref[...], g_ref[...]).astype(BF16)
    _group_rms(_dot(xn, w_ref[:, :MEM_W]), kn_ref, MEM_DH, 1.0, _put_heads_and_cols(k_ref, kb_ref))
    v = _dot(xn, w_ref[:, MEM_W:])
    put_v = _put_heads_and_cols(v_ref, vb_ref)
    for h in range(HEADS):
        put_v(h, v[:, h * LANES:(h + 1) * LANES])


def _memkv(mem, gain, w, kn):
    n = mem.shape[0]
    tm = min(TOK_TILE, n)
    const = lambda shape: pl.BlockSpec(shape, lambda i: (0, 0))
    heads = pl.BlockSpec((tm, HEADS, LANES), lambda i: (i, 0, 0))
    dense = pl.BlockSpec((tm, MEM_W), lambda i: (i, 0))
    return pl.pallas_call(
        _memkv_body,
        grid=(n // tm,),
        in_specs=[pl.BlockSpec((tm, D_MODEL), lambda i: (i, 0)), const((1, D_MODEL)),
                  const((D_MODEL, 2 * MEM_W)), const((1, MEM_W))],
        out_specs=[heads, heads, dense, dense],
        out_shape=[jax.ShapeDtypeStruct((n, HEADS, LANES), F32)] * 2 + [jax.ShapeDtypeStruct((n, MEM_W), BF16)] * 2,
        compiler_params=_params(("parallel",)),
        name="memkv",
    )(mem, gain, w, kn)


def _split3(x):
    hi = x.astype(BF16)
    r1 = x - hi.astype(F32)
    mid = r1.astype(BF16)
    lo = (r1 - mid.astype(F32)).astype(BF16)
    return hi, mid, lo


def _gla_body(*refs, chunk, n_chunks, n_steps, has_state):
    if has_state:
        gq_ref, gk_ref, gg_ref, gv_ref, gr_ref, on_ref, s0_ref, go_ref, st_ref, state = refs
    else:
        gq_ref, gk_ref, gg_ref, gv_ref, gr_ref, on_ref, go_ref, st_ref, state = refs
    t = pl.program_id(1)

    @pl.when(t == 0)
    def _():
        for h in range(GLA_HEADS):
            if has_state:
                state[h] = s0_ref[0, h].T
            else:
                state[h] = jnp.zeros((GLA_DV, GLA_DK), F32)

    row = lax.broadcasted_iota(jnp.int32, (chunk, chunk), 0)
    col = lax.broadcasted_iota(jnp.int32, (chunk, chunk), 1)
    causal = row >= col
    tril = causal.astype(BF16)

    chunks = [slice(c * chunk, (c + 1) * chunk) for c in range(n_chunks)]
    kcs = [slice(h * GLA_DK, (h + 1) * GLA_DK) for h in range(GLA_HEADS)]
    vcs = [slice(h * GLA_DV, (h + 1) * GLA_DV) for h in range(GLA_HEADS)]

    bs = []
    for rows in chunks:
        g_hi, g_mid, g_lo = _split3(gg_ref[rows, :])
        bs.append(_dot(tril, g_hi) + _dot(tril, g_mid) + _dot(tril, g_lo))

    qes, kes, kds, decays = [], [], [], []
    for rows, b in zip(chunks, bs):
        b_last = b[chunk - 1:chunk, :]
        q = gq_ref[rows, :]
        k = gk_ref[rows, :]
        qes.append((q * jnp.exp(b)).astype(BF16))
        kes.append((k * jnp.exp(-b)).astype(BF16))
        kds.append((k * jnp.exp(b_last - b)).astype(BF16))
        decays.append(jnp.exp(b_last))

    a_s = [[jnp.where(causal, _dot_nt(qe[:, kc], ke[:, kc]), 0.0).astype(BF16) for kc in kcs]
           for qe, ke in zip(qes, kes)]
    incs = [[_dot_tn(gv_ref[rows, vc], kd[:, kc]) for kc, vc in zip(kcs, vcs)] for rows, kd in zip(chunks, kds)]

    s_in = []
    s_cur = [state[h] for h in range(GLA_HEADS)]
    for c in range(n_chunks):
        s_in.append([s.astype(BF16) for s in s_cur])
        s_cur = [s * decays[c][:, kc] + inc for s, kc, inc in zip(s_cur, kcs, incs[c])]
    for h in range(GLA_HEADS):
        state[h] = s_cur[h]

    for c, rows in enumerate(chunks):
        for h, (kc, vc) in enumerate(zip(kcs, vcs)):
            o = _dot_nt(qes[c][:, kc], s_in[c][h]) + _dot(a_s[c][h], gv_ref[rows, vc])
            go_ref[rows, vc] = (_rms(o, on_ref[...]) * jax.nn.silu(gr_ref[rows, vc])).astype(BF16)

    @pl.when(t == n_steps - 1)
    def _():
        for h in range(GLA_HEADS):
            st_ref[0, h] = state[h].T


def _gla(gq, gk, gg, gv, gr, onorm, s0, batch, seq, chunk):
    tt = min(GLA_TILE, seq)
    n_steps = seq // tt
    tok = lambda width: pl.BlockSpec((tt, width), lambda b, t: (b * n_steps + t, 0))
    st_spec = pl.BlockSpec((1, GLA_HEADS, GLA_DK, GLA_DV), lambda b, t: (b, 0, 0, 0))
    in_specs = [tok(GLA_QK), tok(GLA_QK), tok(GLA_QK), tok(GLA_V), tok(GLA_V),
                pl.BlockSpec((1, GLA_DV), lambda b, t: (0, 0))]
    args = [gq, gk, gg, gv, gr, onorm]
    if s0 is not None:
        in_specs.append(st_spec)
        args.append(s0)
    return pl.pallas_call(
        functools.partial(_gla_body, chunk=chunk, n_chunks=tt // chunk, n_steps=n_steps, has_state=s0 is not None),
        grid=(batch, n_steps),
        in_specs=in_specs,
        out_specs=[tok(GLA_V), st_spec],
        out_shape=[jax.ShapeDtypeStruct((batch * seq, GLA_V), BF16),
                   jax.ShapeDtypeStruct((batch, GLA_HEADS, GLA_DK, GLA_DV), F32)],
        scratch_shapes=[pltpu.VMEM((GLA_HEADS, GLA_DV, GLA_DK), F32)],
        compiler_params=_params(("parallel", "arbitrary")),
        name="gla",
    )(*args)


def _t5_bucket(rel):
    nb = REL_BUCKETS // 2
    max_exact = nb // 2
    ret = jnp.where(rel > 0, nb, 0)
    n = jnp.abs(rel)
    nf = jnp.maximum(n, 1).astype(F32)
    large = max_exact + (jnp.log(nf / max_exact) / math.log(REL_MAX_DIST / max_exact)
                         * (nb - max_exact)).astype(jnp.int32)
    large = jnp.minimum(large, nb - 1)
    return ret + jnp.where(n < max_exact, n, large)


def _bucket_tile(q_pos, k_pos):
    visible = (k_pos[None, :] // CHUNK) <= (q_pos[:, None] // CHUNK)
    return jnp.where(visible, _t5_bucket(k_pos[None, :] - q_pos[:, None]), MASKED_BUCKET).astype(jnp.int32)


def _bias_from_buckets(idx, tab_ref, head):
    def step(bk, acc):
        return jnp.where(idx == bk, tab_ref[bk * DIFF_HEADS + head], acc)
    return lax.fori_loop(0, REL_BUCKETS, step, jnp.full(idx.shape, NEG_INF, F32))


def _lambda(lam_ref, lam_init):
    l = lam_ref[...]
    return (jnp.exp(jnp.sum(l[0:1] * l[1:2], axis=-1, keepdims=True))
            - jnp.exp(jnp.sum(l[2:3] * l[3:4], axis=-1, keepdims=True)) + lam_init)


def _comp_masks(q):
    lane = lax.broadcasted_iota(jnp.int32, q.shape, 1)
    zero = jnp.zeros_like(q)
    return jnp.where(lane < DIFF_DH, q, zero), jnp.where(lane < DIFF_DH, zero, q)


def _diff_finish(o0, o1, lam, on_ref, lam_init, out_dtype):
    o = o0 - lam * o1
    return (_rms(o, on_ref[...]) * (1.0 - lam_init)).astype(out_dtype)


def _scores(q_c, parts):
    return [_dot_nt(q_c, k) + bias if jnp.ndim(bias) == 2 else _dot_nt(q_c, k) for k, _, bias in parts]


def _softmax_pv(scores, parts):
    shifts = [0.0 if jnp.ndim(bias) == 2 else bias for _, _, bias in parts]
    m = None
    for s, shift in zip(scores, shifts):
        part_max = s.max(axis=-1, keepdims=True) + shift
        m = part_max if m is None else jnp.maximum(m, part_max)
    ol = 0.0
    for s, shift, (_, v1, _) in zip(scores, shifts, parts):
        ol = ol + _dot(jnp.exp(s - (m - shift)).astype(BF16), v1)
    return ol[:, :DIFF_DV] * (1.0 / ol[:, DIFF_DV:])


def _diff_prompt_body(tab_ref, q_ref, kb, vb, idx_ref, lam_ref, on_ref, o_ref, bias, v1, *, seq, lam_init):
    b = pl.program_id(0)
    h = pl.program_id(1)

    @pl.when(b == 0)
    def _():
        for t in range(2):
            bias[h, t] = _bias_from_buckets(idx_ref[t], tab_ref, h)

    v1[:, :DIFF_DV] = vb[...]
    v1[:, DIFF_DV:] = jnp.ones((seq, LANES), BF16)
    lam = _lambda(lam_ref, lam_init)
    far_bias = tab_ref[(REL_BUCKETS // 2 - 1) * DIFF_HEADS + h]

    def key_parts(i):
        near0 = max(i - 1, 0) * Q_BLOCK
        near = slice(near0, near0 + 2 * Q_BLOCK)
        parts = [(kb[near, :], v1[near, :], bias[h, min(i, 1)])]
        if near0 > 0:
            parts.append((kb[0:near0, :], v1[0:near0, :], far_bias))
        return parts

    items = [(i, c) for i in range(seq // Q_BLOCK) for c in range(2)]
    comps = {}

    def scores_of(item):
        i, c = item
        if i not in comps:
            comps[i] = _comp_masks(q_ref[i * Q_BLOCK:(i + 1) * Q_BLOCK, :])
        return _scores(comps[i][c], key_parts(i))

    outs = {}
    ahead = scores_of(items[0])
    for n, (i, c) in enumerate(items):
        cur = ahead
        if n + 1 < len(items):
            ahead = scores_of(items[n + 1])
        outs[c] = _softmax_pv(cur, key_parts(i))
        if c == 1:
            o_ref[i * Q_BLOCK:(i + 1) * Q_BLOCK, :] = _diff_finish(outs[0], outs[1], lam, on_ref, lam_init,
                                                                    o_ref.dtype)


def _diff_prompt(table, dq, dk, dv, lam_p, onorm, batch, seq, lam_init):
    q_pos = jnp.arange(Q_BLOCK)
    idx = jnp.stack([_bucket_tile(q_pos, jnp.arange(2 * Q_BLOCK)),
                     _bucket_tile(q_pos + Q_BLOCK, jnp.arange(2 * Q_BLOCK))])
    head = lambda: pl.BlockSpec((seq, LANES), lambda b, h: (b, h))
    return pl.pallas_call(
        functools.partial(_diff_prompt_body, seq=seq, lam_init=lam_init),
        grid=(batch, DIFF_HEADS),
        in_specs=[
            pl.BlockSpec(memory_space=pltpu.SMEM),
            head(), head(), head(),
            pl.BlockSpec((2, Q_BLOCK, 2 * Q_BLOCK), lambda b, h: (0, 0, 0)),
            pl.BlockSpec((4, DIFF_DH), lambda b, h: (0, 0)),
            pl.BlockSpec((1, DIFF_DV), lambda b, h: (0, 0)),
        ],
        out_specs=head(),
        out_shape=jax.ShapeDtypeStruct((batch * seq, DIFF_V), BF16),
        scratch_shapes=[pltpu.VMEM((DIFF_HEADS, 2, Q_BLOCK, 2 * Q_BLOCK), F32),
                        pltpu.VMEM((seq, DIFF_DV + LANES), BF16)],
        compiler_params=_params(("arbitrary", "arbitrary")),
        name="diff_prompt",
    )(table, dq, dk, dv, idx, lam_p, onorm)


def _diff_sample_body(tab_ref, q_ref, k_ref, v_ref, ck_ref, cv_ref, idxc_ref, idxn_ref, lam_ref, on_ref, o_ref,
                      bias_c, bias_n, *, seq, lam_init):
    groups = [(h, c) for h in range(DIFF_HEADS) for c in range(2)]

    @pl.when(pl.program_id(0) == 0)
    def _():
        for h in range(DIFF_HEADS):
            tile = _bias_from_buckets(idxc_ref[h], tab_ref, h)
            for g in (2 * h, 2 * h + 1):
                bias_c[g * seq:(g + 1) * seq, :] = tile
            bias_n[h] = _bias_from_buckets(idxn_ref[...], tab_ref, h)

    lam = _lambda(lam_ref, lam_init)
    head_cols = [slice(h * LANES, (h + 1) * LANES) for h in range(DIFF_HEADS)]
    q_groups = [q_c for cols in head_cols for q_c in _comp_masks(q_ref[:, cols])]
    s_old = _dot_nt(jnp.concatenate(q_groups, axis=0), ck_ref[...].astype(BF16)) + bias_c[...]
    s_new = jnp.concatenate([_dot_nt(q_g, k_ref[:, head_cols[h]]) + bias_n[h]
                             for q_g, (h, _) in zip(q_groups, groups)], axis=0)
    m = jnp.maximum(s_old.max(axis=-1, keepdims=True), s_new.max(axis=-1, keepdims=True))
    e_old = jnp.exp(s_old - m)
    e_new = jnp.exp(s_new - m)
    l = e_old.sum(axis=-1, keepdims=True) + e_new.sum(axis=-1, keepdims=True)
    o_old = _dot(e_old.astype(BF16), cv_ref[...].astype(BF16))
    e_new = e_new.astype(BF16)
    o_new = jnp.concatenate([_dot(e_new[g * seq:(g + 1) * seq, :], v_ref[:, head_cols[h]])
                             for g, (h, _) in enumerate(groups)], axis=0)
    o = (o_old + o_new) * (1.0 / l)
    for h, cols in enumerate(head_cols):
        o0 = o[(2 * h) * seq:(2 * h + 1) * seq, :]
        o1 = o[(2 * h + 1) * seq:(2 * h + 2) * seq, :]
        o_ref[:, cols] = _diff_finish(o0, o1, lam, on_ref, lam_init, o_ref.dtype)


def _diff_sample(table, dq, dk, dv, cache_k, cache_v, lam_p, onorm, batch, seq, past, lam_init):
    q_pos = past + jnp.arange(seq)
    idx_n = _bucket_tile(q_pos, past + jnp.arange(seq))
    idx_rows = jnp.repeat(_bucket_tile(q_pos, jnp.arange(past)), HEADS, axis=1)
    row_head = jnp.arange(past * HEADS) % HEADS
    idx_c = jnp.where(row_head[None, None, :] == jnp.arange(HEADS)[:, None, None], idx_rows[None], MASKED_BUCKET)
    new_q = pl.BlockSpec((seq, DIFF_QK), lambda b: (b, 0))
    old = lambda: pl.BlockSpec((past * HEADS, LANES), lambda b: (b, 0))
    const = lambda shape: pl.BlockSpec(shape, lambda b: (0,) * len(shape))
    n_groups = 2 * DIFF_HEADS
    return pl.pallas_call(
        functools.partial(_diff_sample_body, seq=seq, lam_init=lam_init),
        grid=(batch,),
        in_specs=[pl.BlockSpec(memory_space=pltpu.SMEM), new_q, new_q, new_q, old(), old(),
                  const((HEADS, seq, past * HEADS)), const((seq, seq)), const((4, DIFF_DH)), const((1, DIFF_DV))],
        out_specs=new_q,
        out_shape=jax.ShapeDtypeStruct((batch * seq, DIFF_V), BF16),
        scratch_shapes=[pltpu.VMEM((n_groups * seq, past * HEADS), F32), pltpu.VMEM((DIFF_HEADS, seq, seq), F32)],
        compiler_params=_params(("arbitrary",)),
        name="diff_sample",
    )(table, dq, dk, dv, cache_k, cache_v, idx_c, idx_n, lam_p, onorm)


def _softmax(s):
    e = jnp.exp(s - s.max(axis=-1, keepdims=True))
    return (e * (1.0 / e.sum(axis=-1, keepdims=True))).astype(BF16)


def _memattn_body(q_ref, k_ref, v_ref, o_ref, *, interleaved):
    head_cols = [slice(h * MEM_DH, (h + 1) * MEM_DH) for h in range(MEM_HEADS)]
    scale = MEM_DH ** -0.5
    if interleaved:
        tq = q_ref.shape[0]
        s = _dot_nt(jnp.concatenate([q_ref[:, cols] for cols in head_cols], axis=0), k_ref[...].astype(BF16)) * scale
        q_head = lax.broadcasted_iota(jnp.int32, s.shape, 0) // tq
        k_head = lax.broadcasted_iota(jnp.int32, s.shape, 1) % MEM_HEADS
        o = _dot(_softmax(jnp.where(q_head == k_head, s, NEG_INF)), v_ref[...].astype(BF16))
        for h, cols in enumerate(head_cols):
            o_ref[:, cols] = o[h * tq:(h + 1) * tq, :].astype(o_ref.dtype)
    else:
        scores = [_dot_nt(q_ref[:, cols], k_ref[:, cols]) * scale for cols in head_cols]
        probs = [_softmax(s) for s in scores]
        for cols, p in zip(head_cols, probs):
            o_ref[:, cols] = _dot(p, v_ref[:, cols]).astype(o_ref.dtype)


def _memattn(mq, mk, mv, batch, seq, n_mem):
    tq = min(MEMATTN_TILE, seq)
    nq = seq // tq
    q_spec = pl.BlockSpec((tq, MEM_W), lambda b, i: (b * nq + i, 0))
    interleaved = mk.shape[1] == LANES
    kv_spec = pl.BlockSpec((mk.shape[0] // batch, mk.shape[1]), lambda b, i: (b, 0))
    return pl.pallas_call(
        functools.partial(_memattn_body, interleaved=interleaved),
        grid=(batch, nq),
        in_specs=[q_spec, kv_spec, kv_spec],
        out_specs=q_spec,
        out_shape=jax.ShapeDtypeStruct((batch * seq, MEM_W), BF16),
        compiler_params=_params(("parallel", "arbitrary")),
        name="memattn",
    )(mq, mk, mv)


def _outproj_body(x_ref, g_ref, d_ref, m_ref, wg_ref, wd_ref, wm_ref, o_ref):
    o_ref[...] = (x_ref[...] + _dot(g_ref[...], wg_ref[...]) + _dot(d_ref[...], wd_ref[...])
                  + _dot(m_ref[...], wm_ref[...]))


def _outproj(x, g, d, m, wo):
    n = x.shape[0]
    tm = min(OUTPROJ_TILE, n)
    tok = lambda width: pl.BlockSpec((tm, width), lambda i: (i, 0))
    w_rows = lambda rows, blk: pl.BlockSpec((rows, D_MODEL), lambda i: (blk, 0), pipeline_mode=pl.Buffered(1))
    vmem_bytes = 5 * tm * D_MODEL * 4 + 2 * tm * D_MODEL * 2 + D_MODEL * D_MODEL * 2
    return pl.pallas_call(
        _outproj_body,
        grid=(n // tm,),
        in_specs=[tok(D_MODEL), tok(GLA_V), tok(DIFF_V), tok(MEM_W),
                  w_rows(GLA_V, 0), w_rows(DIFF_V, GLA_V // DIFF_V), w_rows(MEM_W, (GLA_V + DIFF_V) // MEM_W)],
        out_specs=tok(D_MODEL),
        out_shape=jax.ShapeDtypeStruct((n, D_MODEL), F32),
        compiler_params=_params(("parallel",), vmem_bytes // MIB + 4),
        name="outproj",
    )(x, g, d, m, wo, wo, wo)


def _row(v):
    return v.reshape(1, -1).astype(F32)


def kernel(x_prompt, x_sample, mem_prompt, cache_diff_k, cache_diff_v, state_gla, cache_mem_k, cache_mem_v, rel_bias_table, norm_ffn1, w_ffn1_in, w_ffn1_out, norm_mix, w_in, w_gla_g2, b_gla_g, gla_out_norm, diff_q_norm, diff_k_norm, diff_lambda, diff_out_norm, mem_norm, w_mem_kv, mem_q_norm, mem_k_norm, w_o, norm_ffn2, w_ffn2_in, w_ffn2_out, norm_final):
    depth = norm_ffn1.shape[0]
    assert depth == 1, "single-layer step"
    layer = 0
    batch, seq, _ = x_prompt.shape
    dec_batch, dec_seq, _ = x_sample.shape
    past = cache_diff_k.shape[2]
    n_mem = mem_prompt.shape[1]
    lam_init = 0.8 - 0.6 * math.exp(-0.3 * layer)

    w_g2 =jnp.pad(w_gla_g2[layer].astype(BF16), ((0, LANES - GLA_GATE_RANK), (0, 0)))
    b_g = _row(b_gla_g[layer])
    qn = _row(jnp.tile(diff_q_norm[layer], DIFF_QK // DIFF_DH))
    kn = _row(jnp.tile(diff_k_norm[layer], DIFF_QK // DIFF_DH))
    mqn = _row(jnp.tile(mem_q_norm[layer], MEM_HEADS))
    mkn = _row(jnp.tile(mem_k_norm[layer], MEM_HEADS))
    table = rel_bias_table.astype(F32).reshape(-1)
    lam_p = diff_lambda[layer].astype(F32)
    gla_on = _row(gla_out_norm[layer])
    diff_on = _row(diff_out_norm[layer])

    mk, mv, mk_b, mv_b = _memkv(mem_prompt.reshape(batch * n_mem, D_MODEL), _row(mem_norm[layer]),
                                w_mem_kv[layer].astype(BF16), mkn)

    n1 = _row(norm_ffn1[layer])
    xs1, *ffn1 = _ffn(x_sample.reshape(dec_batch * dec_seq, D_MODEL), n1, w_ffn1_in[layer], w_ffn1_in[layer],
                      w_ffn1_out[layer], up_col0=D_FF, ff_tile=FF_TILE_F32, emit_bf16=True)
    n_in_blocks = w_ffn2_in.shape[2] // LANES
    xp1, w2_in, w2_out, w_t, wo = _ffn(
        x_prompt.reshape(batch * seq, D_MODEL), n1, *ffn1, up_col0=0,
        cast_later=[(w_ffn2_in[layer], (D_MODEL, LANES), 0),
                    (w_ffn2_out[layer], (D_FF // n_in_blocks, D_MODEL), n_in_blocks),
                    (jnp.swapaxes(w_in[layer], 0, 1), (CAST_ROWS_W_IN, D_MODEL), 0),
                    (w_o[layer], (CAST_ROWS_W_O, D_MODEL), 0)])
    ffn2 = (w2_in, w2_in, w2_out)

    def layer_fn(x, b, t, chunk, s0, diff_fn, mem_k, mem_v):
        gq, gk, gv, gr, gg, dq, dk, dv, mq, dk_b, dv_b = _proj(x, _row(norm_mix[layer]), w_t, w_g2, b_g,
                                                               qn, kn, mqn)
        g_out, g_state = _gla(gq, gk, gg, gv, gr, gla_on, s0, b, t, chunk)
        d_out = diff_fn(dq, dk_b, dv_b)
        m_out = _memattn(mq, mem_k, mem_v, b, t, n_mem)
        x = _outproj(x, g_out, d_out, m_out, wo)
        x = _ffn(x, _row(norm_ffn2[layer]), *ffn2, up_col0=D_FF, final_gain=_row(norm_final[layer]))
        return x, dk, dv, g_state

    yp, dk_p, dv_p, g_p = layer_fn(
        xp1, batch, seq, CHUNK, None,
        lambda dq, dk, dv: _diff_prompt(table, dq, dk, dv, lam_p, diff_on, batch, seq, lam_init), mk_b, mv_b)
    ys, dk_s, dv_s, g_s = layer_fn(
        xs1, dec_batch, dec_seq, dec_seq, state_gla[layer],
        lambda dq, dk, dv: _diff_sample(table, dq, dk, dv, cache_diff_k[layer].reshape(-1, LANES),
                                        cache_diff_v[layer].reshape(-1, LANES),
                                        lam_p, diff_on, dec_batch, dec_seq, past, lam_init),
        cache_mem_k[layer].reshape(-1, LANES), cache_mem_v[layer].reshape(-1, LANES))

    head4 = lambda a, b, t: a.reshape(1, b, t, HEADS, LANES)
    return (yp.reshape(batch, seq, D_MODEL), ys.reshape(dec_batch, dec_seq, D_MODEL),
            head4(dk_p, batch, seq), head4(dv_p, batch, seq), g_p[None],
            head4(mk, batch, n_mem), head4(mv, batch, n_mem),
            head4(dk_s, dec_batch, dec_seq), head4(dv_s, dec_batch, dec_seq), g_s[None])
```

```python
import functools
import math

import jax
import jax.numpy as jnp
from jax import lax
from jax.experimental import pallas as pl
from jax.experimental.pallas import tpu as pltpu

F32 = jnp.float32
BF16 = jnp.bfloat16

D_MODEL = 2048
CHUNK = 64
EPS = 1e-6
NEG_INF = -1e30
GLA_HEADS, GLA_DK, GLA_DV, GLA_GATE_RANK, GLA_GATE_NORM = 4, 128, 256, 16, 16.0
DIFF_HEADS, DIFF_DH, DIFF_DV = 4, 64, 128
MEM_HEADS, MEM_DH = 4, 128
REL_BUCKETS, REL_MAX_DIST = 32, 128
D_FF = 5504
GLA_QK = GLA_HEADS * GLA_DK
GLA_V = GLA_HEADS * GLA_DV
DIFF_QK = DIFF_HEADS * 2 * DIFF_DH
DIFF_V = DIFF_HEADS * DIFF_DV
MEM_W = MEM_HEADS * MEM_DH
GLR_OFF = 2 * GLA_QK + 2 * GLA_V

LANES = 128
HEADS = 4
FF_TILE = 512
FF_TILE_F32 = 256
FFN_TOK_TILE = 1024
TOK_TILE = 512
GLA_TILE = 1024
MEMATTN_TILE = 2048
PROJ_TILE = 256
Q_BLOCK = 256
MASKED_BUCKET = REL_BUCKETS
MIB = 1024 * 1024
VMEM_CAP_MIB = 60
CAST_ROWS_W_IN = 48
CAST_ROWS_W_O = 16

_W_GQ, _W_GK, _W_GV, _W_GR, _W_GLR = 0, GLA_QK, 2 * GLA_QK, 2 * GLA_QK + GLA_V, GLR_OFF
_W_DQ = GLR_OFF + GLA_GATE_RANK
_W_DK = _W_DQ + DIFF_QK
_W_DV = _W_DK + DIFF_QK
_W_MQ = _W_DV + DIFF_V
IN_WIDTH = _W_MQ + MEM_W


def _dot(a, b):
    return jnp.dot(a, b, preferred_element_type=F32)


def _dot_nt(a, b):
    return lax.dot_general(a, b, (((1,), (1,)), ((), ())), preferred_element_type=F32)


def _dot_tn(a, b):
    return lax.dot_general(a, b, (((0,), (0,)), ((), ())), preferred_element_type=F32)


def _rms(x, gain):
    return x * lax.rsqrt(jnp.mean(x * x, axis=-1, keepdims=True) + EPS) * gain


def _params(sem, vmem_mib=48):
    return pltpu.CompilerParams(dimension_semantics=sem, vmem_limit_bytes=min(vmem_mib, VMEM_CAP_MIB) * MIB)


def _ffn_body(*refs, n_ff, overlap, final_norm, n_cast, emit_bf16):
    x_ref, g_ref, wg_ref, wu_ref, wo_ref = refs[:5]
    refs = refs[5:]
    if final_norm:
        fg_ref, refs = refs[0], refs[1:]
    cast_src, refs = refs[:n_cast], refs[n_cast:]
    o_ref, refs = refs[0], refs[1:]
    cast_dst, refs = refs[:n_cast], refs[n_cast:]
    if emit_bf16:
        (wg_out, wu_out, wo_out), refs = refs[:3], refs[3:]
    (xn_ref,) = refs
    j = pl.program_id(1)

    def tile(lo):
        for src, dst in zip(cast_src, cast_dst):
            dst[...] = src[...].astype(BF16)
        xn = xn_ref[...]
        if emit_bf16:
            def front(a, axis):
                a = a.astype(BF16)
                if lo == 0:
                    return a
                return jnp.concatenate([lax.slice_in_dim(a, lo, None, axis=axis),
                                        lax.slice_in_dim(a, 0, lo, axis=axis)], axis=axis)
            wg_out[...] = front(wg_ref[...], 1)
            wu_out[...] = front(wu_ref[...], 1)
            wo_out[...] = front(wo_ref[...], 0)
            live = wg_out.shape[1] - lo
            gate = _dot(xn, wg_out[:, :live])
            up = _dot(xn, wu_out[:, :live])
            act = (jax.nn.silu(gate) * up).astype(BF16)
            return 0.5 * _dot(act, wo_out[:live, :])
        gate = _dot(xn, wg_ref[:, lo:])
        up = _dot(xn, wu_ref[:, lo:])
        act = (jax.nn.silu(gate) * up).astype(BF16)
        return 0.5 * _dot(act, wo_ref[lo:, :])

    @pl.when(j == 0)
    def _():
        xn_ref[...] = _rms(x_ref[...], g_ref[...]).astype(BF16)
        o_ref[...] = x_ref[...] + tile(0)

    @pl.when((j > 0) & (j < n_ff - 1))
    def _():
        o_ref[...] += tile(0)

    @pl.when(j == n_ff - 1)
    def _():
        o = o_ref[...] + tile(overlap)
        o_ref[...] = _rms(o, fg_ref[...]) if final_norm else o


def _ffn(x, gain, w_gate, w_up, w_out, *, up_col0, final_gain=None, cast_later=(), ff_tile=FF_TILE,
         emit_bf16=False):
    n = x.shape[0]
    tm = min(FFN_TOK_TILE, n)
    n_ff = -(-D_FF // ff_tile)
    row = pl.BlockSpec((1, D_MODEL), lambda i, j: (0, 0))
    start = lambda j: jnp.minimum(j * (ff_tile // LANES), (D_FF - ff_tile) // LANES)
    w_in_tile = (pl.Element(D_MODEL), pl.Element(ff_tile))
    w_out_tile = (pl.Element(ff_tile), pl.Element(D_MODEL))
    gate_index = lambda i, j: (0, start(j) * LANES)
    out_index = lambda i, j: (start(j) * LANES, 0)
    tok = pl.BlockSpec((tm, D_MODEL), lambda i, j: (i, 0))
    in_specs = [
        tok,
        row,
        pl.BlockSpec(w_in_tile, gate_index),
        pl.BlockSpec(w_in_tile, lambda i, j: (0, (up_col0 // LANES + start(j)) * LANES)),
        pl.BlockSpec(w_out_tile, out_index),
    ]
    args = [x, gain, w_gate, w_up, w_out]
    if final_gain is not None:
        in_specs.append(row)
        args.append(final_gain)
    w_tile_bytes = D_MODEL * ff_tile * ((4 + 2) if emit_bf16 else 2)
    vmem_bytes = 4 * tm * D_MODEL * 4 + tm * D_MODEL * 2 + 6 * w_tile_bytes + 3 * tm * ff_tile * 4
    out_specs = [pl.BlockSpec((tm, D_MODEL), lambda i, j: (i, 0))]
    out_shape = [jax.ShapeDtypeStruct((n, D_MODEL), F32)]
    n_steps = (n // tm) * n_ff
    for mat, blk, first in cast_later:
        rows, cols = mat.shape
        assert rows % blk[0] == 0 and cols % blk[1] == 0 and (blk[0] == rows or blk[1] == cols)
        n_blk = (rows // blk[0]) * (cols // blk[1])
        assert first + n_blk <= n_steps
        by_rows = blk[1] == cols

        def index(i, j, first=first, n_blk=n_blk, by_rows=by_rows):
            k = jnp.clip(i * n_ff + j - first, 0, n_blk - 1)
            return (k, 0) if by_rows else (0, k)

        in_specs.append(pl.BlockSpec(blk, index))
        args.append(mat)
        out_specs.append(pl.BlockSpec(blk, index))
        out_shape.append(jax.ShapeDtypeStruct(mat.shape, BF16))
        vmem_bytes += 2 * blk[0] * blk[1] * (4 + 2)
    if emit_bf16:
        d_ff_pad = n_ff * ff_tile
        cols_j = pl.BlockSpec((D_MODEL, ff_tile), lambda i, j: (0, j))
        out_specs += [cols_j, cols_j, pl.BlockSpec((ff_tile, D_MODEL), lambda i, j: (j, 0))]
        out_shape += ([jax.ShapeDtypeStruct((D_MODEL, d_ff_pad), BF16)] * 2
                      + [jax.ShapeDtypeStruct((d_ff_pad, D_MODEL), BF16)])
    outs = pl.pallas_call(
        functools.partial(_ffn_body, n_ff=n_ff, overlap=n_ff * ff_tile - D_FF, final_norm=final_gain is not None,
                          n_cast=len(cast_later), emit_bf16=emit_bf16),
        grid=(n // tm, n_ff),
        in_specs=in_specs,
        out_specs=out_specs,
        out_shape=out_shape,
        scratch_shapes=[pltpu.VMEM((tm, D_MODEL), BF16)],
        compiler_params=_params(("arbitrary", "arbitrary"), vmem_bytes // MIB + 4),
        name="ffn",
    )(*args)
    return outs[0] if len(outs) == 1 else outs


def _group_rms(acc, gain_ref, width, scale, put):
    lane = lax.broadcasted_iota(jnp.int32, (1, LANES), 1)
    low = lane < width
    for c in range(acc.shape[1] // LANES):
        cols = slice(c * LANES, (c + 1) * LANES)
        xc = acc[:, cols]
        sq = xc * xc
        if width == LANES:
            r = lax.rsqrt(jnp.mean(sq, axis=-1, keepdims=True) + EPS)
        else:
            s_lo = jnp.sum(jnp.where(low, sq, 0.0), axis=-1, keepdims=True)
            s_hi = jnp.sum(jnp.where(low, 0.0, sq), axis=-1, keepdims=True)
            r = jnp.where(low, lax.rsqrt(s_lo / width + EPS), lax.rsqrt(s_hi / width + EPS))
        y = xc * r * gain_ref[:, cols]
        if scale != 1.0:
            y = y * scale
        put(c, y)


def _put_cols(ref):
    def put(c, y):
        ref[:, c * LANES:(c + 1) * LANES] = y.astype(ref.dtype)
    return put


def _put_heads_and_cols(heads_ref, cols_ref):
    def put(c, y):
        heads_ref[:, c, :] = y
        cols_ref[:, c * LANES:(c + 1) * LANES] = y.astype(cols_ref.dtype)
    return put


def _proj_body(x_ref, g_ref, wt_ref, wg2_ref, bg_ref, qn_ref, kn_ref, mqn_ref,
               gq_ref, gk_ref, gv_ref, gr_ref, gg_ref, dq_ref, dk_ref, dv_ref, mq_ref, dkb_ref, dvb_ref, xn_ref):
    xn_ref[...] = _rms(x_ref[...], g_ref[...]).astype(BF16)

    def cols(start, width):
        return _dot_nt(xn_ref[...], wt_ref[start:start + width, :])

    gq_ref[...] = cols(_W_GQ, GLA_QK) * (GLA_DK ** -0.5)
    gk_ref[...] = cols(_W_GK, GLA_QK)
    gv_ref[...] = cols(_W_GV, GLA_V).astype(BF16)
    gr_ref[...] = cols(_W_GR, GLA_V)
    glr = cols(_W_GLR, LANES)
    z = _dot(glr.astype(BF16), wg2_ref[...]) + bg_ref[...]
    gg_ref[...] = jax.nn.log_sigmoid(z) / GLA_GATE_NORM
    _group_rms(cols(_W_DQ, DIFF_QK), qn_ref, DIFF_DH, DIFF_DH ** -0.5, _put_cols(dq_ref))
    _group_rms(cols(_W_DK, DIFF_QK), kn_ref, DIFF_DH, 1.0, _put_heads_and_cols(dk_ref, dkb_ref))
    dv = cols(_W_DV, DIFF_V)
    put_v = _put_heads_and_cols(dv_ref, dvb_ref)
    for h in range(HEADS):
        put_v(h, dv[:, h * LANES:(h + 1) * LANES])
    _group_rms(cols(_W_MQ, MEM_W), mqn_ref, MEM_DH, 1.0, _put_cols(mq_ref))


def _proj(x, gain, wt, wg2, bg, qn, kn, mqn):
    n = x.shape[0]
    tm = min(PROJ_TILE, n)
    const = lambda shape: pl.BlockSpec(shape, lambda i: (0, 0))
    out = lambda width: pl.BlockSpec((tm, width), lambda i: (i, 0))
    heads = pl.BlockSpec((tm, HEADS, LANES), lambda i: (i, 0, 0))
    shp = lambda width, dt: jax.ShapeDtypeStruct((n, width), dt)
    shp_heads = jax.ShapeDtypeStruct((n, HEADS, LANES), F32)
    return pl.pallas_call(
        _proj_body,
        grid=(n // tm,),
        in_specs=[
            pl.BlockSpec((tm, D_MODEL), lambda i: (i, 0)),
            const((1, D_MODEL)),
            pl.BlockSpec((IN_WIDTH, D_MODEL), lambda i: (0, 0), pipeline_mode=pl.Buffered(1)),
            const((LANES, GLA_QK)),
            const((1, GLA_QK)),
            const((1, DIFF_QK)),
            const((1, DIFF_QK)),
            const((1, MEM_W)),
        ],
        out_specs=[out(GLA_QK), out(GLA_QK), out(GLA_V), out(GLA_V), out(GLA_QK),
                   out(DIFF_QK), heads, heads, out(MEM_W), out(DIFF_QK), out(DIFF_V)],
        out_shape=[shp(GLA_QK, F32), shp(GLA_QK, F32), shp(GLA_V, BF16), shp(GLA_V, F32), shp(GLA_QK, F32),
                   shp(DIFF_QK, BF16), shp_heads, shp_heads, shp(MEM_W, BF16), shp(DIFF_QK, BF16), shp(DIFF_V, BF16)],
        scratch_shapes=[pltpu.VMEM((tm, D_MODEL), BF16)],
        compiler_params=_params(("parallel",)),
        name="proj",
    )(x, gain, wt, wg2, bg, qn, kn, mqn)


def _memkv_body(x_ref, g_ref, w_ref, kn_ref, k_ref, v_ref, kb_ref, vb_ref):
    xn = _rms(x_ref[...], g_ref[...]).astype(BF16)
    _group_rms(_dot(xn, w_ref[:, :MEM_W]), kn_ref, MEM_DH, 1.0, _put_heads_and_cols(k_ref, kb_ref))
    v = _dot(xn, w_ref[:, MEM_W:])
    put_v = _put_heads_and_cols(v_ref, vb_ref)
    for h in range(HEADS):
        put_v(h, v[:, h * LANES:(h + 1) * LANES])


def _memkv(mem, gain, w, kn):
    n = mem.shape[0]
    tm = min(TOK_TILE, n)
    const = lambda shape: pl.BlockSpec(shape, lambda i: (0, 0))
    heads = pl.BlockSpec((tm, HEADS, LANES), lambda i: (i, 0, 0))
    dense = pl.BlockSpec((tm, MEM_W), lambda i: (i, 0))
    return pl.pallas_call(
        _memkv_body,
        grid=(n // tm,),
        in_specs=[pl.BlockSpec((tm, D_MODEL), lambda i: (i, 0)), const((1, D_MODEL)),
                  const((D_MODEL, 2 * MEM_W)), const((1, MEM_W))],
        out_specs=[heads, heads, dense, dense],
        out_shape=[jax.ShapeDtypeStruct((n, HEADS, LANES), F32)] * 2 + [jax.ShapeDtypeStruct((n, MEM_W), BF16)] * 2,
        compiler_params=_params(("parallel",)),
        name="memkv",
    )(mem, gain, w, kn)


def _split3(x):
    hi = x.astype(BF16)
    r1 = x - hi.astype(F32)
    mid = r1.astype(BF16)
    lo = (r1 - mid.astype(F32)).astype(BF16)
    return hi, mid, lo


def _gla_body(*refs, chunk, n_chunks, n_steps, has_state):
    if has_state:
        gq_ref, gk_ref, gg_ref, gv_ref, gr_ref, on_ref, s0_ref, go_ref, st_ref, state = refs
    else:
        gq_ref, gk_ref, gg_ref, gv_ref, gr_ref, on_ref, go_ref, st_ref, state = refs
    t = pl.program_id(1)

    @pl.when(t == 0)
    def _():
        for h in range(GLA_HEADS):
            if has_state:
                state[h] = s0_ref[0, h].T
            else:
                state[h] = jnp.zeros((GLA_DV, GLA_DK), F32)

    row = lax.broadcasted_iota(jnp.int32, (chunk, chunk), 0)
    col = lax.broadcasted_iota(jnp.int32, (chunk, chunk), 1)
    causal = row >= col
    tril = causal.astype(BF16)

    chunks = [slice(c * chunk, (c + 1) * chunk) for c in range(n_chunks)]
    kcs = [slice(h * GLA_DK, (h + 1) * GLA_DK) for h in range(GLA_HEADS)]
    vcs = [slice(h * GLA_DV, (h + 1) * GLA_DV) for h in range(GLA_HEADS)]

    bs = []
    for rows in chunks:
        g_hi, g_mid, g_lo = _split3(gg_ref[rows, :])
        bs.append(_dot(tril, g_hi) + _dot(tril, g_mid) + _dot(tril, g_lo))

    qes, kes, kds, decays = [], [], [], []
    for rows, b in zip(chunks, bs):
        b_last = b[chunk - 1:chunk, :]
        q = gq_ref[rows, :]
        k = gk_ref[rows, :]
        qes.append((q * jnp.exp(b)).astype(BF16))
        kes.append((k * jnp.exp(-b)).astype(BF16))
        kds.append((k * jnp.exp(b_last - b)).astype(BF16))
        decays.append(jnp.exp(b_last))

    a_s = [[jnp.where(causal, _dot_nt(qe[:, kc], ke[:, kc]), 0.0).astype(BF16) for kc in kcs]
           for qe, ke in zip(qes, kes)]
    incs = [[_dot_tn(gv_ref[rows, vc], kd[:, kc]) for kc, vc in zip(kcs, vcs)] for rows, kd in zip(chunks, kds)]

    s_in = []
    s_cur = [state[h] for h in range(GLA_HEADS)]
    for c in range(n_chunks):
        s_in.append([s.astype(BF16) for s in s_cur])
        s_cur = [s * decays[c][:, kc] + inc for s, kc, inc in zip(s_cur, kcs, incs[c])]
    for h in range(GLA_HEADS):
        state[h] = s_cur[h]

    for c, rows in enumerate(chunks):
        for h, (kc, vc) in enumerate(zip(kcs, vcs)):
            o = _dot_nt(qes[c][:, kc], s_in[c][h]) + _dot(a_s[c][h], gv_ref[rows, vc])
            go_ref[rows, vc] = (_rms(o, on_ref[...]) * jax.nn.silu(gr_ref[rows, vc])).astype(BF16)

    @pl.when(t == n_steps - 1)
    def _():
        for h in range(GLA_HEADS):
            st_ref[0, h] = state[h].T


def _gla(gq, gk, gg, gv, gr, onorm, s0, batch, seq, chunk):
    tt = min(GLA_TILE, seq)
    n_steps = seq // tt
    tok = lambda width: pl.BlockSpec((tt, width), lambda b, t: (b * n_steps + t, 0))
    st_spec = pl.BlockSpec((1, GLA_HEADS, GLA_DK, GLA_DV), lambda b, t: (b, 0, 0, 0))
    in_specs = [tok(GLA_QK), tok(GLA_QK), tok(GLA_QK), tok(GLA_V), tok(GLA_V),
                pl.BlockSpec((1, GLA_DV), lambda b, t: (0, 0))]
    args = [gq, gk, gg, gv, gr, onorm]
    if s0 is not None:
        in_specs.append(st_spec)
        args.append(s0)
    return pl.pallas_call(
        functools.partial(_gla_body, chunk=chunk, n_chunks=tt // chunk, n_steps=n_steps, has_state=s0 is not None),
        grid=(batch, n_steps),
        in_specs=in_specs,
        out_specs=[tok(GLA_V), st_spec],
        out_shape=[jax.ShapeDtypeStruct((batch * seq, GLA_V), BF16),
                   jax.ShapeDtypeStruct((batch, GLA_HEADS, GLA_DK, GLA_DV), F32)],
        scratch_shapes=[pltpu.VMEM((GLA_HEADS, GLA_DV, GLA_DK), F32)],
        compiler_params=_params(("parallel", "arbitrary")),
        name="gla",
    )(*args)


def _t5_bucket(rel):
    nb = REL_BUCKETS // 2
    max_exact = nb // 2
    ret = jnp.where(rel > 0, nb, 0)
    n = jnp.abs(rel)
    nf = jnp.maximum(n, 1).astype(F32)
    large = max_exact + (jnp.log(nf / max_exact) / math.log(REL_MAX_DIST / max_exact)
                         * (nb - max_exact)).astype(jnp.int32)
    large = jnp.minimum(large, nb - 1)
    return ret + jnp.where(n < max_exact, n, large)


def _bucket_tile(q_pos, k_pos):
    visible = (k_pos[None, :] // CHUNK) <= (q_pos[:, None] // CHUNK)
    return jnp.where(visible, _t5_bucket(k_pos[None, :] - q_pos[:, None]), MASKED_BUCKET).astype(jnp.int32)


def _bias_from_buckets(idx, tab_ref, head):
    def step(bk, acc):
        return jnp.where(idx == bk, tab_ref[bk * DIFF_HEADS + head], acc)
    return lax.fori_loop(0, REL_BUCKETS, step, jnp.full(idx.shape, NEG_INF, F32))


def _lambda(lam_ref, lam_init):
    l = lam_ref[...]
    return (jnp.exp(jnp.sum(l[0:1] * l[1:2], axis=-1, keepdims=True))
            - jnp.exp(jnp.sum(l[2:3] * l[3:4], axis=-1, keepdims=True)) + lam_init)


def _comp_masks(q):
    lane = lax.broadcasted_iota(jnp.int32, q.shape, 1)
    zero = jnp.zeros_like(q)
    return jnp.where(lane < DIFF_DH, q, zero), jnp.where(lane < DIFF_DH, zero, q)


def _diff_finish(o0, o1, lam, on_ref, lam_init, out_dtype):
    o = o0 - lam * o1
    return (_rms(o, on_ref[...]) * (1.0 - lam_init)).astype(out_dtype)


def _scores(q_c, parts):
    return [_dot_nt(q_c, k) + bias if jnp.ndim(bias) == 2 else _dot_nt(q_c, k) for k, _, bias in parts]


def _softmax_pv(scores, parts):
    shifts = [0.0 if jnp.ndim(bias) == 2 else bias for _, _, bias in parts]
    m = None
    for s, shift in zip(scores, shifts):
        part_max = s.max(axis=-1, keepdims=True) + shift
        m = part_max if m is None else jnp.maximum(m, part_max)
    ol = 0.0
    for s, shift, (_, v1, _) in zip(scores, shifts, parts):
        ol = ol + _dot(jnp.exp(s - (m - shift)).astype(BF16), v1)
    return ol[:, :DIFF_DV] * (1.0 / ol[:, DIFF_DV:])


def _diff_prompt_body(tab_ref, q_ref, kb, vb, idx_ref, lam_ref, on_ref, o_ref, bias, v1, *, seq, lam_init):
    b = pl.program_id(0)
    h = pl.program_id(1)

    @pl.when(b == 0)
    def _():
        bias[h] = _bias_from_buckets(idx_ref[...], tab_ref, h)

    v1[:, :DIFF_DV] = vb[...]
    v1[:, DIFF_DV:] = jnp.ones((seq, LANES), BF16)
    lam = _lambda(lam_ref, lam_init)
    far_bias = tab_ref[(REL_BUCKETS // 2 - 1) * DIFF_HEADS + h]

    def key_parts(i):
        if i == 0:
            return [(kb[0:Q_BLOCK, :], v1[0:Q_BLOCK, :], bias[h, :, Q_BLOCK:])]
        near0 = (i - 1) * Q_BLOCK
        near = slice(near0, near0 + 2 * Q_BLOCK)
        parts = [(kb[near, :], v1[near, :], bias[h])]
        if near0 > 0:
            parts.append((kb[0:near0, :], v1[0:near0, :], far_bias))
        return parts

    items = [(i, c) for i in range(seq // Q_BLOCK) for c in range(2)]
    comps = {}

    def scores_of(item):
        i, c = item
        if i not in comps:
            comps[i] = _comp_masks(q_ref[i * Q_BLOCK:(i + 1) * Q_BLOCK, :])
        return _scores(comps[i][c], key_parts(i))

    outs = {}
    ahead = scores_of(items[0])
    for n, (i, c) in enumerate(items):
        cur = ahead
        if n + 1 < len(items):
            ahead = scores_of(items[n + 1])
        outs[c] = _softmax_pv(cur, key_parts(i))
        if c == 1:
            o_ref[i * Q_BLOCK:(i + 1) * Q_BLOCK, :] = _diff_finish(outs[0], outs[1], lam, on_ref, lam_init,
                                                                    o_ref.dtype)


def _diff_prompt(table, dq, dk, dv, lam_p, onorm, batch, seq, lam_init):
    idx = _bucket_tile(Q_BLOCK + jnp.arange(Q_BLOCK), jnp.arange(2 * Q_BLOCK))
    head = lambda: pl.BlockSpec((seq, LANES), lambda b, h: (b, h))
    return pl.pallas_call(
        functools.partial(_diff_prompt_body, seq=seq, lam_init=lam_init),
        grid=(batch, DIFF_HEADS),
        in_specs=[
            pl.BlockSpec(memory_space=pltpu.SMEM),
            head(), head(), head(),
            pl.BlockSpec((Q_BLOCK, 2 * Q_BLOCK), lambda b, h: (0, 0)),
            pl.BlockSpec((4, DIFF_DH), lambda b, h: (0, 0)),
            pl.BlockSpec((1, DIFF_DV), lambda b, h: (0, 0)),
        ],
        out_specs=head(),
        out_shape=jax.ShapeDtypeStruct((batch * seq, DIFF_V), BF16),
        scratch_shapes=[pltpu.VMEM((DIFF_HEADS, Q_BLOCK, 2 * Q_BLOCK), F32),
                        pltpu.VMEM((seq, DIFF_DV + LANES), BF16)],
        compiler_params=_params(("arbitrary", "arbitrary")),
        name="diff_prompt",
    )(table, dq, dk, dv, idx, lam_p, onorm)


def _diff_sample_body(tab_ref, q_ref, k_ref, v_ref, ck_ref, cv_ref, idxc_ref, idxn_ref, lam_ref, on_ref, o_ref,
                      bias_c, bias_n, *, seq, lam_init):
    groups = [(h, c) for h in range(DIFF_HEADS) for c in range(2)]

    @pl.when(pl.program_id(0) == 0)
    def _():
        for h in range(DIFF_HEADS):
            tile = _bias_from_buckets(idxc_ref[h], tab_ref, h)
            for g in (2 * h, 2 * h + 1):
                bias_c[g * seq:(g + 1) * seq, :] = tile
            bias_n[h] = _bias_from_buckets(idxn_ref[...], tab_ref, h)

    lam = _lambda(lam_ref, lam_init)
    head_cols = [slice(h * LANES, (h + 1) * LANES) for h in range(DIFF_HEADS)]
    q_groups = [q_c for cols in head_cols for q_c in _comp_masks(q_ref[:, cols])]
    s_old = _dot_nt(jnp.concatenate(q_groups, axis=0), ck_ref[...].astype(BF16)) + bias_c[...]
    s_new = jnp.concatenate([_dot_nt(q_g, k_ref[:, head_cols[h]]) + bias_n[h]
                             for q_g, (h, _) in zip(q_groups, groups)], axis=0)
    m = jnp.maximum(s_old.max(axis=-1, keepdims=True), s_new.max(axis=-1, keepdims=True))
    e_old = jnp.exp(s_old - m)
    e_new = jnp.exp(s_new - m)
    l = e_old.sum(axis=-1, keepdims=True) + e_new.sum(axis=-1, keepdims=True)
    o_old = _dot(e_old.astype(BF16), cv_ref[...].astype(BF16))
    e_new = e_new.astype(BF16)
    o_new = jnp.concatenate([_dot(e_new[g * seq:(g + 1) * seq, :], v_ref[:, head_cols[h]])
                             for g, (h, _) in enumerate(groups)], axis=0)
    o = (o_old + o_new) * (1.0 / l)
    for h, cols in enumerate(head_cols):
        o0 = o[(2 * h) * seq:(2 * h + 1) * seq, :]
        o1 = o[(2 * h + 1) * seq:(2 * h + 2) * seq, :]
        o_ref[:, cols] = _diff_finish(o0, o1, lam, on_ref, lam_init, o_ref.dtype)


def _diff_sample(table, dq, dk, dv, cache_k, cache_v, lam_p, onorm, batch, seq, past, lam_init):
    q_pos = past + jnp.arange(seq)
    idx_n = _bucket_tile(q_pos, past + jnp.arange(seq))
    idx_rows = jnp.repeat(_bucket_tile(q_pos, jnp.arange(past)), HEADS, axis=1)
    row_head = jnp.arange(past * HEADS) % HEADS
    idx_c = jnp.where(row_head[None, None, :] == jnp.arange(HEADS)[:, None, None], idx_rows[None], MASKED_BUCKET)
    new_q = pl.BlockSpec((seq, DIFF_QK), lambda b: (b, 0))
    old = lambda: pl.BlockSpec((past * HEADS, LANES), lambda b: (b, 0))
    const = lambda shape: pl.BlockSpec(shape, lambda b: (0,) * len(shape))
    n_groups = 2 * DIFF_HEADS
    return pl.pallas_call(
        functools.partial(_diff_sample_body, seq=seq, lam_init=lam_init),
        grid=(batch,),
        in_specs=[pl.BlockSpec(memory_space=pltpu.SMEM), new_q, new_q, new_q, old(), old(),
                  const((HEADS, seq, past * HEADS)), const((seq, seq)), const((4, DIFF_DH)), const((1, DIFF_DV))],
        out_specs=new_q,
        out_shape=jax.ShapeDtypeStruct((batch * seq, DIFF_V), BF16),
        scratch_shapes=[pltpu.VMEM((n_groups * seq, past * HEADS), F32), pltpu.VMEM((DIFF_HEADS, seq, seq), F32)],
        compiler_params=_params(("arbitrary",)),
        name="diff_sample",
    )(table, dq, dk, dv, cache_k, cache_v, idx_c, idx_n, lam_p, onorm)


def _softmax(s):
    e = jnp.exp(s - s.max(axis=-1, keepdims=True))
    return (e * (1.0 / e.sum(axis=-1, keepdims=True))).astype(BF16)


def _memattn_body(q_ref, k_ref, v_ref, o_ref, *, interleaved):
    head_cols = [slice(h * MEM_DH, (h + 1) * MEM_DH) for h in range(MEM_HEADS)]
    scale = MEM_DH ** -0.5
    if interleaved:
        tq = q_ref.shape[0]
        s = _dot_nt(jnp.concatenate([q_ref[:, cols] for cols in head_cols], axis=0), k_ref[...].astype(BF16)) * scale
        q_head = lax.broadcasted_iota(jnp.int32, s.shape, 0) // tq
        k_head = lax.broadcasted_iota(jnp.int32, s.shape, 1) % MEM_HEADS
        o = _dot(_softmax(jnp.where(q_head == k_head, s, NEG_INF)), v_ref[...].astype(BF16))
        for h, cols in enumerate(head_cols):
            o_ref[:, cols] = o[h * tq:(h + 1) * tq, :].astype(o_ref.dtype)
    else:
        scores = [_dot_nt(q_ref[:, cols], k_ref[:, cols]) * scale for cols in head_cols]
        probs = [_softmax(s) for s in scores]
        for cols, p in zip(head_cols, probs):
            o_ref[:, cols] = _dot(p, v_ref[:, cols]).astype(o_ref.dtype)


def _memattn(mq, mk, mv, batch, seq, n_mem):
    tq = min(MEMATTN_TILE, seq)
    nq = seq // tq
    q_spec = pl.BlockSpec((tq, MEM_W), lambda b, i: (b * nq + i, 0))
    interleaved = mk.shape[1] == LANES
    kv_spec = pl.BlockSpec((mk.shape[0] // batch, mk.shape[1]), lambda b, i: (b, 0))
    return pl.pallas_call(
        functools.partial(_memattn_body, interleaved=interleaved),
        grid=(batch, nq),
        in_specs=[q_spec, kv_spec, kv_spec],
        out_specs=q_spec,
        out_shape=jax.ShapeDtypeStruct((batch * seq, MEM_W), BF16),
        compiler_params=_params(("parallel", "arbitrary")),
        name="memattn",
    )(mq, mk, mv)


def _outproj_body(x_ref, g_ref, d_ref, m_ref, wg_ref, wd_ref, wm_ref, o_ref):
    o_ref[...] = (x_ref[...] + _dot(g_ref[...], wg_ref[...]) + _dot(d_ref[...], wd_ref[...])
                  + _dot(m_ref[...], wm_ref[...]))


def _outproj(x, g, d, m, wo):
    n = x.shape[0]
    tm = min(TOK_TILE, n)
    tok = lambda width: pl.BlockSpec((tm, width), lambda i: (i, 0))
    w_rows = lambda rows, blk: pl.BlockSpec((rows, D_MODEL), lambda i: (blk, 0))
    return pl.pallas_call(
        _outproj_body,
        grid=(n // tm,),
        in_specs=[tok(D_MODEL), tok(GLA_V), tok(DIFF_V), tok(MEM_W),
                  w_rows(GLA_V, 0), w_rows(DIFF_V, GLA_V // DIFF_V), w_rows(MEM_W, (GLA_V + DIFF_V) // MEM_W)],
        out_specs=tok(D_MODEL),
        out_shape=jax.ShapeDtypeStruct((n, D_MODEL), F32),
        compiler_params=_params(("parallel",)),
        name="outproj",
    )(x, g, d, m, wo, wo, wo)


def _row(v):
    return v.reshape(1, -1).astype(F32)


def kernel(x_prompt, x_sample, mem_prompt, cache_diff_k, cache_diff_v, state_gla, cache_mem_k, cache_mem_v, rel_bias_table, norm_ffn1, w_ffn1_in, w_ffn1_out, norm_mix, w_in, w_gla_g2, b_gla_g, gla_out_norm, diff_q_norm, diff_k_norm, diff_lambda, diff_out_norm, mem_norm, w_mem_kv, mem_q_norm, mem_k_norm, w_o, norm_ffn2, w_ffn2_in, w_ffn2_out, norm_final):
    depth = norm_ffn1.shape[0]
    assert depth == 1, "single-layer step"
    layer = 0
    batch, seq, _ = x_prompt.shape
    dec_batch, dec_seq, _ = x_sample.shape
    past = cache_diff_k.shape[2]
    n_mem = mem_prompt.shape[1]
    lam_init = 0.8 - 0.6 * math.exp(-0.3 * layer)

    w_g2 =jnp.pad(w_gla_g2[layer].astype(BF16), ((0, LANES - GLA_GATE_RANK), (0, 0)))
    b_g = _row(b_gla_g[layer])
    qn = _row(jnp.tile(diff_q_norm[layer], DIFF_QK // DIFF_DH))
    kn = _row(jnp.tile(diff_k_norm[layer], DIFF_QK // DIFF_DH))
    mqn = _row(jnp.tile(mem_q_norm[layer], MEM_HEADS))
    mkn = _row(jnp.tile(mem_k_norm[layer], MEM_HEADS))
    table = rel_bias_table.astype(F32).reshape(-1)
    lam_p = diff_lambda[layer].astype(F32)
    gla_on = _row(gla_out_norm[layer])
    diff_on = _row(diff_out_norm[layer])

    mk, mv, mk_b, mv_b = _memkv(mem_prompt.reshape(batch * n_mem, D_MODEL), _row(mem_norm[layer]),
                                w_mem_kv[layer].astype(BF16), mkn)

    n1 = _row(norm_ffn1[layer])
    xs1, *ffn1 = _ffn(x_sample.reshape(dec_batch * dec_seq, D_MODEL), n1, w_ffn1_in[layer], w_ffn1_in[layer],
                      w_ffn1_out[layer], up_col0=D_FF, ff_tile=FF_TILE_F32, emit_bf16=True)
    n_in_blocks = w_ffn2_in.shape[2] // LANES
    xp1, w2_in, w2_out, w_t, wo = _ffn(
        x_prompt.reshape(batch * seq, D_MODEL), n1, *ffn1, up_col0=0,
        cast_later=[(w_ffn2_in[layer], (D_MODEL, LANES), 0),
                    (w_ffn2_out[layer], (D_FF // n_in_blocks, D_MODEL), n_in_blocks),
                    (jnp.swapaxes(w_in[layer], 0, 1), (CAST_ROWS_W_IN, D_MODEL), 0),
                    (w_o[layer], (CAST_ROWS_W_O, D_MODEL), 0)])
    ffn2 = (w2_in, w2_in, w2_out)

    def layer_fn(x, b, t, chunk, s0, diff_fn, mem_k, mem_v):
        gq, gk, gv, gr, gg, dq, dk, dv, mq, dk_b, dv_b = _proj(x, _row(norm_mix[layer]), w_t, w_g2, b_g,
                                                               qn, kn, mqn)
        g_out, g_state = _gla(gq, gk, gg, gv, gr, gla_on, s0, b, t, chunk)
        d_out = diff_fn(dq, dk_b, dv_b)
        m_out = _memattn(mq, mem_k, mem_v, b, t, n_mem)
        x = _outproj(x, g_out, d_out, m_out, wo)
        x = _ffn(x, _row(norm_ffn2[layer]), *ffn2, up_col0=D_FF, final_gain=_row(norm_final[layer]))
        return x, dk, dv, g_state

    yp, dk_p, dv_p, g_p = layer_fn(
        xp1, batch, seq, CHUNK, None,
        lambda dq, dk, dv: _diff_prompt(table, dq, dk, dv, lam_p, diff_on, batch, seq, lam_init), mk_b, mv_b)
    ys, dk_s, dv_s, g_s = layer_fn(
        xs1, dec_batch, dec_seq, dec_seq, state_gla[layer],
        lambda dq, dk, dv: _diff_sample(table, dq, dk, dv, cache_diff_k[layer].reshape(-1, LANES),
                                        cache_diff_v[layer].reshape(-1, LANES),
                                        lam_p, diff_on, dec_batch, dec_seq, past, lam_init),
        cache_mem_k[layer].reshape(-1, LANES), cache_mem_v[layer].reshape(-1, LANES))

    head4 = lambda a, b, t: a.reshape(1, b, t, HEADS, LANES)
    return (yp.reshape(batch, seq, D_MODEL), ys.reshape(dec_batch, dec_seq, D_MODEL),
            head4(dk_p, batch, seq), head4(dv_p, batch, seq), g_p[None],
            head4(mk, batch, n_mem), head4(mv, batch, n_mem),
            head4(dk_s, dec_batch, dec_seq), head4(dv_s, dec_batch, dec_seq), g_s[None])
```

```python
import functools
import math

import jax
import jax.numpy as jnp
from jax import lax
from jax.experimental import pallas as pl
from jax.experimental.pallas import tpu as pltpu

F32 = jnp.float32
BF16 = jnp.bfloat16

D_MODEL = 2048
CHUNK = 64
EPS = 1e-6
NEG_INF = -1e30
GLA_HEADS, GLA_DK, GLA_DV, GLA_GATE_RANK, GLA_GATE_NORM = 4, 128, 256, 16, 16.0
DIFF_HEADS, DIFF_DH, DIFF_DV = 4, 64, 128
MEM_HEADS, MEM_DH = 4, 128
REL_BUCKETS, REL_MAX_DIST = 32, 128
D_FF = 5504
GLA_QK = GLA_HEADS * GLA_DK
GLA_V = GLA_HEADS * GLA_DV
DIFF_QK = DIFF_HEADS * 2 * DIFF_DH
DIFF_V = DIFF_HEADS * DIFF_DV
MEM_W = MEM_HEADS * MEM_DH
GLR_OFF = 2 * GLA_QK + 2 * GLA_V

LANES = 128
SUBLANES = 8
HEADS = 4
FF_TILE = 512
FF_TILE_F32 = 256
FFN_TOK_TILE = 1024
TOK_TILE = 512
GLA_TILE = 1024
MEMATTN_TILE = 2048
PROJ_TILE = 256
Q_BLOCK = 256
MASKED_BUCKET = REL_BUCKETS
MIB = 1024 * 1024
VMEM_CAP_MIB = 60
CAST_ROWS_W_IN = 48
CAST_ROWS_W_O = 16

_W_GQ, _W_GK, _W_GV, _W_GR, _W_GLR = 0, GLA_QK, 2 * GLA_QK, 2 * GLA_QK + GLA_V, GLR_OFF
_W_DQ = GLR_OFF + GLA_GATE_RANK
_W_DK = _W_DQ + DIFF_QK
_W_DV = _W_DK + DIFF_QK
_W_MQ = _W_DV + DIFF_V
IN_WIDTH = _W_MQ + MEM_W


def _dot(a, b):
    return jnp.dot(a, b, preferred_element_type=F32)


def _dot_nt(a, b):
    return lax.dot_general(a, b, (((1,), (1,)), ((), ())), preferred_element_type=F32)


def _dot_tn(a, b):
    return lax.dot_general(a, b, (((0,), (0,)), ((), ())), preferred_element_type=F32)


def _rms(x, gain):
    return x * lax.rsqrt(jnp.mean(x * x, axis=-1, keepdims=True) + EPS) * gain


def _params(sem, vmem_mib=48):
    return pltpu.CompilerParams(dimension_semantics=sem, vmem_limit_bytes=min(vmem_mib, VMEM_CAP_MIB) * MIB)


def _ffn_body(*refs, n_ff, overlap, final_norm, n_cast, emit_bf16):
    x_ref, g_ref, wg_ref, wu_ref, wo_ref = refs[:5]
    refs = refs[5:]
    if final_norm:
        fg_ref, refs = refs[0], refs[1:]
    cast_src, refs = refs[:n_cast], refs[n_cast:]
    o_ref, refs = refs[0], refs[1:]
    cast_dst, refs = refs[:n_cast], refs[n_cast:]
    if emit_bf16:
        (wg_out, wu_out, wo_out), refs = refs[:3], refs[3:]
    (xn_ref,) = refs
    j = pl.program_id(1)

    def tile(lo):
        for src, dst in zip(cast_src, cast_dst):
            dst[...] = src[...].astype(BF16)
        xn = xn_ref[...]
        if emit_bf16:
            def front(a, axis):
                a = a.astype(BF16)
                if lo == 0:
                    return a
                return jnp.concatenate([lax.slice_in_dim(a, lo, None, axis=axis),
                                        lax.slice_in_dim(a, 0, lo, axis=axis)], axis=axis)
            wg_out[...] = front(wg_ref[...], 1)
            wu_out[...] = front(wu_ref[...], 1)
            wo_out[...] = front(wo_ref[...], 0)
            live = wg_out.shape[1] - lo
            gate = _dot(xn, wg_out[:, :live])
            up = _dot(xn, wu_out[:, :live])
            act = (jax.nn.silu(gate) * up).astype(BF16)
            return 0.5 * _dot(act, wo_out[:live, :])
        gate = _dot(xn, wg_ref[:, lo:])
        up = _dot(xn, wu_ref[:, lo:])
        act = (jax.nn.silu(gate) * up).astype(BF16)
        return 0.5 * _dot(act, wo_ref[lo:, :])

    @pl.when(j == 0)
    def _():
        xn_ref[...] = _rms(x_ref[...], g_ref[...]).astype(BF16)
        o_ref[...] = x_ref[...] + tile(0)

    @pl.when((j > 0) & (j < n_ff - 1))
    def _():
        o_ref[...] += tile(0)

    @pl.when(j == n_ff - 1)
    def _():
        o = o_ref[...] + tile(overlap)
        o_ref[...] = _rms(o, fg_ref[...]) if final_norm else o


def _ffn(x, gain, w_gate, w_up, w_out, *, up_col0, final_gain=None, cast_later=(), ff_tile=FF_TILE,
         emit_bf16=False):
    n = x.shape[0]
    tm = min(FFN_TOK_TILE, n)
    n_ff = -(-D_FF // ff_tile)
    row = pl.BlockSpec((1, D_MODEL), lambda i, j: (0, 0))
    start = lambda j: jnp.minimum(j * (ff_tile // LANES), (D_FF - ff_tile) // LANES)
    w_in_tile = (pl.Element(D_MODEL), pl.Element(ff_tile))
    w_out_tile = (pl.Element(ff_tile), pl.Element(D_MODEL))
    gate_index = lambda i, j: (0, start(j) * LANES)
    out_index = lambda i, j: (start(j) * LANES, 0)
    tok = pl.BlockSpec((tm, D_MODEL), lambda i, j: (i, 0))
    in_specs = [
        tok,
        row,
        pl.BlockSpec(w_in_tile, gate_index),
        pl.BlockSpec(w_in_tile, lambda i, j: (0, (up_col0 // LANES + start(j)) * LANES)),
        pl.BlockSpec(w_out_tile, out_index),
    ]
    args = [x, gain, w_gate, w_up, w_out]
    if final_gain is not None:
        in_specs.append(row)
        args.append(final_gain)
    w_tile_bytes = D_MODEL * ff_tile * ((4 + 2) if emit_bf16 else 2)
    vmem_bytes = 4 * tm * D_MODEL * 4 + tm * D_MODEL * 2 + 6 * w_tile_bytes + 3 * tm * ff_tile * 4
    out_specs = [pl.BlockSpec((tm, D_MODEL), lambda i, j: (i, 0))]
    out_shape = [jax.ShapeDtypeStruct((n, D_MODEL), F32)]
    n_steps = (n // tm) * n_ff
    for mat, blk, first in cast_later:
        rows, cols = mat.shape
        assert rows % blk[0] == 0 and cols % blk[1] == 0 and (blk[0] == rows or blk[1] == cols)
        n_blk = (rows // blk[0]) * (cols // blk[1])
        assert first + n_blk <= n_steps
        by_rows = blk[1] == cols

        def index(i, j, first=first, n_blk=n_blk, by_rows=by_rows):
            k = jnp.clip(i * n_ff + j - first, 0, n_blk - 1)
            return (k, 0) if by_rows else (0, k)

        in_specs.append(pl.BlockSpec(blk, index))
        args.append(mat)
        out_specs.append(pl.BlockSpec(blk, index))
        out_shape.append(jax.ShapeDtypeStruct(mat.shape, BF16))
        vmem_bytes += 2 * blk[0] * blk[1] * (4 + 2)
    if emit_bf16:
        d_ff_pad = n_ff * ff_tile
        cols_j = pl.BlockSpec((D_MODEL, ff_tile), lambda i, j: (0, j))
        out_specs += [cols_j, cols_j, pl.BlockSpec((ff_tile, D_MODEL), lambda i, j: (j, 0))]
        out_shape += ([jax.ShapeDtypeStruct((D_MODEL, d_ff_pad), BF16)] * 2
                      + [jax.ShapeDtypeStruct((d_ff_pad, D_MODEL), BF16)])
    outs = pl.pallas_call(
        functools.partial(_ffn_body, n_ff=n_ff, overlap=n_ff * ff_tile - D_FF, final_norm=final_gain is not None,
                          n_cast=len(cast_later), emit_bf16=emit_bf16),
        grid=(n // tm, n_ff),
        in_specs=in_specs,
        out_specs=out_specs,
        out_shape=out_shape,
        scratch_shapes=[pltpu.VMEM((tm, D_MODEL), BF16)],
        compiler_params=_params(("arbitrary", "arbitrary"), vmem_bytes // MIB + 4),
        name="ffn",
    )(*args)
    return outs[0] if len(outs) == 1 else outs


def _group_rms(acc, gain_ref, width, scale, put):
    lane = lax.broadcasted_iota(jnp.int32, (1, LANES), 1)
    low = lane < width
    for c in range(acc.shape[1] // LANES):
        cols = slice(c * LANES, (c + 1) * LANES)
        xc = acc[:, cols]
        sq = xc * xc
        if width == LANES:
            r = lax.rsqrt(jnp.mean(sq, axis=-1, keepdims=True) + EPS)
        else:
            s_lo = jnp.sum(jnp.where(low, sq, 0.0), axis=-1, keepdims=True)
            s_hi = jnp.sum(jnp.where(low, 0.0, sq), axis=-1, keepdims=True)
            r = jnp.where(low, lax.rsqrt(s_lo / width + EPS), lax.rsqrt(s_hi / width + EPS))
        y = xc * r * gain_ref[:, cols]
        if scale != 1.0:
            y = y * scale
        put(c, y)


def _put_cols(ref):
    def put(c, y):
        ref[:, c * LANES:(c + 1) * LANES] = y.astype(ref.dtype)
    return put


def _put_heads_and_cols(heads_ref, cols_ref):
    def put(c, y):
        heads_ref[:, c, :] = y
        cols_ref[:, c * LANES:(c + 1) * LANES] = y.astype(cols_ref.dtype)
    return put


def _proj_body(x_ref, g_ref, wt_ref, wg2_ref, bg_ref, qn_ref, kn_ref, mqn_ref,
               gq_ref, gk_ref, gv_ref, gr_ref, gg_ref, dq_ref, dk_ref, dv_ref, mq_ref, dkb_ref, dvb_ref, xn_ref):
    xn_ref[...] = _rms(x_ref[...], g_ref[...]).astype(BF16)

    def cols(start, width):
        return _dot_nt(xn_ref[...], wt_ref[start:start + width, :])

    gq_ref[...] = cols(_W_GQ, GLA_QK) * (GLA_DK ** -0.5)
    gk_ref[...] = cols(_W_GK, GLA_QK)
    gv_ref[...] = cols(_W_GV, GLA_V).astype(BF16)
    gr_ref[...] = cols(_W_GR, GLA_V)
    glr = cols(_W_GLR, LANES)
    z = _dot(glr.astype(BF16), wg2_ref[...]) + bg_ref[...]
    gg_ref[...] = jax.nn.log_sigmoid(z) / GLA_GATE_NORM
    _group_rms(cols(_W_DQ, DIFF_QK), qn_ref, DIFF_DH, DIFF_DH ** -0.5, _put_cols(dq_ref))
    _group_rms(cols(_W_DK, DIFF_QK), kn_ref, DIFF_DH, 1.0, _put_heads_and_cols(dk_ref, dkb_ref))
    dv = cols(_W_DV, DIFF_V)
    put_v = _put_heads_and_cols(dv_ref, dvb_ref)
    for h in range(HEADS):
        put_v(h, dv[:, h * LANES:(h + 1) * LANES])
    _group_rms(cols(_W_MQ, MEM_W), mqn_ref, MEM_DH, 1.0, _put_cols(mq_ref))


def _proj(x, gain, wt, wg2, bg, qn, kn, mqn):
    n = x.shape[0]
    tm = min(PROJ_TILE, n)
    const = lambda shape: pl.BlockSpec(shape, lambda i: (0, 0))
    out = lambda width: pl.BlockSpec((tm, width), lambda i: (i, 0))
    heads = pl.BlockSpec((tm, HEADS, LANES), lambda i: (i, 0, 0))
    shp = lambda width, dt: jax.ShapeDtypeStruct((n, width), dt)
    shp_heads = jax.ShapeDtypeStruct((n, HEADS, LANES), F32)
    return pl.pallas_call(
        _proj_body,
        grid=(n // tm,),
        in_specs=[
            pl.BlockSpec((tm, D_MODEL), lambda i: (i, 0)),
            const((1, D_MODEL)),
            pl.BlockSpec((IN_WIDTH, D_MODEL), lambda i: (0, 0), pipeline_mode=pl.Buffered(1)),
            const((LANES, GLA_QK)),
            const((1, GLA_QK)),
            const((1, DIFF_QK)),
            const((1, DIFF_QK)),
            const((1, MEM_W)),
        ],
        out_specs=[out(GLA_QK), out(GLA_QK), out(GLA_V), out(GLA_V), out(GLA_QK),
                   out(DIFF_QK), heads, heads, out(MEM_W), out(DIFF_QK), out(DIFF_V)],
        out_shape=[shp(GLA_QK, F32), shp(GLA_QK, F32), shp(GLA_V, BF16), shp(GLA_V, F32), shp(GLA_QK, F32),
                   shp(DIFF_QK, BF16), shp_heads, shp_heads, shp(MEM_W, BF16), shp(DIFF_QK, BF16), shp(DIFF_V, BF16)],
        scratch_shapes=[pltpu.VMEM((tm, D_MODEL), BF16)],
        compiler_params=_params(("parallel",)),
        name="proj",
    )(x, gain, wt, wg2, bg, qn, kn, mqn)


def _memkv_body(x_ref, g_ref, w_ref, kn_ref, k_ref, v_ref, kb_ref, vb_ref):
    xn = _rms(x_ref[...], g_ref[...]).astype(BF16)
    _group_rms(_dot(xn, w_ref[:, :MEM_W]), kn_ref, MEM_DH, 1.0, _put_heads_and_cols(k_ref, kb_ref))
    v = _dot(xn, w_ref[:, MEM_W:])
    put_v = _put_heads_and_cols(v_ref, vb_ref)
    for h in range(HEADS):
        put_v(h, v[:, h * LANES:(h + 1) * LANES])


def _memkv(mem, gain, w, kn):
    n = mem.shape[0]
    tm = min(TOK_TILE, n)
    const = lambda shape: pl.BlockSpec(shape, lambda i: (0, 0))
    heads = pl.BlockSpec((tm, HEADS, LANES), lambda i: (i, 0, 0))
    dense = pl.BlockSpec((tm, MEM_W), lambda i: (i, 0))
    return pl.pallas_call(
        _memkv_body,
        grid=(n // tm,),
        in_specs=[pl.BlockSpec((tm, D_MODEL), lambda i: (i, 0)), const((1, D_MODEL)),
                  const((D_MODEL, 2 * MEM_W)), const((1, MEM_W))],
        out_specs=[heads, heads, dense, dense],
        out_shape=[jax.ShapeDtypeStruct((n, HEADS, LANES), F32)] * 2 + [jax.ShapeDtypeStruct((n, MEM_W), BF16)] * 2,
        compiler_params=_params(("parallel",)),
        name="memkv",
    )(mem, gain, w, kn)


def _split3(x):
    hi = x.astype(BF16)
    r1 = x - hi.astype(F32)
    mid = r1.astype(BF16)
    lo = (r1 - mid.astype(F32)).astype(BF16)
    return hi, mid, lo


def _gla_body(*refs, chunk, n_chunks, n_steps, has_state):
    if has_state:
        gq_ref, gk_ref, gg_ref, gv_ref, gr_ref, on_ref, s0_ref, go_ref, st_ref, state = refs
    else:
        gq_ref, gk_ref, gg_ref, gv_ref, gr_ref, on_ref, go_ref, st_ref, state = refs
    t = pl.program_id(1)

    @pl.when(t == 0)
    def _():
        for h in range(GLA_HEADS):
            if has_state:
                state[h] = s0_ref[0, h].T
            else:
                state[h] = jnp.zeros((GLA_DV, GLA_DK), F32)

    row = lax.broadcasted_iota(jnp.int32, (chunk, chunk), 0)
    col = lax.broadcasted_iota(jnp.int32, (chunk, chunk), 1)
    causal = row >= col
    tril = causal.astype(BF16)

    chunks = [slice(c * chunk, (c + 1) * chunk) for c in range(n_chunks)]
    kcs = [slice(h * GLA_DK, (h + 1) * GLA_DK) for h in range(GLA_HEADS)]
    vcs = [slice(h * GLA_DV, (h + 1) * GLA_DV) for h in range(GLA_HEADS)]

    bs = []
    for rows in chunks:
        g_hi, g_mid, g_lo = _split3(gg_ref[rows, :])
        bs.append(_dot(tril, g_hi) + _dot(tril, g_mid) + _dot(tril, g_lo))

    qes, kes, kds, decays = [], [], [], []
    for rows, b in zip(chunks, bs):
        b_last = b[chunk - 1:chunk, :]
        q = gq_ref[rows, :]
        k = gk_ref[rows, :]
        qes.append((q * jnp.exp(b)).astype(BF16))
        kes.append((k * jnp.exp(-b)).astype(BF16))
        kds.append((k * jnp.exp(b_last - b)).astype(BF16))
        decays.append(jnp.exp(b_last))

    a_s = [[jnp.where(causal, _dot_nt(qe[:, kc], ke[:, kc]), 0.0).astype(BF16) for kc in kcs]
           for qe, ke in zip(qes, kes)]
    incs = [[_dot_tn(gv_ref[rows, vc], kd[:, kc]) for kc, vc in zip(kcs, vcs)] for rows, kd in zip(chunks, kds)]

    s_in = []
    s_cur = [state[h] for h in range(GLA_HEADS)]
    for c in range(n_chunks):
        s_in.append([s.astype(BF16) for s in s_cur])
        s_cur = [s * decays[c][:, kc] + inc for s, kc, inc in zip(s_cur, kcs, incs[c])]
    for h in range(GLA_HEADS):
        state[h] = s_cur[h]

    for c, rows in enumerate(chunks):
        for h, (kc, vc) in enumerate(zip(kcs, vcs)):
            o = _dot_nt(qes[c][:, kc], s_in[c][h]) + _dot(a_s[c][h], gv_ref[rows, vc])
            go_ref[rows, vc] = (_rms(o, on_ref[...]) * jax.nn.silu(gr_ref[rows, vc])).astype(BF16)

    @pl.when(t == n_steps - 1)
    def _():
        for h in range(GLA_HEADS):
            st_ref[0, h] = state[h].T


def _gla(gq, gk, gg, gv, gr, onorm, s0, batch, seq, chunk):
    tt = min(GLA_TILE, seq)
    n_steps = seq // tt
    tok = lambda width: pl.BlockSpec((tt, width), lambda b, t: (b * n_steps + t, 0))
    st_spec = pl.BlockSpec((1, GLA_HEADS, GLA_DK, GLA_DV), lambda b, t: (b, 0, 0, 0))
    in_specs = [tok(GLA_QK), tok(GLA_QK), tok(GLA_QK), tok(GLA_V), tok(GLA_V),
                pl.BlockSpec((1, GLA_DV), lambda b, t: (0, 0))]
    args = [gq, gk, gg, gv, gr, onorm]
    if s0 is not None:
        in_specs.append(st_spec)
        args.append(s0)
    return pl.pallas_call(
        functools.partial(_gla_body, chunk=chunk, n_chunks=tt // chunk, n_steps=n_steps, has_state=s0 is not None),
        grid=(batch, n_steps),
        in_specs=in_specs,
        out_specs=[tok(GLA_V), st_spec],
        out_shape=[jax.ShapeDtypeStruct((batch * seq, GLA_V), BF16),
                   jax.ShapeDtypeStruct((batch, GLA_HEADS, GLA_DK, GLA_DV), F32)],
        scratch_shapes=[pltpu.VMEM((GLA_HEADS, GLA_DV, GLA_DK), F32)],
        compiler_params=_params(("parallel", "arbitrary")),
        name="gla",
    )(*args)


def _t5_bucket(rel):
    nb = REL_BUCKETS // 2
    max_exact = nb // 2
    ret = jnp.where(rel > 0, nb, 0)
    n = jnp.abs(rel)
    nf = jnp.maximum(n, 1).astype(F32)
    large = max_exact + (jnp.log(nf / max_exact) / math.log(REL_MAX_DIST / max_exact)
                         * (nb - max_exact)).astype(jnp.int32)
    large = jnp.minimum(large, nb - 1)
    return ret + jnp.where(n < max_exact, n, large)


def _bucket_tile(q_pos, k_pos):
    visible = (k_pos[None, :] // CHUNK) <= (q_pos[:, None] // CHUNK)
    return jnp.where(visible, _t5_bucket(k_pos[None, :] - q_pos[:, None]), MASKED_BUCKET).astype(jnp.int32)


def _bias_from_buckets(idx, tab_ref, head):
    def step(bk, acc):
        return jnp.where(idx == bk, tab_ref[bk * DIFF_HEADS + head], acc)
    return lax.fori_loop(0, REL_BUCKETS, step, jnp.full(idx.shape, NEG_INF, F32))


def _lambda(lam_ref, lam_init):
    l = lam_ref[...]
    return (jnp.exp(jnp.sum(l[0:1] * l[1:2], axis=-1, keepdims=True))
            - jnp.exp(jnp.sum(l[2:3] * l[3:4], axis=-1, keepdims=True)) + lam_init)


def _comp_masks(q):
    lane = lax.broadcasted_iota(jnp.int32, q.shape, 1)
    zero = jnp.zeros_like(q)
    return jnp.where(lane < DIFF_DH, q, zero), jnp.where(lane < DIFF_DH, zero, q)


def _diff_finish(o0, o1, lam, on_ref, lam_init, out_dtype):
    o = o0 - lam * o1
    return (_rms(o, on_ref[...]) * (1.0 - lam_init)).astype(out_dtype)


def _scores(q_c, parts):
    return [_dot_nt(q_c, k) + bias if jnp.ndim(bias) == 2 else _dot_nt(q_c, k) for k, _, bias in parts]


def _softmax_pv(scores, parts):
    shifts = [0.0 if jnp.ndim(bias) == 2 else bias for _, _, bias in parts]
    m = None
    for s, shift in zip(scores, shifts):
        part_max = s.max(axis=-1, keepdims=True) + shift
        m = part_max if m is None else jnp.maximum(m, part_max)
    ol = 0.0
    for s, shift, (_, v1, _) in zip(scores, shifts, parts):
        ol = ol + _dot(jnp.exp(s - (m - shift)).astype(BF16), v1)
    return ol[:, :DIFF_DV] * (1.0 / ol[:, DIFF_DV:])


def _diff_prompt_body(tab_ref, q_ref, kb, vb, idx_ref, lam_ref, on_ref, o_ref, bias, v1, *, seq, lam_init):
    b = pl.program_id(0)
    h = pl.program_id(1)

    @pl.when(b == 0)
    def _():
        per_offset = _bias_from_buckets(idx_ref[...], tab_ref, h)[0:1]
        rows = jnp.broadcast_to(per_offset, (Q_BLOCK, 4 * Q_BLOCK))
        tile = pltpu.roll(rows, 3 * Q_BLOCK, 1, stride=1, stride_axis=0)[:, :2 * Q_BLOCK]
        q_pos = Q_BLOCK + lax.broadcasted_iota(jnp.int32, tile.shape, 0)
        k_pos = lax.broadcasted_iota(jnp.int32, tile.shape, 1)
        bias[h] = jnp.where(k_pos // CHUNK <= q_pos // CHUNK, tile, NEG_INF)

    v1[:, :DIFF_DV] = vb[...]
    v1[:, DIFF_DV:] = jnp.ones((seq, LANES), BF16)
    lam = _lambda(lam_ref, lam_init)
    far_bias = tab_ref[(REL_BUCKETS // 2 - 1) * DIFF_HEADS + h]

    def key_parts(i):
        if i == 0:
            return [(kb[0:Q_BLOCK, :], v1[0:Q_BLOCK, :], bias[h, :, Q_BLOCK:])]
        near0 = (i - 1) * Q_BLOCK
        near = slice(near0, near0 + 2 * Q_BLOCK)
        parts = [(kb[near, :], v1[near, :], bias[h])]
        if near0 > 0:
            parts.append((kb[0:near0, :], v1[0:near0, :], far_bias))
        return parts

    items = [(i, c) for i in range(seq // Q_BLOCK) for c in range(2)]
    comps = {}

    def scores_of(item):
        i, c = item
        if i not in comps:
            comps[i] = _comp_masks(q_ref[i * Q_BLOCK:(i + 1) * Q_BLOCK, :])
        return _scores(comps[i][c], key_parts(i))

    outs = {}
    ahead = scores_of(items[0])
    for n, (i, c) in enumerate(items):
        cur = ahead
        if n + 1 < len(items):
            ahead = scores_of(items[n + 1])
        outs[c] = _softmax_pv(cur, key_parts(i))
        if c == 1:
            o_ref[i * Q_BLOCK:(i + 1) * Q_BLOCK, :] = _diff_finish(outs[0], outs[1], lam, on_ref, lam_init,
                                                                    o_ref.dtype)


def _diff_prompt(table, dq, dk, dv, lam_p, onorm, batch, seq, lam_init):
    offsets = jnp.arange(4 * Q_BLOCK) - 2 * Q_BLOCK
    idx = jnp.broadcast_to(_t5_bucket(offsets).astype(jnp.int32)[None, :], (SUBLANES, 4 * Q_BLOCK))
    head = lambda: pl.BlockSpec((seq, LANES), lambda b, h: (b, h))
    return pl.pallas_call(
        functools.partial(_diff_prompt_body, seq=seq, lam_init=lam_init),
        grid=(batch, DIFF_HEADS),
        in_specs=[
            pl.BlockSpec(memory_space=pltpu.SMEM),
            head(), head(), head(),
            pl.BlockSpec((SUBLANES, 4 * Q_BLOCK), lambda b, h: (0, 0)),
            pl.BlockSpec((4, DIFF_DH), lambda b, h: (0, 0)),
            pl.BlockSpec((1, DIFF_DV), lambda b, h: (0, 0)),
        ],
        out_specs=head(),
        out_shape=jax.ShapeDtypeStruct((batch * seq, DIFF_V), BF16),
        scratch_shapes=[pltpu.VMEM((DIFF_HEADS, Q_BLOCK, 2 * Q_BLOCK), F32),
                        pltpu.VMEM((seq, DIFF_DV + LANES), BF16)],
        compiler_params=_params(("arbitrary", "arbitrary")),
        name="diff_prompt",
    )(table, dq, dk, dv, idx, lam_p, onorm)


def _diff_sample_body(tab_ref, q_ref, k_ref, v_ref, ck_ref, cv_ref, idxc_ref, idxn_ref, lam_ref, on_ref, o_ref,
                      bias_c, bias_n, *, seq, lam_init):
    groups = [(h, c) for h in range(DIFF_HEADS) for c in range(2)]

    @pl.when(pl.program_id(0) == 0)
    def _():
        for h in range(DIFF_HEADS):
            tile = _bias_from_buckets(idxc_ref[h], tab_ref, h)
            for g in (2 * h, 2 * h + 1):
                bias_c[g * seq:(g + 1) * seq, :] = tile
            bias_n[h] = _bias_from_buckets(idxn_ref[...], tab_ref, h)

    lam = _lambda(lam_ref, lam_init)
    head_cols = [slice(h * LANES, (h + 1) * LANES) for h in range(DIFF_HEADS)]
    q_groups = [q_c for cols in head_cols for q_c in _comp_masks(q_ref[:, cols])]
    s_old = _dot_nt(jnp.concatenate(q_groups, axis=0), ck_ref[...].astype(BF16)) + bias_c[...]
    s_new = jnp.concatenate([_dot_nt(q_g, k_ref[:, head_cols[h]]) + bias_n[h]
                             for q_g, (h, _) in zip(q_groups, groups)], axis=0)
    m = jnp.maximum(s_old.max(axis=-1, keepdims=True), s_new.max(axis=-1, keepdims=True))
    e_old = jnp.exp(s_old - m)
    e_new = jnp.exp(s_new - m)
    l = e_old.sum(axis=-1, keepdims=True) + e_new.sum(axis=-1, keepdims=True)
    o_old = _dot(e_old.astype(BF16), cv_ref[...].astype(BF16))
    e_new = e_new.astype(BF16)
    o_new = jnp.concatenate([_dot(e_new[g * seq:(g + 1) * seq, :], v_ref[:, head_cols[h]])
                             for g, (h, _) in enumerate(groups)], axis=0)
    o = (o_old + o_new) * (1.0 / l)
    for h, cols in enumerate(head_cols):
        o0 = o[(2 * h) * seq:(2 * h + 1) * seq, :]
        o1 = o[(2 * h + 1) * seq:(2 * h + 2) * seq, :]
        o_ref[:, cols] = _diff_finish(o0, o1, lam, on_ref, lam_init, o_ref.dtype)


def _diff_sample(table, dq, dk, dv, cache_k, cache_v, lam_p, onorm, batch, seq, past, lam_init):
    q_pos = past + jnp.arange(seq)
    idx_n = _bucket_tile(q_pos, past + jnp.arange(seq))
    idx_rows = jnp.repeat(_bucket_tile(q_pos, jnp.arange(past)), HEADS, axis=1)
    row_head = jnp.arange(past * HEADS) % HEADS
    idx_c = jnp.where(row_head[None, None, :] == jnp.arange(HEADS)[:, None, None], idx_rows[None], MASKED_BUCKET)
    new_q = pl.BlockSpec((seq, DIFF_QK), lambda b: (b, 0))
    old = lambda: pl.BlockSpec((past * HEADS, LANES), lambda b: (b, 0))
    const = lambda shape: pl.BlockSpec(shape, lambda b: (0,) * len(shape))
    n_groups = 2 * DIFF_HEADS
    return pl.pallas_call(
        functools.partial(_diff_sample_body, seq=seq, lam_init=lam_init),
        grid=(batch,),
        in_specs=[pl.BlockSpec(memory_space=pltpu.SMEM), new_q, new_q, new_q, old(), old(),
                  const((HEADS, seq, past * HEADS)), const((seq, seq)), const((4, DIFF_DH)), const((1, DIFF_DV))],
        out_specs=new_q,
        out_shape=jax.ShapeDtypeStruct((batch * seq, DIFF_V), BF16),
        scratch_shapes=[pltpu.VMEM((n_groups * seq, past * HEADS), F32), pltpu.VMEM((DIFF_HEADS, seq, seq), F32)],
        compiler_params=_params(("arbitrary",)),
        name="diff_sample",
    )(table, dq, dk, dv, cache_k, cache_v, idx_c, idx_n, lam_p, onorm)


def _softmax(s):
    e = jnp.exp(s - s.max(axis=-1, keepdims=True))
    return (e * (1.0 / e.sum(axis=-1, keepdims=True))).astype(BF16)


def _memattn_body(q_ref, k_ref, v_ref, o_ref, *, interleaved):
    head_cols = [slice(h * MEM_DH, (h + 1) * MEM_DH) for h in range(MEM_HEADS)]
    scale = MEM_DH ** -0.5
    if interleaved:
        tq = q_ref.shape[0]
        s = _dot_nt(jnp.concatenate([q_ref[:, cols] for cols in head_cols], axis=0), k_ref[...].astype(BF16)) * scale
        q_head = lax.broadcasted_iota(jnp.int32, s.shape, 0) // tq
        k_head = lax.broadcasted_iota(jnp.int32, s.shape, 1) % MEM_HEADS
        o = _dot(_softmax(jnp.where(q_head == k_head, s, NEG_INF)), v_ref[...].astype(BF16))
        for h, cols in enumerate(head_cols):
            o_ref[:, cols] = o[h * tq:(h + 1) * tq, :].astype(o_ref.dtype)
    else:
        scores = [_dot_nt(q_ref[:, cols], k_ref[:, cols]) * scale for cols in head_cols]
        probs = [_softmax(s) for s in scores]
        for cols, p in zip(head_cols, probs):
            o_ref[:, cols] = _dot(p, v_ref[:, cols]).astype(o_ref.dtype)


def _memattn(mq, mk, mv, batch, seq, n_mem):
    tq = min(MEMATTN_TILE, seq)
    nq = seq // tq
    q_spec = pl.BlockSpec((tq, MEM_W), lambda b, i: (b * nq + i, 0))
    interleaved = mk.shape[1] == LANES
    kv_spec = pl.BlockSpec((mk.shape[0] // batch, mk.shape[1]), lambda b, i: (b, 0))
    return pl.pallas_call(
        functools.partial(_memattn_body, interleaved=interleaved),
        grid=(batch, nq),
        in_specs=[q_spec, kv_spec, kv_spec],
        out_specs=q_spec,
        out_shape=jax.ShapeDtypeStruct((batch * seq, MEM_W), BF16),
        compiler_params=_params(("parallel", "arbitrary")),
        name="memattn",
    )(mq, mk, mv)


def _outproj_body(x_ref, g_ref, d_ref, m_ref, wg_ref, wd_ref, wm_ref, o_ref):
    o_ref[...] = (x_ref[...] + _dot(g_ref[...], wg_ref[...]) + _dot(d_ref[...], wd_ref[...])
                  + _dot(m_ref[...], wm_ref[...]))


def _outproj(x, g, d, m, wo):
    n = x.shape[0]
    tm = min(TOK_TILE, n)
    tok = lambda width: pl.BlockSpec((tm, width), lambda i: (i, 0))
    w_rows = lambda rows, blk: pl.BlockSpec((rows, D_MODEL), lambda i: (blk, 0))
    return pl.pallas_call(
        _outproj_body,
        grid=(n // tm,),
        in_specs=[tok(D_MODEL), tok(GLA_V), tok(DIFF_V), tok(MEM_W),
                  w_rows(GLA_V, 0), w_rows(DIFF_V, GLA_V // DIFF_V), w_rows(MEM_W, (GLA_V + DIFF_V) // MEM_W)],
        out_specs=tok(D_MODEL),
        out_shape=jax.ShapeDtypeStruct((n, D_MODEL), F32),
        compiler_params=_params(("parallel",)),
        name="outproj",
    )(x, g, d, m, wo, wo, wo)


def _row(v):
    return v.reshape(1, -1).astype(F32)


def kernel(x_prompt, x_sample, mem_prompt, cache_diff_k, cache_diff_v, state_gla, cache_mem_k, cache_mem_v, rel_bias_table, norm_ffn1, w_ffn1_in, w_ffn1_out, norm_mix, w_in, w_gla_g2, b_gla_g, gla_out_norm, diff_q_norm, diff_k_norm, diff_lambda, diff_out_norm, mem_norm, w_mem_kv, mem_q_norm, mem_k_norm, w_o, norm_ffn2, w_ffn2_in, w_ffn2_out, norm_final):
    depth = norm_ffn1.shape[0]
    assert depth == 1, "single-layer step"
    layer = 0
    batch, seq, _ = x_prompt.shape
    dec_batch, dec_seq, _ = x_sample.shape
    past = cache_diff_k.shape[2]
    n_mem = mem_prompt.shape[1]
    lam_init = 0.8 - 0.6 * math.exp(-0.3 * layer)

    w_g2 =jnp.pad(w_gla_g2[layer].astype(BF16), ((0, LANES - GLA_GATE_RANK), (0, 0)))
    b_g = _row(b_gla_g[layer])
    qn = _row(jnp.tile(diff_q_norm[layer], DIFF_QK // DIFF_DH))
    kn = _row(jnp.tile(diff_k_norm[layer], DIFF_QK // DIFF_DH))
    mqn = _row(jnp.tile(mem_q_norm[layer], MEM_HEADS))
    mkn = _row(jnp.tile(mem_k_norm[layer], MEM_HEADS))
    table = rel_bias_table.astype(F32).reshape(-1)
    lam_p = diff_lambda[layer].astype(F32)
    gla_on = _row(gla_out_norm[layer])
    diff_on = _row(diff_out_norm[layer])

    mk, mv, mk_b, mv_b = _memkv(mem_prompt.reshape(batch * n_mem, D_MODEL), _row(mem_norm[layer]),
                                w_mem_kv[layer].astype(BF16), mkn)

    n1 = _row(norm_ffn1[layer])
    xs1, *ffn1 = _ffn(x_sample.reshape(dec_batch * dec_seq, D_MODEL), n1, w_ffn1_in[layer], w_ffn1_in[layer],
                      w_ffn1_out[layer], up_col0=D_FF, ff_tile=FF_TILE_F32, emit_bf16=True)
    n_in_blocks = w_ffn2_in.shape[2] // LANES
    xp1, w2_in, w2_out, w_t, wo = _ffn(
        x_prompt.reshape(batch * seq, D_MODEL), n1, *ffn1, up_col0=0,
        cast_later=[(w_ffn2_in[layer], (D_MODEL, LANES), 0),
                    (w_ffn2_out[layer], (D_FF // n_in_blocks, D_MODEL), n_in_blocks),
                    (jnp.swapaxes(w_in[layer], 0, 1), (CAST_ROWS_W_IN, D_MODEL), 0),
                    (w_o[layer], (CAST_ROWS_W_O, D_MODEL), 0)])
    ffn2 = (w2_in, w2_in, w2_out)

    def layer_fn(x, b, t, chunk, s0, diff_fn, mem_k, mem_v):
        gq, gk, gv, gr, gg, dq, dk, dv, mq, dk_b, dv_b = _proj(x, _row(norm_mix[layer]), w_t, w_g2, b_g,
                                                               qn, kn, mqn)
        g_out, g_state = _gla(gq, gk, gg, gv, gr, gla_on, s0, b, t, chunk)
        d_out = diff_fn(dq, dk_b, dv_b)
        m_out = _memattn(mq, mem_k, mem_v, b, t, n_mem)
        x = _outproj(x, g_out, d_out, m_out, wo)
        x = _ffn(x, _row(norm_ffn2[layer]), *ffn2, up_col0=D_FF, final_gain=_row(norm_final[layer]))
        return x, dk, dv, g_state

    yp, dk_p, dv_p, g_p = layer_fn(
        xp1, batch, seq, CHUNK, None,
        lambda dq, dk, dv: _diff_prompt(table, dq, dk, dv, lam_p, diff_on, batch, seq, lam_init), mk_b, mv_b)
    ys, dk_s, dv_s, g_s = layer_fn(
        xs1, dec_batch, dec_seq, dec_seq, state_gla[layer],
        lambda dq, dk, dv: _diff_sample(table, dq, dk, dv, cache_diff_k[layer].reshape(-1, LANES),
                                        cache_diff_v[layer].reshape(-1, LANES),
                                        lam_p, diff_on, dec_batch, dec_seq, past, lam_init),
        cache_mem_k[layer].reshape(-1, LANES), cache_mem_v[layer].reshape(-1, LANES))

    head4 = lambda a, b, t: a.reshape(1, b, t, HEADS, LANES)
    return (yp.reshape(batch, seq, D_MODEL), ys.reshape(dec_batch, dec_seq, D_MODEL),
            head4(dk_p, batch, seq), head4(dv_p, batch, seq), g_p[None],
            head4(mk, batch, n_mem), head4(mv, batch, n_mem),
            head4(dk_s, dec_batch, dec_seq), head4(dv_s, dec_batch, dec_seq), g_s[None])
```

```python
import functools
import math

import jax
import jax.numpy as jnp
from jax import lax
from jax.experimental import pallas as pl
from jax.experimental.pallas import tpu as pltpu

F32 = jnp.float32
BF16 = jnp.bfloat16

D_MODEL = 2048
CHUNK = 64
EPS = 1e-6
NEG_INF = -1e30
GLA_HEADS, GLA_DK, GLA_DV, GLA_GATE_RANK, GLA_GATE_NORM = 4, 128, 256, 16, 16.0
DIFF_HEADS, DIFF_DH, DIFF_DV = 4, 64, 128
MEM_HEADS, MEM_DH = 4, 128
REL_BUCKETS, REL_MAX_DIST = 32, 128
D_FF = 5504
GLA_QK = GLA_HEADS * GLA_DK
GLA_V = GLA_HEADS * GLA_DV
DIFF_QK = DIFF_HEADS * 2 * DIFF_DH
DIFF_V = DIFF_HEADS * DIFF_DV
MEM_W = MEM_HEADS * MEM_DH
GLR_OFF = 2 * GLA_QK + 2 * GLA_V

LANES = 128
SUBLANES = 8
HEADS = 4
FF_TILE = 512
FF_TILE_F32 = 256
FFN_TOK_TILE = 1024
TOK_TILE = 512
GLA_TILE = 1024
MEMATTN_TILE = 2048
PROJ_TILE = 256
Q_BLOCK = 256
MASKED_BUCKET = REL_BUCKETS
MIB = 1024 * 1024
VMEM_CAP_MIB = 60
CAST_ROWS_W_IN = 48
CAST_ROWS_W_O = 16

_W_GQ, _W_GK, _W_GV, _W_GR, _W_GLR = 0, GLA_QK, 2 * GLA_QK, 2 * GLA_QK + GLA_V, GLR_OFF
_W_DQ = GLR_OFF + GLA_GATE_RANK
_W_DK = _W_DQ + DIFF_QK
_W_DV = _W_DK + DIFF_QK
_W_MQ = _W_DV + DIFF_V
IN_WIDTH = _W_MQ + MEM_W


def _dot(a, b):
    return jnp.dot(a, b, preferred_element_type=F32)


def _dot_nt(a, b):
    return lax.dot_general(a, b, (((1,), (1,)), ((), ())), preferred_element_type=F32)


def _dot_tn(a, b):
    return lax.dot_general(a, b, (((0,), (0,)), ((), ())), preferred_element_type=F32)


def _rms(x, gain):
    return x * lax.rsqrt(jnp.mean(x * x, axis=-1, keepdims=True) + EPS) * gain


def _params(sem, vmem_mib=48):
    return pltpu.CompilerParams(dimension_semantics=sem, vmem_limit_bytes=min(vmem_mib, VMEM_CAP_MIB) * MIB)


def _ffn_body(*refs, n_ff, overlap, final_norm, n_cast, emit_bf16):
    x_ref, g_ref, wg_ref, wu_ref, wo_ref = refs[:5]
    refs = refs[5:]
    if final_norm:
        fg_ref, refs = refs[0], refs[1:]
    cast_src, refs = refs[:n_cast], refs[n_cast:]
    o_ref, refs = refs[0], refs[1:]
    cast_dst, refs = refs[:n_cast], refs[n_cast:]
    if emit_bf16:
        (wg_out, wu_out, wo_out), refs = refs[:3], refs[3:]
    (xn_ref,) = refs
    j = pl.program_id(1)

    def tile(lo):
        for src, dst in zip(cast_src, cast_dst):
            dst[...] = src[...].astype(BF16)
        xn = xn_ref[...]
        if emit_bf16:
            def front(a, axis):
                a = a.astype(BF16)
                if lo == 0:
                    return a
                return jnp.concatenate([lax.slice_in_dim(a, lo, None, axis=axis),
                                        lax.slice_in_dim(a, 0, lo, axis=axis)], axis=axis)
            wg_out[...] = front(wg_ref[...], 1)
            wu_out[...] = front(wu_ref[...], 1)
            wo_out[...] = front(wo_ref[...], 0)
            live = wg_out.shape[1] - lo
            gate = _dot(xn, wg_out[:, :live])
            up = _dot(xn, wu_out[:, :live])
            act = (jax.nn.silu(gate) * up).astype(BF16)
            return 0.5 * _dot(act, wo_out[:live, :])
        gate = _dot(xn, wg_ref[:, lo:])
        up = _dot(xn, wu_ref[:, lo:])
        act = (jax.nn.silu(gate) * up).astype(BF16)
        return 0.5 * _dot(act, wo_ref[lo:, :])

    @pl.when(j == 0)
    def _():
        xn_ref[...] = _rms(x_ref[...], g_ref[...]).astype(BF16)
        o_ref[...] = x_ref[...] + tile(0)

    @pl.when((j > 0) & (j < n_ff - 1))
    def _():
        o_ref[...] += tile(0)

    @pl.when(j == n_ff - 1)
    def _():
        o = o_ref[...] + tile(overlap)
        o_ref[...] = _rms(o, fg_ref[...]) if final_norm else o


def _ffn(x, gain, w_gate, w_up, w_out, *, up_col0, final_gain=None, cast_later=(), ff_tile=FF_TILE,
         emit_bf16=False):
    n = x.shape[0]
    tm = min(FFN_TOK_TILE, n)
    n_ff = -(-D_FF // ff_tile)
    row = pl.BlockSpec((1, D_MODEL), lambda i, j: (0, 0))
    start = lambda j: jnp.minimum(j * (ff_tile // LANES), (D_FF - ff_tile) // LANES)
    w_in_tile = (pl.Element(D_MODEL), pl.Element(ff_tile))
    w_out_tile = (pl.Element(ff_tile), pl.Element(D_MODEL))
    gate_index = lambda i, j: (0, start(j) * LANES)
    out_index = lambda i, j: (start(j) * LANES, 0)
    tok = pl.BlockSpec((tm, D_MODEL), lambda i, j: (i, 0))
    in_specs = [
        tok,
        row,
        pl.BlockSpec(w_in_tile, gate_index),
        pl.BlockSpec(w_in_tile, lambda i, j: (0, (up_col0 // LANES + start(j)) * LANES)),
        pl.BlockSpec(w_out_tile, out_index),
    ]
    args = [x, gain, w_gate, w_up, w_out]
    if final_gain is not None:
        in_specs.append(row)
        args.append(final_gain)
    w_tile_bytes = D_MODEL * ff_tile * ((4 + 2) if emit_bf16 else 2)
    vmem_bytes = 4 * tm * D_MODEL * 4 + tm * D_MODEL * 2 + 6 * w_tile_bytes + 3 * tm * ff_tile * 4
    out_specs = [pl.BlockSpec((tm, D_MODEL), lambda i, j: (i, 0))]
    out_shape = [jax.ShapeDtypeStruct((n, D_MODEL), F32)]
    n_steps = (n // tm) * n_ff
    for mat, blk, first in cast_later:
        rows, cols = mat.shape
        assert rows % blk[0] == 0 and cols % blk[1] == 0 and (blk[0] == rows or blk[1] == cols)
        n_blk = (rows // blk[0]) * (cols // blk[1])
        assert first + n_blk <= n_steps
        by_rows = blk[1] == cols

        def index(i, j, first=first, n_blk=n_blk, by_rows=by_rows):
            k = jnp.clip(i * n_ff + j - first, 0, n_blk - 1)
            return (k, 0) if by_rows else (0, k)

        in_specs.append(pl.BlockSpec(blk, index))
        args.append(mat)
        out_specs.append(pl.BlockSpec(blk, index))
        out_shape.append(jax.ShapeDtypeStruct(mat.shape, BF16))
        vmem_bytes += 2 * blk[0] * blk[1] * (4 + 2)
    if emit_bf16:
        d_ff_pad = n_ff * ff_tile
        cols_j = pl.BlockSpec((D_MODEL, ff_tile), lambda i, j: (0, j))
        out_specs += [cols_j, cols_j, pl.BlockSpec((ff_tile, D_MODEL), lambda i, j: (j, 0))]
        out_shape += ([jax.ShapeDtypeStruct((D_MODEL, d_ff_pad), BF16)] * 2
                      + [jax.ShapeDtypeStruct((d_ff_pad, D_MODEL), BF16)])
    outs = pl.pallas_call(
        functools.partial(_ffn_body, n_ff=n_ff, overlap=n_ff * ff_tile - D_FF, final_norm=final_gain is not None,
                          n_cast=len(cast_later), emit_bf16=emit_bf16),
        grid=(n // tm, n_ff),
        in_specs=in_specs,
        out_specs=out_specs,
        out_shape=out_shape,
        scratch_shapes=[pltpu.VMEM((tm, D_MODEL), BF16)],
        compiler_params=_params(("arbitrary", "arbitrary"), vmem_bytes // MIB + 4),
        name="ffn",
    )(*args)
    return outs[0] if len(outs) == 1 else outs


def _group_rms(acc, gain_ref, width, scale, put):
    lane = lax.broadcasted_iota(jnp.int32, (1, LANES), 1)
    low = lane < width
    for c in range(acc.shape[1] // LANES):
        cols = slice(c * LANES, (c + 1) * LANES)
        xc = acc[:, cols]
        sq = xc * xc
        if width == LANES:
            r = lax.rsqrt(jnp.mean(sq, axis=-1, keepdims=True) + EPS)
        else:
            s_lo = jnp.sum(jnp.where(low, sq, 0.0), axis=-1, keepdims=True)
            s_hi = jnp.sum(jnp.where(low, 0.0, sq), axis=-1, keepdims=True)
            r = jnp.where(low, lax.rsqrt(s_lo / width + EPS), lax.rsqrt(s_hi / width + EPS))
        y = xc * r * gain_ref[:, cols]
        if scale != 1.0:
            y = y * scale
        put(c, y)


def _put_cols(ref):
    def put(c, y):
        ref[:, c * LANES:(c + 1) * LANES] = y.astype(ref.dtype)
    return put


def _put_heads_and_cols(heads_ref, cols_ref):
    def put(c, y):
        heads_ref[:, c, :] = y
        cols_ref[:, c * LANES:(c + 1) * LANES] = y.astype(cols_ref.dtype)
    return put


def _proj_body(x_ref, g_ref, wt_ref, wg2_ref, bg_ref, qn_ref, kn_ref, mqn_ref,
               gq_ref, gk_ref, gv_ref, gr_ref, gg_ref, dq_ref, dk_ref, dv_ref, mq_ref, dkb_ref, dvb_ref, xn_ref):
    xn_ref[...] = _rms(x_ref[...], g_ref[...]).astype(BF16)

    def cols(start, width):
        return _dot_nt(xn_ref[...], wt_ref[start:start + width, :])

    gq_ref[...] = cols(_W_GQ, GLA_QK) * (GLA_DK ** -0.5)
    gk_ref[...] = cols(_W_GK, GLA_QK)
    gv_ref[...] = cols(_W_GV, GLA_V).astype(BF16)
    gr_ref[...] = cols(_W_GR, GLA_V)
    glr = cols(_W_GLR, LANES)
    z = _dot(glr.astype(BF16), wg2_ref[...]) + bg_ref[...]
    gg_ref[...] = jax.nn.log_sigmoid(z) / GLA_GATE_NORM
    _group_rms(cols(_W_DQ, DIFF_QK), qn_ref, DIFF_DH, DIFF_DH ** -0.5, _put_cols(dq_ref))
    _group_rms(cols(_W_DK, DIFF_QK), kn_ref, DIFF_DH, 1.0, _put_heads_and_cols(dk_ref, dkb_ref))
    dv = cols(_W_DV, DIFF_V)
    put_v = _put_heads_and_cols(dv_ref, dvb_ref)
    for h in range(HEADS):
        put_v(h, dv[:, h * LANES:(h + 1) * LANES])
    _group_rms(cols(_W_MQ, MEM_W), mqn_ref, MEM_DH, 1.0, _put_cols(mq_ref))


def _proj(x, gain, wt, wg2, bg, qn, kn, mqn):
    n = x.shape[0]
    tm = min(PROJ_TILE, n)
    const = lambda shape: pl.BlockSpec(shape, lambda i: (0, 0))
    out = lambda width: pl.BlockSpec((tm, width), lambda i: (i, 0))
    heads = pl.BlockSpec((tm, HEADS, LANES), lambda i: (i, 0, 0))
    shp = lambda width, dt: jax.ShapeDtypeStruct((n, width), dt)
    shp_heads = jax.ShapeDtypeStruct((n, HEADS, LANES), F32)
    return pl.pallas_call(
        _proj_body,
        grid=(n // tm,),
        in_specs=[
            pl.BlockSpec((tm, D_MODEL), lambda i: (i, 0)),
            const((1, D_MODEL)),
            pl.BlockSpec((IN_WIDTH, D_MODEL), lambda i: (0, 0), pipeline_mode=pl.Buffered(1)),
            const((LANES, GLA_QK)),
            const((1, GLA_QK)),
            const((1, DIFF_QK)),
            const((1, DIFF_QK)),
            const((1, MEM_W)),
        ],
        out_specs=[out(GLA_QK), out(GLA_QK), out(GLA_V), out(GLA_V), out(GLA_QK),
                   out(DIFF_QK), heads, heads, out(MEM_W), out(DIFF_QK), out(DIFF_V)],
        out_shape=[shp(GLA_QK, F32), shp(GLA_QK, F32), shp(GLA_V, BF16), shp(GLA_V, F32), shp(GLA_QK, F32),
                   shp(DIFF_QK, BF16), shp_heads, shp_heads, shp(MEM_W, BF16), shp(DIFF_QK, BF16), shp(DIFF_V, BF16)],
        scratch_shapes=[pltpu.VMEM((tm, D_MODEL), BF16)],
        compiler_params=_params(("parallel",)),
        name="proj",
    )(x, gain, wt, wg2, bg, qn, kn, mqn)


def _memkv_body(x_ref, g_ref, w_ref, kn_ref, k_ref, v_ref, kb_ref, vb_ref):
    xn = _rms(x_ref[...], g_ref[...]).astype(BF16)
    _group_rms(_dot(xn, w_ref[:, :MEM_W]), kn_ref, MEM_DH, 1.0, _put_heads_and_cols(k_ref, kb_ref))
    v = _dot(xn, w_ref[:, MEM_W:])
    put_v = _put_heads_and_cols(v_ref, vb_ref)
    for h in range(HEADS):
        put_v(h, v[:, h * LANES:(h + 1) * LANES])


def _memkv(mem, gain, w, kn):
    n = mem.shape[0]
    tm = min(TOK_TILE, n)
    const = lambda shape: pl.BlockSpec(shape, lambda i: (0, 0))
    heads = pl.BlockSpec((tm, HEADS, LANES), lambda i: (i, 0, 0))
    dense = pl.BlockSpec((tm, MEM_W), lambda i: (i, 0))
    return pl.pallas_call(
        _memkv_body,
        grid=(n // tm,),
        in_specs=[pl.BlockSpec((tm, D_MODEL), lambda i: (i, 0)), const((1, D_MODEL)),
                  const((D_MODEL, 2 * MEM_W)), const((1, MEM_W))],
        out_specs=[heads, heads, dense, dense],
        out_shape=[jax.ShapeDtypeStruct((n, HEADS, LANES), F32)] * 2 + [jax.ShapeDtypeStruct((n, MEM_W), BF16)] * 2,
        compiler_params=_params(("parallel",)),
        name="memkv",
    )(mem, gain, w, kn)


def _split3(x):
    hi = x.astype(BF16)
    r1 = x - hi.astype(F32)
    mid = r1.astype(BF16)
    lo = (r1 - mid.astype(F32)).astype(BF16)
    return hi, mid, lo


def _gla_body(*refs, chunk, n_chunks, n_steps, has_state):
    if has_state:
        gq_ref, gk_ref, gg_ref, gv_ref, gr_ref, on_ref, s0_ref, go_ref, st_ref, state = refs
    else:
        gq_ref, gk_ref, gg_ref, gv_ref, gr_ref, on_ref, go_ref, st_ref, state = refs
    t = pl.program_id(1)

    @pl.when(t == 0)
    def _():
        for h in range(GLA_HEADS):
            if has_state:
                state[h] = s0_ref[0, h].T
            else:
                state[h] = jnp.zeros((GLA_DV, GLA_DK), F32)

    row = lax.broadcasted_iota(jnp.int32, (chunk, chunk), 0)
    col = lax.broadcasted_iota(jnp.int32, (chunk, chunk), 1)
    causal = row >= col
    tril = causal.astype(BF16)

    chunks = [slice(c * chunk, (c + 1) * chunk) for c in range(n_chunks)]
    kcs = [slice(h * GLA_DK, (h + 1) * GLA_DK) for h in range(GLA_HEADS)]
    vcs = [slice(h * GLA_DV, (h + 1) * GLA_DV) for h in range(GLA_HEADS)]

    bs = []
    for rows in chunks:
        g_hi, g_mid, g_lo = _split3(gg_ref[rows, :])
        bs.append(_dot(tril, g_hi) + _dot(tril, g_mid) + _dot(tril, g_lo))

    qes, kes, kds, decays = [], [], [], []
    for rows, b in zip(chunks, bs):
        b_last = b[chunk - 1:chunk, :]
        q = gq_ref[rows, :]
        k = gk_ref[rows, :]
        qes.append((q * jnp.exp(b)).astype(BF16))
        kes.append((k * jnp.exp(-b)).astype(BF16))
        kds.append((k * jnp.exp(b_last - b)).astype(BF16))
        decays.append(jnp.exp(b_last))

    a_s = [[jnp.where(causal, _dot_nt(qe[:, kc], ke[:, kc]), 0.0).astype(BF16) for kc in kcs]
           for qe, ke in zip(qes, kes)]
    incs = [[_dot_tn(gv_ref[rows, vc], kd[:, kc]) for kc, vc in zip(kcs, vcs)] for rows, kd in zip(chunks, kds)]

    s_in = []
    s_cur = [state[h] for h in range(GLA_HEADS)]
    for c in range(n_chunks):
        s_in.append([s.astype(BF16) for s in s_cur])
        s_cur = [s * decays[c][:, kc] + inc for s, kc, inc in zip(s_cur, kcs, incs[c])]
    for h in range(GLA_HEADS):
        state[h] = s_cur[h]

    for c, rows in enumerate(chunks):
        for h, (kc, vc) in enumerate(zip(kcs, vcs)):
            o = _dot_nt(qes[c][:, kc], s_in[c][h]) + _dot(a_s[c][h], gv_ref[rows, vc])
            go_ref[rows, vc] = (_rms(o, on_ref[...]) * jax.nn.silu(gr_ref[rows, vc])).astype(BF16)

    @pl.when(t == n_steps - 1)
    def _():
        for h in range(GLA_HEADS):
            st_ref[0, h] = state[h].T


def _gla(gq, gk, gg, gv, gr, onorm, s0, batch, seq, chunk):
    tt = min(GLA_TILE, seq)
    n_steps = seq // tt
    tok = lambda width: pl.BlockSpec((tt, width), lambda b, t: (b * n_steps + t, 0))
    st_spec = pl.BlockSpec((1, GLA_HEADS, GLA_DK, GLA_DV), lambda b, t: (b, 0, 0, 0))
    in_specs = [tok(GLA_QK), tok(GLA_QK), tok(GLA_QK), tok(GLA_V), tok(GLA_V),
                pl.BlockSpec((1, GLA_DV), lambda b, t: (0, 0))]
    args = [gq, gk, gg, gv, gr, onorm]
    if s0 is not None:
        in_specs.append(st_spec)
        args.append(s0)
    return pl.pallas_call(
        functools.partial(_gla_body, chunk=chunk, n_chunks=tt // chunk, n_steps=n_steps, has_state=s0 is not None),
        grid=(batch, n_steps),
        in_specs=in_specs,
        out_specs=[tok(GLA_V), st_spec],
        out_shape=[jax.ShapeDtypeStruct((batch * seq, GLA_V), BF16),
                   jax.ShapeDtypeStruct((batch, GLA_HEADS, GLA_DK, GLA_DV), F32)],
        scratch_shapes=[pltpu.VMEM((GLA_HEADS, GLA_DV, GLA_DK), F32)],
        compiler_params=_params(("parallel", "arbitrary")),
        name="gla",
    )(*args)


def _t5_bucket(rel):
    nb = REL_BUCKETS // 2
    max_exact = nb // 2
    ret = jnp.where(rel > 0, nb, 0)
    n = jnp.abs(rel)
    nf = jnp.maximum(n, 1).astype(F32)
    large = max_exact + (jnp.log(nf / max_exact) / math.log(REL_MAX_DIST / max_exact)
                         * (nb - max_exact)).astype(jnp.int32)
    large = jnp.minimum(large, nb - 1)
    return ret + jnp.where(n < max_exact, n, large)


def _bucket_tile(q_pos, k_pos):
    visible = (k_pos[None, :] // CHUNK) <= (q_pos[:, None] // CHUNK)
    return jnp.where(visible, _t5_bucket(k_pos[None, :] - q_pos[:, None]), MASKED_BUCKET).astype(jnp.int32)


def _bias_from_buckets(idx, tab_ref, head):
    def step(bk, acc):
        return jnp.where(idx == bk, tab_ref[bk * DIFF_HEADS + head], acc)
    return lax.fori_loop(0, REL_BUCKETS, step, jnp.full(idx.shape, NEG_INF, F32))


def _lambda(lam_ref, lam_init):
    l = lam_ref[...]
    return (jnp.exp(jnp.sum(l[0:1] * l[1:2], axis=-1, keepdims=True))
            - jnp.exp(jnp.sum(l[2:3] * l[3:4], axis=-1, keepdims=True)) + lam_init)


def _comp_masks(q):
    lane = lax.broadcasted_iota(jnp.int32, q.shape, 1)
    zero = jnp.zeros_like(q)
    return jnp.where(lane < DIFF_DH, q, zero), jnp.where(lane < DIFF_DH, zero, q)


def _diff_finish(o0, o1, lam, on_ref, lam_init, out_dtype):
    o = o0 - lam * o1
    return (_rms(o, on_ref[...]) * (1.0 - lam_init)).astype(out_dtype)


def _scores(q_c, parts):
    return [_dot_nt(q_c, k) + bias if jnp.ndim(bias) == 2 else _dot_nt(q_c, k) for k, _, bias in parts]


def _softmax_pv(scores, parts):
    shifts = [0.0 if jnp.ndim(bias) == 2 else bias for _, _, bias in parts]
    m = None
    for s, shift in zip(scores, shifts):
        part_max = s.max(axis=-1, keepdims=True) + shift
        m = part_max if m is None else jnp.maximum(m, part_max)
    ol = 0.0
    for s, shift, (_, v1, _) in zip(scores, shifts, parts):
        ol = ol + _dot(jnp.exp(s - (m - shift)).astype(BF16), v1)
    return ol[:, :DIFF_DV] * (1.0 / ol[:, DIFF_DV:])


def _diff_prompt_body(tab_ref, q_ref, kb, vb, idx_ref, lam_ref, on_ref, o_ref, bias, v1, *, seq, lam_init):
    b = pl.program_id(0)
    h = pl.program_id(1)

    @pl.when(b == 0)
    def _():
        per_offset = _bias_from_buckets(idx_ref[...], tab_ref, h)[0:1]
        rows = jnp.broadcast_to(per_offset, (Q_BLOCK, 4 * Q_BLOCK))
        tile = pltpu.roll(rows, 3 * Q_BLOCK, 1, stride=1, stride_axis=0)[:, :2 * Q_BLOCK]
        q_pos = Q_BLOCK + lax.broadcasted_iota(jnp.int32, tile.shape, 0)
        k_pos = lax.broadcasted_iota(jnp.int32, tile.shape, 1)
        bias[h] = jnp.where(k_pos // CHUNK <= q_pos // CHUNK, tile, NEG_INF)

    v1[:, :DIFF_DV] = vb[...]
    v1[:, DIFF_DV:] = jnp.ones((seq, LANES), BF16)
    lam = _lambda(lam_ref, lam_init)
    far_bias = tab_ref[(REL_BUCKETS // 2 - 1) * DIFF_HEADS + h]

    def key_parts(i):
        if i == 0:
            return [(kb[0:Q_BLOCK, :], v1[0:Q_BLOCK, :], bias[h, :, Q_BLOCK:])]
        near0 = (i - 1) * Q_BLOCK
        near = slice(near0, near0 + 2 * Q_BLOCK)
        parts = [(kb[near, :], v1[near, :], bias[h])]
        if near0 > 0:
            parts.append((kb[0:near0, :], v1[0:near0, :], far_bias))
        return parts

    items = [(i, c) for i in range(seq // Q_BLOCK) for c in range(2)]
    comps = {}

    def scores_of(item):
        i, c = item
        if i not in comps:
            comps[i] = _comp_masks(q_ref[i * Q_BLOCK:(i + 1) * Q_BLOCK, :])
        return _scores(comps[i][c], key_parts(i))

    outs = {}
    ahead = scores_of(items[0])
    for n, (i, c) in enumerate(items):
        cur = ahead
        if n + 1 < len(items):
            ahead = scores_of(items[n + 1])
        outs[c] = _softmax_pv(cur, key_parts(i))
        if c == 1:
            o_ref[i * Q_BLOCK:(i + 1) * Q_BLOCK, :] = _diff_finish(outs[0], outs[1], lam, on_ref, lam_init,
                                                                    o_ref.dtype)


def _diff_prompt(table, dq, dk, dv, lam_p, onorm, batch, seq, lam_init):
    offsets = jnp.arange(4 * Q_BLOCK) - 2 * Q_BLOCK
    idx = jnp.broadcast_to(_t5_bucket(offsets).astype(jnp.int32)[None, :], (SUBLANES, 4 * Q_BLOCK))
    head = lambda: pl.BlockSpec((seq, LANES), lambda b, h: (b, h))
    return pl.pallas_call(
        functools.partial(_diff_prompt_body, seq=seq, lam_init=lam_init),
        grid=(batch, DIFF_HEADS),
        in_specs=[
            pl.BlockSpec(memory_space=pltpu.SMEM),
            head(), head(), head(),
            pl.BlockSpec((SUBLANES, 4 * Q_BLOCK), lambda b, h: (0, 0)),
            pl.BlockSpec((4, DIFF_DH), lambda b, h: (0, 0)),
            pl.BlockSpec((1, DIFF_DV), lambda b, h: (0, 0)),
        ],
        out_specs=head(),
        out_shape=jax.ShapeDtypeStruct((batch * seq, DIFF_V), BF16),
        scratch_shapes=[pltpu.VMEM((DIFF_HEADS, Q_BLOCK, 2 * Q_BLOCK), F32),
                        pltpu.VMEM((seq, DIFF_DV + LANES), BF16)],
        compiler_params=_params(("arbitrary", "arbitrary")),
        name="diff_prompt",
    )(table, dq, dk, dv, idx, lam_p, onorm)


def _diff_sample_body(tab_ref, q_ref, k_ref, v_ref, ck_ref, cv_ref, idxc_ref, idxn_ref, lam_ref, on_ref, o_ref,
                      bias_c, bias_n, *, seq, lam_init):
    groups = [(h, c) for h in range(DIFF_HEADS) for c in range(2)]

    @pl.when(pl.program_id(0) == 0)
    def _():
        for h in range(DIFF_HEADS):
            per_offset = _bias_from_buckets(idxc_ref[h], tab_ref, h)[0:1]
            rows = jnp.broadcast_to(per_offset, (seq, per_offset.shape[1]))
            tile = pltpu.roll(rows, rows.shape[1] - HEADS * seq, 1, stride=HEADS, stride_axis=0)[:, :bias_c.shape[1]]
            for g in (2 * h, 2 * h + 1):
                bias_c[g * seq:(g + 1) * seq, :] = tile
            bias_n[h] = _bias_from_buckets(idxn_ref[...], tab_ref, h)

    lam = _lambda(lam_ref, lam_init)
    head_cols = [slice(h * LANES, (h + 1) * LANES) for h in range(DIFF_HEADS)]
    q_groups = [q_c for cols in head_cols for q_c in _comp_masks(q_ref[:, cols])]
    s_old = _dot_nt(jnp.concatenate(q_groups, axis=0), ck_ref[...].astype(BF16)) + bias_c[...]
    s_new = jnp.concatenate([_dot_nt(q_g, k_ref[:, head_cols[h]]) + bias_n[h]
                             for q_g, (h, _) in zip(q_groups, groups)], axis=0)
    m = jnp.maximum(s_old.max(axis=-1, keepdims=True), s_new.max(axis=-1, keepdims=True))
    e_old = jnp.exp(s_old - m)
    e_new = jnp.exp(s_new - m)
    l = e_old.sum(axis=-1, keepdims=True) + e_new.sum(axis=-1, keepdims=True)
    o_old = _dot(e_old.astype(BF16), cv_ref[...].astype(BF16))
    e_new = e_new.astype(BF16)
    o_new = jnp.concatenate([_dot(e_new[g * seq:(g + 1) * seq, :], v_ref[:, head_cols[h]])
                             for g, (h, _) in enumerate(groups)], axis=0)
    o = (o_old + o_new) * (1.0 / l)
    for h, cols in enumerate(head_cols):
        o0 = o[(2 * h) * seq:(2 * h + 1) * seq, :]
        o1 = o[(2 * h + 1) * seq:(2 * h + 2) * seq, :]
        o_ref[:, cols] = _diff_finish(o0, o1, lam, on_ref, lam_init, o_ref.dtype)


def _diff_sample(table, dq, dk, dv, cache_k, cache_v, lam_p, onorm, batch, seq, past, lam_init):
    q_pos = past + jnp.arange(seq)
    idx_n = _bucket_tile(q_pos, past + jnp.arange(seq))
    row = jnp.arange(HEADS * (past + seq))
    idx_row = _t5_bucket(row // HEADS - seq - past)
    idx_c = jnp.where((row % HEADS)[None, :] == jnp.arange(HEADS)[:, None], idx_row[None, :], MASKED_BUCKET)
    idx_c = jnp.broadcast_to(idx_c.astype(jnp.int32)[:, None, :], (HEADS, SUBLANES, row.size))
    new_q = pl.BlockSpec((seq, DIFF_QK), lambda b: (b, 0))
    old = lambda: pl.BlockSpec((past * HEADS, LANES), lambda b: (b, 0))
    const = lambda shape: pl.BlockSpec(shape, lambda b: (0,) * len(shape))
    n_groups = 2 * DIFF_HEADS
    return pl.pallas_call(
        functools.partial(_diff_sample_body, seq=seq, lam_init=lam_init),
        grid=(batch,),
        in_specs=[pl.BlockSpec(memory_space=pltpu.SMEM), new_q, new_q, new_q, old(), old(),
                  const((HEADS, SUBLANES, HEADS * (past + seq))), const((seq, seq)), const((4, DIFF_DH)), const((1, DIFF_DV))],
        out_specs=new_q,
        out_shape=jax.ShapeDtypeStruct((batch * seq, DIFF_V), BF16),
        scratch_shapes=[pltpu.VMEM((n_groups * seq, past * HEADS), F32), pltpu.VMEM((DIFF_HEADS, seq, seq), F32)],
        compiler_params=_params(("arbitrary",)),
        name="diff_sample",
    )(table, dq, dk, dv, cache_k, cache_v, idx_c, idx_n, lam_p, onorm)


def _softmax(s):
    e = jnp.exp(s - s.max(axis=-1, keepdims=True))
    return (e * (1.0 / e.sum(axis=-1, keepdims=True))).astype(BF16)


def _memattn_body(q_ref, k_ref, v_ref, o_ref, *, interleaved):
    head_cols = [slice(h * MEM_DH, (h + 1) * MEM_DH) for h in range(MEM_HEADS)]
    scale = MEM_DH ** -0.5
    if interleaved:
        tq = q_ref.shape[0]
        s = _dot_nt(jnp.concatenate([q_ref[:, cols] for cols in head_cols], axis=0), k_ref[...].astype(BF16)) * scale
        q_head = lax.broadcasted_iota(jnp.int32, s.shape, 0) // tq
        k_head = lax.broadcasted_iota(jnp.int32, s.shape, 1) % MEM_HEADS
        o = _dot(_softmax(jnp.where(q_head == k_head, s, NEG_INF)), v_ref[...].astype(BF16))
        for h, cols in enumerate(head_cols):
            o_ref[:, cols] = o[h * tq:(h + 1) * tq, :].astype(o_ref.dtype)
    else:
        scores = [_dot_nt(q_ref[:, cols], k_ref[:, cols]) * scale for cols in head_cols]
        probs = [_softmax(s) for s in scores]
        for cols, p in zip(head_cols, probs):
            o_ref[:, cols] = _dot(p, v_ref[:, cols]).astype(o_ref.dtype)


def _memattn(mq, mk, mv, batch, seq, n_mem):
    tq = min(MEMATTN_TILE, seq)
    nq = seq // tq
    q_spec = pl.BlockSpec((tq, MEM_W), lambda b, i: (b * nq + i, 0))
    interleaved = mk.shape[1] == LANES
    kv_spec = pl.BlockSpec((mk.shape[0] // batch, mk.shape[1]), lambda b, i: (b, 0))
    return pl.pallas_call(
        functools.partial(_memattn_body, interleaved=interleaved),
        grid=(batch, nq),
        in_specs=[q_spec, kv_spec, kv_spec],
        out_specs=q_spec,
        out_shape=jax.ShapeDtypeStruct((batch * seq, MEM_W), BF16),
        compiler_params=_params(("parallel", "arbitrary")),
        name="memattn",
    )(mq, mk, mv)


def _outproj_body(x_ref, g_ref, d_ref, m_ref, wg_ref, wd_ref, wm_ref, o_ref):
    o_ref[...] = (x_ref[...] + _dot(g_ref[...], wg_ref[...]) + _dot(d_ref[...], wd_ref[...])
                  + _dot(m_ref[...], wm_ref[...]))


def _outproj(x, g, d, m, wo):
    n = x.shape[0]
    tm = min(TOK_TILE, n)
    tok = lambda width: pl.BlockSpec((tm, width), lambda i: (i, 0))
    w_rows = lambda rows, blk: pl.BlockSpec((rows, D_MODEL), lambda i: (blk, 0))
    return pl.pallas_call(
        _outproj_body,
        grid=(n // tm,),
        in_specs=[tok(D_MODEL), tok(GLA_V), tok(DIFF_V), tok(MEM_W),
                  w_rows(GLA_V, 0), w_rows(DIFF_V, GLA_V // DIFF_V), w_rows(MEM_W, (GLA_V + DIFF_V) // MEM_W)],
        out_specs=tok(D_MODEL),
        out_shape=jax.ShapeDtypeStruct((n, D_MODEL), F32),
        compiler_params=_params(("parallel",)),
        name="outproj",
    )(x, g, d, m, wo, wo, wo)


def _row(v):
    return v.reshape(1, -1).astype(F32)


def kernel(x_prompt, x_sample, mem_prompt, cache_diff_k, cache_diff_v, state_gla, cache_mem_k, cache_mem_v, rel_bias_table, norm_ffn1, w_ffn1_in, w_ffn1_out, norm_mix, w_in, w_gla_g2, b_gla_g, gla_out_norm, diff_q_norm, diff_k_norm, diff_lambda, diff_out_norm, mem_norm, w_mem_kv, mem_q_norm, mem_k_norm, w_o, norm_ffn2, w_ffn2_in, w_ffn2_out, norm_final):
    depth = norm_ffn1.shape[0]
    assert depth == 1, "single-layer step"
    layer = 0
    batch, seq, _ = x_prompt.shape
    dec_batch, dec_seq, _ = x_sample.shape
    past = cache_diff_k.shape[2]
    n_mem = mem_prompt.shape[1]
    lam_init = 0.8 - 0.6 * math.exp(-0.3 * layer)

    w_g2 =jnp.pad(w_gla_g2[layer].astype(BF16), ((0, LANES - GLA_GATE_RANK), (0, 0)))
    b_g = _row(b_gla_g[layer])
    qn = _row(jnp.tile(diff_q_norm[layer], DIFF_QK // DIFF_DH))
    kn = _row(jnp.tile(diff_k_norm[layer], DIFF_QK // DIFF_DH))
    mqn = _row(jnp.tile(mem_q_norm[layer], MEM_HEADS))
    mkn = _row(jnp.tile(mem_k_norm[layer], MEM_HEADS))
    table = rel_bias_table.astype(F32).reshape(-1)
    lam_p = diff_lambda[layer].astype(F32)
    gla_on = _row(gla_out_norm[layer])
    diff_on = _row(diff_out_norm[layer])

    mk, mv, mk_b, mv_b = _memkv(mem_prompt.reshape(batch * n_mem, D_MODEL), _row(mem_norm[layer]),
                                w_mem_kv[layer].astype(BF16), mkn)

    n1 = _row(norm_ffn1[layer])
    xs1, *ffn1 = _ffn(x_sample.reshape(dec_batch * dec_seq, D_MODEL), n1, w_ffn1_in[layer], w_ffn1_in[layer],
                      w_ffn1_out[layer], up_col0=D_FF, ff_tile=FF_TILE_F32, emit_bf16=True)
    n_in_blocks = w_ffn2_in.shape[2] // LANES
    xp1, w2_in, w2_out, w_t, wo = _ffn(
        x_prompt.reshape(batch * seq, D_MODEL), n1, *ffn1, up_col0=0,
        cast_later=[(w_ffn2_in[layer], (D_MODEL, LANES), 0),
                    (w_ffn2_out[layer], (D_FF // n_in_blocks, D_MODEL), n_in_blocks),
                    (jnp.swapaxes(w_in[layer], 0, 1), (CAST_ROWS_W_IN, D_MODEL), 0),
                    (w_o[layer], (CAST_ROWS_W_O, D_MODEL), 0)])
    ffn2 = (w2_in, w2_in, w2_out)

    def layer_fn(x, b, t, chunk, s0, diff_fn, mem_k, mem_v):
        gq, gk, gv, gr, gg, dq, dk, dv, mq, dk_b, dv_b = _proj(x, _row(norm_mix[layer]), w_t, w_g2, b_g,
                                                               qn, kn, mqn)
        g_out, g_state = _gla(gq, gk, gg, gv, gr, gla_on, s0, b, t, chunk)
        d_out = diff_fn(dq, dk_b, dv_b)
        m_out = _memattn(mq, mem_k, mem_v, b, t, n_mem)
        x = _outproj(x, g_out, d_out, m_out, wo)
        x = _ffn(x, _row(norm_ffn2[layer]), *ffn2, up_col0=D_FF, final_gain=_row(norm_final[layer]))
        return x, dk, dv, g_state

    yp, dk_p, dv_p, g_p = layer_fn(
        xp1, batch, seq, CHUNK, None,
        lambda dq, dk, dv: _diff_prompt(table, dq, dk, dv, lam_p, diff_on, batch, seq, lam_init), mk_b, mv_b)
    ys, dk_s, dv_s, g_s = layer_fn(
        xs1, dec_batch, dec_seq, dec_seq, state_gla[layer],
        lambda dq, dk, dv: _diff_sample(table, dq, dk, dv, cache_diff_k[layer].reshape(-1, LANES),
                                        cache_diff_v[layer].reshape(-1, LANES),
                                        lam_p, diff_on, dec_batch, dec_seq, past, lam_init),
        cache_mem_k[layer].reshape(-1, LANES), cache_mem_v[layer].reshape(-1, LANES))

    head4 = lambda a, b, t: a.reshape(1, b, t, HEADS, LANES)
    return (yp.reshape(batch, seq, D_MODEL), ys.reshape(dec_batch, dec_seq, D_MODEL),
            head4(dk_p, batch, seq), head4(dv_p, batch, seq), g_p[None],
            head4(mk, batch, n_mem), head4(mv, batch, n_mem),
            head4(dk_s, dec_batch, dec_seq), head4(dv_s, dec_batch, dec_seq), g_s[None])
```

```python
import functools
import math

import jax
import jax.numpy as jnp
from jax import lax
from jax.experimental import pallas as pl
from jax.experimental.pallas import tpu as pltpu

F32 = jnp.float32
BF16 = jnp.bfloat16

D_MODEL = 2048
CHUNK = 64
EPS = 1e-6
NEG_INF = -1e30
GLA_HEADS, GLA_DK, GLA_DV, GLA_GATE_RANK, GLA_GATE_NORM = 4, 128, 256, 16, 16.0
DIFF_HEADS, DIFF_DH, DIFF_DV = 4, 64, 128
MEM_HEADS, MEM_DH = 4, 128
REL_BUCKETS, REL_MAX_DIST = 32, 128
D_FF = 5504
GLA_QK = GLA_HEADS * GLA_DK
GLA_V = GLA_HEADS * GLA_DV
DIFF_QK = DIFF_HEADS * 2 * DIFF_DH
DIFF_V = DIFF_HEADS * DIFF_DV
MEM_W = MEM_HEADS * MEM_DH
GLR_OFF = 2 * GLA_QK + 2 * GLA_V

LANES = 128
SUBLANES = 8
HEADS = 4
FF_TILE = 512
FF_TILE_F32 = 256
FFN_TOK_TILE = 1024
TOK_TILE = 512
GLA_TILE = 1024
MEMATTN_TILE = 2048
PROJ_TILE = 256
Q_BLOCK = 256
MASKED_BUCKET = REL_BUCKETS
MIB = 1024 * 1024
VMEM_CAP_MIB = 60
CAST_ROWS_W_IN = 48
CAST_ROWS_W_O = 16

_W_GQ, _W_GK, _W_GV, _W_GR, _W_GLR = 0, GLA_QK, 2 * GLA_QK, 2 * GLA_QK + GLA_V, GLR_OFF
_W_DQ = GLR_OFF + GLA_GATE_RANK
_W_DK = _W_DQ + DIFF_QK
_W_DV = _W_DK + DIFF_QK
_W_MQ = _W_DV + DIFF_V
IN_WIDTH = _W_MQ + MEM_W


def _dot(a, b):
    return jnp.dot(a, b, preferred_element_type=F32)


def _dot_nt(a, b):
    return lax.dot_general(a, b, (((1,), (1,)), ((), ())), preferred_element_type=F32)


def _dot_tn(a, b):
    return lax.dot_general(a, b, (((0,), (0,)), ((), ())), preferred_element_type=F32)


def _rms(x, gain):
    return x * lax.rsqrt(jnp.mean(x * x, axis=-1, keepdims=True) + EPS) * gain


def _params(sem, vmem_mib=48):
    return pltpu.CompilerParams(dimension_semantics=sem, vmem_limit_bytes=min(vmem_mib, VMEM_CAP_MIB) * MIB)


def _ffn_body(*refs, n_ff, overlap, final_norm, n_cast, emit_bf16):
    x_ref, g_ref, wg_ref, wu_ref, wo_ref = refs[:5]
    refs = refs[5:]
    if final_norm:
        fg_ref, refs = refs[0], refs[1:]
    cast_src, refs = refs[:n_cast], refs[n_cast:]
    o_ref, refs = refs[0], refs[1:]
    cast_dst, refs = refs[:n_cast], refs[n_cast:]
    if emit_bf16:
        (wg_out, wu_out, wo_out), refs = refs[:3], refs[3:]
    (xn_ref,) = refs
    j = pl.program_id(1)

    def tile(lo):
        for src, dst in zip(cast_src, cast_dst):
            dst[...] = src[...].astype(BF16)
        xn = xn_ref[...]
        if emit_bf16:
            def front(a, axis):
                a = a.astype(BF16)
                if lo == 0:
                    return a
                return jnp.concatenate([lax.slice_in_dim(a, lo, None, axis=axis),
                                        lax.slice_in_dim(a, 0, lo, axis=axis)], axis=axis)
            wg_out[...] = front(wg_ref[...], 1)
            wu_out[...] = front(wu_ref[...], 1)
            wo_out[...] = front(wo_ref[...], 0)
            live = wg_out.shape[1] - lo
            gate = _dot(xn, wg_out[:, :live])
            up = _dot(xn, wu_out[:, :live])
            act = (jax.nn.silu(gate) * up).astype(BF16)
            return 0.5 * _dot(act, wo_out[:live, :])
        gate = _dot(xn, wg_ref[:, lo:])
        up = _dot(xn, wu_ref[:, lo:])
        act = (jax.nn.silu(gate) * up).astype(BF16)
        return 0.5 * _dot(act, wo_ref[lo:, :])

    @pl.when(j == 0)
    def _():
        xn_ref[...] = _rms(x_ref[...], g_ref[...]).astype(BF16)
        o_ref[...] = x_ref[...] + tile(0)

    @pl.when((j > 0) & (j < n_ff - 1))
    def _():
        o_ref[...] += tile(0)

    @pl.when(j == n_ff - 1)
    def _():
        o = o_ref[...] + tile(overlap)
        o_ref[...] = _rms(o, fg_ref[...]) if final_norm else o


def _ffn(x, gain, w_gate, w_up, w_out, *, up_col0, final_gain=None, cast_later=(), ff_tile=FF_TILE,
         emit_bf16=False):
    n = x.shape[0]
    tm = min(FFN_TOK_TILE, n)
    n_ff = -(-D_FF // ff_tile)
    row = pl.BlockSpec((1, D_MODEL), lambda i, j: (0, 0))
    start = lambda j: jnp.minimum(j * (ff_tile // LANES), (D_FF - ff_tile) // LANES)
    w_in_tile = (pl.Element(D_MODEL), pl.Element(ff_tile))
    w_out_tile = (pl.Element(ff_tile), pl.Element(D_MODEL))
    gate_index = lambda i, j: (0, start(j) * LANES)
    out_index = lambda i, j: (start(j) * LANES, 0)
    tok = pl.BlockSpec((tm, D_MODEL), lambda i, j: (i, 0))
    in_specs = [
        tok,
        row,
        pl.BlockSpec(w_in_tile, gate_index),
        pl.BlockSpec(w_in_tile, lambda i, j: (0, (up_col0 // LANES + start(j)) * LANES)),
        pl.BlockSpec(w_out_tile, out_index),
    ]
    args = [x, gain, w_gate, w_up, w_out]
    if final_gain is not None:
        in_specs.append(row)
        args.append(final_gain)
    w_tile_bytes = D_MODEL * ff_tile * ((4 + 2) if emit_bf16 else 2)
    vmem_bytes = 4 * tm * D_MODEL * 4 + tm * D_MODEL * 2 + 6 * w_tile_bytes + 3 * tm * ff_tile * 4
    out_specs = [pl.BlockSpec((tm, D_MODEL), lambda i, j: (i, 0))]
    out_shape = [jax.ShapeDtypeStruct((n, D_MODEL), F32)]
    n_steps = (n // tm) * n_ff
    for mat, blk, first in cast_later:
        rows, cols = mat.shape
        assert rows % blk[0] == 0 and cols % blk[1] == 0 and (blk[0] == rows or blk[1] == cols)
        n_blk = (rows // blk[0]) * (cols // blk[1])
        assert first + n_blk <= n_steps
        by_rows = blk[1] == cols

        def index(i, j, first=first, n_blk=n_blk, by_rows=by_rows):
            k = jnp.clip(i * n_ff + j - first, 0, n_blk - 1)
            return (k, 0) if by_rows else (0, k)

        in_specs.append(pl.BlockSpec(blk, index))
        args.append(mat)
        out_specs.append(pl.BlockSpec(blk, index))
        out_shape.append(jax.ShapeDtypeStruct(mat.shape, BF16))
        vmem_bytes += 2 * blk[0] * blk[1] * (4 + 2)
    if emit_bf16:
        d_ff_pad = n_ff * ff_tile
        cols_j = pl.BlockSpec((D_MODEL, ff_tile), lambda i, j: (0, j))
        out_specs += [cols_j, cols_j, pl.BlockSpec((ff_tile, D_MODEL), lambda i, j: (j, 0))]
        out_shape += ([jax.ShapeDtypeStruct((D_MODEL, d_ff_pad), BF16)] * 2
                      + [jax.ShapeDtypeStruct((d_ff_pad, D_MODEL), BF16)])
    outs = pl.pallas_call(
        functools.partial(_ffn_body, n_ff=n_ff, overlap=n_ff * ff_tile - D_FF, final_norm=final_gain is not None,
                          n_cast=len(cast_later), emit_bf16=emit_bf16),
        grid=(n // tm, n_ff),
        in_specs=in_specs,
        out_specs=out_specs,
        out_shape=out_shape,
        scratch_shapes=[pltpu.VMEM((tm, D_MODEL), BF16)],
        compiler_params=_params(("arbitrary", "arbitrary"), vmem_bytes // MIB + 4),
        name="ffn",
    )(*args)
    return outs[0] if len(outs) == 1 else outs


def _group_rms(acc, gain_ref, width, scale, put):
    lane = lax.broadcasted_iota(jnp.int32, (1, LANES), 1)
    low = lane < width
    for c in range(acc.shape[1] // LANES):
        cols = slice(c * LANES, (c + 1) * LANES)
        xc = acc[:, cols]
        sq = xc * xc
        if width == LANES:
            r = lax.rsqrt(jnp.mean(sq, axis=-1, keepdims=True) + EPS)
        else:
            s_lo = jnp.sum(jnp.where(low, sq, 0.0), axis=-1, keepdims=True)
            s_hi = jnp.sum(jnp.where(low, 0.0, sq), axis=-1, keepdims=True)
            r = jnp.where(low, lax.rsqrt(s_lo / width + EPS), lax.rsqrt(s_hi / width + EPS))
        y = xc * r * gain_ref[:, cols]
        if scale != 1.0:
            y = y * scale
        put(c, y)


def _put_cols(ref):
    def put(c, y):
        ref[:, c * LANES:(c + 1) * LANES] = y.astype(ref.dtype)
    return put


def _put_heads_and_cols(heads_ref, cols_ref):
    def put(c, y):
        heads_ref[:, c, :] = y
        cols_ref[:, c * LANES:(c + 1) * LANES] = y.astype(cols_ref.dtype)
    return put


def _proj_body(x_ref, g_ref, wt_ref, wg2_ref, bg_ref, qn_ref, kn_ref, mqn_ref,
               gq_ref, gk_ref, gv_ref, gr_ref, gg_ref, dq_ref, dk_ref, dv_ref, mq_ref, dkb_ref, dvb_ref, xn_ref):
    xn_ref[...] = _rms(x_ref[...], g_ref[...]).astype(BF16)

    def cols(start, width):
        return _dot_nt(xn_ref[...], wt_ref[start:start + width, :])

    gq_ref[...] = cols(_W_GQ, GLA_QK) * (GLA_DK ** -0.5)
    gk_ref[...] = cols(_W_GK, GLA_QK)
    gv_ref[...] = cols(_W_GV, GLA_V).astype(BF16)
    gr_ref[...] = cols(_W_GR, GLA_V)
    glr = cols(_W_GLR, LANES)
    z = _dot(glr.astype(BF16), wg2_ref[...]) + bg_ref[...]
    gg_ref[...] = jax.nn.log_sigmoid(z) / GLA_GATE_NORM
    _group_rms(cols(_W_DQ, DIFF_QK), qn_ref, DIFF_DH, DIFF_DH ** -0.5, _put_cols(dq_ref))
    _group_rms(cols(_W_DK, DIFF_QK), kn_ref, DIFF_DH, 1.0, _put_heads_and_cols(dk_ref, dkb_ref))
    dv = cols(_W_DV, DIFF_V)
    put_v = _put_heads_and_cols(dv_ref, dvb_ref)
    for h in range(HEADS):
        put_v(h, dv[:, h * LANES:(h + 1) * LANES])
    _group_rms(cols(_W_MQ, MEM_W), mqn_ref, MEM_DH, 1.0, _put_cols(mq_ref))


def _proj(x, gain, wt, wg2, bg, qn, kn, mqn):
    n = x.shape[0]
    tm = min(PROJ_TILE, n)
    const = lambda shape: pl.BlockSpec(shape, lambda i: (0, 0))
    out = lambda width: pl.BlockSpec((tm, width), lambda i: (i, 0))
    heads = pl.BlockSpec((tm, HEADS, LANES), lambda i: (i, 0, 0))
    shp = lambda width, dt: jax.ShapeDtypeStruct((n, width), dt)
    shp_heads = jax.ShapeDtypeStruct((n, HEADS, LANES), F32)
    return pl.pallas_call(
        _proj_body,
        grid=(n // tm,),
        in_specs=[
            pl.BlockSpec((tm, D_MODEL), lambda i: (i, 0)),
            const((1, D_MODEL)),
            pl.BlockSpec((IN_WIDTH, D_MODEL), lambda i: (0, 0), pipeline_mode=pl.Buffered(1)),
            const((LANES, GLA_QK)),
            const((1, GLA_QK)),
            const((1, DIFF_QK)),
            const((1, DIFF_QK)),
            const((1, MEM_W)),
        ],
        out_specs=[out(GLA_QK), out(GLA_QK), out(GLA_V), out(GLA_V), out(GLA_QK),
                   out(DIFF_QK), heads, heads, out(MEM_W), out(DIFF_QK), out(DIFF_V)],
        out_shape=[shp(GLA_QK, F32), shp(GLA_QK, F32), shp(GLA_V, BF16), shp(GLA_V, F32), shp(GLA_QK, F32),
                   shp(DIFF_QK, BF16), shp_heads, shp_heads, shp(MEM_W, BF16), shp(DIFF_QK, BF16), shp(DIFF_V, BF16)],
        scratch_shapes=[pltpu.VMEM((tm, D_MODEL), BF16)],
        compiler_params=_params(("parallel",)),
        name="proj",
    )(x, gain, wt, wg2, bg, qn, kn, mqn)


def _memkv_body(x_ref, g_ref, w_ref, kn_ref, k_ref, v_ref, kb_ref, vb_ref):
    xn = _rms(x_ref[...], g_ref[...]).astype(BF16)
    _group_rms(_dot(xn, w_ref[:, :MEM_W]), kn_ref, MEM_DH, 1.0, _put_heads_and_cols(k_ref, kb_ref))
    v = _dot(xn, w_ref[:, MEM_W:])
    put_v = _put_heads_and_cols(v_ref, vb_ref)
    for h in range(HEADS):
        put_v(h, v[:, h * LANES:(h + 1) * LANES])


def _memkv(mem, gain, w, kn):
    n = mem.shape[0]
    tm = min(TOK_TILE, n)
    const = lambda shape: pl.BlockSpec(shape, lambda i: (0, 0))
    heads = pl.BlockSpec((tm, HEADS, LANES), lambda i: (i, 0, 0))
    dense = pl.BlockSpec((tm, MEM_W), lambda i: (i, 0))
    return pl.pallas_call(
        _memkv_body,
        grid=(n // tm,),
        in_specs=[pl.BlockSpec((tm, D_MODEL), lambda i: (i, 0)), const((1, D_MODEL)),
                  const((D_MODEL, 2 * MEM_W)), const((1, MEM_W))],
        out_specs=[heads, heads, dense, dense],
        out_shape=[jax.ShapeDtypeStruct((n, HEADS, LANES), F32)] * 2 + [jax.ShapeDtypeStruct((n, MEM_W), BF16)] * 2,
        compiler_params=_params(("parallel",)),
        name="memkv",
    )(mem, gain, w, kn)


def _split3(x):
    hi = x.astype(BF16)
    r1 = x - hi.astype(F32)
    mid = r1.astype(BF16)
    lo = (r1 - mid.astype(F32)).astype(BF16)
    return hi, mid, lo


def _gla_body(*refs, chunk, n_chunks, n_steps, has_state):
    if has_state:
        gq_ref, gk_ref, gg_ref, gv_ref, gr_ref, on_ref, s0_ref, go_ref, st_ref, state = refs
    else:
        gq_ref, gk_ref, gg_ref, gv_ref, gr_ref, on_ref, go_ref, st_ref, state = refs
    t = pl.program_id(1)

    @pl.when(t == 0)
    def _():
        for h in range(GLA_HEADS):
            if has_state:
                state[h] = s0_ref[0, h].T
            else:
                state[h] = jnp.zeros((GLA_DV, GLA_DK), F32)

    row = lax.broadcasted_iota(jnp.int32, (chunk, chunk), 0)
    col = lax.broadcasted_iota(jnp.int32, (chunk, chunk), 1)
    causal = row >= col
    tril = causal.astype(BF16)

    chunks = [slice(c * chunk, (c + 1) * chunk) for c in range(n_chunks)]
    kcs = [slice(h * GLA_DK, (h + 1) * GLA_DK) for h in range(GLA_HEADS)]
    vcs = [slice(h * GLA_DV, (h + 1) * GLA_DV) for h in range(GLA_HEADS)]

    bs = []
    for rows in chunks:
        g_hi, g_mid, g_lo = _split3(gg_ref[rows, :])
        bs.append(_dot(tril, g_hi) + _dot(tril, g_mid) + _dot(tril, g_lo))

    qes, kes, kds, decays = [], [], [], []
    for rows, b in zip(chunks, bs):
        b_last = b[chunk - 1:chunk, :]
        q = gq_ref[rows, :]
        k = gk_ref[rows, :]
        qes.append((q * jnp.exp(b)).astype(BF16))
        kes.append((k * jnp.exp(-b)).astype(BF16))
        kds.append((k * jnp.exp(b_last - b)).astype(BF16))
        decays.append(jnp.exp(b_last))

    a_s = [[jnp.where(causal, _dot_nt(qe[:, kc], ke[:, kc]), 0.0).astype(BF16) for kc in kcs]
           for qe, ke in zip(qes, kes)]
    incs = [[_dot_tn(gv_ref[rows, vc], kd[:, kc]) for kc, vc in zip(kcs, vcs)] for rows, kd in zip(chunks, kds)]

    s_in = []
    s_cur = [state[h] for h in range(GLA_HEADS)]
    for c in range(n_chunks):
        s_in.append([s.astype(BF16) for s in s_cur])
        s_cur = [s * decays[c][:, kc] + inc for s, kc, inc in zip(s_cur, kcs, incs[c])]
    for h in range(GLA_HEADS):
        state[h] = s_cur[h]

    for c, rows in enumerate(chunks):
        for h, (kc, vc) in enumerate(zip(kcs, vcs)):
            o = _dot_nt(qes[c][:, kc], s_in[c][h]) + _dot(a_s[c][h], gv_ref[rows, vc])
            go_ref[rows, vc] = (_rms(o, on_ref[...]) * jax.nn.silu(gr_ref[rows, vc])).astype(BF16)

    @pl.when(t == n_steps - 1)
    def _():
        for h in range(GLA_HEADS):
            st_ref[0, h] = state[h].T


def _gla(gq, gk, gg, gv, gr, onorm, s0, batch, seq, chunk):
    tt = min(GLA_TILE, seq)
    n_steps = seq // tt
    tok = lambda width: pl.BlockSpec((tt, width), lambda b, t: (b * n_steps + t, 0))
    st_spec = pl.BlockSpec((1, GLA_HEADS, GLA_DK, GLA_DV), lambda b, t: (b, 0, 0, 0))
    in_specs = [tok(GLA_QK), tok(GLA_QK), tok(GLA_QK), tok(GLA_V), tok(GLA_V),
                pl.BlockSpec((1, GLA_DV), lambda b, t: (0, 0))]
    args = [gq, gk, gg, gv, gr, onorm]
    if s0 is not None:
        in_specs.append(st_spec)
        args.append(s0)
    return pl.pallas_call(
        functools.partial(_gla_body, chunk=chunk, n_chunks=tt // chunk, n_steps=n_steps, has_state=s0 is not None),
        grid=(batch, n_steps),
        in_specs=in_specs,
        out_specs=[tok(GLA_V), st_spec],
        out_shape=[jax.ShapeDtypeStruct((batch * seq, GLA_V), BF16),
                   jax.ShapeDtypeStruct((batch, GLA_HEADS, GLA_DK, GLA_DV), F32)],
        scratch_shapes=[pltpu.VMEM((GLA_HEADS, GLA_DV, GLA_DK), F32)],
        compiler_params=_params(("parallel", "arbitrary")),
        name="gla",
    )(*args)


def _t5_bucket(rel):
    nb = REL_BUCKETS // 2
    max_exact = nb // 2
    ret = jnp.where(rel > 0, nb, 0)
    n = jnp.abs(rel)
    nf = jnp.maximum(n, 1).astype(F32)
    large = max_exact + (jnp.log(nf / max_exact) / math.log(REL_MAX_DIST / max_exact)
                         * (nb - max_exact)).astype(jnp.int32)
    large = jnp.minimum(large, nb - 1)
    return ret + jnp.where(n < max_exact, n, large)


def _bucket_tile(q_pos, k_pos):
    visible = (k_pos[None, :] // CHUNK) <= (q_pos[:, None] // CHUNK)
    return jnp.where(visible, _t5_bucket(k_pos[None, :] - q_pos[:, None]), MASKED_BUCKET).astype(jnp.int32)


def _bias_from_buckets(idx, tab_ref, head):
    def step(bk, acc):
        return jnp.where(idx == bk, tab_ref[bk * DIFF_HEADS + head], acc)
    return lax.fori_loop(0, REL_BUCKETS, step, jnp.full(idx.shape, NEG_INF, F32))


def _lambda(lam_ref, lam_init):
    l = lam_ref[...]
    return (jnp.exp(jnp.sum(l[0:1] * l[1:2], axis=-1, keepdims=True))
            - jnp.exp(jnp.sum(l[2:3] * l[3:4], axis=-1, keepdims=True)) + lam_init)


def _comp_masks(q):
    lane = lax.broadcasted_iota(jnp.int32, q.shape, 1)
    zero = jnp.zeros_like(q)
    return jnp.where(lane < DIFF_DH, q, zero), jnp.where(lane < DIFF_DH, zero, q)


def _diff_finish(o0, o1, lam, on_ref, lam_init, out_dtype):
    o = o0 - lam * o1
    return (_rms(o, on_ref[...]) * (1.0 - lam_init)).astype(out_dtype)


def _scores(q_c, parts):
    return [_dot_nt(q_c, k) + bias if jnp.ndim(bias) == 2 else _dot_nt(q_c, k) for k, _, bias in parts]


def _softmax_pv(scores, parts):
    shifts = [0.0 if jnp.ndim(bias) == 2 else bias for _, _, bias in parts]
    m = None
    for s, shift in zip(scores, shifts):
        part_max = s.max(axis=-1, keepdims=True) + shift
        m = part_max if m is None else jnp.maximum(m, part_max)
    ol = 0.0
    for s, shift, (_, v1, _) in zip(scores, shifts, parts):
        ol = ol + _dot(jnp.exp((s - (m - shift)).astype(BF16)), v1)
    return ol[:, :DIFF_DV] * (1.0 / ol[:, DIFF_DV:])


def _diff_prompt_body(tab_ref, q_ref, kb, vb, idx_ref, lam_ref, on_ref, o_ref, bias, v1, *, seq, lam_init):
    b = pl.program_id(0)
    h = pl.program_id(1)

    @pl.when(b == 0)
    def _():
        per_offset = _bias_from_buckets(idx_ref[...], tab_ref, h)[0:1]
        rows = jnp.broadcast_to(per_offset, (Q_BLOCK, 4 * Q_BLOCK))
        tile = pltpu.roll(rows, 3 * Q_BLOCK, 1, stride=1, stride_axis=0)[:, :2 * Q_BLOCK]
        q_pos = Q_BLOCK + lax.broadcasted_iota(jnp.int32, tile.shape, 0)
        k_pos = lax.broadcasted_iota(jnp.int32, tile.shape, 1)
        bias[h] = jnp.where(k_pos // CHUNK <= q_pos // CHUNK, tile, NEG_INF)

    v1[:, :DIFF_DV] = vb[...]
    v1[:, DIFF_DV:] = jnp.ones((seq, LANES), BF16)
    lam = _lambda(lam_ref, lam_init)
    far_bias = tab_ref[(REL_BUCKETS // 2 - 1) * DIFF_HEADS + h]

    def key_parts(i):
        if i == 0:
            return [(kb[0:Q_BLOCK, :], v1[0:Q_BLOCK, :], bias[h, :, Q_BLOCK:])]
        near0 = (i - 1) * Q_BLOCK
        near = slice(near0, near0 + 2 * Q_BLOCK)
        parts = [(kb[near, :], v1[near, :], bias[h])]
        if near0 > 0:
            parts.append((kb[0:near0, :], v1[0:near0, :], far_bias))
        return parts

    items = [(i, c) for i in range(seq // Q_BLOCK) for c in range(2)]
    comps = {}

    def scores_of(item):
        i, c = item
        if i not in comps:
            comps[i] = _comp_masks(q_ref[i * Q_BLOCK:(i + 1) * Q_BLOCK, :])
        return _scores(comps[i][c], key_parts(i))

    outs = {}
    ahead = scores_of(items[0])
    for n, (i, c) in enumerate(items):
        cur = ahead
        if n + 1 < len(items):
            ahead = scores_of(items[n + 1])
        outs[c] = _softmax_pv(cur, key_parts(i))
        if c == 1:
            o_ref[i * Q_BLOCK:(i + 1) * Q_BLOCK, :] = _diff_finish(outs[0], outs[1], lam, on_ref, lam_init,
                                                                    o_ref.dtype)


def _diff_prompt(table, dq, dk, dv, lam_p, onorm, batch, seq, lam_init):
    offsets = jnp.arange(4 * Q_BLOCK) - 2 * Q_BLOCK
    idx = jnp.broadcast_to(_t5_bucket(offsets).astype(jnp.int32)[None, :], (SUBLANES, 4 * Q_BLOCK))
    head = lambda: pl.BlockSpec((seq, LANES), lambda b, h: (b, h))
    return pl.pallas_call(
        functools.partial(_diff_prompt_body, seq=seq, lam_init=lam_init),
        grid=(batch, DIFF_HEADS),
        in_specs=[
            pl.BlockSpec(memory_space=pltpu.SMEM),
            head(), head(), head(),
            pl.BlockSpec((SUBLANES, 4 * Q_BLOCK), lambda b, h: (0, 0)),
            pl.BlockSpec((4, DIFF_DH), lambda b, h: (0, 0)),
            pl.BlockSpec((1, DIFF_DV), lambda b, h: (0, 0)),
        ],
        out_specs=head(),
        out_shape=jax.ShapeDtypeStruct((batch * seq, DIFF_V), BF16),
        scratch_shapes=[pltpu.VMEM((DIFF_HEADS, Q_BLOCK, 2 * Q_BLOCK), F32),
                        pltpu.VMEM((seq, DIFF_DV + LANES), BF16)],
        compiler_params=_params(("arbitrary", "arbitrary")),
        name="diff_prompt",
    )(table, dq, dk, dv, idx, lam_p, onorm)


def _diff_sample_body(tab_ref, q_ref, k_ref, v_ref, ck_ref, cv_ref, idxc_ref, idxn_ref, lam_ref, on_ref, o_ref,
                      bias_c, bias_n, *, seq, lam_init):
    groups = [(h, c) for h in range(DIFF_HEADS) for c in range(2)]

    @pl.when(pl.program_id(0) == 0)
    def _():
        for h in range(DIFF_HEADS):
            per_offset = _bias_from_buckets(idxc_ref[h], tab_ref, h)[0:1]
            rows = jnp.broadcast_to(per_offset, (seq, per_offset.shape[1]))
            tile = pltpu.roll(rows, rows.shape[1] - HEADS * seq, 1, stride=HEADS, stride_axis=0)[:, :bias_c.shape[1]]
            for g in (2 * h, 2 * h + 1):
                bias_c[g * seq:(g + 1) * seq, :] = tile
            bias_n[h] = _bias_from_buckets(idxn_ref[...], tab_ref, h)

    lam = _lambda(lam_ref, lam_init)
    head_cols = [slice(h * LANES, (h + 1) * LANES) for h in range(DIFF_HEADS)]
    q_groups = [q_c for cols in head_cols for q_c in _comp_masks(q_ref[:, cols])]
    s_old = _dot_nt(jnp.concatenate(q_groups, axis=0), ck_ref[...].astype(BF16)) + bias_c[...]
    s_new = jnp.concatenate([_dot_nt(q_g, k_ref[:, head_cols[h]]) + bias_n[h]
                             for q_g, (h, _) in zip(q_groups, groups)], axis=0)
    m = jnp.maximum(s_old.max(axis=-1, keepdims=True), s_new.max(axis=-1, keepdims=True))
    e_old = jnp.exp(s_old - m)
    e_new = jnp.exp(s_new - m)
    l = e_old.sum(axis=-1, keepdims=True) + e_new.sum(axis=-1, keepdims=True)
    o_old = _dot(e_old.astype(BF16), cv_ref[...].astype(BF16))
    e_new = e_new.astype(BF16)
    o_new = jnp.concatenate([_dot(e_new[g * seq:(g + 1) * seq, :], v_ref[:, head_cols[h]])
                             for g, (h, _) in enumerate(groups)], axis=0)
    o = (o_old + o_new) * (1.0 / l)
    for h, cols in enumerate(head_cols):
        o0 = o[(2 * h) * seq:(2 * h + 1) * seq, :]
        o1 = o[(2 * h + 1) * seq:(2 * h + 2) * seq, :]
        o_ref[:, cols] = _diff_finish(o0, o1, lam, on_ref, lam_init, o_ref.dtype)


def _diff_sample(table, dq, dk, dv, cache_k, cache_v, lam_p, onorm, batch, seq, past, lam_init):
    q_pos = past + jnp.arange(seq)
    idx_n = _bucket_tile(q_pos, past + jnp.arange(seq))
    row = jnp.arange(HEADS * (past + seq))
    idx_row = _t5_bucket(row // HEADS - seq - past)
    idx_c = jnp.where((row % HEADS)[None, :] == jnp.arange(HEADS)[:, None], idx_row[None, :], MASKED_BUCKET)
    idx_c = jnp.broadcast_to(idx_c.astype(jnp.int32)[:, None, :], (HEADS, SUBLANES, row.size))
    new_q = pl.BlockSpec((seq, DIFF_QK), lambda b: (b, 0))
    old = lambda: pl.BlockSpec((past * HEADS, LANES), lambda b: (b, 0))
    const = lambda shape: pl.BlockSpec(shape, lambda b: (0,) * len(shape))
    n_groups = 2 * DIFF_HEADS
    return pl.pallas_call(
        functools.partial(_diff_sample_body, seq=seq, lam_init=lam_init),
        grid=(batch,),
        in_specs=[pl.BlockSpec(memory_space=pltpu.SMEM), new_q, new_q, new_q, old(), old(),
                  const((HEADS, SUBLANES, HEADS * (past + seq))), const((seq, seq)), const((4, DIFF_DH)), const((1, DIFF_DV))],
        out_specs=new_q,
        out_shape=jax.ShapeDtypeStruct((batch * seq, DIFF_V), BF16),
        scratch_shapes=[pltpu.VMEM((n_groups * seq, past * HEADS), F32), pltpu.VMEM((DIFF_HEADS, seq, seq), F32)],
        compiler_params=_params(("arbitrary",)),
        name="diff_sample",
    )(table, dq, dk, dv, cache_k, cache_v, idx_c, idx_n, lam_p, onorm)


def _softmax(s):
    e = jnp.exp(s - s.max(axis=-1, keepdims=True))
    return (e * (1.0 / e.sum(axis=-1, keepdims=True))).astype(BF16)


def _memattn_body(q_ref, k_ref, v_ref, o_ref, *, interleaved):
    head_cols = [slice(h * MEM_DH, (h + 1) * MEM_DH) for h in range(MEM_HEADS)]
    scale = MEM_DH ** -0.5
    if interleaved:
        tq = q_ref.shape[0]
        s = _dot_nt(jnp.concatenate([q_ref[:, cols] for cols in head_cols], axis=0), k_ref[...].astype(BF16)) * scale
        q_head = lax.broadcasted_iota(jnp.int32, s.shape, 0) // tq
        k_head = lax.broadcasted_iota(jnp.int32, s.shape, 1) % MEM_HEADS
        o = _dot(_softmax(jnp.where(q_head == k_head, s, NEG_INF)), v_ref[...].astype(BF16))
        for h, cols in enumerate(head_cols):
            o_ref[:, cols] = o[h * tq:(h + 1) * tq, :].astype(o_ref.dtype)
    else:
        scores = [_dot_nt(q_ref[:, cols], k_ref[:, cols]) * scale for cols in head_cols]
        probs = [_softmax(s) for s in scores]
        for cols, p in zip(head_cols, probs):
            o_ref[:, cols] = _dot(p, v_ref[:, cols]).astype(o_ref.dtype)


def _memattn(mq, mk, mv, batch, seq, n_mem):
    tq = min(MEMATTN_TILE, seq)
    nq = seq // tq
    q_spec = pl.BlockSpec((tq, MEM_W), lambda b, i: (b * nq + i, 0))
    interleaved = mk.shape[1] == LANES
    kv_spec = pl.BlockSpec((mk.shape[0] // batch, mk.shape[1]), lambda b, i: (b, 0))
    return pl.pallas_call(
        functools.partial(_memattn_body, interleaved=interleaved),
        grid=(batch, nq),
        in_specs=[q_spec, kv_spec, kv_spec],
        out_specs=q_spec,
        out_shape=jax.ShapeDtypeStruct((batch * seq, MEM_W), BF16),
        compiler_params=_params(("parallel", "arbitrary")),
        name="memattn",
    )(mq, mk, mv)


def _outproj_body(x_ref, g_ref, d_ref, m_ref, wg_ref, wd_ref, wm_ref, o_ref):
    o_ref[...] = (x_ref[...] + _dot(g_ref[...], wg_ref[...]) + _dot(d_ref[...], wd_ref[...])
                  + _dot(m_ref[...], wm_ref[...]))


def _outproj(x, g, d, m, wo):
    n = x.shape[0]
    tm = min(TOK_TILE, n)
    tok = lambda width: pl.BlockSpec((tm, width), lambda i: (i, 0))
    w_rows = lambda rows, blk: pl.BlockSpec((rows, D_MODEL), lambda i: (blk, 0))
    return pl.pallas_call(
        _outproj_body,
        grid=(n // tm,),
        in_specs=[tok(D_MODEL), tok(GLA_V), tok(DIFF_V), tok(MEM_W),
                  w_rows(GLA_V, 0), w_rows(DIFF_V, GLA_V // DIFF_V), w_rows(MEM_W, (GLA_V + DIFF_V) // MEM_W)],
        out_specs=tok(D_MODEL),
        out_shape=jax.ShapeDtypeStruct((n, D_MODEL), F32),
        compiler_params=_params(("parallel",)),
        name="outproj",
    )(x, g, d, m, wo, wo, wo)


def _row(v):
    return v.reshape(1, -1).astype(F32)


def kernel(x_prompt, x_sample, mem_prompt, cache_diff_k, cache_diff_v, state_gla, cache_mem_k, cache_mem_v, rel_bias_table, norm_ffn1, w_ffn1_in, w_ffn1_out, norm_mix, w_in, w_gla_g2, b_gla_g, gla_out_norm, diff_q_norm, diff_k_norm, diff_lambda, diff_out_norm, mem_norm, w_mem_kv, mem_q_norm, mem_k_norm, w_o, norm_ffn2, w_ffn2_in, w_ffn2_out, norm_final):
    depth = norm_ffn1.shape[0]
    assert depth == 1, "single-layer step"
    layer = 0
    batch, seq, _ = x_prompt.shape
    dec_batch, dec_seq, _ = x_sample.shape
    past = cache_diff_k.shape[2]
    n_mem = mem_prompt.shape[1]
    lam_init = 0.8 - 0.6 * math.exp(-0.3 * layer)

    w_g2 =jnp.pad(w_gla_g2[layer].astype(BF16), ((0, LANES - GLA_GATE_RANK), (0, 0)))
    b_g = _row(b_gla_g[layer])
    qn = _row(jnp.tile(diff_q_norm[layer], DIFF_QK // DIFF_DH))
    kn = _row(jnp.tile(diff_k_norm[layer], DIFF_QK // DIFF_DH))
    mqn = _row(jnp.tile(mem_q_norm[layer], MEM_HEADS))
    mkn = _row(jnp.tile(mem_k_norm[layer], MEM_HEADS))
    table = rel_bias_table.astype(F32).reshape(-1)
    lam_p = diff_lambda[layer].astype(F32)
    gla_on = _row(gla_out_norm[layer])
    diff_on = _row(diff_out_norm[layer])

    mk, mv, mk_b, mv_b = _memkv(mem_prompt.reshape(batch * n_mem, D_MODEL), _row(mem_norm[layer]),
                                w_mem_kv[layer].astype(BF16), mkn)

    n1 = _row(norm_ffn1[layer])
    xs1, *ffn1 = _ffn(x_sample.reshape(dec_batch * dec_seq, D_MODEL), n1, w_ffn1_in[layer], w_ffn1_in[layer],
                      w_ffn1_out[layer], up_col0=D_FF, ff_tile=FF_TILE_F32, emit_bf16=True)
    n_in_blocks = w_ffn2_in.shape[2] // LANES
    xp1, w2_in, w2_out, w_t, wo = _ffn(
        x_prompt.reshape(batch * seq, D_MODEL), n1, *ffn1, up_col0=0,
        cast_later=[(w_ffn2_in[layer], (D_MODEL, LANES), 0),
                    (w_ffn2_out[layer], (D_FF // n_in_blocks, D_MODEL), n_in_blocks),
                    (jnp.swapaxes(w_in[layer], 0, 1), (CAST_ROWS_W_IN, D_MODEL), 0),
                    (w_o[layer], (CAST_ROWS_W_O, D_MODEL), 0)])
    ffn2 = (w2_in, w2_in, w2_out)

    def layer_fn(x, b, t, chunk, s0, diff_fn, mem_k, mem_v):
        gq, gk, gv, gr, gg, dq, dk, dv, mq, dk_b, dv_b = _proj(x, _row(norm_mix[layer]), w_t, w_g2, b_g,
                                                               qn, kn, mqn)
        g_out, g_state = _gla(gq, gk, gg, gv, gr, gla_on, s0, b, t, chunk)
        d_out = diff_fn(dq, dk_b, dv_b)
        m_out = _memattn(mq, mem_k, mem_v, b, t, n_mem)
        x = _outproj(x, g_out, d_out, m_out, wo)
        x = _ffn(x, _row(norm_ffn2[layer]), *ffn2, up_col0=D_FF, final_gain=_row(norm_final[layer]))
        return x, dk, dv, g_state

    yp, dk_p, dv_p, g_p = layer_fn(
        xp1, batch, seq, CHUNK, None,
        lambda dq, dk, dv: _diff_prompt(table, dq, dk, dv, lam_p, diff_on, batch, seq, lam_init), mk_b, mv_b)
    ys, dk_s, dv_s, g_s = layer_fn(
        xs1, dec_batch, dec_seq, dec_seq, state_gla[layer],
        lambda dq, dk, dv: _diff_sample(table, dq, dk, dv, cache_diff_k[layer].reshape(-1, LANES),
                                        cache_diff_v[layer].reshape(-1, LANES),
                                        lam_p, diff_on, dec_batch, dec_seq, past, lam_init),
        cache_mem_k[layer].reshape(-1, LANES), cache_mem_v[layer].reshape(-1, LANES))

    head4 = lambda a, b, t: a.reshape(1, b, t, HEADS, LANES)
    return (yp.reshape(batch, seq, D_MODEL), ys.reshape(dec_batch, dec_seq, D_MODEL),
            head4(dk_p, batch, seq), head4(dv_p, batch, seq), g_p[None],
            head4(mk, batch, n_mem), head4(mv, batch, n_mem),
            head4(dk_s, dec_batch, dec_seq), head4(dv_s, dec_batch, dec_seq), g_s[None])
```

```python
import functools
import math

import jax
import jax.numpy as jnp
from jax import lax
from jax.experimental import pallas as pl
from jax.experimental.pallas import tpu as pltpu

F32 = jnp.float32
BF16 = jnp.bfloat16

D_MODEL = 2048
CHUNK = 64
EPS = 1e-6
NEG_INF = -1e30
GLA_HEADS, GLA_DK, GLA_DV, GLA_GATE_RANK, GLA_GATE_NORM = 4, 128, 256, 16, 16.0
DIFF_HEADS, DIFF_DH, DIFF_DV = 4, 64, 128
MEM_HEADS, MEM_DH = 4, 128
REL_BUCKETS, REL_MAX_DIST = 32, 128
D_FF = 5504
GLA_QK = GLA_HEADS * GLA_DK
GLA_V = GLA_HEADS * GLA_DV
DIFF_QK = DIFF_HEADS * 2 * DIFF_DH
DIFF_V = DIFF_HEADS * DIFF_DV
MEM_W = MEM_HEADS * MEM_DH
GLR_OFF = 2 * GLA_QK + 2 * GLA_V

LANES = 128
SUBLANES = 8
HEADS = 4
FF_TILE = 512
FF_TILE_F32 = 256
FFN_TOK_TILE = 1024
TOK_TILE = 512
GLA_TILE = 1024
MEMATTN_TILE = 2048
PROJ_TILE = 256
Q_BLOCK = 256
MASKED_BUCKET = REL_BUCKETS
MIB = 1024 * 1024
VMEM_CAP_MIB = 60
CAST_ROWS_W_IN = 48
CAST_ROWS_W_O = 16

_W_GQ, _W_GK, _W_GV, _W_GR, _W_GLR = 0, GLA_QK, 2 * GLA_QK, 2 * GLA_QK + GLA_V, GLR_OFF
_W_DQ = GLR_OFF + GLA_GATE_RANK
_W_DK = _W_DQ + DIFF_QK
_W_DV = _W_DK + DIFF_QK
_W_MQ = _W_DV + DIFF_V
IN_WIDTH = _W_MQ + MEM_W


def _dot(a, b):
    return jnp.dot(a, b, preferred_element_type=F32)


def _dot_nt(a, b):
    return lax.dot_general(a, b, (((1,), (1,)), ((), ())), preferred_element_type=F32)


def _dot_tn(a, b):
    return lax.dot_general(a, b, (((0,), (0,)), ((), ())), preferred_element_type=F32)


def _rms(x, gain):
    return x * lax.rsqrt(jnp.mean(x * x, axis=-1, keepdims=True) + EPS) * gain


def _params(sem, vmem_mib=48):
    return pltpu.CompilerParams(dimension_semantics=sem, vmem_limit_bytes=min(vmem_mib, VMEM_CAP_MIB) * MIB)


def _ffn_body(*refs, n_ff, overlap, final_norm, n_cast, emit_bf16):
    x_ref, g_ref, wg_ref, wu_ref, wo_ref = refs[:5]
    refs = refs[5:]
    if final_norm:
        fg_ref, refs = refs[0], refs[1:]
    cast_src, refs = refs[:n_cast], refs[n_cast:]
    o_ref, refs = refs[0], refs[1:]
    cast_dst, refs = refs[:n_cast], refs[n_cast:]
    if emit_bf16:
        (wg_out, wu_out, wo_out), refs = refs[:3], refs[3:]
    (xn_ref,) = refs
    j = pl.program_id(1)

    def tile(lo):
        for src, dst in zip(cast_src, cast_dst):
            dst[...] = src[...].astype(BF16)
        xn = xn_ref[...]
        if emit_bf16:
            def front(a, axis):
                a = a.astype(BF16)
                if lo == 0:
                    return a
                return jnp.concatenate([lax.slice_in_dim(a, lo, None, axis=axis),
                                        lax.slice_in_dim(a, 0, lo, axis=axis)], axis=axis)
            wg_out[...] = front(wg_ref[...], 1)
            wu_out[...] = front(wu_ref[...], 1)
            wo_out[...] = front(wo_ref[...], 0)
            live = wg_out.shape[1] - lo
            gate = _dot(xn, wg_out[:, :live])
            up = _dot(xn, wu_out[:, :live])
            act = (jax.nn.silu(gate) * up).astype(BF16)
            return 0.5 * _dot(act, wo_out[:live, :])
        gate = _dot(xn, wg_ref[:, lo:])
        up = _dot(xn, wu_ref[:, lo:])
        act = (jax.nn.silu(gate) * up).astype(BF16)
        return 0.5 * _dot(act, wo_ref[lo:, :])

    @pl.when(j == 0)
    def _():
        xn_ref[...] = _rms(x_ref[...], g_ref[...]).astype(BF16)
        o_ref[...] = x_ref[...] + tile(0)

    @pl.when((j > 0) & (j < n_ff - 1))
    def _():
        o_ref[...] += tile(0)

    @pl.when(j == n_ff - 1)
    def _():
        o = o_ref[...] + tile(overlap)
        o_ref[...] = _rms(o, fg_ref[...]) if final_norm else o


def _ffn(x, gain, w_gate, w_up, w_out, *, up_col0, final_gain=None, cast_later=(), ff_tile=FF_TILE,
         emit_bf16=False):
    n = x.shape[0]
    tm = min(FFN_TOK_TILE, n)
    n_ff = -(-D_FF // ff_tile)
    row = pl.BlockSpec((1, D_MODEL), lambda i, j: (0, 0))
    start = lambda j: jnp.minimum(j * (ff_tile // LANES), (D_FF - ff_tile) // LANES)
    w_in_tile = (pl.Element(D_MODEL), pl.Element(ff_tile))
    w_out_tile = (pl.Element(ff_tile), pl.Element(D_MODEL))
    gate_index = lambda i, j: (0, start(j) * LANES)
    out_index = lambda i, j: (start(j) * LANES, 0)
    tok = pl.BlockSpec((tm, D_MODEL), lambda i, j: (i, 0))
    in_specs = [
        tok,
        row,
        pl.BlockSpec(w_in_tile, gate_index),
        pl.BlockSpec(w_in_tile, lambda i, j: (0, (up_col0 // LANES + start(j)) * LANES)),
        pl.BlockSpec(w_out_tile, out_index),
    ]
    args = [x, gain, w_gate, w_up, w_out]
    if final_gain is not None:
        in_specs.append(row)
        args.append(final_gain)
    w_tile_bytes = D_MODEL * ff_tile * ((4 + 2) if emit_bf16 else 2)
    vmem_bytes = 4 * tm * D_MODEL * 4 + tm * D_MODEL * 2 + 6 * w_tile_bytes + 3 * tm * ff_tile * 4
    out_specs = [pl.BlockSpec((tm, D_MODEL), lambda i, j: (i, 0))]
    out_shape = [jax.ShapeDtypeStruct((n, D_MODEL), F32)]
    n_steps = (n // tm) * n_ff
    for mat, blk, first in cast_later:
        rows, cols = mat.shape
        assert rows % blk[0] == 0 and cols % blk[1] == 0 and (blk[0] == rows or blk[1] == cols)
        n_blk = (rows // blk[0]) * (cols // blk[1])
        assert first + n_blk <= n_steps
        by_rows = blk[1] == cols

        def index(i, j, first=first, n_blk=n_blk, by_rows=by_rows):
            k = jnp.clip(i * n_ff + j - first, 0, n_blk - 1)
            return (k, 0) if by_rows else (0, k)

        in_specs.append(pl.BlockSpec(blk, index))
        args.append(mat)
        out_specs.append(pl.BlockSpec(blk, index))
        out_shape.append(jax.ShapeDtypeStruct(mat.shape, BF16))
        vmem_bytes += 2 * blk[0] * blk[1] * (4 + 2)
    if emit_bf16:
        d_ff_pad = n_ff * ff_tile
        cols_j = pl.BlockSpec((D_MODEL, ff_tile), lambda i, j: (0, j))
        out_specs += [cols_j, cols_j, pl.BlockSpec((ff_tile, D_MODEL), lambda i, j: (j, 0))]
        out_shape += ([jax.ShapeDtypeStruct((D_MODEL, d_ff_pad), BF16)] * 2
                      + [jax.ShapeDtypeStruct((d_ff_pad, D_MODEL), BF16)])
    outs = pl.pallas_call(
        functools.partial(_ffn_body, n_ff=n_ff, overlap=n_ff * ff_tile - D_FF, final_norm=final_gain is not None,
                          n_cast=len(cast_later), emit_bf16=emit_bf16),
        grid=(n // tm, n_ff),
        in_specs=in_specs,
        out_specs=out_specs,
        out_shape=out_shape,
        scratch_shapes=[pltpu.VMEM((tm, D_MODEL), BF16)],
        compiler_params=_params(("arbitrary", "arbitrary"), vmem_bytes // MIB + 4),
        name="ffn",
    )(*args)
    return outs[0] if len(outs) == 1 else outs


def _ffn_up_body(x_ref, g_ref, wg_ref, wu_ref, a_ref, xn_ref, *, n_ff, overlap):
    j = pl.program_id(1)

    def act():
        xn = xn_ref[...]
        return (jax.nn.silu(_dot(xn, wg_ref[...])) * _dot(xn, wu_ref[...])).astype(BF16)

    @pl.when(j == 0)
    def _():
        xn_ref[...] = _rms(x_ref[...], g_ref[...]).astype(BF16)
        a_ref[...] = act()

    @pl.when((j > 0) & (j < n_ff - 1))
    def _():
        a_ref[...] = act()

    @pl.when(j == n_ff - 1)
    def _():
        a = act()
        a_ref[...] = jnp.concatenate([a[:, overlap:], jnp.zeros((a.shape[0], overlap), BF16)], axis=1)


def _ffn_down_body(x_ref, a_ref, w_ref, fg_ref, o_ref):
    o = x_ref[...] + 0.5 * _dot(a_ref[:, :D_FF], w_ref[...])
    o_ref[...] = _rms(o, fg_ref[...])


def _ffn_two_pass(x, gain, w_in, w_out, final_gain):
    n = x.shape[0]
    tm = min(FFN_TOK_TILE, n)
    n_ff = -(-D_FF // FF_TILE)
    row = pl.BlockSpec((1, D_MODEL), lambda *_: (0, 0))
    start = lambda j: jnp.minimum(j * (FF_TILE // LANES), (D_FF - FF_TILE) // LANES)
    w_tile = (pl.Element(D_MODEL), pl.Element(FF_TILE))
    act = pl.pallas_call(
        functools.partial(_ffn_up_body, n_ff=n_ff, overlap=n_ff * FF_TILE - D_FF),
        grid=(n // tm, n_ff),
        in_specs=[pl.BlockSpec((tm, D_MODEL), lambda i, j: (i, 0)), row,
                  pl.BlockSpec(w_tile, lambda i, j: (0, start(j) * LANES)),
                  pl.BlockSpec(w_tile, lambda i, j: (0, (D_FF // LANES + start(j)) * LANES))],
        out_specs=pl.BlockSpec((tm, FF_TILE), lambda i, j: (i, j)),
        out_shape=jax.ShapeDtypeStruct((n, n_ff * FF_TILE), BF16),
        scratch_shapes=[pltpu.VMEM((tm, D_MODEL), BF16)],
        compiler_params=_params(("arbitrary", "arbitrary")),
        name="ffn_up",
    )(x, gain, w_in, w_in)
    td = min(TOK_TILE, n)
    return pl.pallas_call(
        _ffn_down_body,
        grid=(n // td,),
        in_specs=[pl.BlockSpec((td, D_MODEL), lambda i: (i, 0)),
                  pl.BlockSpec((td, n_ff * FF_TILE), lambda i: (i, 0)),
                  pl.BlockSpec((D_FF, D_MODEL), lambda i: (0, 0), pipeline_mode=pl.Buffered(1)),
                  row],
        out_specs=pl.BlockSpec((td, D_MODEL), lambda i: (i, 0)),
        out_shape=jax.ShapeDtypeStruct((n, D_MODEL), F32),
        compiler_params=_params(("arbitrary",), VMEM_CAP_MIB),
        name="ffn_down",
    )(x, act, w_out, final_gain)


def _group_rms(acc, gain_ref, width, scale, put):
    lane = lax.broadcasted_iota(jnp.int32, (1, LANES), 1)
    low = lane < width
    for c in range(acc.shape[1] // LANES):
        cols = slice(c * LANES, (c + 1) * LANES)
        xc = acc[:, cols]
        sq = xc * xc
        if width == LANES:
            r = lax.rsqrt(jnp.mean(sq, axis=-1, keepdims=True) + EPS)
        else:
            s_lo = jnp.sum(jnp.where(low, sq, 0.0), axis=-1, keepdims=True)
            s_hi = jnp.sum(jnp.where(low, 0.0, sq), axis=-1, keepdims=True)
            r = jnp.where(low, lax.rsqrt(s_lo / width + EPS), lax.rsqrt(s_hi / width + EPS))
        y = xc * r * gain_ref[:, cols]
        if scale != 1.0:
            y = y * scale
        put(c, y)


def _put_cols(ref):
    def put(c, y):
        ref[:, c * LANES:(c + 1) * LANES] = y.astype(ref.dtype)
    return put


def _put_heads_and_cols(heads_ref, cols_ref):
    def put(c, y):
        heads_ref[:, c, :] = y
        cols_ref[:, c * LANES:(c + 1) * LANES] = y.astype(cols_ref.dtype)
    return put


def _proj_body(x_ref, g_ref, wt_ref, wg2_ref, bg_ref, qn_ref, kn_ref, mqn_ref,
               gq_ref, gk_ref, gv_ref, gr_ref, gg_ref, dq_ref, dk_ref, dv_ref, mq_ref, dkb_ref, dvb_ref, xn_ref):
    xn_ref[...] = _rms(x_ref[...], g_ref[...]).astype(BF16)

    def cols(start, width):
        return _dot_nt(xn_ref[...], wt_ref[start:start + width, :])

    gq_ref[...] = cols(_W_GQ, GLA_QK) * (GLA_DK ** -0.5)
    gk_ref[...] = cols(_W_GK, GLA_QK)
    gv_ref[...] = cols(_W_GV, GLA_V).astype(BF16)
    gr_ref[...] = cols(_W_GR, GLA_V)
    glr = cols(_W_GLR, LANES)
    z = _dot(glr.astype(BF16), wg2_ref[...]) + bg_ref[...]
    gg_ref[...] = jax.nn.log_sigmoid(z) / GLA_GATE_NORM
    _group_rms(cols(_W_DQ, DIFF_QK), qn_ref, DIFF_DH, DIFF_DH ** -0.5, _put_cols(dq_ref))
    _group_rms(cols(_W_DK, DIFF_QK), kn_ref, DIFF_DH, 1.0, _put_heads_and_cols(dk_ref, dkb_ref))
    dv = cols(_W_DV, DIFF_V)
    put_v = _put_heads_and_cols(dv_ref, dvb_ref)
    for h in range(HEADS):
        put_v(h, dv[:, h * LANES:(h + 1) * LANES])
    _group_rms(cols(_W_MQ, MEM_W), mqn_ref, MEM_DH, 1.0, _put_cols(mq_ref))


def _proj(x, gain, wt, wg2, bg, qn, kn, mqn):
    n = x.shape[0]
    tm = min(PROJ_TILE, n)
    const = lambda shape: pl.BlockSpec(shape, lambda i: (0, 0))
    out = lambda width: pl.BlockSpec((tm, width), lambda i: (i, 0))
    heads = pl.BlockSpec((tm, HEADS, LANES), lambda i: (i, 0, 0))
    shp = lambda width, dt: jax.ShapeDtypeStruct((n, width), dt)
    shp_heads = jax.ShapeDtypeStruct((n, HEADS, LANES), F32)
    return pl.pallas_call(
        _proj_body,
        grid=(n // tm,),
        in_specs=[
            pl.BlockSpec((tm, D_MODEL), lambda i: (i, 0)),
            const((1, D_MODEL)),
            pl.BlockSpec((IN_WIDTH, D_MODEL), lambda i: (0, 0), pipeline_mode=pl.Buffered(1)),
            const((LANES, GLA_QK)),
            const((1, GLA_QK)),
            const((1, DIFF_QK)),
            const((1, DIFF_QK)),
            const((1, MEM_W)),
        ],
        out_specs=[out(GLA_QK), out(GLA_QK), out(GLA_V), out(GLA_V), out(GLA_QK),
                   out(DIFF_QK), heads, heads, out(MEM_W), out(DIFF_QK), out(DIFF_V)],
        out_shape=[shp(GLA_QK, F32), shp(GLA_QK, F32), shp(GLA_V, BF16), shp(GLA_V, F32), shp(GLA_QK, F32),
                   shp(DIFF_QK, BF16), shp_heads, shp_heads, shp(MEM_W, BF16), shp(DIFF_QK, BF16), shp(DIFF_V, BF16)],
        scratch_shapes=[pltpu.VMEM((tm, D_MODEL), BF16)],
        compiler_params=_params(("parallel",)),
        name="proj",
    )(x, gain, wt, wg2, bg, qn, kn, mqn)


def _memkv_body(x_ref, g_ref, w_ref, kn_ref, k_ref, v_ref, kb_ref, vb_ref):
    xn = _rms(x_ref[...], g_ref[...]).astype(BF16)
    _group_rms(_dot(xn, w_ref[:, :MEM_W]), kn_ref, MEM_DH, 1.0, _put_heads_and_cols(k_ref, kb_ref))
    v = _dot(xn, w_ref[:, MEM_W:])
    put_v = _put_heads_and_cols(v_ref, vb_ref)
    for h in range(HEADS):
        put_v(h, v[:, h * LANES:(h + 1) * LANES])


def _memkv(mem, gain, w, kn):
    n = mem.shape[0]
    tm = min(TOK_TILE, n)
    const = lambda shape: pl.BlockSpec(shape, lambda i: (0, 0))
    heads = pl.BlockSpec((tm, HEADS, LANES), lambda i: (i, 0, 0))
    dense = pl.BlockSpec((tm, MEM_W), lambda i: (i, 0))
    return pl.pallas_call(
        _memkv_body,
        grid=(n // tm,),
        in_specs=[pl.BlockSpec((tm, D_MODEL), lambda i: (i, 0)), const((1, D_MODEL)),
                  const((D_MODEL, 2 * MEM_W)), const((1, MEM_W))],
        out_specs=[heads, heads, dense, dense],
        out_shape=[jax.ShapeDtypeStruct((n, HEADS, LANES), F32)] * 2 + [jax.ShapeDtypeStruct((n, MEM_W), BF16)] * 2,
        compiler_params=_params(("parallel",)),
        name="memkv",
    )(mem, gain, w, kn)


def _split3(x):
    hi = x.astype(BF16)
    r1 = x - hi.astype(F32)
    mid = r1.astype(BF16)
    lo = (r1 - mid.astype(F32)).astype(BF16)
    return hi, mid, lo


def _gla_body(*refs, chunk, n_chunks, n_steps, has_state):
    if has_state:
        gq_ref, gk_ref, gg_ref, gv_ref, gr_ref, on_ref, s0_ref, go_ref, st_ref, state = refs
    else:
        gq_ref, gk_ref, gg_ref, gv_ref, gr_ref, on_ref, go_ref, st_ref, state = refs
    t = pl.program_id(1)

    @pl.when(t == 0)
    def _():
        for h in range(GLA_HEADS):
            if has_state:
                state[h] = s0_ref[0, h].T
            else:
                state[h] = jnp.zeros((GLA_DV, GLA_DK), F32)

    row = lax.broadcasted_iota(jnp.int32, (chunk, chunk), 0)
    col = lax.broadcasted_iota(jnp.int32, (chunk, chunk), 1)
    causal = row >= col
    tril = causal.astype(BF16)

    chunks = [slice(c * chunk, (c + 1) * chunk) for c in range(n_chunks)]
    kcs = [slice(h * GLA_DK, (h + 1) * GLA_DK) for h in range(GLA_HEADS)]
    vcs = [slice(h * GLA_DV, (h + 1) * GLA_DV) for h in range(GLA_HEADS)]

    bs = []
    for rows in chunks:
        g_hi, g_mid, g_lo = _split3(gg_ref[rows, :])
        bs.append(_dot(tril, g_hi) + _dot(tril, g_mid) + _dot(tril, g_lo))

    qes, kes, kds, decays = [], [], [], []
    for rows, b in zip(chunks, bs):
        b_last = b[chunk - 1:chunk, :]
        q = gq_ref[rows, :]
        k = gk_ref[rows, :]
        qes.append((q * jnp.exp(b)).astype(BF16))
        kes.append((k * jnp.exp(-b)).astype(BF16))
        kds.append((k * jnp.exp(b_last - b)).astype(BF16))
        decays.append(jnp.exp(b_last))

    a_s = [[jnp.where(causal, _dot_nt(qe[:, kc], ke[:, kc]), 0.0).astype(BF16) for kc in kcs]
           for qe, ke in zip(qes, kes)]
    incs = [[_dot_tn(gv_ref[rows, vc], kd[:, kc]) for kc, vc in zip(kcs, vcs)] for rows, kd in zip(chunks, kds)]

    s_in = []
    s_cur = [state[h] for h in range(GLA_HEADS)]
    for c in range(n_chunks):
        s_in.append([s.astype(BF16) for s in s_cur])
        s_cur = [s * decays[c][:, kc] + inc for s, kc, inc in zip(s_cur, kcs, incs[c])]
    for h in range(GLA_HEADS):
        state[h] = s_cur[h]

    for c, rows in enumerate(chunks):
        for h, (kc, vc) in enumerate(zip(kcs, vcs)):
            o = _dot_nt(qes[c][:, kc], s_in[c][h]) + _dot(a_s[c][h], gv_ref[rows, vc])
            go_ref[rows, vc] = (_rms(o, on_ref[...]) * jax.nn.silu(gr_ref[rows, vc])).astype(BF16)

    @pl.when(t == n_steps - 1)
    def _():
        for h in range(GLA_HEADS):
            st_ref[0, h] = state[h].T


def _gla(gq, gk, gg, gv, gr, onorm, s0, batch, seq, chunk):
    tt = min(GLA_TILE, seq)
    n_steps = seq // tt
    tok = lambda width: pl.BlockSpec((tt, width), lambda b, t: (b * n_steps + t, 0))
    st_spec = pl.BlockSpec((1, GLA_HEADS, GLA_DK, GLA_DV), lambda b, t: (b, 0, 0, 0))
    in_specs = [tok(GLA_QK), tok(GLA_QK), tok(GLA_QK), tok(GLA_V), tok(GLA_V),
                pl.BlockSpec((1, GLA_DV), lambda b, t: (0, 0))]
    args = [gq, gk, gg, gv, gr, onorm]
    if s0 is not None:
        in_specs.append(st_spec)
        args.append(s0)
    return pl.pallas_call(
        functools.partial(_gla_body, chunk=chunk, n_chunks=tt // chunk, n_steps=n_steps, has_state=s0 is not None),
        grid=(batch, n_steps),
        in_specs=in_specs,
        out_specs=[tok(GLA_V), st_spec],
        out_shape=[jax.ShapeDtypeStruct((batch * seq, GLA_V), BF16),
                   jax.ShapeDtypeStruct((batch, GLA_HEADS, GLA_DK, GLA_DV), F32)],
        scratch_shapes=[pltpu.VMEM((GLA_HEADS, GLA_DV, GLA_DK), F32)],
        compiler_params=_params(("parallel", "arbitrary")),
        name="gla",
    )(*args)


def _t5_bucket(rel):
    nb = REL_BUCKETS // 2
    max_exact = nb // 2
    ret = jnp.where(rel > 0, nb, 0)
    n = jnp.abs(rel)
    nf = jnp.maximum(n, 1).astype(F32)
    large = max_exact + (jnp.log(nf / max_exact) / math.log(REL_MAX_DIST / max_exact)
                         * (nb - max_exact)).astype(jnp.int32)
    large = jnp.minimum(large, nb - 1)
    return ret + jnp.where(n < max_exact, n, large)


def _bucket_tile(q_pos, k_pos):
    visible = (k_pos[None, :] // CHUNK) <= (q_pos[:, None] // CHUNK)
    return jnp.where(visible, _t5_bucket(k_pos[None, :] - q_pos[:, None]), MASKED_BUCKET).astype(jnp.int32)


def _bias_from_buckets(idx, tab_ref, head):
    def step(bk, acc):
        return jnp.where(idx == bk, tab_ref[bk * DIFF_HEADS + head], acc)
    return lax.fori_loop(0, REL_BUCKETS, step, jnp.full(idx.shape, NEG_INF, F32))


def _lambda(lam_ref, lam_init):
    l = lam_ref[...]
    return (jnp.exp(jnp.sum(l[0:1] * l[1:2], axis=-1, keepdims=True))
            - jnp.exp(jnp.sum(l[2:3] * l[3:4], axis=-1, keepdims=True)) + lam_init)


def _comp_masks(q):
    lane = lax.broadcasted_iota(jnp.int32, q.shape, 1)
    zero = jnp.zeros_like(q)
    return jnp.where(lane < DIFF_DH, q, zero), jnp.where(lane < DIFF_DH, zero, q)


def _diff_finish(o0, o1, lam, on_ref, lam_init, out_dtype):
    o = o0 - lam * o1
    return (_rms(o, on_ref[...]) * (1.0 - lam_init)).astype(out_dtype)


def _scores(q_c, parts):
    return [_dot_nt(q_c, k) + bias if jnp.ndim(bias) == 2 else _dot_nt(q_c, k) for k, _, bias in parts]


def _softmax_pv(scores, parts):
    shifts = [0.0 if jnp.ndim(bias) == 2 else bias for _, _, bias in parts]
    m = None
    for s, shift in zip(scores, shifts):
        part_max = s.max(axis=-1, keepdims=True) + shift
        m = part_max if m is None else jnp.maximum(m, part_max)
    ol = 0.0
    for s, shift, (_, v1, _) in zip(scores, shifts, parts):
        ol = ol + _dot(jnp.exp((s - (m - shift)).astype(BF16)), v1)
    return ol[:, :DIFF_DV] * (1.0 / ol[:, DIFF_DV:])


def _diff_prompt_body(tab_ref, q_ref, kb, vb, idx_ref, lam_ref, on_ref, o_ref, bias, v1, *, seq, lam_init):
    b = pl.program_id(0)
    h = pl.program_id(1)

    @pl.when(b == 0)
    def _():
        per_offset = _bias_from_buckets(idx_ref[...], tab_ref, h)[0:1]
        rows = jnp.broadcast_to(per_offset, (Q_BLOCK, 4 * Q_BLOCK))
        tile = pltpu.roll(rows, 3 * Q_BLOCK, 1, stride=1, stride_axis=0)[:, :2 * Q_BLOCK]
        q_pos = Q_BLOCK + lax.broadcasted_iota(jnp.int32, tile.shape, 0)
        k_pos = lax.broadcasted_iota(jnp.int32, tile.shape, 1)
        bias[h] = jnp.where(k_pos // CHUNK <= q_pos // CHUNK, tile, NEG_INF)

    v1[:, :DIFF_DV] = vb[...]
    v1[:, DIFF_DV:] = jnp.ones((seq, LANES), BF16)
    lam = _lambda(lam_ref, lam_init)
    far_bias = tab_ref[(REL_BUCKETS // 2 - 1) * DIFF_HEADS + h]

    def key_parts(i):
        if i == 0:
            return [(kb[0:Q_BLOCK, :], v1[0:Q_BLOCK, :], bias[h, :, Q_BLOCK:])]
        near0 = (i - 1) * Q_BLOCK
        near = slice(near0, near0 + 2 * Q_BLOCK)
        parts = [(kb[near, :], v1[near, :], bias[h])]
        if near0 > 0:
            parts.append((kb[0:near0, :], v1[0:near0, :], far_bias))
        return parts

    items = [(i, c) for i in range(seq // Q_BLOCK) for c in range(2)]
    comps = {}

    def scores_of(item):
        i, c = item
        if i not in comps:
            comps[i] = _comp_masks(q_ref[i * Q_BLOCK:(i + 1) * Q_BLOCK, :])
        return _scores(comps[i][c], key_parts(i))

    outs = {}
    ahead = scores_of(items[0])
    for n, (i, c) in enumerate(items):
        cur = ahead
        if n + 1 < len(items):
            ahead = scores_of(items[n + 1])
        outs[c] = _softmax_pv(cur, key_parts(i))
        if c == 1:
            o_ref[i * Q_BLOCK:(i + 1) * Q_BLOCK, :] = _diff_finish(outs[0], outs[1], lam, on_ref, lam_init,
                                                                    o_ref.dtype)


def _diff_prompt(table, dq, dk, dv, lam_p, onorm, batch, seq, lam_init):
    offsets = jnp.arange(4 * Q_BLOCK) - 2 * Q_BLOCK
    idx = jnp.broadcast_to(_t5_bucket(offsets).astype(jnp.int32)[None, :], (SUBLANES, 4 * Q_BLOCK))
    head = lambda: pl.BlockSpec((seq, LANES), lambda b, h: (b, h))
    return pl.pallas_call(
        functools.partial(_diff_prompt_body, seq=seq, lam_init=lam_init),
        grid=(batch, DIFF_HEADS),
        in_specs=[
            pl.BlockSpec(memory_space=pltpu.SMEM),
            head(), head(), head(),
            pl.BlockSpec((SUBLANES, 4 * Q_BLOCK), lambda b, h: (0, 0)),
            pl.BlockSpec((4, DIFF_DH), lambda b, h: (0, 0)),
            pl.BlockSpec((1, DIFF_DV), lambda b, h: (0, 0)),
        ],
        out_specs=head(),
        out_shape=jax.ShapeDtypeStruct((batch * seq, DIFF_V), BF16),
        scratch_shapes=[pltpu.VMEM((DIFF_HEADS, Q_BLOCK, 2 * Q_BLOCK), F32),
                        pltpu.VMEM((seq, DIFF_DV + LANES), BF16)],
        compiler_params=_params(("arbitrary", "arbitrary")),
        name="diff_prompt",
    )(table, dq, dk, dv, idx, lam_p, onorm)


def _diff_sample_body(tab_ref, q_ref, k_ref, v_ref, ck_ref, cv_ref, idxc_ref, idxn_ref, lam_ref, on_ref, o_ref,
                      bias_c, bias_n, *, seq, lam_init):
    groups = [(h, c) for h in range(DIFF_HEADS) for c in range(2)]

    @pl.when(pl.program_id(0) == 0)
    def _():
        for h in range(DIFF_HEADS):
            per_offset = _bias_from_buckets(idxc_ref[h], tab_ref, h)[0:1]
            rows = jnp.broadcast_to(per_offset, (seq, per_offset.shape[1]))
            tile = pltpu.roll(rows, rows.shape[1] - HEADS * seq, 1, stride=HEADS, stride_axis=0)[:, :bias_c.shape[1]]
            for g in (2 * h, 2 * h + 1):
                bias_c[g * seq:(g + 1) * seq, :] = tile
            bias_n[h] = _bias_from_buckets(idxn_ref[...], tab_ref, h)

    lam = _lambda(lam_ref, lam_init)
    head_cols = [slice(h * LANES, (h + 1) * LANES) for h in range(DIFF_HEADS)]
    q_groups = [q_c for cols in head_cols for q_c in _comp_masks(q_ref[:, cols])]
    s_old = _dot_nt(jnp.concatenate(q_groups, axis=0), ck_ref[...].astype(BF16)) + bias_c[...]
    s_new = jnp.concatenate([_dot_nt(q_g, k_ref[:, head_cols[h]]) + bias_n[h]
                             for q_g, (h, _) in zip(q_groups, groups)], axis=0)
    m = jnp.maximum(s_old.max(axis=-1, keepdims=True), s_new.max(axis=-1, keepdims=True))
    e_old = jnp.exp(s_old - m)
    e_new = jnp.exp(s_new - m)
    l = e_old.sum(axis=-1, keepdims=True) + e_new.sum(axis=-1, keepdims=True)
    o_old = _dot(e_old.astype(BF16), cv_ref[...].astype(BF16))
    e_new = e_new.astype(BF16)
    o_new = jnp.concatenate([_dot(e_new[g * seq:(g + 1) * seq, :], v_ref[:, head_cols[h]])
                             for g, (h, _) in enumerate(groups)], axis=0)
    o = (o_old + o_new) * (1.0 / l)
    for h, cols in enumerate(head_cols):
        o0 = o[(2 * h) * seq:(2 * h + 1) * seq, :]
        o1 = o[(2 * h + 1) * seq:(2 * h + 2) * seq, :]
        o_ref[:, cols] = _diff_finish(o0, o1, lam, on_ref, lam_init, o_ref.dtype)


def _diff_sample(table, dq, dk, dv, cache_k, cache_v, lam_p, onorm, batch, seq, past, lam_init):
    q_pos = past + jnp.arange(seq)
    idx_n = _bucket_tile(q_pos, past + jnp.arange(seq))
    row = jnp.arange(HEADS * (past + seq))
    idx_row = _t5_bucket(row // HEADS - seq - past)
    idx_c = jnp.where((row % HEADS)[None, :] == jnp.arange(HEADS)[:, None], idx_row[None, :], MASKED_BUCKET)
    idx_c = jnp.broadcast_to(idx_c.astype(jnp.int32)[:, None, :], (HEADS, SUBLANES, row.size))
    new_q = pl.BlockSpec((seq, DIFF_QK), lambda b: (b, 0))
    old = lambda: pl.BlockSpec((past * HEADS, LANES), lambda b: (b, 0))
    const = lambda shape: pl.BlockSpec(shape, lambda b: (0,) * len(shape))
    n_groups = 2 * DIFF_HEADS
    return pl.pallas_call(
        functools.partial(_diff_sample_body, seq=seq, lam_init=lam_init),
        grid=(batch,),
        in_specs=[pl.BlockSpec(memory_space=pltpu.SMEM), new_q, new_q, new_q, old(), old(),
                  const((HEADS, SUBLANES, HEADS * (past + seq))), const((seq, seq)), const((4, DIFF_DH)), const((1, DIFF_DV))],
        out_specs=new_q,
        out_shape=jax.ShapeDtypeStruct((batch * seq, DIFF_V), BF16),
        scratch_shapes=[pltpu.VMEM((n_groups * seq, past * HEADS), F32), pltpu.VMEM((DIFF_HEADS, seq, seq), F32)],
        compiler_params=_params(("arbitrary",)),
        name="diff_sample",
    )(table, dq, dk, dv, cache_k, cache_v, idx_c, idx_n, lam_p, onorm)


def _softmax(s):
    e = jnp.exp(s - s.max(axis=-1, keepdims=True))
    return (e * (1.0 / e.sum(axis=-1, keepdims=True))).astype(BF16)


def _memattn_body(q_ref, k_ref, v_ref, o_ref, *, interleaved):
    head_cols = [slice(h * MEM_DH, (h + 1) * MEM_DH) for h in range(MEM_HEADS)]
    scale = MEM_DH ** -0.5
    if interleaved:
        tq = q_ref.shape[0]
        s = _dot_nt(jnp.concatenate([q_ref[:, cols] for cols in head_cols], axis=0), k_ref[...].astype(BF16)) * scale
        q_head = lax.broadcasted_iota(jnp.int32, s.shape, 0) // tq
        k_head = lax.broadcasted_iota(jnp.int32, s.shape, 1) % MEM_HEADS
        o = _dot(_softmax(jnp.where(q_head == k_head, s, NEG_INF)), v_ref[...].astype(BF16))
        for h, cols in enumerate(head_cols):
            o_ref[:, cols] = o[h * tq:(h + 1) * tq, :].astype(o_ref.dtype)
    else:
        scores = [_dot_nt(q_ref[:, cols], k_ref[:, cols]) * scale for cols in head_cols]
        probs = [_softmax(s) for s in scores]
        for cols, p in zip(head_cols, probs):
            o_ref[:, cols] = _dot(p, v_ref[:, cols]).astype(o_ref.dtype)


def _memattn(mq, mk, mv, batch, seq, n_mem):
    tq = min(MEMATTN_TILE, seq)
    nq = seq // tq
    q_spec = pl.BlockSpec((tq, MEM_W), lambda b, i: (b * nq + i, 0))
    interleaved = mk.shape[1] == LANES
    kv_spec = pl.BlockSpec((mk.shape[0] // batch, mk.shape[1]), lambda b, i: (b, 0))
    return pl.pallas_call(
        functools.partial(_memattn_body, interleaved=interleaved),
        grid=(batch, nq),
        in_specs=[q_spec, kv_spec, kv_spec],
        out_specs=q_spec,
        out_shape=jax.ShapeDtypeStruct((batch * seq, MEM_W), BF16),
        compiler_params=_params(("parallel", "arbitrary")),
        name="memattn",
    )(mq, mk, mv)


def _outproj_body(x_ref, g_ref, d_ref, m_ref, wg_ref, wd_ref, wm_ref, o_ref):
    o_ref[...] = (x_ref[...] + _dot(g_ref[...], wg_ref[...]) + _dot(d_ref[...], wd_ref[...])
                  + _dot(m_ref[...], wm_ref[...]))


def _outproj(x, g, d, m, wo):
    n = x.shape[0]
    tm = min(TOK_TILE, n)
    tok = lambda width: pl.BlockSpec((tm, width), lambda i: (i, 0))
    w_rows = lambda rows, blk: pl.BlockSpec((rows, D_MODEL), lambda i: (blk, 0))
    return pl.pallas_call(
        _outproj_body,
        grid=(n // tm,),
        in_specs=[tok(D_MODEL), tok(GLA_V), tok(DIFF_V), tok(MEM_W),
                  w_rows(GLA_V, 0), w_rows(DIFF_V, GLA_V // DIFF_V), w_rows(MEM_W, (GLA_V + DIFF_V) // MEM_W)],
        out_specs=tok(D_MODEL),
        out_shape=jax.ShapeDtypeStruct((n, D_MODEL), F32),
        compiler_params=_params(("parallel",)),
        name="outproj",
    )(x, g, d, m, wo, wo, wo)


def _row(v):
    return v.reshape(1, -1).astype(F32)


def kernel(x_prompt, x_sample, mem_prompt, cache_diff_k, cache_diff_v, state_gla, cache_mem_k, cache_mem_v, rel_bias_table, norm_ffn1, w_ffn1_in, w_ffn1_out, norm_mix, w_in, w_gla_g2, b_gla_g, gla_out_norm, diff_q_norm, diff_k_norm, diff_lambda, diff_out_norm, mem_norm, w_mem_kv, mem_q_norm, mem_k_norm, w_o, norm_ffn2, w_ffn2_in, w_ffn2_out, norm_final):
    depth = norm_ffn1.shape[0]
    assert depth == 1, "single-layer step"
    layer = 0
    batch, seq, _ = x_prompt.shape
    dec_batch, dec_seq, _ = x_sample.shape
    past = cache_diff_k.shape[2]
    n_mem = mem_prompt.shape[1]
    lam_init = 0.8 - 0.6 * math.exp(-0.3 * layer)

    w_g2 =jnp.pad(w_gla_g2[layer].astype(BF16), ((0, LANES - GLA_GATE_RANK), (0, 0)))
    b_g = _row(b_gla_g[layer])
    qn = _row(jnp.tile(diff_q_norm[layer], DIFF_QK // DIFF_DH))
    kn = _row(jnp.tile(diff_k_norm[layer], DIFF_QK // DIFF_DH))
    mqn = _row(jnp.tile(mem_q_norm[layer], MEM_HEADS))
    mkn = _row(jnp.tile(mem_k_norm[layer], MEM_HEADS))
    table = rel_bias_table.astype(F32).reshape(-1)
    lam_p = diff_lambda[layer].astype(F32)
    gla_on = _row(gla_out_norm[layer])
    diff_on = _row(diff_out_norm[layer])

    mk, mv, mk_b, mv_b = _memkv(mem_prompt.reshape(batch * n_mem, D_MODEL), _row(mem_norm[layer]),
                                w_mem_kv[layer].astype(BF16), mkn)

    n1 = _row(norm_ffn1[layer])
    xs1, *ffn1 = _ffn(x_sample.reshape(dec_batch * dec_seq, D_MODEL), n1, w_ffn1_in[layer], w_ffn1_in[layer],
                      w_ffn1_out[layer], up_col0=D_FF, ff_tile=FF_TILE_F32, emit_bf16=True)
    n_in_blocks = w_ffn2_in.shape[2] // LANES
    xp1, w2_in, w2_out, w_t, wo = _ffn(
        x_prompt.reshape(batch * seq, D_MODEL), n1, *ffn1, up_col0=0,
        cast_later=[(w_ffn2_in[layer], (D_MODEL, LANES), 0),
                    (w_ffn2_out[layer], (D_FF // n_in_blocks, D_MODEL), n_in_blocks),
                    (jnp.swapaxes(w_in[layer], 0, 1), (CAST_ROWS_W_IN, D_MODEL), 0),
                    (w_o[layer], (CAST_ROWS_W_O, D_MODEL), 0)])
    ffn2 = (w2_in, w2_in, w2_out)

    def layer_fn(x, b, t, chunk, s0, diff_fn, mem_k, mem_v):
        gq, gk, gv, gr, gg, dq, dk, dv, mq, dk_b, dv_b = _proj(x, _row(norm_mix[layer]), w_t, w_g2, b_g,
                                                               qn, kn, mqn)
        g_out, g_state = _gla(gq, gk, gg, gv, gr, gla_on, s0, b, t, chunk)
        d_out = diff_fn(dq, dk_b, dv_b)
        m_out = _memattn(mq, mem_k, mem_v, b, t, n_mem)
        x = _outproj(x, g_out, d_out, m_out, wo)
        if x.shape[0] > FFN_TOK_TILE:
            x = _ffn_two_pass(x, _row(norm_ffn2[layer]), w2_in, w2_out, _row(norm_final[layer]))
        else:
            x = _ffn(x, _row(norm_ffn2[layer]), *ffn2, up_col0=D_FF, final_gain=_row(norm_final[layer]))
        return x, dk, dv, g_state

    yp, dk_p, dv_p, g_p = layer_fn(
        xp1, batch, seq, CHUNK, None,
        lambda dq, dk, dv: _diff_prompt(table, dq, dk, dv, lam_p, diff_on, batch, seq, lam_init), mk_b, mv_b)
    ys, dk_s, dv_s, g_s = layer_fn(
        xs1, dec_batch, dec_seq, dec_seq, state_gla[layer],
        lambda dq, dk, dv: _diff_sample(table, dq, dk, dv, cache_diff_k[layer].reshape(-1, LANES),
                                        cache_diff_v[layer].reshape(-1, LANES),
                                        lam_p, diff_on, dec_batch, dec_seq, past, lam_init),
        cache_mem_k[layer].reshape(-1, LANES), cache_mem_v[layer].reshape(-1, LANES))

    head4 = lambda a, b, t: a.reshape(1, b, t, HEADS, LANES)
    return (yp.reshape(batch, seq, D_MODEL), ys.reshape(dec_batch, dec_seq, D_MODEL),
            head4(dk_p, batch, seq), head4(dv_p, batch, seq), g_p[None],
            head4(mk, batch, n_mem), head4(mv, batch, n_mem),
            head4(dk_s, dec_batch, dec_seq), head4(dv_s, dec_batch, dec_seq), g_s[None])
```
